```python
import math
import jax, jax.numpy as jnp
from jax import lax
import numpy as np

D_MODEL = 1024
BATCH = 8
SEQ = 2048
DEPTH = 4

D_MIX = D_MODEL
ATTN_WIDTH = D_MIX // 2
HEAD_DIM = 64
N_HEADS = ATTN_WIDTH // HEAD_DIM
CONV_WIDTH = D_MIX - ATTN_WIDTH
CONV_GROUPS = CONV_WIDTH // HEAD_DIM
CONV_K = 31
D_IN = 3 * ATTN_WIDTH + N_HEADS + 2 * CONV_WIDTH
D_FF = int(math.ceil((8 * D_MODEL / 3) / 256) * 256)
Q_BLOCK = 128
N_MOD = 6
EPS = 1e-6

kernel_name = "fox_conformer_hymba_adaln_trunk"


def rms_norm(x, g):
    xf = x.astype(jnp.float32)
    y = xf * lax.rsqrt(jnp.mean(xf * xf, axis=-1, keepdims=True) + EPS)
    return (y * g.astype(jnp.float32)).astype(x.dtype)


def layer_norm(x, g, b):
    xf = x.astype(jnp.float32)
    mu = jnp.mean(xf, axis=-1, keepdims=True)
    var = jnp.mean(jnp.square(xf - mu), axis=-1, keepdims=True)
    y = (xf - mu) * lax.rsqrt(var + EPS)
    return (y * g.astype(jnp.float32) + b.astype(jnp.float32)).astype(x.dtype)


def fox_attention(q, k, v, log_f):
    B, S, H, Dh = q.shape
    nb = S // Q_BLOCK
    scale = Dh ** -0.5
    cum = jnp.cumsum(log_f, axis=1)
    cum_k = jnp.transpose(cum, (0, 2, 1))[:, :, None, :]
    q_blocks = jnp.transpose(q.reshape(B, nb, Q_BLOCK, H, Dh), (1, 0, 2, 3, 4))
    c_blocks = jnp.transpose(cum.reshape(B, nb, Q_BLOCK, H), (1, 0, 3, 2))
    k_pos = jnp.arange(S, dtype=jnp.int32)

    def one_block(args):
        i, q_blk, c_blk = args
        s = jnp.einsum('bqhd,bkhd->bhqk', q_blk, k).astype(jnp.float32) * scale
        s = s + c_blk[..., None] - cum_k
        q_pos = i * Q_BLOCK + jnp.arange(Q_BLOCK, dtype=jnp.int32)
        mask = k_pos[None, :] <= q_pos[:, None]
        s = jnp.where(mask[None, None], s, -jnp.inf)
        p = jax.nn.softmax(s, axis=-1)
        return jnp.einsum('bhqk,bkhd->bqhd', p.astype(v.dtype), v)

    outs = lax.map(one_block, (jnp.arange(nb, dtype=jnp.int32), q_blocks, c_blocks))
    return jnp.transpose(outs, (1, 0, 2, 3, 4)).reshape(B, S, H * Dh)


def causal_depthwise_conv(u, w, b):
    C = u.shape[-1]
    out = lax.conv_general_dilated(
        u, w[:, None, :].astype(u.dtype), window_strides=(1,), padding=((CONV_K - 1, 0),),
        dimension_numbers=('NWC', 'WIO', 'NWC'), feature_group_count=C)
    return out + b.astype(u.dtype)


def hybrid_mixer(h, w_in, b_f, conv_w, conv_b, conv_ln_g, conv_ln_b, w_o):
    B, S, _ = h.shape
    proj = h @ w_in
    splits = [ATTN_WIDTH, 2 * ATTN_WIDTH, 3 * ATTN_WIDTH, 3 * ATTN_WIDTH + N_HEADS,
              3 * ATTN_WIDTH + N_HEADS + CONV_WIDTH]
    q, k, v, f_logit, conv_val, conv_gate = jnp.split(proj, splits, axis=-1)
    q = q.reshape(B, S, N_HEADS, HEAD_DIM)
    k = k.reshape(B, S, N_HEADS, HEAD_DIM)
    v = v.reshape(B, S, N_HEADS, HEAD_DIM)
    log_f = jax.nn.log_sigmoid((f_logit + b_f).astype(jnp.float32))
    attn = fox_attention(q, k, v, log_f)
    u = conv_val * jax.nn.sigmoid(conv_gate)
    u = causal_depthwise_conv(u, conv_w, conv_b)
    u = jax.nn.silu(layer_norm(u, conv_ln_g, conv_ln_b))
    return jnp.concatenate([attn, u], axis=-1) @ w_o


def swiglu_ffn(h, w_ffn_in, w_ffn_out):
    g, u = jnp.split(h @ w_ffn_in, 2, axis=-1)
    return (jax.nn.silu(g) * u) @ w_ffn_out


def _fwd_setup_inputs(seed: int = 0) -> dict:
    key = jax.random.key(seed)
    ks = jax.random.split(key, 24)
    f32 = jnp.float32
    L, D = DEPTH, D_MODEL

    def nrm(k, shape, s):
        return jax.random.normal(k, shape, f32) * s

    x = jax.random.normal(ks[0], (BATCH, SEQ, D), f32)
    c = jax.random.normal(ks[1], (BATCH, D), f32)
    w_in = jnp.concatenate([
        nrm(ks[2], (L, D, 3 * ATTN_WIDTH), D ** -0.5),
        nrm(ks[3], (L, D, N_HEADS), 0.5 * D ** -0.5),
        nrm(ks[4], (L, D, 2 * CONV_WIDTH), D ** -0.5),
    ], axis=-1)
    b_f = 2.5 + 0.5 * jax.random.normal(ks[5], (L, N_HEADS), f32)
    conv_w = nrm(ks[6], (L, CONV_K, CONV_WIDTH), CONV_K ** -0.5)
    conv_b = nrm(ks[7], (L, CONV_WIDTH), 0.01)
    conv_ln_g = 1.0 + nrm(ks[8], (L, CONV_WIDTH), 0.05)
    conv_ln_b = nrm(ks[9], (L, CONV_WIDTH), 0.01)
    w_o = nrm(ks[10], (L, D_MIX, D), D_MIX ** -0.5)
    w_ffn_in = nrm(ks[11], (L, D, 2 * D_FF), D ** -0.5)
    w_ffn_out = nrm(ks[12], (L, D_FF, D), D_FF ** -0.5)
    mix_pre_g = 1.0 + nrm(ks[13], (L, D), 0.05)
    mix_post_g = 1.0 + nrm(ks[14], (L, D), 0.05)
    ffn_pre_g = 1.0 + nrm(ks[15], (L, D), 0.05)
    ffn_post_g = 1.0 + nrm(ks[16], (L, D), 0.05)
    ada_w = nrm(ks[17], (L, D, N_MOD * D), 0.5 * D ** -0.5)
    ada_b = nrm(ks[18], (L, N_MOD * D), 0.01)
    return {"x": x, "c": c, "w_in": w_in, "b_f": b_f, "conv_w": conv_w, "conv_b": conv_b,
            "conv_ln_g": conv_ln_g, "conv_ln_b": conv_ln_b, "w_o": w_o,
            "w_ffn_in": w_ffn_in, "w_ffn_out": w_ffn_out,
            "mix_pre_g": mix_pre_g, "mix_post_g": mix_post_g,
            "ffn_pre_g": ffn_pre_g, "ffn_post_g": ffn_post_g,
            "ada_w": ada_w, "ada_b": ada_b}


def _fwd_reference(x, c, w_in, b_f, conv_w, conv_b, conv_ln_g, conv_ln_b, w_o,
              w_ffn_in, w_ffn_out, mix_pre_g, mix_post_g, ffn_pre_g, ffn_post_g,
              ada_w, ada_b):
    c_act = jax.nn.silu(c)
    for i in range(DEPTH):
        mod = c_act @ ada_w[i] + ada_b[i]
        sh1, sc1, g1, sh2, sc2, g2 = [m[:, None, :] for m in jnp.split(mod, N_MOD, axis=-1)]
        h = rms_norm(x, mix_pre_g[i]) * (1.0 + sc1) + sh1
        y = hybrid_mixer(h, w_in[i], b_f[i], conv_w[i], conv_b[i], conv_ln_g[i], conv_ln_b[i], w_o[i])
        x = x + g1 * rms_norm(y, mix_post_g[i])
        h = rms_norm(x, ffn_pre_g[i]) * (1.0 + sc2) + sh2
        y = swiglu_ffn(h, w_ffn_in[i], w_ffn_out[i])
        x = x + g2 * rms_norm(y, ffn_post_g[i])
    return x


import jax as _jax
import jax.numpy as _jnp

TWIN_FORMAT = 'train_step'
FWD_PARAMS = ['x', 'c', 'w_in', 'b_f', 'conv_w', 'conv_b', 'conv_ln_g', 'conv_ln_b', 'w_o', 'w_ffn_in', 'w_ffn_out', 'mix_pre_g', 'mix_post_g', 'ffn_pre_g', 'ffn_post_g', 'ada_w', 'ada_b']
TWIN_WEIGHTS = ['w_in', 'b_f', 'conv_w', 'conv_b', 'conv_ln_g', 'conv_ln_b', 'w_o', 'w_ffn_in', 'w_ffn_out', 'mix_pre_g', 'mix_post_g', 'ffn_pre_g', 'ffn_post_g', 'ada_w', 'ada_b']
TWIN_DIFF_INPUT = 'x'
TWIN_INPUTS = ['x', 'c', 'w_in', 'b_f', 'conv_w', 'conv_b', 'conv_ln_g', 'conv_ln_b', 'w_o', 'w_ffn_in', 'w_ffn_out', 'mix_pre_g', 'mix_post_g', 'ffn_pre_g', 'ffn_post_g', 'ada_w', 'ada_b', 'loss_target', 'm_w_in', 'm_b_f', 'm_conv_w', 'm_conv_b', 'm_conv_ln_g', 'm_conv_ln_b', 'm_w_o', 'm_w_ffn_in', 'm_w_ffn_out', 'm_mix_pre_g', 'm_mix_post_g', 'm_ffn_pre_g', 'm_ffn_post_g', 'm_ada_w', 'm_ada_b', 'v_w_in', 'v_b_f', 'v_conv_w', 'v_conv_b', 'v_conv_ln_g', 'v_conv_ln_b', 'v_w_o', 'v_w_ffn_in', 'v_w_ffn_out', 'v_mix_pre_g', 'v_mix_post_g', 'v_ffn_pre_g', 'v_ffn_post_g', 'v_ada_w', 'v_ada_b']
TWIN_OUTPUTS = ['loss', 'grad_x', 'grad_w_in', 'grad_b_f', 'grad_conv_w', 'grad_conv_b', 'grad_conv_ln_g', 'grad_conv_ln_b', 'grad_w_o', 'grad_w_ffn_in', 'grad_w_ffn_out', 'grad_mix_pre_g', 'grad_mix_post_g', 'grad_ffn_pre_g', 'grad_ffn_post_g', 'grad_ada_w', 'grad_ada_b', 'delta_w_in', 'delta_b_f', 'delta_conv_w', 'delta_conv_b', 'delta_conv_ln_g', 'delta_conv_ln_b', 'delta_w_o', 'delta_w_ffn_in', 'delta_w_ffn_out', 'delta_mix_pre_g', 'delta_mix_post_g', 'delta_ffn_pre_g', 'delta_ffn_post_g', 'delta_ada_w', 'delta_ada_b', 'new_m_w_in', 'new_m_b_f', 'new_m_conv_w', 'new_m_conv_b', 'new_m_conv_ln_g', 'new_m_conv_ln_b', 'new_m_w_o', 'new_m_w_ffn_in', 'new_m_w_ffn_out', 'new_m_mix_pre_g', 'new_m_mix_post_g', 'new_m_ffn_pre_g', 'new_m_ffn_post_g', 'new_m_ada_w', 'new_m_ada_b', 'new_v_w_in', 'new_v_b_f', 'new_v_conv_w', 'new_v_conv_b', 'new_v_conv_ln_g', 'new_v_conv_ln_b', 'new_v_w_o', 'new_v_w_ffn_in', 'new_v_w_ffn_out', 'new_v_mix_pre_g', 'new_v_mix_post_g', 'new_v_ffn_pre_g', 'new_v_ffn_post_g', 'new_v_ada_w', 'new_v_ada_b']
TWIN_LEAF_KINDS = {'loss': 'loss', 'grad_x': 'grad_x', 'grad_w_in': 'grad_w', 'grad_b_f': 'grad_w', 'grad_conv_w': 'grad_w', 'grad_conv_b': 'grad_w', 'grad_conv_ln_g': 'grad_w', 'grad_conv_ln_b': 'grad_w', 'grad_w_o': 'grad_w', 'grad_w_ffn_in': 'grad_w', 'grad_w_ffn_out': 'grad_w', 'grad_mix_pre_g': 'grad_w', 'grad_mix_post_g': 'grad_w', 'grad_ffn_pre_g': 'grad_w', 'grad_ffn_post_g': 'grad_w', 'grad_ada_w': 'grad_w', 'grad_ada_b': 'grad_w', 'delta_w_in': 'delta_w', 'delta_b_f': 'delta_w', 'delta_conv_w': 'delta_w', 'delta_conv_b': 'delta_w', 'delta_conv_ln_g': 'delta_w', 'delta_conv_ln_b': 'delta_w', 'delta_w_o': 'delta_w', 'delta_w_ffn_in': 'delta_w', 'delta_w_ffn_out': 'delta_w', 'delta_mix_pre_g': 'delta_w', 'delta_mix_post_g': 'delta_w', 'delta_ffn_pre_g': 'delta_w', 'delta_ffn_post_g': 'delta_w', 'delta_ada_w': 'delta_w', 'delta_ada_b': 'delta_w', 'new_m_w_in': 'new_m', 'new_m_b_f': 'new_m', 'new_m_conv_w': 'new_m', 'new_m_conv_b': 'new_m', 'new_m_conv_ln_g': 'new_m', 'new_m_conv_ln_b': 'new_m', 'new_m_w_o': 'new_m', 'new_m_w_ffn_in': 'new_m', 'new_m_w_ffn_out': 'new_m', 'new_m_mix_pre_g': 'new_m', 'new_m_mix_post_g': 'new_m', 'new_m_ffn_pre_g': 'new_m', 'new_m_ffn_post_g': 'new_m', 'new_m_ada_w': 'new_m', 'new_m_ada_b': 'new_m', 'new_v_w_in': 'new_v', 'new_v_b_f': 'new_v', 'new_v_conv_w': 'new_v', 'new_v_conv_b': 'new_v', 'new_v_conv_ln_g': 'new_v', 'new_v_conv_ln_b': 'new_v', 'new_v_w_o': 'new_v', 'new_v_w_ffn_in': 'new_v', 'new_v_w_ffn_out': 'new_v', 'new_v_mix_pre_g': 'new_v', 'new_v_mix_post_g': 'new_v', 'new_v_ffn_pre_g': 'new_v', 'new_v_ffn_post_g': 'new_v', 'new_v_ada_w': 'new_v', 'new_v_ada_b': 'new_v'}


def _forward(args):
    return _fwd_reference(*[args[k] for k in FWD_PARAMS])


def _output_shape():
    out = _jax.eval_shape(lambda: _forward(_fwd_setup_inputs(0)))
    return out.shape, out.dtype

N_MICROBATCH = 1
ADAM_LR = 0.001
ADAM_B1 = 0.9
ADAM_B2 = 0.999
ADAM_EPS = 1e-08
ADAM_WD = 0.01
ADAM_STEP = 10
PER_EXAMPLE_BATCH_AXIS = {'x': 0, 'c': 0, 'loss_target': 0}
SHARED_INPUTS = []
_WEIGHT_DTYPES = {'w_in': _jnp.float32, 'b_f': _jnp.float32, 'conv_w': _jnp.float32, 'conv_b': _jnp.float32, 'conv_ln_g': _jnp.float32, 'conv_ln_b': _jnp.float32, 'w_o': _jnp.float32, 'w_ffn_in': _jnp.float32, 'w_ffn_out': _jnp.float32, 'mix_pre_g': _jnp.float32, 'mix_post_g': _jnp.float32, 'ffn_pre_g': _jnp.float32, 'ffn_post_g': _jnp.float32, 'ada_w': _jnp.float32, 'ada_b': _jnp.float32}
MOMENT_SCALE = {'w_in': 3.191320e-01, 'b_f': 3.170447e-01, 'conv_w': 3.007518e-01, 'conv_b': 1.581290e+00, 'conv_ln_g': 7.256742e-01, 'conv_ln_b': 9.770697e-01, 'w_o': 5.905601e-01, 'w_ffn_in': 8.753980e-02, 'w_ffn_out': 1.670707e-01, 'mix_pre_g': 2.325927e-01, 'mix_post_g': 2.014463e+00, 'ffn_pre_g': 1.492380e-01, 'ffn_post_g': 1.916232e+00, 'ada_w': 1.111751e+00, 'ada_b': 1.996910e+00}


def _to_microbatches(a, axis):
    t = _jnp.moveaxis(a, axis, 0)
    t = t.reshape((N_MICROBATCH, t.shape[0] // N_MICROBATCH) + t.shape[1:])
    return _jnp.moveaxis(t, 1, axis + 1)


def setup_inputs(seed: int = 0) -> dict:
    inp = _fwd_setup_inputs(seed)
    key = _jax.random.fold_in(_jax.random.key(seed), 7919)
    shape, _ = _output_shape()
    out = dict(inp)
    out["loss_target"] = _jax.random.normal(_jax.random.fold_in(key, 0), shape, _jnp.float32)
    for i, name in enumerate(TWIN_WEIGHTS):
        w = inp[name].astype(_jnp.float32)
        if MOMENT_SCALE is None:
            s = _jnp.sqrt(_jnp.mean(_jnp.square(w)) + 1e-30)
        else:
            s = MOMENT_SCALE[name]
        km, kv = _jax.random.split(_jax.random.fold_in(key, i + 1))
        out[name] = w
        out["m_" + name] = s * _jax.random.normal(km, w.shape, _jnp.float32)
        out["v_" + name] = (s * s) * _jax.random.uniform(kv, w.shape, _jnp.float32, 0.5, 1.5)
    if N_MICROBATCH > 1:
        for name, axis in PER_EXAMPLE_BATCH_AXIS.items():
            out[name] = _to_microbatches(out[name], axis)
    return {'x': out['x'], 'c': out['c'], 'w_in': out['w_in'], 'b_f': out['b_f'], 'conv_w': out['conv_w'], 'conv_b': out['conv_b'], 'conv_ln_g': out['conv_ln_g'], 'conv_ln_b': out['conv_ln_b'], 'w_o': out['w_o'], 'w_ffn_in': out['w_ffn_in'], 'w_ffn_out': out['w_ffn_out'], 'mix_pre_g': out['mix_pre_g'], 'mix_post_g': out['mix_post_g'], 'ffn_pre_g': out['ffn_pre_g'], 'ffn_post_g': out['ffn_post_g'], 'ada_w': out['ada_w'], 'ada_b': out['ada_b'], 'loss_target': out['loss_target'], 'm_w_in': out['m_w_in'], 'm_b_f': out['m_b_f'], 'm_conv_w': out['m_conv_w'], 'm_conv_b': out['m_conv_b'], 'm_conv_ln_g': out['m_conv_ln_g'], 'm_conv_ln_b': out['m_conv_ln_b'], 'm_w_o': out['m_w_o'], 'm_w_ffn_in': out['m_w_ffn_in'], 'm_w_ffn_out': out['m_w_ffn_out'], 'm_mix_pre_g': out['m_mix_pre_g'], 'm_mix_post_g': out['m_mix_post_g'], 'm_ffn_pre_g': out['m_ffn_pre_g'], 'm_ffn_post_g': out['m_ffn_post_g'], 'm_ada_w': out['m_ada_w'], 'm_ada_b': out['m_ada_b'], 'v_w_in': out['v_w_in'], 'v_b_f': out['v_b_f'], 'v_conv_w': out['v_conv_w'], 'v_conv_b': out['v_conv_b'], 'v_conv_ln_g': out['v_conv_ln_g'], 'v_conv_ln_b': out['v_conv_ln_b'], 'v_w_o': out['v_w_o'], 'v_w_ffn_in': out['v_w_ffn_in'], 'v_w_ffn_out': out['v_w_ffn_out'], 'v_mix_pre_g': out['v_mix_pre_g'], 'v_mix_post_g': out['v_mix_post_g'], 'v_ffn_pre_g': out['v_ffn_pre_g'], 'v_ffn_post_g': out['v_ffn_post_g'], 'v_ada_w': out['v_ada_w'], 'v_ada_b': out['v_ada_b']}


def _loss(weights, diff, rest, loss_target):
    with _jax.named_scope("forward"):
        args = {**rest, TWIN_DIFF_INPUT: diff, **{k: w.astype(_WEIGHT_DTYPES[k]) for k, w in weights.items()}}
        y = _forward(args)
    with _jax.named_scope("loss_head"):
        err = _jnp.square(y.astype(_jnp.float32) - loss_target)
        return 0.5 * _jnp.sum(_jnp.mean(err, axis=-1)) if err.ndim else 0.5 * err


def _adamw(w, g, m, v):
    m = ADAM_B1 * m + (1.0 - ADAM_B1) * g
    v = ADAM_B2 * v + (1.0 - ADAM_B2) * _jnp.square(g)
    m_hat = m / (1.0 - ADAM_B1 ** ADAM_STEP)
    v_hat = v / (1.0 - ADAM_B2 ** ADAM_STEP)
    delta = -ADAM_LR * (m_hat / (_jnp.sqrt(v_hat) + ADAM_EPS) + ADAM_WD * w)
    return delta, m, v


def reference(x, c, w_in, b_f, conv_w, conv_b, conv_ln_g, conv_ln_b, w_o, w_ffn_in, w_ffn_out, mix_pre_g, mix_post_g, ffn_pre_g, ffn_post_g, ada_w, ada_b, loss_target, m_w_in, m_b_f, m_conv_w, m_conv_b, m_conv_ln_g, m_conv_ln_b, m_w_o, m_w_ffn_in, m_w_ffn_out, m_mix_pre_g, m_mix_post_g, m_ffn_pre_g, m_ffn_post_g, m_ada_w, m_ada_b, v_w_in, v_b_f, v_conv_w, v_conv_b, v_conv_ln_g, v_conv_ln_b, v_w_o, v_w_ffn_in, v_w_ffn_out, v_mix_pre_g, v_mix_post_g, v_ffn_pre_g, v_ffn_post_g, v_ada_w, v_ada_b):
    given = dict(x=x, c=c, w_in=w_in, b_f=b_f, conv_w=conv_w, conv_b=conv_b, conv_ln_g=conv_ln_g, conv_ln_b=conv_ln_b, w_o=w_o, w_ffn_in=w_ffn_in, w_ffn_out=w_ffn_out, mix_pre_g=mix_pre_g, mix_post_g=mix_post_g, ffn_pre_g=ffn_pre_g, ffn_post_g=ffn_post_g, ada_w=ada_w, ada_b=ada_b, loss_target=loss_target, m_w_in=m_w_in, m_b_f=m_b_f, m_conv_w=m_conv_w, m_conv_b=m_conv_b, m_conv_ln_g=m_conv_ln_g, m_conv_ln_b=m_conv_ln_b, m_w_o=m_w_o, m_w_ffn_in=m_w_ffn_in, m_w_ffn_out=m_w_ffn_out, m_mix_pre_g=m_mix_pre_g, m_mix_post_g=m_mix_post_g, m_ffn_pre_g=m_ffn_pre_g, m_ffn_post_g=m_ffn_post_g, m_ada_w=m_ada_w, m_ada_b=m_ada_b, v_w_in=v_w_in, v_b_f=v_b_f, v_conv_w=v_conv_w, v_conv_b=v_conv_b, v_conv_ln_g=v_conv_ln_g, v_conv_ln_b=v_conv_ln_b, v_w_o=v_w_o, v_w_ffn_in=v_w_ffn_in, v_w_ffn_out=v_w_ffn_out, v_mix_pre_g=v_mix_pre_g, v_mix_post_g=v_mix_post_g, v_ffn_pre_g=v_ffn_pre_g, v_ffn_post_g=v_ffn_post_g, v_ada_w=v_ada_w, v_ada_b=v_ada_b)
    weights = {n: given[n] for n in TWIN_WEIGHTS}
    shared = {n: given[n] for n in SHARED_INPUTS}
    per_example = {n: given[n] for n in ['x', 'c']}
    grad_fn = _jax.value_and_grad(_loss, argnums=(0, 1))

    def one_microbatch(ex, loss_target):
        ex = dict(ex)
        diff = ex.pop(TWIN_DIFF_INPUT)
        return grad_fn(weights, diff, {**shared, **ex}, loss_target)

    if N_MICROBATCH == 1:
        loss, (grad_w, grad_x) = one_microbatch(per_example, given["loss_target"])
    else:
        def body(carry, xs):
            loss_sum, grad_sum = carry
            l_k, (gw_k, gx_k) = one_microbatch(xs[0], xs[1])
            with _jax.named_scope("update"):
                return (loss_sum + l_k, _jax.tree.map(_jnp.add, grad_sum, gw_k)), gx_k

        init = (_jnp.zeros((), _jnp.float32), _jax.tree.map(_jnp.zeros_like, weights))
        (loss, grad_w), grad_x = _jax.lax.scan(body, init, (per_example, given["loss_target"]))
    with _jax.named_scope("update"):
        delta_w, new_m, new_v = {}, {}, {}
        for n in TWIN_WEIGHTS:
            delta_w[n], new_m[n], new_v[n] = _adamw(weights[n], grad_w[n], given["m_" + n], given["v_" + n])
    return (loss, grad_x, *[grad_w[n] for n in TWIN_WEIGHTS], *[delta_w[n] for n in TWIN_WEIGHTS],
            *[new_m[n] for n in TWIN_WEIGHTS], *[new_v[n] for n in TWIN_WEIGHTS])
```

```python
import collections
import functools

import jax
import jax.numpy as jnp
from jax import lax
from jax.experimental import pallas as pl
from jax.experimental.pallas import tpu as pltpu

F32 = jnp.float32
BF16 = jnp.bfloat16
MESH = pl.DeviceIdType.MESH

HEAD_DIM = 64
CONV_K = 31
N_MOD = 6
EPS = 1e-6
N_CHIPS = 4
N_DEV = 8
LANES = 128
HALO = 32
ROW_TILE = 256
MM_TM = 512
MM_TN_MAX = 1408
TN_TM = 256
VMEM_LIMIT = 56 * 1024 * 1024

ADAM_LR = 0.001
ADAM_B1 = 0.9
ADAM_B2 = 0.999
ADAM_EPS = 1e-08
ADAM_WD = 0.01
ADAM_STEP = 10

WEIGHTS = ['w_in', 'b_f', 'conv_w', 'conv_b', 'conv_ln_g', 'conv_ln_b', 'w_o', 'w_ffn_in', 'w_ffn_out',
           'mix_pre_g', 'mix_post_g', 'ffn_pre_g', 'ffn_post_g', 'ada_w', 'ada_b']

Dims = collections.namedtuple("Dims", "S D A C H F L NP ts")


def _cp(n_grid=0):
    if n_grid:
        return pltpu.CompilerParams(dimension_semantics=("arbitrary",) * n_grid, vmem_limit_bytes=VMEM_LIMIT)
    return pltpu.CompilerParams(vmem_limit_bytes=VMEM_LIMIT)


def _tile(n, cap, also=None):
    best = None
    t = LANES
    while t <= min(n, cap):
        if n % t == 0 and (also is None or also % t == 0):
            best = t
        t += LANES
    assert best is not None, (n, cap, also)
    return best


def _bf(v):
    return v if v.dtype == BF16 else v.astype(BF16)


def _place():
    return lax.axis_index("x"), lax.axis_index("y"), lax.axis_index("c")


def _flip(v, d):
    return 1 - v if d else v


def _all_gather_devices(a, name):
    def body(a_ref, o_ref, send, recv, lsem):
        x, y, c = _place()
        me = 4 * x + 2 * y + c
        local = pltpu.make_async_copy(a_ref, o_ref.at[me], lsem)
        local.start()
        copies = []
        for k in range(1, N_DEV):
            peer = (_flip(x, (k >> 2) & 1), _flip(y, (k >> 1) & 1), _flip(c, k & 1))
            cp = pltpu.make_async_remote_copy(src_ref=a_ref, dst_ref=o_ref.at[me], send_sem=send.at[k - 1],
                                              recv_sem=recv.at[k - 1], device_id=peer, device_id_type=MESH)
            cp.start()
            copies.append(cp)
        for cp in copies:
            cp.wait()
        local.wait()

    return pl.pallas_call(
        body, name=name,
        out_shape=jax.ShapeDtypeStruct((N_DEV,) + a.shape, a.dtype),
        in_specs=[pl.BlockSpec(memory_space=pl.ANY)],
        out_specs=pl.BlockSpec(memory_space=pl.ANY),
        scratch_shapes=[pltpu.SemaphoreType.DMA((N_DEV - 1,)), pltpu.SemaphoreType.DMA((N_DEV - 1,)),
                        pltpu.SemaphoreType.DMA],
    )(a)


def _all_gather_chips(arrays, name):
    n = len(arrays)

    def body(*refs):
        a_refs, o_refs = refs[:n], refs[n:2 * n]
        send, recv, lsem = refs[2 * n:]
        x, y, c = _place()
        me = 2 * x + y
        copies = []
        for i in range(n):
            local = pltpu.make_async_copy(a_refs[i], o_refs[i].at[me], lsem.at[i])
            local.start()
            copies.append(local)
            for k in range(1, N_CHIPS):
                peer = (_flip(x, (k >> 1) & 1), _flip(y, k & 1), c)
                cp = pltpu.make_async_remote_copy(src_ref=a_refs[i], dst_ref=o_refs[i].at[me],
                                                  send_sem=send.at[i, k - 1], recv_sem=recv.at[i, k - 1],
                                                  device_id=peer, device_id_type=MESH)
                cp.start()
                copies.append(cp)
        for cp in copies:
            cp.wait()

    return pl.pallas_call(
        body, name=name,
        out_shape=[jax.ShapeDtypeStruct((N_CHIPS,) + a.shape, a.dtype) for a in arrays],
        in_specs=[pl.BlockSpec(memory_space=pl.ANY)] * n,
        out_specs=[pl.BlockSpec(memory_space=pl.ANY)] * n,
        scratch_shapes=[pltpu.SemaphoreType.DMA((n, N_CHIPS - 1)), pltpu.SemaphoreType.DMA((n, N_CHIPS - 1)),
                        pltpu.SemaphoreType.DMA((n,))],
    )(*arrays)


def _reduce_scatter_chips(arrays, name):
    n = len(arrays)

    def body(*refs):
        g_refs, near, far = refs[:n], refs[n:2 * n], refs[2 * n:3 * n]
        send_a, recv_a, send_b, recv_b, lsem = refs[3 * n:]
        x, y, c = _place()
        me = 2 * x + y
        sibling = (x, y, 1 - c)
        flips = [((k >> 1) & 1, k & 1) for k in range(1, N_CHIPS)]
        chips = [2 * _flip(x, dx) + _flip(y, dy) for dx, dy in flips]
        sends, locals_ = [], []
        for i in range(n):
            local = pltpu.make_async_copy(g_refs[i].at[me], near[i].at[me], lsem.at[i])
            local.start()
            locals_.append(local)
            for k, (dx, dy) in enumerate(flips):
                cp = pltpu.make_async_remote_copy(src_ref=g_refs[i].at[chips[k]], dst_ref=near[i].at[me],
                                                  send_sem=send_a.at[i, k], recv_sem=recv_a.at[i, k],
                                                  device_id=(_flip(x, dx), _flip(y, dy), c), device_id_type=MESH)
                cp.start()
                sends.append(cp)
        forwards = []
        for i in range(n):
            locals_[i].wait()
            slots = [me] + chips
            for k, slot in enumerate(slots):
                if k > 0:
                    pltpu.make_async_remote_copy(src_ref=g_refs[i].at[slot], dst_ref=near[i].at[slot],
                                                 send_sem=send_a.at[i, k - 1], recv_sem=recv_a.at[i, k - 1],
                                                 device_id=sibling, device_id_type=MESH).wait_recv()
                fw = pltpu.make_async_remote_copy(src_ref=near[i].at[slot], dst_ref=far[i].at[slot],
                                                  send_sem=send_b.at[i, k], recv_sem=recv_b.at[i, k],
                                                  device_id=sibling, device_id_type=MESH)
                fw.start()
                forwards.append(fw)
        for cp in sends:
            cp.wait_send()
        for fw in forwards:
            fw.wait()

    outs = pl.pallas_call(
        body, name=name,
        out_shape=[jax.ShapeDtypeStruct(a.shape, a.dtype) for a in arrays] * 2,
        in_specs=[pl.BlockSpec(memory_space=pl.ANY)] * n,
        out_specs=[pl.BlockSpec(memory_space=pl.ANY)] * (2 * n),
        scratch_shapes=[pltpu.SemaphoreType.DMA((n, N_CHIPS - 1)), pltpu.SemaphoreType.DMA((n, N_CHIPS - 1)),
                        pltpu.SemaphoreType.DMA((n, N_CHIPS)), pltpu.SemaphoreType.DMA((n, N_CHIPS)),
                        pltpu.SemaphoreType.DMA((n,))],
    )(*arrays)
    return outs[:n], outs[n:]


def _matmul(a, b, contract, grid, a_spec, b_spec, o_spec, out_shape, name, nk=1, acc_shape=None):
    def body(a_ref, b_ref, o_ref, *acc):
        r = lax.dot_general(_bf(a_ref[...]), _bf(b_ref[...]), (contract, ((), ())), preferred_element_type=F32)
        if nk == 1:
            o_ref[...] = r.astype(o_ref.dtype)
        else:
            k = pl.program_id(len(grid) - 1)

            @pl.when(k == 0)
            def _():
                acc[0][...] = r

            @pl.when(k > 0)
            def _():
                acc[0][...] += r

            @pl.when(k == nk - 1)
            def _():
                o_ref[...] = acc[0][...].astype(o_ref.dtype)

    return pl.pallas_call(
        body, name=name, grid=grid, in_specs=[a_spec, b_spec], out_specs=o_spec, out_shape=out_shape,
        scratch_shapes=[pltpu.VMEM(acc_shape, F32)] if nk > 1 else [],
        compiler_params=_cp(len(grid)),
    )(a, b)


def _mm_nn(a, b, out_dtype, name, col0=0, n=None):
    m, k = a.shape
    n = b.shape[1] - col0 if n is None else n
    tm = min(MM_TM, m)
    tn = _tile(n, MM_TN_MAX, also=col0 if col0 else None)
    off = col0 // tn
    return _matmul(a, b, ((1,), (0,)), (n // tn, m // tm),
                   pl.BlockSpec((tm, k), lambda j, i: (i, 0)),
                   pl.BlockSpec((k, tn), lambda j, i: (0, j + off)),
                   pl.BlockSpec((tm, tn), lambda j, i: (i, j)),
                   jax.ShapeDtypeStruct((m, n), out_dtype), name)


def _mm_nn_blocked(a, b3, out_dtype, name):
    m, k = a.shape
    nj, _, nb = b3.shape
    tm = min(MM_TM, m)
    return _matmul(a, b3, ((1,), (0,)), (nj, m // tm),
                   pl.BlockSpec((tm, k), lambda j, i: (i, 0)),
                   pl.BlockSpec((None, k, nb), lambda j, i: (j, 0, 0)),
                   pl.BlockSpec((tm, nb), lambda j, i: (i, j)),
                   jax.ShapeDtypeStruct((m, nj * nb), out_dtype), name)


def _mm_nt(a, b, out_dtype, name):
    m, k = a.shape
    n = b.shape[0]
    tm = min(MM_TM, m)
    tn = _tile(n, MM_TN_MAX)
    return _matmul(a, b, ((1,), (1,)), (n // tn, m // tm),
                   pl.BlockSpec((tm, k), lambda j, i: (i, 0)),
                   pl.BlockSpec((tn, k), lambda j, i: (j, 0)),
                   pl.BlockSpec((tm, tn), lambda j, i: (i, j)),
                   jax.ShapeDtypeStruct((m, n), out_dtype), name)


def _mm_nt_blocked(a, b3, out_dtype, name):
    m = a.shape[0]
    nj, n, nb = b3.shape
    tm = min(MM_TM, m)
    tn = _tile(n, 512)
    return _matmul(a, b3, ((1,), (1,)), (n // tn, m // tm, nj),
                   pl.BlockSpec((tm, nb), lambda j, i, k: (i, k)),
                   pl.BlockSpec((None, tn, nb), lambda j, i, k: (k, j, 0)),
                   pl.BlockSpec((tm, tn), lambda j, i, k: (i, j)),
                   jax.ShapeDtypeStruct((m, n), out_dtype), name, nk=nj, acc_shape=(tm, tn))


def _mm_tn(a, b, out_dtype, name, blocked=False):
    k, m = a.shape
    n = b.shape[1]
    tm = min(TN_TM, m)
    tn = n // N_CHIPS if blocked else _tile(n, MM_TN_MAX)
    if blocked:
        o_spec = pl.BlockSpec((None, tm, tn), lambda j, i: (j, i, 0))
        out_shape = jax.ShapeDtypeStruct((n // tn, m, tn), out_dtype)
    else:
        o_spec = pl.BlockSpec((tm, tn), lambda j, i: (i, j))
        out_shape = jax.ShapeDtypeStruct((m, n), out_dtype)
    return _matmul(a, b, ((0,), (0,)), (n // tn, m // tm),
                   pl.BlockSpec((k, tm), lambda j, i: (0, i)),
                   pl.BlockSpec((k, tn), lambda j, i: (0, j)),
                   o_spec, out_shape, name)


def _vec_spec(d):
    return pl.BlockSpec((1, d), lambda i: (0, 0))


def _row_spec(ts, d, col=0):
    return pl.BlockSpec((ts, d), lambda i: (i, col))


def _rms(x):
    return lax.rsqrt(jnp.mean(x * x, axis=-1, keepdims=True) + EPS)


def _pre_norm(x, gain, scale, shift, dm):
    def body(x_ref, g_ref, sc_ref, sh_ref, h_ref):
        xv = x_ref[...]
        h_ref[...] = (((xv * _rms(xv)) * g_ref[...]) * (1.0 + sc_ref[...]) + sh_ref[...]).astype(h_ref.dtype)

    return pl.pallas_call(
        body, name="pre_norm", grid=(dm.S // dm.ts,),
        in_specs=[_row_spec(dm.ts, dm.D)] + [_vec_spec(dm.D)] * 3,
        out_specs=_row_spec(dm.ts, dm.D),
        out_shape=jax.ShapeDtypeStruct((dm.S, dm.D), BF16), compiler_params=_cp(1),
    )(x, gain, scale, shift)


def _res_norm(x, y, gpost, gate, gain, scale, shift, dm):
    def body(x_ref, y_ref, gp_ref, gt_ref, g_ref, sc_ref, sh_ref, xo_ref, h_ref):
        yv = y_ref[...]
        xn = x_ref[...] + gt_ref[...] * ((yv * _rms(yv)) * gp_ref[...])
        xo_ref[...] = xn
        h_ref[...] = (((xn * _rms(xn)) * g_ref[...]) * (1.0 + sc_ref[...]) + sh_ref[...]).astype(h_ref.dtype)

    return pl.pallas_call(
        body, name="res_norm", grid=(dm.S // dm.ts,),
        in_specs=[_row_spec(dm.ts, dm.D)] * 2 + [_vec_spec(dm.D)] * 5,
        out_specs=[_row_spec(dm.ts, dm.D)] * 2,
        out_shape=[jax.ShapeDtypeStruct((dm.S, dm.D), F32), jax.ShapeDtypeStruct((dm.S, dm.D), BF16)],
        compiler_params=_cp(1),
    )(x, y, gpost, gate, gain, scale, shift)


def _res_loss(x, y, gpost, gate, target, dm):
    def body(x_ref, y_ref, gp_ref, gt_ref, t_ref, dx_ref, loss_ref):
        i = pl.program_id(0)

        @pl.when(i == 0)
        def _():
            loss_ref[...] = jnp.zeros_like(loss_ref)
        yv = y_ref[...]
        err = x_ref[...] + gt_ref[...] * ((yv * _rms(yv)) * gp_ref[...]) - t_ref[...]
        dx_ref[...] = err * (1.0 / dm.D)
        per_row = jnp.mean(err * err, axis=-1, keepdims=True)
        loss_ref[...] += 0.5 * jnp.sum(per_row, axis=0, keepdims=True)

    return pl.pallas_call(
        body, name="res_loss", grid=(dm.S // dm.ts,),
        in_specs=[_row_spec(dm.ts, dm.D)] * 2 + [_vec_spec(dm.D)] * 2 + [_row_spec(dm.ts, dm.D)],
        out_specs=[_row_spec(dm.ts, dm.D), _vec_spec(LANES)],
        out_shape=[jax.ShapeDtypeStruct((dm.S, dm.D), F32), jax.ShapeDtypeStruct((1, LANES), F32)],
        compiler_params=_cp(1),
    )(x, y, gpost, gate, target)


def _post_bwd(dxo, y, gpost, gate, dm):
    def body(dx_ref, y_ref, gp_ref, gt_ref, dy_ref, dgp_ref, dgt_ref):
        i = pl.program_id(0)

        @pl.when(i == 0)
        def _():
            dgp_ref[...] = jnp.zeros_like(dgp_ref)
            dgt_ref[...] = jnp.zeros_like(dgt_ref)
        yv, dx = y_ref[...], dx_ref[...]
        r = _rms(yv)
        t = yv * r
        dgp_ref[...] += jnp.sum(dx * gt_ref[...] * t, axis=0, keepdims=True)
        dgt_ref[...] += jnp.sum(dx * (t * gp_ref[...]), axis=0, keepdims=True)
        dt = dx * (gt_ref[...] * gp_ref[...])
        dy_ref[...] = (r * (dt - t * jnp.mean(dt * t, axis=-1, keepdims=True))).astype(dy_ref.dtype)

    return pl.pallas_call(
        body, name="post_bwd", grid=(dm.S // dm.ts,),
        in_specs=[_row_spec(dm.ts, dm.D)] * 2 + [_vec_spec(dm.D)] * 2,
        out_specs=[_row_spec(dm.ts, dm.D), _vec_spec(dm.D), _vec_spec(dm.D)],
        out_shape=[jax.ShapeDtypeStruct((dm.S, dm.D), BF16)] + [jax.ShapeDtypeStruct((1, dm.D), F32)] * 2,
        compiler_params=_cp(1),
    )(dxo, y, gpost, gate)


def _pre_bwd(dh, dxo, x, gain, scale, dm):
    def body(dh_ref, dxo_ref, x_ref, g_ref, sc_ref, dx_ref, dsh_ref, dsc_ref, dg_ref):
        i = pl.program_id(0)

        @pl.when(i == 0)
        def _():
            dsh_ref[...] = jnp.zeros_like(dsh_ref)
            dsc_ref[...] = jnp.zeros_like(dsc_ref)
            dg_ref[...] = jnp.zeros_like(dg_ref)
        xv, dh_ = x_ref[...], dh_ref[...]
        r = _rms(xv)
        nrm = xv * r
        one_sc = 1.0 + sc_ref[...]
        dsh_ref[...] += jnp.sum(dh_, axis=0, keepdims=True)
        dsc_ref[...] += jnp.sum(dh_ * (nrm * g_ref[...]), axis=0, keepdims=True)
        dg_ref[...] += jnp.sum(dh_ * nrm * one_sc, axis=0, keepdims=True)
        dn = dh_ * (g_ref[...] * one_sc)
        dx_ref[...] = dxo_ref[...] + r * (dn - nrm * jnp.mean(dn * nrm, axis=-1, keepdims=True))

    return pl.pallas_call(
        body, name="pre_bwd", grid=(dm.S // dm.ts,),
        in_specs=[_row_spec(dm.ts, dm.D)] * 3 + [_vec_spec(dm.D)] * 2,
        out_specs=[_row_spec(dm.ts, dm.D)] + [_vec_spec(dm.D)] * 3,
        out_shape=[jax.ShapeDtypeStruct((dm.S, dm.D), F32)] + [jax.ShapeDtypeStruct((1, dm.D), F32)] * 3,
        compiler_params=_cp(1),
    )(dh, dxo, x, gain, scale)


def _sigmoid(z):
    return 1.0 / (1.0 + jnp.exp(-z))


def _swiglu_fwd(gu, dm):
    def body(g_ref, u_ref, a_ref):
        g = g_ref[...]
        a_ref[...] = (g * _sigmoid(g) * u_ref[...]).astype(a_ref.dtype)

    return pl.pallas_call(
        body, name="swiglu_fwd", grid=(dm.S // dm.ts,),
        in_specs=[_row_spec(dm.ts, dm.F, 0), _row_spec(dm.ts, dm.F, 1)],
        out_specs=_row_spec(dm.ts, dm.F),
        out_shape=jax.ShapeDtypeStruct((dm.S, dm.F), BF16), compiler_params=_cp(1),
    )(gu, gu)


def _swiglu_bwd(da, gu, dm):
    def body(da_ref, g_ref, u_ref, d_ref):
        g, u, da_ = g_ref[...], u_ref[...], da_ref[...]
        sg = _sigmoid(g)
        d_ref[:, :dm.F] = (da_ * u * (sg * (1.0 + g * (1.0 - sg)))).astype(d_ref.dtype)
        d_ref[:, dm.F:] = (da_ * (g * sg)).astype(d_ref.dtype)

    return pl.pallas_call(
        body, name="swiglu_bwd", grid=(dm.S // dm.ts,),
        in_specs=[_row_spec(dm.ts, dm.F), _row_spec(dm.ts, dm.F, 0), _row_spec(dm.ts, dm.F, 1)],
        out_specs=_row_spec(dm.ts, 2 * dm.F),
        out_shape=jax.ShapeDtypeStruct((dm.S, 2 * dm.F), BF16), compiler_params=_cp(1),
    )(da, gu, gu)


def _tri(n, upper):
    r = lax.broadcasted_iota(jnp.int32, (n, n), 0)
    c = lax.broadcasted_iota(jnp.int32, (n, n), 1)
    return (c >= r if upper else r >= c).astype(F32)


def _gates_fwd(cf, bf, dm):
    ts = dm.ts
    fcol = 2 * dm.C // LANES

    def body(f_ref, b_ref, cum_ref, cumt_ref, carry):
        i = pl.program_id(0)

        @pl.when(i == 0)
        def _():
            carry[...] = jnp.zeros_like(carry)
        z = f_ref[...] + b_ref[...]
        lf = jnp.minimum(z, 0.0) - jnp.log(1.0 + jnp.exp(-jnp.abs(z)))
        cs = jnp.dot(_tri(ts, False), lf, precision=lax.Precision.HIGHEST, preferred_element_type=F32) + carry[...]
        cum_ref[...] = cs
        cumt_ref[...] = cs.T[:8, :]
        carry[...] = cs[ts - 1:ts, :]

    return pl.pallas_call(
        body, name="gates_fwd", grid=(dm.S // ts,),
        in_specs=[pl.BlockSpec((ts, LANES), lambda i: (i, fcol)), _vec_spec(LANES)],
        out_specs=[_row_spec(ts, LANES), pl.BlockSpec((None, 8, ts), lambda i: (i, 0, 0))],
        out_shape=[jax.ShapeDtypeStruct((dm.S, LANES), F32), jax.ShapeDtypeStruct((dm.S // ts, 8, ts), F32)],
        scratch_shapes=[pltpu.VMEM((1, LANES), F32)], compiler_params=_cp(1),
    )(cf, bf)


def _gates_bwd(dcq, dckt, cf, bf, dm):
    ts = dm.ts
    nb = dm.S // ts
    fcol = 2 * dm.C // LANES

    def body(dcq_ref, dck_ref, f_ref, b_ref, df_ref, db_ref, carry):
        i = pl.program_id(0)

        @pl.when(i == 0)
        def _():
            carry[...] = jnp.zeros_like(carry)
            db_ref[...] = jnp.zeros_like(db_ref)
        dck = jnp.concatenate([dck_ref[...], jnp.zeros((LANES - 8, ts), F32)], axis=0).T
        dc = dcq_ref[...] + dck
        dlf = jnp.dot(_tri(ts, True), dc, precision=lax.Precision.HIGHEST, preferred_element_type=F32) + carry[...]
        carry[...] = dlf[0:1, :]
        dz = dlf * (1.0 - _sigmoid(f_ref[...] + b_ref[...]))
        df_ref[...] = dz.astype(df_ref.dtype)
        db_ref[...] += jnp.sum(dz, axis=0, keepdims=True)

    return pl.pallas_call(
        body, name="gates_bwd", grid=(nb,),
        in_specs=[pl.BlockSpec((ts, LANES), lambda i: (nb - 1 - i, 0)),
                  pl.BlockSpec((None, 8, ts), lambda i: (nb - 1 - i, 0, 0)),
                  pl.BlockSpec((ts, LANES), lambda i: (nb - 1 - i, fcol)), _vec_spec(LANES)],
        out_specs=[pl.BlockSpec((ts, LANES), lambda i: (nb - 1 - i, 0)), _vec_spec(LANES)],
        out_shape=[jax.ShapeDtypeStruct((dm.S, LANES), BF16), jax.ShapeDtypeStruct((1, LANES), F32)],
        scratch_shapes=[pltpu.VMEM((1, LANES), F32)], compiler_params=_cp(1),
    )(dcq, dckt, cf, bf)


def _dot_nt(a, b):
    return lax.dot_general(a, b, (((1,), (1,)), ((), ())), preferred_element_type=F32)


def _dot_tn(a, b):
    return lax.dot_general(a, b, (((0,), (0,)), ((), ())), preferred_element_type=F32)


def _attn_fwd(qkv, cum, cumt, dm):
    tq = dm.ts
    A, H = dm.A, dm.H
    scale = HEAD_DIM ** -0.5

    def body(q_ref, kv_ref, cq_ref, ckt_ref, o_ref, lse_ref):
        i = pl.program_id(0)
        row = lax.broadcasted_iota(jnp.int32, (tq, tq), 0)
        col = lax.broadcasted_iota(jnp.int32, (tq, tq), 1)
        lane = lax.broadcasted_iota(jnp.int32, (tq, LANES), 1)
        lse_all = jnp.zeros((tq, LANES), F32)
        for h in range(H):
            lo, hi = h * HEAD_DIM, (h + 1) * HEAD_DIM
            q = q_ref[:, lo:hi]
            cq = cq_ref[:, h:h + 1]

            def step(j, carry, lo=lo, hi=hi, q=q, cq=cq, h=h):
                m, l, acc = carry
                r0 = pl.multiple_of(j * tq, tq)
                k = kv_ref[pl.ds(r0, tq), A + lo:A + hi]
                v = kv_ref[pl.ds(r0, tq), 2 * A + lo:2 * A + hi]
                ck = ckt_ref[j, h:h + 1, :]
                s = _dot_nt(q, k) * scale + cq - ck
                s = jnp.where(row + (i - j) * tq >= col, s, -1e30)
                m_new = jnp.maximum(m, jnp.max(s, axis=1, keepdims=True))
                p = jnp.exp(s - m_new)
                alpha = jnp.exp(m - m_new)
                l = alpha * l + jnp.sum(p, axis=1, keepdims=True)
                acc = alpha * acc + jnp.dot(p.astype(BF16), v, preferred_element_type=F32)
                return m_new, l, acc

            m, l, acc = lax.fori_loop(0, i + 1, step, (jnp.full((tq, 1), -1e30, F32), jnp.zeros((tq, 1), F32),
                                                       jnp.zeros((tq, HEAD_DIM), F32)))
            o_ref[:, lo:hi] = (acc / l).astype(o_ref.dtype)
            lse_all = jnp.where(lane == h, m + jnp.log(l), lse_all)
        lse_ref[...] = lse_all

    nq = dm.S // tq
    return pl.pallas_call(
        body, name="attn_fwd", grid=(nq,),
        in_specs=[pl.BlockSpec((tq, A), lambda i: (i, 0)), pl.BlockSpec((dm.S, 3 * A), lambda i: (0, 0)),
                  _row_spec(tq, LANES), pl.BlockSpec((nq, 8, tq), lambda i: (0, 0, 0))],
        out_specs=[pl.BlockSpec((tq, A), lambda i: (i, 0)), _row_spec(tq, LANES)],
        out_shape=[jax.ShapeDtypeStruct((dm.S, A), BF16), jax.ShapeDtypeStruct((dm.S, LANES), F32)],
        compiler_params=_cp(1),
    )(qkv, qkv, cum, cumt)


def _attn_delta(dcat, o, dm):
    ts = dm.ts

    def body(do_ref, o_ref, dl_ref, dob_ref):
        lane = lax.broadcasted_iota(jnp.int32, (ts, LANES), 1)
        prod = do_ref[...] * o_ref[...].astype(F32)
        dl = jnp.zeros((ts, LANES), F32)
        for h in range(dm.H):
            dl = jnp.where(lane == h, jnp.sum(prod[:, h * HEAD_DIM:(h + 1) * HEAD_DIM], axis=1, keepdims=True), dl)
        dl_ref[...] = dl
        dob_ref[...] = do_ref[...].astype(dob_ref.dtype)

    return pl.pallas_call(
        body, name="attn_delta", grid=(dm.S // ts,),
        in_specs=[_row_spec(ts, dm.A, 0), _row_spec(ts, dm.A)],
        out_specs=[_row_spec(ts, LANES), _row_spec(ts, dm.A)],
        out_shape=[jax.ShapeDtypeStruct((dm.S, LANES), F32), jax.ShapeDtypeStruct((dm.S, dm.A), BF16)],
        compiler_params=_cp(1),
    )(dcat, o)


def _attn_bwd(qkv, do, lse, delta, cum, cumt, dm):
    tq = dm.ts
    A, H = dm.A, dm.H
    nq = dm.S // tq
    scale = HEAD_DIM ** -0.5

    def body(kv_ref, q_ref, do_ref, lse_ref, dl_ref, cq_ref, ckt_ref, dq_ref, dkv_ref, dcq_ref, dckt_ref):
        j = pl.program_id(0)

        @pl.when(j == 0)
        def _():
            dq_ref[...] = jnp.zeros_like(dq_ref)
            dcq_ref[...] = jnp.zeros_like(dcq_ref)
        row = lax.broadcasted_iota(jnp.int32, (tq, tq), 0)
        col = lax.broadcasted_iota(jnp.int32, (tq, tq), 1)
        lane = lax.broadcasted_iota(jnp.int32, (tq, LANES), 1)
        sub = lax.broadcasted_iota(jnp.int32, (8, tq), 0)
        dck_all = jnp.zeros((8, tq), F32)
        for h in range(H):
            lo, hi = h * HEAD_DIM, (h + 1) * HEAD_DIM
            k = kv_ref[:, A + lo:A + hi]
            v = kv_ref[:, 2 * A + lo:2 * A + hi]
            ck = ckt_ref[h:h + 1, :]

            def step(i, carry, lo=lo, hi=hi, k=k, v=v, ck=ck, h=h):
                dk, dv, dck = carry
                r0 = pl.multiple_of(i * tq, tq)
                rows = pl.ds(r0, tq)
                q = q_ref[rows, lo:hi]
                do_ = do_ref[rows, lo:hi]
                s = _dot_nt(q, k) * scale + cq_ref[rows, h:h + 1] - ck
                s = jnp.where(row + (i - j) * tq >= col, s, -1e30)
                p = jnp.exp(s - lse_ref[rows, h:h + 1])
                ds = p * (_dot_nt(do_, v) - dl_ref[rows, h:h + 1])
                dsb = ds.astype(BF16)
                dv = dv + _dot_tn(p.astype(BF16), do_)
                dk = dk + _dot_tn(dsb, q)
                dq_ref[rows, lo:hi] += jnp.dot(dsb, k, preferred_element_type=F32)
                dcq_ref[rows, :] += jnp.where(lane == h, jnp.sum(ds, axis=1, keepdims=True), 0.0)
                dck = dck - jnp.sum(ds, axis=0, keepdims=True)
                return dk, dv, dck

            dk, dv, dck = lax.fori_loop(j, nq, step, (jnp.zeros((tq, HEAD_DIM), F32), jnp.zeros((tq, HEAD_DIM), F32),
                                                      jnp.zeros((1, tq), F32)))
            dkv_ref[:, lo:hi] = (dk * scale).astype(dkv_ref.dtype)
            dkv_ref[:, A + lo:A + hi] = dv.astype(dkv_ref.dtype)
            dck_all = jnp.where(sub == h, dck, dck_all)
        dckt_ref[...] = dck_all

        @pl.when(j == nq - 1)
        def _():
            dq_ref[...] = dq_ref[...] * scale

    full = lambda w: pl.BlockSpec((dm.S, w), lambda j: (0, 0))
    return pl.pallas_call(
        body, name="attn_bwd", grid=(nq,),
        in_specs=[pl.BlockSpec((tq, 3 * A), lambda j: (j, 0)), full(A), full(A), full(LANES), full(LANES), full(LANES),
                  pl.BlockSpec((None, 8, tq), lambda j: (j, 0, 0))],
        out_specs=[full(A), pl.BlockSpec((tq, 2 * A), lambda j: (j, 0)), full(LANES),
                   pl.BlockSpec((None, 8, tq), lambda j: (j, 0, 0))],
        out_shape=[jax.ShapeDtypeStruct((dm.S, A), F32), jax.ShapeDtypeStruct((dm.S, 2 * A), BF16),
                   jax.ShapeDtypeStruct((dm.S, LANES), F32), jax.ShapeDtypeStruct((nq, 8, tq), F32)],
        compiler_params=_cp(1),
    )(qkv, qkv, do, lse, delta, cum, cumt)


def _glu(cf_rows, c):
    return cf_rows[:, :c] * _sigmoid(cf_rows[:, c:2 * c])


def _conv_fwd(cf, cw, cb, lg, lb, dm):
    ts, C = dm.ts, dm.C
    per = ts // HALO

    def body(cf_ref, halo_ref, w_ref, cb_ref, lg_ref, lb_ref, u3_ref, u1_ref):
        i = pl.program_id(0)
        prev = jnp.where(i > 0, _glu(halo_ref[...], C), 0.0)
        win = jnp.concatenate([prev, _glu(cf_ref[...], C)], axis=0)
        u1 = jnp.zeros((ts, C), F32) + cb_ref[...]
        off = HALO - (CONV_K - 1)
        for k in range(CONV_K):
            u1 = u1 + w_ref[k:k + 1, :] * win[off + k:off + k + ts, :]
        u1_ref[...] = u1
        mu = jnp.mean(u1, axis=-1, keepdims=True)
        cen = u1 - mu
        rstd = lax.rsqrt(jnp.mean(cen * cen, axis=-1, keepdims=True) + EPS)
        u2 = cen * rstd * lg_ref[...] + lb_ref[...]
        u3_ref[...] = (u2 * _sigmoid(u2)).astype(u3_ref.dtype)

    return pl.pallas_call(
        body, name="conv_fwd", grid=(dm.S // ts,),
        in_specs=[pl.BlockSpec((ts, 2 * C), lambda i: (i, 0)),
                  pl.BlockSpec((HALO, 2 * C), lambda i: (jnp.maximum(i * per - 1, 0), 0)),
                  pl.BlockSpec((HALO, C), lambda i: (0, 0))] + [_vec_spec(C)] * 3,
        out_specs=[_row_spec(ts, C)] * 2,
        out_shape=[jax.ShapeDtypeStruct((dm.S, C), BF16), jax.ShapeDtypeStruct((dm.S, C), F32)],
        compiler_params=_cp(1),
    )(cf, cf, cw, cb, lg, lb)


def _conv_bwd(dcat, u1, cf, cw, lg, lb, dm):
    ts, C = dm.ts, dm.C
    per = ts // HALO
    nt = dm.S // ts
    last_halo = dm.S // HALO - 1

    def ln_bwd(du3, u1v, lg_v, lb_v):
        mu = jnp.mean(u1v, axis=-1, keepdims=True)
        cen = u1v - mu
        rstd = lax.rsqrt(jnp.mean(cen * cen, axis=-1, keepdims=True) + EPS)
        uhat = cen * rstd
        u2 = uhat * lg_v + lb_v
        sg = _sigmoid(u2)
        du2 = du3 * (sg * (1.0 + u2 * (1.0 - sg)))
        duh = du2 * lg_v
        du1 = rstd * (duh - jnp.mean(duh, axis=-1, keepdims=True) - uhat * jnp.mean(duh * uhat, axis=-1, keepdims=True))
        return du1, du2, uhat

    def body(d_ref, dn_ref, u1_ref, u1n_ref, cf_ref, halo_ref, w_ref, lg_ref, lb_ref,
             dcf_ref, dw_ref, dcb_ref, dlg_ref, dlb_ref):
        i = pl.program_id(0)

        @pl.when(i == 0)
        def _():
            dw_ref[...] = jnp.zeros_like(dw_ref)
            dcb_ref[...] = jnp.zeros_like(dcb_ref)
            dlg_ref[...] = jnp.zeros_like(dlg_ref)
            dlb_ref[...] = jnp.zeros_like(dlb_ref)
        lg_v, lb_v = lg_ref[...], lb_ref[...]
        du1, du2, uhat = ln_bwd(d_ref[...], u1_ref[...], lg_v, lb_v)
        du1n, _, _ = ln_bwd(dn_ref[...], u1n_ref[...], lg_v, lb_v)
        du1n = jnp.where(i < nt - 1, du1n, 0.0)
        dlg_ref[...] += jnp.sum(du2 * uhat, axis=0, keepdims=True)
        dlb_ref[...] += jnp.sum(du2, axis=0, keepdims=True)
        dcb_ref[...] += jnp.sum(du1, axis=0, keepdims=True)
        dwin = jnp.concatenate([du1, du1n], axis=0)
        cfv = cf_ref[...]
        cv, sg = cfv[:, :C], _sigmoid(cfv[:, C:2 * C])
        prev = jnp.where(i > 0, _glu(halo_ref[...], C), 0.0)
        uwin = jnp.concatenate([prev, cv * sg], axis=0)
        du0 = jnp.zeros((ts, C), F32)
        off = HALO - (CONV_K - 1)
        for k in range(CONV_K):
            back = CONV_K - 1 - k
            du0 = du0 + w_ref[k:k + 1, :] * dwin[back:back + ts, :]
            dw_ref[k:k + 1, :] += jnp.sum(du1 * uwin[off + k:off + k + ts, :], axis=0, keepdims=True)
        dcf_ref[:, :C] = (du0 * sg).astype(dcf_ref.dtype)
        dcf_ref[:, C:] = (du0 * cv * sg * (1.0 - sg)).astype(dcf_ref.dtype)

    ucol = dm.A // C
    return pl.pallas_call(
        body, name="conv_bwd", grid=(nt,),
        in_specs=[pl.BlockSpec((ts, C), lambda i: (i, ucol)),
                  pl.BlockSpec((HALO, C), lambda i: (jnp.minimum((i + 1) * per, last_halo), ucol)),
                  pl.BlockSpec((ts, C), lambda i: (i, 0)),
                  pl.BlockSpec((HALO, C), lambda i: (jnp.minimum((i + 1) * per, last_halo), 0)),
                  pl.BlockSpec((ts, 2 * C), lambda i: (i, 0)),
                  pl.BlockSpec((HALO, 2 * C), lambda i: (jnp.maximum(i * per - 1, 0), 0)),
                  pl.BlockSpec((HALO, C), lambda i: (0, 0)), _vec_spec(C), _vec_spec(C)],
        out_specs=[_row_spec(ts, 2 * C), pl.BlockSpec((HALO, C), lambda i: (0, 0))] + [_vec_spec(C)] * 3,
        out_shape=[jax.ShapeDtypeStruct((dm.S, 2 * C), BF16), jax.ShapeDtypeStruct((HALO, C), F32)]
        + [jax.ShapeDtypeStruct((1, C), F32)] * 3,
        compiler_params=_cp(1),
    )(dcat, dcat, u1, u1, cf, cf, cw, lg, lb)


def _ada_fwd(c16, ada_w, ada_b_cols, dm):
    L, D, n = ada_w.shape
    tn = _tile(n, 512)

    def body(c_ref, w_ref, b_ref, o_ref, a_ref):
        cv = c_ref[...]
        act = (cv * _sigmoid(cv)).astype(BF16)
        a_ref[...] = act
        o_ref[...] = jnp.dot(act, w_ref[...].astype(BF16), preferred_element_type=F32) + b_ref[...]

    return pl.pallas_call(
        body, name="ada_fwd", grid=(L, n // tn),
        in_specs=[pl.BlockSpec((16, D), lambda l, j: (0, 0)), pl.BlockSpec((None, D, tn), lambda l, j: (l, 0, j)),
                  pl.BlockSpec((None, 1, tn), lambda l, j: (l, 0, j))],
        out_specs=[pl.BlockSpec((None, 16, tn), lambda l, j: (l, 0, j)), pl.BlockSpec((16, D), lambda l, j: (0, 0))],
        out_shape=[jax.ShapeDtypeStruct((L, 16, n), F32), jax.ShapeDtypeStruct((16, D), BF16)],
        compiler_params=_cp(2),
    )(c16, ada_w, ada_b_cols)


def _ada_bwd(act16, dmod16):
    L, _, n = dmod16.shape
    D = act16.shape[1]
    tm = min(TN_TM, D)

    def body(a_ref, d_ref, o_ref):
        o_ref[...] = _dot_tn(a_ref[...], d_ref[...])

    return pl.pallas_call(
        body, name="ada_bwd", grid=(L, D // tm),
        in_specs=[pl.BlockSpec((16, tm), lambda l, i: (0, i)), pl.BlockSpec((None, 16, n), lambda l, i: (l, 0, 0))],
        out_specs=pl.BlockSpec((None, tm, n), lambda l, i: (l, i, 0)),
        out_shape=jax.ShapeDtypeStruct((L, D, n), F32), compiler_params=_cp(2),
    )(act16, dmod16)


def _sum_devices(g8):
    _, R, _ = g8.shape
    tr = _rows_tile(R)

    def body(g_ref, o_ref):
        acc = g_ref[0]
        for d in range(1, N_DEV):
            acc = acc + g_ref[d]
        o_ref[...] = acc

    return pl.pallas_call(
        body, name="sum_devices", grid=(R // tr,),
        in_specs=[pl.BlockSpec((N_DEV, tr, LANES), lambda i: (0, i, 0))],
        out_specs=pl.BlockSpec((tr, LANES), lambda i: (i, 0)),
        out_shape=jax.ShapeDtypeStruct((R, LANES), F32), compiler_params=_cp(1),
    )(g8)


def _rows_tile(r, cap=512):
    for t in (512, 256, 128, 64, 32, 16, 8):
        if t <= cap and r % t == 0:
            return t
    return r


def _adam_math(w, g, m, v):
    m = ADAM_B1 * m + (1.0 - ADAM_B1) * g
    v = ADAM_B2 * v + (1.0 - ADAM_B2) * (g * g)
    m_hat = m / (1.0 - ADAM_B1 ** ADAM_STEP)
    v_hat = v / (1.0 - ADAM_B2 ** ADAM_STEP)
    delta = -ADAM_LR * (m_hat / (jnp.sqrt(v_hat) + ADAM_EPS) + ADAM_WD * w)
    return delta, m, v


def _adamw_dense(w, m, v, g, name):
    R, Cc = w.shape
    tr = _rows_tile(R, 128)

    def body(w_ref, m_ref, v_ref, g_ref, d_ref, mo_ref, vo_ref):
        d, mn, vn = _adam_math(w_ref[...], g_ref[...], m_ref[...], v_ref[...])
        d_ref[...] = d
        mo_ref[...] = mn
        vo_ref[...] = vn

    spec = pl.BlockSpec((tr, Cc), lambda i: (i, 0))
    return pl.pallas_call(
        body, name=name, grid=(R // tr,), in_specs=[spec] * 4, out_specs=[spec] * 3,
        out_shape=[jax.ShapeDtypeStruct((R, Cc), F32)] * 3, compiler_params=_cp(1),
    )(w, m, v, g)


def _adamw_shard(w, m, v, near, far, layer, prev, name):
    L, r, cc = w.shape
    tr = _rows_tile(r, 128)

    def body(w_ref, m_ref, v_ref, n_ref, f_ref, *rest):
        g_ref, d_ref, mo_ref, vo_ref = rest[-4:]
        g = n_ref[0].astype(F32) + f_ref[0].astype(F32)
        for k in range(1, N_CHIPS):
            g = g + (n_ref[k].astype(F32) + f_ref[k].astype(F32))
        d, mn, vn = _adam_math(w_ref[...], g, m_ref[...], v_ref[...])
        g_ref[...] = g
        d_ref[...] = d
        mo_ref[...] = mn
        vo_ref[...] = vn

    wspec = pl.BlockSpec((None, tr, cc), lambda i: (layer, i, 0))
    sspec = pl.BlockSpec((N_CHIPS, tr, cc), lambda i: (0, i, 0))
    n_prev = 0 if prev is None else 4
    return pl.pallas_call(
        body, name=name, grid=(r // tr,),
        in_specs=[wspec] * 3 + [sspec] * 2 + [pl.BlockSpec(memory_space=pl.ANY)] * n_prev,
        out_specs=[wspec] * 4,
        out_shape=[jax.ShapeDtypeStruct((L, r, cc), F32)] * 4,
        input_output_aliases={5 + t: t for t in range(n_prev)},
        compiler_params=_cp(1),
    )(w, m, v, near, far, *(prev or ()))


def _pack(vs):
    flat = jnp.concatenate([v.reshape(-1).astype(F32) for v in vs])
    pad = (-flat.shape[0]) % (8 * LANES)
    return jnp.pad(flat, (0, pad)).reshape(-1, LANES)


def _unpack(packed, shapes):
    flat = packed.reshape(-1)
    out, pos = [], 0
    for s in shapes:
        n = 1
        for d in s:
            n *= d
        out.append(flat[pos:pos + n].reshape(s))
        pos += n
    return out


def kernel(x, c, w_in, b_f, conv_w, conv_b, conv_ln_g, conv_ln_b, w_o, w_ffn_in, w_ffn_out, mix_pre_g, mix_post_g, ffn_pre_g, ffn_post_g, ada_w, ada_b, loss_target, m_w_in, m_b_f, m_conv_w, m_conv_b, m_conv_ln_g, m_conv_ln_b, m_w_o, m_w_ffn_in, m_w_ffn_out, m_mix_pre_g, m_mix_post_g, m_ffn_pre_g, m_ffn_post_g, m_ada_w, m_ada_b, v_w_in, v_b_f, v_conv_w, v_conv_b, v_conv_ln_g, v_conv_ln_b, v_w_o, v_w_ffn_in, v_w_ffn_out, v_mix_pre_g, v_mix_post_g, v_ffn_pre_g, v_ffn_post_g, v_ada_w, v_ada_b):
    params = dict(w_in=w_in, b_f=b_f, conv_w=conv_w, conv_b=conv_b, conv_ln_g=conv_ln_g, conv_ln_b=conv_ln_b, w_o=w_o,
                  w_ffn_in=w_ffn_in, w_ffn_out=w_ffn_out, mix_pre_g=mix_pre_g, mix_post_g=mix_post_g,
                  ffn_pre_g=ffn_pre_g, ffn_post_g=ffn_post_g, ada_w=ada_w, ada_b=ada_b)
    mom = dict(w_in=m_w_in, b_f=m_b_f, conv_w=m_conv_w, conv_b=m_conv_b, conv_ln_g=m_conv_ln_g, conv_ln_b=m_conv_ln_b,
               w_o=m_w_o, w_ffn_in=m_w_ffn_in, w_ffn_out=m_w_ffn_out, mix_pre_g=m_mix_pre_g, mix_post_g=m_mix_post_g,
               ffn_pre_g=m_ffn_pre_g, ffn_post_g=m_ffn_post_g, ada_w=m_ada_w, ada_b=m_ada_b)
    var = dict(w_in=v_w_in, b_f=v_b_f, conv_w=v_conv_w, conv_b=v_conv_b, conv_ln_g=v_conv_ln_g, conv_ln_b=v_conv_ln_b,
               w_o=v_w_o, w_ffn_in=v_w_ffn_in, w_ffn_out=v_w_ffn_out, mix_pre_g=v_mix_pre_g, mix_post_g=v_mix_post_g,
               ffn_pre_g=v_ffn_pre_g, ffn_post_g=v_ffn_post_g, ada_w=v_ada_w, ada_b=v_ada_b)

    S, D = x.shape[1], x.shape[2]
    L = w_in.shape[0]
    A = D // 2
    C = D - A
    H = A // HEAD_DIM
    F = w_ffn_out.shape[1] * N_CHIPS
    d_in = w_in.shape[2] * N_CHIPS
    NP = 3 * A + 2 * C + LANES
    dm = Dims(S=S, D=D, A=A, C=C, H=H, F=F, L=L, NP=NP, ts=min(ROW_TILE, S))
    assert H <= 8 and A == C and d_in == 3 * A + H + 2 * C

    ix, iy, ic = _place()
    chip = 2 * ix + iy
    dev = 4 * ix + 2 * iy + ic
    x2 = x.reshape(S, D)
    tgt = loss_target.reshape(S, D)

    c_all = _all_gather_devices(c.reshape(D // LANES, LANES), "gather_c").reshape(N_DEV, D)
    c16 = jnp.pad(c_all, ((0, 16 - N_DEV), (0, 0)))
    n_ada = ada_w.shape[2]
    ada_b_cols = lax.dynamic_slice_in_dim(ada_b, chip * n_ada, n_ada, axis=1).reshape(L, 1, n_ada)
    mod_cols, act16 = _ada_fwd(c16, ada_w, ada_b_cols, dm)
    conv_w_all, mod_all = _all_gather_chips([conv_w.reshape(L * CONV_K, -1), mod_cols.reshape(L * 16, n_ada)], "gather_mod")
    cwc = conv_w.shape[2]
    conv_w_full = conv_w_all.reshape(N_CHIPS, L, CONV_K, cwc).transpose(1, 2, 0, 3).reshape(L, CONV_K, C)
    conv_w_full = jnp.pad(conv_w_full, ((0, 0), (0, HALO - CONV_K), (0, 0)))
    mod_all = mod_all.reshape(N_CHIPS, L, 16, n_ada)
    mod_me = lax.dynamic_index_in_dim(mod_all, dev, axis=2, keepdims=False)
    mod_me = mod_me.transpose(1, 0, 2).reshape(L, N_MOD, 1, D)

    gathered = []
    for l in range(L):
        g_in, g_o, g_fi, g_fo = _all_gather_chips(
            [w_in[l].astype(BF16), w_o[l].astype(BF16), w_ffn_in[l].astype(BF16), w_ffn_out[l].astype(BF16)], "gather_w")
        w_nat = g_in.transpose(1, 0, 2).reshape(D, d_in)
        w_p = jnp.concatenate([w_nat[:, :3 * A], w_nat[:, 3 * A + H:], w_nat[:, 3 * A:3 * A + H],
                               jnp.zeros((D, LANES - H), BF16)], axis=1)
        gathered.append((w_p, g_o.reshape(D, D), g_fi, g_fo.reshape(F, D)))

    vec = lambda p, l: p[l].reshape(1, -1)
    bf_pad = jnp.pad(b_f, ((0, 0), (0, LANES - H)))

    saved = []
    xin = x2
    h = _pre_norm(xin, vec(mix_pre_g, 0), mod_me[0, 1], mod_me[0, 0], dm)
    dx = loss_part = None
    for l in range(L):
        w_p, wo, wfi, wfo = gathered[l]
        qkv = _mm_nn(h, w_p, BF16, "mm_qkv", 0, 3 * A)
        cf = _mm_nn(h, w_p, F32, "mm_cf", 3 * A, 2 * C + LANES)
        cum, cumt = _gates_fwd(cf, vec(bf_pad, l), dm)
        o, lse = _attn_fwd(qkv, cum, cumt, dm)
        u3, u1 = _conv_fwd(cf, conv_w_full[l], vec(conv_b, l), vec(conv_ln_g, l), vec(conv_ln_b, l), dm)
        cat = jnp.concatenate([o, u3], axis=1)
        y = _mm_nn(cat, wo, F32, "mm_o")
        x1, h2 = _res_norm(xin, y, vec(mix_post_g, l), mod_me[l, 2], vec(ffn_pre_g, l), mod_me[l, 4], mod_me[l, 3], dm)
        gu = _mm_nn_blocked(h2, wfi, F32, "mm_ffn_in")
        a = _swiglu_fwd(gu, dm)
        y2 = _mm_nn(a, wfo, F32, "mm_ffn_out")
        saved.append(dict(xin=xin, h=h, qkv=qkv, cf=cf, cum=cum, cumt=cumt, o=o, lse=lse, u1=u1, cat=cat, y=y,
                          x1=x1, h2=h2, gu=gu, a=a, y2=y2))
        if l + 1 < L:
            xin, h = _res_norm(x1, y2, vec(ffn_post_g, l), mod_me[l, 5], vec(mix_pre_g, l + 1),
                               mod_me[l + 1, 1], mod_me[l + 1, 0], dm)
        else:
            dx, loss_part = _res_loss(x1, y2, vec(ffn_post_g, l), mod_me[l, 5], tgt, dm)
    loss = lax.psum(loss_part[0, 0], ("x", "y", "c"))

    small = [None] * L
    big = [None] * L
    for l in reversed(range(L)):
        w_p, wo, wfi, wfo = gathered[l]
        sv = saved[l]
        dy2, d_gfpost, d_g2 = _post_bwd(dx, sv["y2"], vec(ffn_post_g, l), mod_me[l, 5], dm)
        da = _mm_nt(dy2, wfo, F32, "mm_da")
        dgu = _swiglu_bwd(da, sv["gu"], dm)
        g_wfo = _mm_tn(sv["a"], dy2, BF16, "mm_dwfo")
        g_wfi = _mm_tn(sv["h2"], dgu, BF16, "mm_dwfi", blocked=True)
        dh2 = _mm_nt_blocked(dgu, wfi, F32, "mm_dh2")
        dx1, d_sh2, d_sc2, d_gfpre = _pre_bwd(dh2, dx, sv["x1"], vec(ffn_pre_g, l), mod_me[l, 4], dm)
        dy, d_gpost, d_g1 = _post_bwd(dx1, sv["y"], vec(mix_post_g, l), mod_me[l, 2], dm)
        dcat = _mm_nt(dy, wo, F32, "mm_dcat")
        g_wo = _mm_tn(sv["cat"], dy, BF16, "mm_dwo")
        dcfc, d_cw, d_cb, d_lg, d_lb = _conv_bwd(dcat, sv["u1"], sv["cf"], conv_w_full[l], vec(conv_ln_g, l),
                                                 vec(conv_ln_b, l), dm)
        delta, do = _attn_delta(dcat, sv["o"], dm)
        dq, dkv, dcq, dckt = _attn_bwd(sv["qkv"], do, sv["lse"], delta, sv["cum"], sv["cumt"], dm)
        dfl, d_bf = _gates_bwd(dcq, dckt, sv["cf"], vec(bf_pad, l), dm)
        dproj = jnp.concatenate([dq.astype(BF16), dkv, dcfc, dfl], axis=1)
        dh = _mm_nt(dproj, w_p, F32, "mm_dh")
        g_wp = _mm_tn(sv["h"], dproj, BF16, "mm_dwp")
        dx, d_sh1, d_sc1, d_gpre = _pre_bwd(dh, dx1, sv["xin"], vec(mix_pre_g, l), mod_me[l, 1], dm)
        g_nat = jnp.concatenate([g_wp[:, :3 * A], g_wp[:, 3 * A + 2 * C:3 * A + 2 * C + H], g_wp[:, 3 * A:3 * A + 2 * C]], axis=1)
        g_win = g_nat.reshape(D, N_CHIPS, d_in // N_CHIPS).transpose(1, 0, 2)
        big[l] = (g_win, g_wo.reshape(N_CHIPS, D // N_CHIPS, D), g_wfi, g_wfo.reshape(N_CHIPS, F // N_CHIPS, D))
        small[l] = dict(b_f=d_bf[0, :H], conv_b=d_cb[0], conv_ln_g=d_lg[0], conv_ln_b=d_lb[0], mix_pre_g=d_gpre[0],
                        mix_post_g=d_gpost[0], ffn_pre_g=d_gfpre[0], ffn_post_g=d_gfpost[0],
                        dmod=jnp.concatenate([d_sh1, d_sc1, d_g1, d_sh2, d_sc2, d_g2], axis=1)[0],
                        conv_w=d_cw[:CONV_K])
    grad_x = dx.reshape(1, S, D)

    names_big = ["w_in", "w_o", "w_ffn_in", "w_ffn_out"]
    res_big = {n: None for n in names_big}
    for l in range(L):
        near, far = _reduce_scatter_chips(list(big[l]), "scatter_grads")
        for t, n in enumerate(names_big):
            res_big[n] = _adamw_shard(params[n], mom[n], var[n], near[t], far[t], l, res_big[n], f"adamw_{n}_{l}")

    keys_small = ["b_f", "conv_b", "conv_ln_g", "conv_ln_b", "mix_pre_g", "mix_post_g", "ffn_pre_g", "ffn_post_g",
                  "dmod", "conv_w"]
    stacked = [jnp.stack([small[l][k] for l in range(L)]) for k in keys_small]
    shapes = [s.shape for s in stacked]
    pack = _pack(stacked)
    pack8 = _all_gather_devices(pack, "gather_small")
    summed = dict(zip(keys_small, _unpack(_sum_devices(pack8), shapes)))
    dmod_all = jnp.stack([_unpack(pack8[d], shapes)[keys_small.index("dmod")] for d in range(N_DEV)])
    grads = {k: summed[k] for k in keys_small[:8]}
    grads["ada_b"] = summed["dmod"]
    grads["conv_w"] = lax.dynamic_slice_in_dim(summed["conv_w"], chip * cwc, cwc, axis=2)

    dmod_cols = lax.dynamic_slice_in_dim(dmod_all.reshape(N_DEV, L, N_CHIPS, n_ada), chip, 1, axis=2)
    dmod16 = jnp.pad(dmod_cols.reshape(N_DEV, L, n_ada).transpose(1, 0, 2), ((0, 0), (0, 16 - N_DEV), (0, 0))).astype(BF16)
    grads["ada_w"] = _ada_bwd(act16, dmod16)

    d_aw, m_aw, v_aw = _adamw_dense(ada_w.reshape(L * D, n_ada), m_ada_w.reshape(L * D, n_ada),
                                    v_ada_w.reshape(L * D, n_ada), grads["ada_w"].reshape(L * D, n_ada), "adamw_ada_w")
    names_small = ["b_f", "conv_w", "conv_b", "conv_ln_g", "conv_ln_b", "mix_pre_g", "mix_post_g", "ffn_pre_g",
                   "ffn_post_g", "ada_b"]
    shapes_small = [params[n].shape for n in names_small]
    d_s, m_s, v_s = _adamw_dense(_pack([params[n] for n in names_small]), _pack([mom[n] for n in names_small]),
                                 _pack([var[n] for n in names_small]), _pack([grads[n] for n in names_small]),
                                 "adamw_small")
    delta_w = dict(zip(names_small, _unpack(d_s, shapes_small)))
    new_m = dict(zip(names_small, _unpack(m_s, shapes_small)))
    new_v = dict(zip(names_small, _unpack(v_s, shapes_small)))
    delta_w["ada_w"], new_m["ada_w"], new_v["ada_w"] = (t.reshape(L, D, n_ada) for t in (d_aw, m_aw, v_aw))
    for n in names_big:
        grads[n], delta_w[n], new_m[n], new_v[n] = res_big[n]

    return (loss, grad_x, *[grads[n] for n in WEIGHTS], *[delta_w[n] for n in WEIGHTS],
            *[new_m[n] for n in WEIGHTS], *[new_v[n] for n in WEIGHTS])
```

```python
import collections
import functools

import jax
import jax.numpy as jnp
from jax import lax
from jax.experimental import pallas as pl
from jax.experimental.pallas import tpu as pltpu

F32 = jnp.float32
BF16 = jnp.bfloat16
MESH = pl.DeviceIdType.MESH

HEAD_DIM = 64
CONV_K = 31
N_MOD = 6
EPS = 1e-6
N_CHIPS = 4
N_DEV = 8
LANES = 128
HALO = 32
ROW_TILE = 256
MM_TM = 512
MM_TN_MAX = 1408
TN_TM = 256
VMEM_LIMIT = 56 * 1024 * 1024

ADAM_LR = 0.001
ADAM_B1 = 0.9
ADAM_B2 = 0.999
ADAM_EPS = 1e-08
ADAM_WD = 0.01
ADAM_STEP = 10

WEIGHTS = ['w_in', 'b_f', 'conv_w', 'conv_b', 'conv_ln_g', 'conv_ln_b', 'w_o', 'w_ffn_in', 'w_ffn_out',
           'mix_pre_g', 'mix_post_g', 'ffn_pre_g', 'ffn_post_g', 'ada_w', 'ada_b']

Dims = collections.namedtuple("Dims", "S D A C H F L NP ts")


def _cp(n_grid=0):
    if n_grid:
        return pltpu.CompilerParams(dimension_semantics=("arbitrary",) * n_grid, vmem_limit_bytes=VMEM_LIMIT)
    return pltpu.CompilerParams(vmem_limit_bytes=VMEM_LIMIT)


def _tile(n, cap, also=None):
    best = None
    t = LANES
    while t <= min(n, cap):
        if n % t == 0 and (also is None or also % t == 0):
            best = t
        t += LANES
    assert best is not None, (n, cap, also)
    return best


def _bf(v):
    return v if v.dtype == BF16 else v.astype(BF16)


def _place():
    return lax.axis_index("x"), lax.axis_index("y"), lax.axis_index("c")


def _flip(v, d):
    return 1 - v if d else v


def _all_gather_devices(a, name):
    def body(a_ref, o_ref, send, recv, lsem):
        x, y, c = _place()
        me = 4 * x + 2 * y + c
        local = pltpu.make_async_copy(a_ref, o_ref.at[me], lsem)
        local.start()
        copies = []
        for k in range(1, N_DEV):
            peer = (_flip(x, (k >> 2) & 1), _flip(y, (k >> 1) & 1), _flip(c, k & 1))
            cp = pltpu.make_async_remote_copy(src_ref=a_ref, dst_ref=o_ref.at[me], send_sem=send.at[k - 1],
                                              recv_sem=recv.at[k - 1], device_id=peer, device_id_type=MESH)
            cp.start()
            copies.append(cp)
        for cp in copies:
            cp.wait()
        local.wait()

    return pl.pallas_call(
        body, name=name,
        out_shape=jax.ShapeDtypeStruct((N_DEV,) + a.shape, a.dtype),
        in_specs=[pl.BlockSpec(memory_space=pl.ANY)],
        out_specs=pl.BlockSpec(memory_space=pl.ANY),
        scratch_shapes=[pltpu.SemaphoreType.DMA((N_DEV - 1,)), pltpu.SemaphoreType.DMA((N_DEV - 1,)),
                        pltpu.SemaphoreType.DMA],
    )(a)


def _all_gather_chips(arrays, name):
    n = len(arrays)

    def body(*refs):
        a_refs, o_refs = refs[:n], refs[n:2 * n]
        send, recv, lsem = refs[2 * n:]
        x, y, c = _place()
        me = 2 * x + y
        copies = []
        for i in range(n):
            local = pltpu.make_async_copy(a_refs[i], o_refs[i].at[me], lsem.at[i])
            local.start()
            copies.append(local)
            for k in range(1, N_CHIPS):
                peer = (_flip(x, (k >> 1) & 1), _flip(y, k & 1), c)
                cp = pltpu.make_async_remote_copy(src_ref=a_refs[i], dst_ref=o_refs[i].at[me],
                                                  send_sem=send.at[i, k - 1], recv_sem=recv.at[i, k - 1],
                                                  device_id=peer, device_id_type=MESH)
                cp.start()
                copies.append(cp)
        for cp in copies:
            cp.wait()

    return pl.pallas_call(
        body, name=name,
        out_shape=[jax.ShapeDtypeStruct((N_CHIPS,) + a.shape, a.dtype) for a in arrays],
        in_specs=[pl.BlockSpec(memory_space=pl.ANY)] * n,
        out_specs=[pl.BlockSpec(memory_space=pl.ANY)] * n,
        scratch_shapes=[pltpu.SemaphoreType.DMA((n, N_CHIPS - 1)), pltpu.SemaphoreType.DMA((n, N_CHIPS - 1)),
                        pltpu.SemaphoreType.DMA((n,))],
    )(*arrays)


_HBM = pl.BlockSpec(memory_space=pltpu.HBM)
_SEM = pl.BlockSpec(memory_space=pltpu.SEMAPHORE)
_EFFECT = pltpu.SideEffectType.DATAFLOW_SIDE_EFFECTING


def _chip_copies(srcs, lands, send, recv, scatter):
    x, y, c = _place()
    me = 2 * x + y
    copies = []
    for i in range(len(srcs)):
        for k in range(1, N_CHIPS):
            px, py = _flip(x, (k >> 1) & 1), _flip(y, k & 1)
            src = srcs[i].at[2 * px + py] if scatter else srcs[i]
            s = i * (N_CHIPS - 1) + k - 1
            copies.append(pltpu.make_async_remote_copy(src_ref=src, dst_ref=lands[i].at[me], send_sem=send.at[s],
                                                       recv_sem=recv.at[s], device_id=(px, py, c), device_id_type=MESH))
    return copies


def _exchange_start(arrays, scatter, after, name):
    n = len(arrays)

    def body(*refs):
        srcs, lands = refs[:n], refs[n:2 * n]
        send, recv = refs[2 * n + 1], refs[2 * n + 2]
        token = refs[-1]
        for cp in _chip_copies(srcs, lands, send, recv, scatter):
            cp.start()
        token[...] = jnp.zeros_like(token)

    land_shapes = [a.shape if scatter else (N_CHIPS,) + a.shape for a in arrays]
    outs = pl.pallas_call(
        body, name=name,
        out_shape=(pltpu.SemaphoreType.DMA((n * (N_CHIPS - 1),)), pltpu.SemaphoreType.DMA((n * (N_CHIPS - 1),)),
                   *[pltpu.HBM(a.shape, a.dtype) for a in arrays],
                   *[pltpu.HBM(s, a.dtype) for s, a in zip(land_shapes, arrays)],
                   jax.ShapeDtypeStruct((8, LANES), F32)),
        in_specs=[_HBM] * (2 * n) + [pl.BlockSpec(memory_space=pl.ANY)],
        out_specs=(_SEM, _SEM, *[_HBM] * (2 * n), pl.BlockSpec(memory_space=pltpu.VMEM)),
        input_output_aliases={i: 2 + i for i in range(2 * n)},
        compiler_params=pltpu.CompilerParams(has_side_effects=_EFFECT),
    )(*[pltpu.with_memory_space_constraint(a, pltpu.HBM) for a in arrays],
      *[pltpu.with_memory_space_constraint(lax.empty(s, a.dtype), pltpu.HBM) for s, a in zip(land_shapes, arrays)],
      after)
    return dict(send=outs[0], recv=outs[1], srcs=outs[2:2 + n], lands=outs[2 + n:2 + 2 * n], token=outs[-1],
                scatter=scatter)


def _exchange_wait(st, after, name):
    n = len(st["srcs"])
    scatter = st["scatter"]

    def body(*refs):
        srcs, lands = refs[:n], refs[n:2 * n]
        send, recv = refs[2 * n], refs[2 * n + 1]
        for cp in _chip_copies(srcs, lands, send, recv, scatter):
            cp.wait_send()
            cp.wait_recv()

    outs = pl.pallas_call(
        body, name=name,
        out_shape=tuple(pltpu.HBM(a.shape, a.dtype) for a in (*st["srcs"], *st["lands"])),
        in_specs=[_HBM] * (2 * n) + [_SEM, _SEM, pl.BlockSpec(memory_space=pl.ANY)],
        out_specs=tuple([_HBM] * (2 * n)),
        input_output_aliases={i: i for i in range(2 * n)},
        compiler_params=pltpu.CompilerParams(has_side_effects=_EFFECT),
    )(*st["srcs"], *st["lands"], st["send"], st["recv"], after)
    return outs[:n], outs[n:]


def _sibling_forward(arrays, name):
    n = len(arrays)

    def body(*refs):
        srcs, outs = refs[:n], refs[n:2 * n]
        send, recv = refs[2 * n:]
        x, y, c = _place()
        copies = []
        for i in range(n):
            cp = pltpu.make_async_remote_copy(src_ref=srcs[i], dst_ref=outs[i], send_sem=send.at[i], recv_sem=recv.at[i],
                                              device_id=(x, y, 1 - c), device_id_type=MESH)
            cp.start()
            copies.append(cp)
        for cp in copies:
            cp.wait()

    return pl.pallas_call(
        body, name=name,
        out_shape=[jax.ShapeDtypeStruct(a.shape, a.dtype) for a in arrays],
        in_specs=[pl.BlockSpec(memory_space=pl.ANY)] * n,
        out_specs=[pl.BlockSpec(memory_space=pl.ANY)] * n,
        scratch_shapes=[pltpu.SemaphoreType.DMA((n,)), pltpu.SemaphoreType.DMA((n,))],
    )(*arrays)


def _matmul(a, b, contract, grid, a_spec, b_spec, o_spec, out_shape, name, nk=1, acc_shape=None):
    def body(a_ref, b_ref, o_ref, *acc):
        r = lax.dot_general(_bf(a_ref[...]), _bf(b_ref[...]), (contract, ((), ())), preferred_element_type=F32)
        if nk == 1:
            o_ref[...] = r.astype(o_ref.dtype)
        else:
            k = pl.program_id(len(grid) - 1)

            @pl.when(k == 0)
            def _():
                acc[0][...] = r

            @pl.when(k > 0)
            def _():
                acc[0][...] += r

            @pl.when(k == nk - 1)
            def _():
                o_ref[...] = acc[0][...].astype(o_ref.dtype)

    return pl.pallas_call(
        body, name=name, grid=grid, in_specs=[a_spec, b_spec], out_specs=o_spec, out_shape=out_shape,
        scratch_shapes=[pltpu.VMEM(acc_shape, F32)] if nk > 1 else [],
        compiler_params=_cp(len(grid)),
    )(a, b)


def _mm_nn(a, b, out_dtype, name, col0=0, n=None):
    m, k = a.shape
    n = b.shape[1] - col0 if n is None else n
    tm = min(MM_TM, m)
    tn = _tile(n, MM_TN_MAX, also=col0 if col0 else None)
    off = col0 // tn
    return _matmul(a, b, ((1,), (0,)), (n // tn, m // tm),
                   pl.BlockSpec((tm, k), lambda j, i: (i, 0)),
                   pl.BlockSpec((k, tn), lambda j, i: (0, j + off)),
                   pl.BlockSpec((tm, tn), lambda j, i: (i, j)),
                   jax.ShapeDtypeStruct((m, n), out_dtype), name)


def _mm_nn_blocked(a, b3, out_dtype, name):
    m, k = a.shape
    nj, _, nb = b3.shape
    tm = min(MM_TM, m)
    return _matmul(a, b3, ((1,), (0,)), (nj, m // tm),
                   pl.BlockSpec((tm, k), lambda j, i: (i, 0)),
                   pl.BlockSpec((None, k, nb), lambda j, i: (j, 0, 0)),
                   pl.BlockSpec((tm, nb), lambda j, i: (i, j)),
                   jax.ShapeDtypeStruct((m, nj * nb), out_dtype), name)


def _mm_nt(a, b, out_dtype, name):
    m, k = a.shape
    n = b.shape[0]
    tm = min(MM_TM, m)
    tn = _tile(n, MM_TN_MAX)
    return _matmul(a, b, ((1,), (1,)), (n // tn, m // tm),
                   pl.BlockSpec((tm, k), lambda j, i: (i, 0)),
                   pl.BlockSpec((tn, k), lambda j, i: (j, 0)),
                   pl.BlockSpec((tm, tn), lambda j, i: (i, j)),
                   jax.ShapeDtypeStruct((m, n), out_dtype), name)


def _mm_nt_blocked(a, b3, out_dtype, name):
    m = a.shape[0]
    nj, n, nb = b3.shape
    tm = min(MM_TM, m)
    tn = _tile(n, 512)
    return _matmul(a, b3, ((1,), (1,)), (n // tn, m // tm, nj),
                   pl.BlockSpec((tm, nb), lambda j, i, k: (i, k)),
                   pl.BlockSpec((None, tn, nb), lambda j, i, k: (k, j, 0)),
                   pl.BlockSpec((tm, tn), lambda j, i, k: (i, j)),
                   jax.ShapeDtypeStruct((m, n), out_dtype), name, nk=nj, acc_shape=(tm, tn))


def _mm_tn(a, b, out_dtype, name, blocked=False):
    k, m = a.shape
    n = b.shape[1]
    tm = min(TN_TM, m)
    tn = n // N_CHIPS if blocked else _tile(n, MM_TN_MAX)
    if blocked:
        o_spec = pl.BlockSpec((None, tm, tn), lambda j, i: (j, i, 0))
        out_shape = jax.ShapeDtypeStruct((n // tn, m, tn), out_dtype)
    else:
        o_spec = pl.BlockSpec((tm, tn), lambda j, i: (i, j))
        out_shape = jax.ShapeDtypeStruct((m, n), out_dtype)
    return _matmul(a, b, ((0,), (0,)), (n // tn, m // tm),
                   pl.BlockSpec((k, tm), lambda j, i: (0, i)),
                   pl.BlockSpec((k, tn), lambda j, i: (0, j)),
                   o_spec, out_shape, name)


def _vec_spec(d):
    return pl.BlockSpec((1, d), lambda i: (0, 0))


def _row_spec(ts, d, col=0):
    return pl.BlockSpec((ts, d), lambda i: (i, col))


def _rms(x):
    return lax.rsqrt(jnp.mean(x * x, axis=-1, keepdims=True) + EPS)


def _pre_norm(x, gain, scale, shift, dm):
    def body(x_ref, g_ref, sc_ref, sh_ref, h_ref):
        xv = x_ref[...]
        h_ref[...] = (((xv * _rms(xv)) * g_ref[...]) * (1.0 + sc_ref[...]) + sh_ref[...]).astype(h_ref.dtype)

    return pl.pallas_call(
        body, name="pre_norm", grid=(dm.S // dm.ts,),
        in_specs=[_row_spec(dm.ts, dm.D)] + [_vec_spec(dm.D)] * 3,
        out_specs=_row_spec(dm.ts, dm.D),
        out_shape=jax.ShapeDtypeStruct((dm.S, dm.D), BF16), compiler_params=_cp(1),
    )(x, gain, scale, shift)


def _res_norm(x, y, gpost, gate, gain, scale, shift, dm):
    def body(x_ref, y_ref, gp_ref, gt_ref, g_ref, sc_ref, sh_ref, xo_ref, h_ref):
        yv = y_ref[...]
        xn = x_ref[...] + gt_ref[...] * ((yv * _rms(yv)) * gp_ref[...])
        xo_ref[...] = xn
        h_ref[...] = (((xn * _rms(xn)) * g_ref[...]) * (1.0 + sc_ref[...]) + sh_ref[...]).astype(h_ref.dtype)

    return pl.pallas_call(
        body, name="res_norm", grid=(dm.S // dm.ts,),
        in_specs=[_row_spec(dm.ts, dm.D)] * 2 + [_vec_spec(dm.D)] * 5,
        out_specs=[_row_spec(dm.ts, dm.D)] * 2,
        out_shape=[jax.ShapeDtypeStruct((dm.S, dm.D), F32), jax.ShapeDtypeStruct((dm.S, dm.D), BF16)],
        compiler_params=_cp(1),
    )(x, y, gpost, gate, gain, scale, shift)


def _res_loss(x, y, gpost, gate, target, dm):
    def body(x_ref, y_ref, gp_ref, gt_ref, t_ref, dx_ref, loss_ref):
        i = pl.program_id(0)

        @pl.when(i == 0)
        def _():
            loss_ref[...] = jnp.zeros_like(loss_ref)
        yv = y_ref[...]
        err = x_ref[...] + gt_ref[...] * ((yv * _rms(yv)) * gp_ref[...]) - t_ref[...]
        dx_ref[...] = err * (1.0 / dm.D)
        per_row = jnp.mean(err * err, axis=-1, keepdims=True)
        loss_ref[...] += 0.5 * jnp.sum(per_row, axis=0, keepdims=True)

    return pl.pallas_call(
        body, name="res_loss", grid=(dm.S // dm.ts,),
        in_specs=[_row_spec(dm.ts, dm.D)] * 2 + [_vec_spec(dm.D)] * 2 + [_row_spec(dm.ts, dm.D)],
        out_specs=[_row_spec(dm.ts, dm.D), _vec_spec(LANES)],
        out_shape=[jax.ShapeDtypeStruct((dm.S, dm.D), F32), jax.ShapeDtypeStruct((1, LANES), F32)],
        compiler_params=_cp(1),
    )(x, y, gpost, gate, target)


def _post_bwd(dxo, y, gpost, gate, dm):
    def body(dx_ref, y_ref, gp_ref, gt_ref, dy_ref, dgp_ref, dgt_ref):
        i = pl.program_id(0)

        @pl.when(i == 0)
        def _():
            dgp_ref[...] = jnp.zeros_like(dgp_ref)
            dgt_ref[...] = jnp.zeros_like(dgt_ref)
        yv, dx = y_ref[...], dx_ref[...]
        r = _rms(yv)
        t = yv * r
        dgp_ref[...] += jnp.sum(dx * gt_ref[...] * t, axis=0, keepdims=True)
        dgt_ref[...] += jnp.sum(dx * (t * gp_ref[...]), axis=0, keepdims=True)
        dt = dx * (gt_ref[...] * gp_ref[...])
        dy_ref[...] = (r * (dt - t * jnp.mean(dt * t, axis=-1, keepdims=True))).astype(dy_ref.dtype)

    return pl.pallas_call(
        body, name="post_bwd", grid=(dm.S // dm.ts,),
        in_specs=[_row_spec(dm.ts, dm.D)] * 2 + [_vec_spec(dm.D)] * 2,
        out_specs=[_row_spec(dm.ts, dm.D), _vec_spec(dm.D), _vec_spec(dm.D)],
        out_shape=[jax.ShapeDtypeStruct((dm.S, dm.D), BF16)] + [jax.ShapeDtypeStruct((1, dm.D), F32)] * 2,
        compiler_params=_cp(1),
    )(dxo, y, gpost, gate)


def _pre_bwd(dh, dxo, x, gain, scale, dm):
    def body(dh_ref, dxo_ref, x_ref, g_ref, sc_ref, dx_ref, dsh_ref, dsc_ref, dg_ref):
        i = pl.program_id(0)

        @pl.when(i == 0)
        def _():
            dsh_ref[...] = jnp.zeros_like(dsh_ref)
            dsc_ref[...] = jnp.zeros_like(dsc_ref)
            dg_ref[...] = jnp.zeros_like(dg_ref)
        xv, dh_ = x_ref[...], dh_ref[...]
        r = _rms(xv)
        nrm = xv * r
        one_sc = 1.0 + sc_ref[...]
        dsh_ref[...] += jnp.sum(dh_, axis=0, keepdims=True)
        dsc_ref[...] += jnp.sum(dh_ * (nrm * g_ref[...]), axis=0, keepdims=True)
        dg_ref[...] += jnp.sum(dh_ * nrm * one_sc, axis=0, keepdims=True)
        dn = dh_ * (g_ref[...] * one_sc)
        dx_ref[...] = dxo_ref[...] + r * (dn - nrm * jnp.mean(dn * nrm, axis=-1, keepdims=True))

    return pl.pallas_call(
        body, name="pre_bwd", grid=(dm.S // dm.ts,),
        in_specs=[_row_spec(dm.ts, dm.D)] * 3 + [_vec_spec(dm.D)] * 2,
        out_specs=[_row_spec(dm.ts, dm.D)] + [_vec_spec(dm.D)] * 3,
        out_shape=[jax.ShapeDtypeStruct((dm.S, dm.D), F32)] + [jax.ShapeDtypeStruct((1, dm.D), F32)] * 3,
        compiler_params=_cp(1),
    )(dh, dxo, x, gain, scale)


def _sigmoid(z):
    return 1.0 / (1.0 + jnp.exp(-z))


def _swiglu_fwd(gu, dm):
    def body(g_ref, u_ref, a_ref):
        g = g_ref[...]
        a_ref[...] = (g * _sigmoid(g) * u_ref[...]).astype(a_ref.dtype)

    return pl.pallas_call(
        body, name="swiglu_fwd", grid=(dm.S // dm.ts,),
        in_specs=[_row_spec(dm.ts, dm.F, 0), _row_spec(dm.ts, dm.F, 1)],
        out_specs=_row_spec(dm.ts, dm.F),
        out_shape=jax.ShapeDtypeStruct((dm.S, dm.F), BF16), compiler_params=_cp(1),
    )(gu, gu)


def _swiglu_bwd(da, gu, dm):
    def body(da_ref, g_ref, u_ref, d_ref):
        g, u, da_ = g_ref[...], u_ref[...], da_ref[...]
        sg = _sigmoid(g)
        d_ref[:, :dm.F] = (da_ * u * (sg * (1.0 + g * (1.0 - sg)))).astype(d_ref.dtype)
        d_ref[:, dm.F:] = (da_ * (g * sg)).astype(d_ref.dtype)

    return pl.pallas_call(
        body, name="swiglu_bwd", grid=(dm.S // dm.ts,),
        in_specs=[_row_spec(dm.ts, dm.F), _row_spec(dm.ts, dm.F, 0), _row_spec(dm.ts, dm.F, 1)],
        out_specs=_row_spec(dm.ts, 2 * dm.F),
        out_shape=jax.ShapeDtypeStruct((dm.S, 2 * dm.F), BF16), compiler_params=_cp(1),
    )(da, gu, gu)


def _tri(n, upper):
    r = lax.broadcasted_iota(jnp.int32, (n, n), 0)
    c = lax.broadcasted_iota(jnp.int32, (n, n), 1)
    return (c >= r if upper else r >= c).astype(F32)


def _gates_fwd(cf, bf, dm):
    ts = dm.ts
    fcol = 2 * dm.C // LANES

    def body(f_ref, b_ref, cum_ref, cumt_ref, carry):
        i = pl.program_id(0)

        @pl.when(i == 0)
        def _():
            carry[...] = jnp.zeros_like(carry)
        z = f_ref[...] + b_ref[...]
        lf = jnp.minimum(z, 0.0) - jnp.log(1.0 + jnp.exp(-jnp.abs(z)))
        cs = jnp.dot(_tri(ts, False), lf, precision=lax.Precision.HIGHEST, preferred_element_type=F32) + carry[...]
        cum_ref[...] = cs
        cumt_ref[...] = cs.T[:8, :]
        carry[...] = cs[ts - 1:ts, :]

    return pl.pallas_call(
        body, name="gates_fwd", grid=(dm.S // ts,),
        in_specs=[pl.BlockSpec((ts, LANES), lambda i: (i, fcol)), _vec_spec(LANES)],
        out_specs=[_row_spec(ts, LANES), pl.BlockSpec((None, 8, ts), lambda i: (i, 0, 0))],
        out_shape=[jax.ShapeDtypeStruct((dm.S, LANES), F32), jax.ShapeDtypeStruct((dm.S // ts, 8, ts), F32)],
        scratch_shapes=[pltpu.VMEM((1, LANES), F32)], compiler_params=_cp(1),
    )(cf, bf)


def _gates_bwd(dcq, dckt, cf, bf, dm):
    ts = dm.ts
    nb = dm.S // ts
    fcol = 2 * dm.C // LANES

    def body(dcq_ref, dck_ref, f_ref, b_ref, df_ref, db_ref, carry):
        i = pl.program_id(0)

        @pl.when(i == 0)
        def _():
            carry[...] = jnp.zeros_like(carry)
            db_ref[...] = jnp.zeros_like(db_ref)
        dck = jnp.concatenate([dck_ref[...], jnp.zeros((LANES - 8, ts), F32)], axis=0).T
        dc = dcq_ref[...] + dck
        dlf = jnp.dot(_tri(ts, True), dc, precision=lax.Precision.HIGHEST, preferred_element_type=F32) + carry[...]
        carry[...] = dlf[0:1, :]
        dz = dlf * (1.0 - _sigmoid(f_ref[...] + b_ref[...]))
        df_ref[...] = dz.astype(df_ref.dtype)
        db_ref[...] += jnp.sum(dz, axis=0, keepdims=True)

    return pl.pallas_call(
        body, name="gates_bwd", grid=(nb,),
        in_specs=[pl.BlockSpec((ts, LANES), lambda i: (nb - 1 - i, 0)),
                  pl.BlockSpec((None, 8, ts), lambda i: (nb - 1 - i, 0, 0)),
                  pl.BlockSpec((ts, LANES), lambda i: (nb - 1 - i, fcol)), _vec_spec(LANES)],
        out_specs=[pl.BlockSpec((ts, LANES), lambda i: (nb - 1 - i, 0)), _vec_spec(LANES)],
        out_shape=[jax.ShapeDtypeStruct((dm.S, LANES), BF16), jax.ShapeDtypeStruct((1, LANES), F32)],
        scratch_shapes=[pltpu.VMEM((1, LANES), F32)], compiler_params=_cp(1),
    )(dcq, dckt, cf, bf)


def _dot_nt(a, b):
    return lax.dot_general(a, b, (((1,), (1,)), ((), ())), preferred_element_type=F32)


def _dot_tn(a, b):
    return lax.dot_general(a, b, (((0,), (0,)), ((), ())), preferred_element_type=F32)


def _attn_fwd(qkv, cum, cumt, dm):
    tq = dm.ts
    A, H = dm.A, dm.H
    scale = HEAD_DIM ** -0.5

    def body(q_ref, kv_ref, cq_ref, ckt_ref, o_ref, lse_ref):
        i = pl.program_id(0)
        row = lax.broadcasted_iota(jnp.int32, (tq, tq), 0)
        col = lax.broadcasted_iota(jnp.int32, (tq, tq), 1)
        lane = lax.broadcasted_iota(jnp.int32, (tq, LANES), 1)
        lse_all = jnp.zeros((tq, LANES), F32)
        for h in range(H):
            lo, hi = h * HEAD_DIM, (h + 1) * HEAD_DIM
            q = q_ref[:, lo:hi]
            cq = cq_ref[:, h:h + 1]

            def step(j, carry, lo=lo, hi=hi, q=q, cq=cq, h=h):
                m, l, acc = carry
                r0 = pl.multiple_of(j * tq, tq)
                k = kv_ref[pl.ds(r0, tq), A + lo:A + hi]
                v = kv_ref[pl.ds(r0, tq), 2 * A + lo:2 * A + hi]
                ck = ckt_ref[j, h:h + 1, :]
                s = _dot_nt(q, k) * scale + cq - ck
                s = jnp.where(row + (i - j) * tq >= col, s, -1e30)
                m_new = jnp.maximum(m, jnp.max(s, axis=1, keepdims=True))
                p = jnp.exp(s - m_new)
                alpha = jnp.exp(m - m_new)
                l = alpha * l + jnp.sum(p, axis=1, keepdims=True)
                acc = alpha * acc + jnp.dot(p.astype(BF16), v, preferred_element_type=F32)
                return m_new, l, acc

            m, l, acc = lax.fori_loop(0, i + 1, step, (jnp.full((tq, 1), -1e30, F32), jnp.zeros((tq, 1), F32),
                                                       jnp.zeros((tq, HEAD_DIM), F32)))
            o_ref[:, lo:hi] = (acc / l).astype(o_ref.dtype)
            lse_all = jnp.where(lane == h, m + jnp.log(l), lse_all)
        lse_ref[...] = lse_all

    nq = dm.S // tq
    return pl.pallas_call(
        body, name="attn_fwd", grid=(nq,),
        in_specs=[pl.BlockSpec((tq, A), lambda i: (i, 0)), pl.BlockSpec((dm.S, 3 * A), lambda i: (0, 0)),
                  _row_spec(tq, LANES), pl.BlockSpec((nq, 8, tq), lambda i: (0, 0, 0))],
        out_specs=[pl.BlockSpec((tq, A), lambda i: (i, 0)), _row_spec(tq, LANES)],
        out_shape=[jax.ShapeDtypeStruct((dm.S, A), BF16), jax.ShapeDtypeStruct((dm.S, LANES), F32)],
        compiler_params=_cp(1),
    )(qkv, qkv, cum, cumt)


def _attn_delta(dcat, o, dm):
    ts = dm.ts

    def body(do_ref, o_ref, dl_ref, dob_ref):
        lane = lax.broadcasted_iota(jnp.int32, (ts, LANES), 1)
        prod = do_ref[...] * o_ref[...].astype(F32)
        dl = jnp.zeros((ts, LANES), F32)
        for h in range(dm.H):
            dl = jnp.where(lane == h, jnp.sum(prod[:, h * HEAD_DIM:(h + 1) * HEAD_DIM], axis=1, keepdims=True), dl)
        dl_ref[...] = dl
        dob_ref[...] = do_ref[...].astype(dob_ref.dtype)

    return pl.pallas_call(
        body, name="attn_delta", grid=(dm.S // ts,),
        in_specs=[_row_spec(ts, dm.A, 0), _row_spec(ts, dm.A)],
        out_specs=[_row_spec(ts, LANES), _row_spec(ts, dm.A)],
        out_shape=[jax.ShapeDtypeStruct((dm.S, LANES), F32), jax.ShapeDtypeStruct((dm.S, dm.A), BF16)],
        compiler_params=_cp(1),
    )(dcat, o)


def _attn_bwd(qkv, do, lse, delta, cum, cumt, dm):
    tq = dm.ts
    A, H = dm.A, dm.H
    nq = dm.S // tq
    scale = HEAD_DIM ** -0.5

    def body(kv_ref, q_ref, do_ref, lse_ref, dl_ref, cq_ref, ckt_ref, dq_ref, dkv_ref, dcq_ref, dckt_ref):
        j = pl.program_id(0)

        @pl.when(j == 0)
        def _():
            dq_ref[...] = jnp.zeros_like(dq_ref)
            dcq_ref[...] = jnp.zeros_like(dcq_ref)
        row = lax.broadcasted_iota(jnp.int32, (tq, tq), 0)
        col = lax.broadcasted_iota(jnp.int32, (tq, tq), 1)
        lane = lax.broadcasted_iota(jnp.int32, (tq, LANES), 1)
        sub = lax.broadcasted_iota(jnp.int32, (8, tq), 0)
        dck_all = jnp.zeros((8, tq), F32)
        for h in range(H):
            lo, hi = h * HEAD_DIM, (h + 1) * HEAD_DIM
            k = kv_ref[:, A + lo:A + hi]
            v = kv_ref[:, 2 * A + lo:2 * A + hi]
            ck = ckt_ref[h:h + 1, :]

            def step(i, carry, lo=lo, hi=hi, k=k, v=v, ck=ck, h=h):
                dk, dv, dck = carry
                r0 = pl.multiple_of(i * tq, tq)
                rows = pl.ds(r0, tq)
                q = q_ref[rows, lo:hi]
                do_ = do_ref[rows, lo:hi]
                s = _dot_nt(q, k) * scale + cq_ref[rows, h:h + 1] - ck
                s = jnp.where(row + (i - j) * tq >= col, s, -1e30)
                p = jnp.exp(s - lse_ref[rows, h:h + 1])
                ds = p * (_dot_nt(do_, v) - dl_ref[rows, h:h + 1])
                dsb = ds.astype(BF16)
                dv = dv + _dot_tn(p.astype(BF16), do_)
                dk = dk + _dot_tn(dsb, q)
                dq_ref[rows, lo:hi] += jnp.dot(dsb, k, preferred_element_type=F32)
                dcq_ref[rows, :] += jnp.where(lane == h, jnp.sum(ds, axis=1, keepdims=True), 0.0)
                dck = dck - jnp.sum(ds, axis=0, keepdims=True)
                return dk, dv, dck

            dk, dv, dck = lax.fori_loop(j, nq, step, (jnp.zeros((tq, HEAD_DIM), F32), jnp.zeros((tq, HEAD_DIM), F32),
                                                      jnp.zeros((1, tq), F32)))
            dkv_ref[:, lo:hi] = (dk * scale).astype(dkv_ref.dtype)
            dkv_ref[:, A + lo:A + hi] = dv.astype(dkv_ref.dtype)
            dck_all = jnp.where(sub == h, dck, dck_all)
        dckt_ref[...] = dck_all

        @pl.when(j == nq - 1)
        def _():
            dq_ref[...] = dq_ref[...] * scale

    full = lambda w: pl.BlockSpec((dm.S, w), lambda j: (0, 0))
    return pl.pallas_call(
        body, name="attn_bwd", grid=(nq,),
        in_specs=[pl.BlockSpec((tq, 3 * A), lambda j: (j, 0)), full(A), full(A), full(LANES), full(LANES), full(LANES),
                  pl.BlockSpec((None, 8, tq), lambda j: (j, 0, 0))],
        out_specs=[full(A), pl.BlockSpec((tq, 2 * A), lambda j: (j, 0)), full(LANES),
                   pl.BlockSpec((None, 8, tq), lambda j: (j, 0, 0))],
        out_shape=[jax.ShapeDtypeStruct((dm.S, A), F32), jax.ShapeDtypeStruct((dm.S, 2 * A), BF16),
                   jax.ShapeDtypeStruct((dm.S, LANES), F32), jax.ShapeDtypeStruct((nq, 8, tq), F32)],
        compiler_params=_cp(1),
    )(qkv, qkv, do, lse, delta, cum, cumt)


def _glu(cf_rows, c):
    return cf_rows[:, :c] * _sigmoid(cf_rows[:, c:2 * c])


def _conv_fwd(cf, cw, cb, lg, lb, dm):
    ts, C = dm.ts, dm.C
    per = ts // HALO

    def body(cf_ref, halo_ref, w_ref, cb_ref, lg_ref, lb_ref, u3_ref, u1_ref):
        i = pl.program_id(0)
        prev = jnp.where(i > 0, _glu(halo_ref[...], C), 0.0)
        win = jnp.concatenate([prev, _glu(cf_ref[...], C)], axis=0)
        u1 = jnp.zeros((ts, C), F32) + cb_ref[...]
        off = HALO - (CONV_K - 1)
        for k in range(CONV_K):
            u1 = u1 + w_ref[k:k + 1, :] * win[off + k:off + k + ts, :]
        u1_ref[...] = u1
        mu = jnp.mean(u1, axis=-1, keepdims=True)
        cen = u1 - mu
        rstd = lax.rsqrt(jnp.mean(cen * cen, axis=-1, keepdims=True) + EPS)
        u2 = cen * rstd * lg_ref[...] + lb_ref[...]
        u3_ref[...] = (u2 * _sigmoid(u2)).astype(u3_ref.dtype)

    return pl.pallas_call(
        body, name="conv_fwd", grid=(dm.S // ts,),
        in_specs=[pl.BlockSpec((ts, 2 * C), lambda i: (i, 0)),
                  pl.BlockSpec((HALO, 2 * C), lambda i: (jnp.maximum(i * per - 1, 0), 0)),
                  pl.BlockSpec((HALO, C), lambda i: (0, 0))] + [_vec_spec(C)] * 3,
        out_specs=[_row_spec(ts, C)] * 2,
        out_shape=[jax.ShapeDtypeStruct((dm.S, C), BF16), jax.ShapeDtypeStruct((dm.S, C), F32)],
        compiler_params=_cp(1),
    )(cf, cf, cw, cb, lg, lb)


def _conv_bwd(dcat, u1, cf, cw, lg, lb, dm):
    ts, C = dm.ts, dm.C
    per = ts // HALO
    nt = dm.S // ts
    last_halo = dm.S // HALO - 1

    def ln_bwd(du3, u1v, lg_v, lb_v):
        mu = jnp.mean(u1v, axis=-1, keepdims=True)
        cen = u1v - mu
        rstd = lax.rsqrt(jnp.mean(cen * cen, axis=-1, keepdims=True) + EPS)
        uhat = cen * rstd
        u2 = uhat * lg_v + lb_v
        sg = _sigmoid(u2)
        du2 = du3 * (sg * (1.0 + u2 * (1.0 - sg)))
        duh = du2 * lg_v
        du1 = rstd * (duh - jnp.mean(duh, axis=-1, keepdims=True) - uhat * jnp.mean(duh * uhat, axis=-1, keepdims=True))
        return du1, du2, uhat

    def body(d_ref, dn_ref, u1_ref, u1n_ref, cf_ref, halo_ref, w_ref, lg_ref, lb_ref,
             dcf_ref, dw_ref, dcb_ref, dlg_ref, dlb_ref):
        i = pl.program_id(0)

        @pl.when(i == 0)
        def _():
            dw_ref[...] = jnp.zeros_like(dw_ref)
            dcb_ref[...] = jnp.zeros_like(dcb_ref)
            dlg_ref[...] = jnp.zeros_like(dlg_ref)
            dlb_ref[...] = jnp.zeros_like(dlb_ref)
        lg_v, lb_v = lg_ref[...], lb_ref[...]
        du1, du2, uhat = ln_bwd(d_ref[...], u1_ref[...], lg_v, lb_v)
        du1n, _, _ = ln_bwd(dn_ref[...], u1n_ref[...], lg_v, lb_v)
        du1n = jnp.where(i < nt - 1, du1n, 0.0)
        dlg_ref[...] += jnp.sum(du2 * uhat, axis=0, keepdims=True)
        dlb_ref[...] += jnp.sum(du2, axis=0, keepdims=True)
        dcb_ref[...] += jnp.sum(du1, axis=0, keepdims=True)
        dwin = jnp.concatenate([du1, du1n], axis=0)
        cfv = cf_ref[...]
        cv, sg = cfv[:, :C], _sigmoid(cfv[:, C:2 * C])
        prev = jnp.where(i > 0, _glu(halo_ref[...], C), 0.0)
        uwin = jnp.concatenate([prev, cv * sg], axis=0)
        du0 = jnp.zeros((ts, C), F32)
        off = HALO - (CONV_K - 1)
        for k in range(CONV_K):
            back = CONV_K - 1 - k
            du0 = du0 + w_ref[k:k + 1, :] * dwin[back:back + ts, :]
            dw_ref[k:k + 1, :] += jnp.sum(du1 * uwin[off + k:off + k + ts, :], axis=0, keepdims=True)
        dcf_ref[:, :C] = (du0 * sg).astype(dcf_ref.dtype)
        dcf_ref[:, C:] = (du0 * cv * sg * (1.0 - sg)).astype(dcf_ref.dtype)

    ucol = dm.A // C
    return pl.pallas_call(
        body, name="conv_bwd", grid=(nt,),
        in_specs=[pl.BlockSpec((ts, C), lambda i: (i, ucol)),
                  pl.BlockSpec((HALO, C), lambda i: (jnp.minimum((i + 1) * per, last_halo), ucol)),
                  pl.BlockSpec((ts, C), lambda i: (i, 0)),
                  pl.BlockSpec((HALO, C), lambda i: (jnp.minimum((i + 1) * per, last_halo), 0)),
                  pl.BlockSpec((ts, 2 * C), lambda i: (i, 0)),
                  pl.BlockSpec((HALO, 2 * C), lambda i: (jnp.maximum(i * per - 1, 0), 0)),
                  pl.BlockSpec((HALO, C), lambda i: (0, 0)), _vec_spec(C), _vec_spec(C)],
        out_specs=[_row_spec(ts, 2 * C), pl.BlockSpec((HALO, C), lambda i: (0, 0))] + [_vec_spec(C)] * 3,
        out_shape=[jax.ShapeDtypeStruct((dm.S, 2 * C), BF16), jax.ShapeDtypeStruct((HALO, C), F32)]
        + [jax.ShapeDtypeStruct((1, C), F32)] * 3,
        compiler_params=_cp(1),
    )(dcat, dcat, u1, u1, cf, cf, cw, lg, lb)


def _ada_fwd(c16, ada_w, ada_b_cols, dm):
    L, D, n = ada_w.shape
    tn = _tile(n, 512)

    def body(c_ref, w_ref, b_ref, o_ref, a_ref):
        cv = c_ref[...]
        act = (cv * _sigmoid(cv)).astype(BF16)
        a_ref[...] = act
        o_ref[...] = jnp.dot(act, w_ref[...].astype(BF16), preferred_element_type=F32) + b_ref[...]

    return pl.pallas_call(
        body, name="ada_fwd", grid=(L, n // tn),
        in_specs=[pl.BlockSpec((16, D), lambda l, j: (0, 0)), pl.BlockSpec((None, D, tn), lambda l, j: (l, 0, j)),
                  pl.BlockSpec((None, 1, tn), lambda l, j: (l, 0, j))],
        out_specs=[pl.BlockSpec((None, 16, tn), lambda l, j: (l, 0, j)), pl.BlockSpec((16, D), lambda l, j: (0, 0))],
        out_shape=[jax.ShapeDtypeStruct((L, 16, n), F32), jax.ShapeDtypeStruct((16, D), BF16)],
        compiler_params=_cp(2),
    )(c16, ada_w, ada_b_cols)


def _ada_bwd(act16, dmod16):
    L, _, n = dmod16.shape
    D = act16.shape[1]
    tm = min(TN_TM, D)

    def body(a_ref, d_ref, o_ref):
        o_ref[...] = _dot_tn(a_ref[...], d_ref[...])

    return pl.pallas_call(
        body, name="ada_bwd", grid=(L, D // tm),
        in_specs=[pl.BlockSpec((16, tm), lambda l, i: (0, i)), pl.BlockSpec((None, 16, n), lambda l, i: (l, 0, 0))],
        out_specs=pl.BlockSpec((None, tm, n), lambda l, i: (l, i, 0)),
        out_shape=jax.ShapeDtypeStruct((L, D, n), F32), compiler_params=_cp(2),
    )(act16, dmod16)


def _sum_devices(g8):
    _, R, _ = g8.shape
    tr = _rows_tile(R)

    def body(g_ref, o_ref):
        acc = g_ref[0]
        for d in range(1, N_DEV):
            acc = acc + g_ref[d]
        o_ref[...] = acc

    return pl.pallas_call(
        body, name="sum_devices", grid=(R // tr,),
        in_specs=[pl.BlockSpec((N_DEV, tr, LANES), lambda i: (0, i, 0))],
        out_specs=pl.BlockSpec((tr, LANES), lambda i: (i, 0)),
        out_shape=jax.ShapeDtypeStruct((R, LANES), F32), compiler_params=_cp(1),
    )(g8)


def _rows_tile(r, cap=512):
    for t in (512, 256, 128, 64, 32, 16, 8):
        if t <= cap and r % t == 0:
            return t
    return r


def _adam_math(w, g, m, v):
    m = ADAM_B1 * m + (1.0 - ADAM_B1) * g
    v = ADAM_B2 * v + (1.0 - ADAM_B2) * (g * g)
    m_hat = m / (1.0 - ADAM_B1 ** ADAM_STEP)
    v_hat = v / (1.0 - ADAM_B2 ** ADAM_STEP)
    delta = -ADAM_LR * (m_hat / (jnp.sqrt(v_hat) + ADAM_EPS) + ADAM_WD * w)
    return delta, m, v


def _adamw_dense(w, m, v, g, name):
    R, Cc = w.shape
    tr = _rows_tile(R, 128)

    def body(w_ref, m_ref, v_ref, g_ref, d_ref, mo_ref, vo_ref):
        d, mn, vn = _adam_math(w_ref[...], g_ref[...], m_ref[...], v_ref[...])
        d_ref[...] = d
        mo_ref[...] = mn
        vo_ref[...] = vn

    spec = pl.BlockSpec((tr, Cc), lambda i: (i, 0))
    return pl.pallas_call(
        body, name=name, grid=(R // tr,), in_specs=[spec] * 4, out_specs=[spec] * 3,
        out_shape=[jax.ShapeDtypeStruct((R, Cc), F32)] * 3, compiler_params=_cp(1),
    )(w, m, v, g)


def _adamw_shard(w, m, v, near, far, layer, prev, name):
    L, r, cc = w.shape
    tr = _rows_tile(r, 128)

    def body(w_ref, m_ref, v_ref, n_ref, f_ref, *rest):
        g_ref, d_ref, mo_ref, vo_ref = rest[-4:]
        g = n_ref[0].astype(F32) + f_ref[0].astype(F32)
        for k in range(1, N_CHIPS):
            g = g + (n_ref[k].astype(F32) + f_ref[k].astype(F32))
        d, mn, vn = _adam_math(w_ref[...], g, m_ref[...], v_ref[...])
        g_ref[...] = g
        d_ref[...] = d
        mo_ref[...] = mn
        vo_ref[...] = vn

    wspec = pl.BlockSpec((None, tr, cc), lambda i: (layer, i, 0))
    sspec = pl.BlockSpec((N_CHIPS, tr, cc), lambda i: (0, i, 0))
    n_prev = 0 if prev is None else 4
    return pl.pallas_call(
        body, name=name, grid=(r // tr,),
        in_specs=[wspec] * 3 + [sspec] * 2 + [pl.BlockSpec(memory_space=pl.ANY)] * n_prev,
        out_specs=[wspec] * 4,
        out_shape=[jax.ShapeDtypeStruct((L, r, cc), F32)] * 4,
        input_output_aliases={5 + t: t for t in range(n_prev)},
        compiler_params=_cp(1),
    )(w, m, v, near, far, *(prev or ()))


def _pack(vs):
    flat = jnp.concatenate([v.reshape(-1).astype(F32) for v in vs])
    pad = (-flat.shape[0]) % (8 * LANES)
    return jnp.pad(flat, (0, pad)).reshape(-1, LANES)


def _unpack(packed, shapes):
    flat = packed.reshape(-1)
    out, pos = [], 0
    for s in shapes:
        n = 1
        for d in s:
            n *= d
        out.append(flat[pos:pos + n].reshape(s))
        pos += n
    return out


def kernel(x, c, w_in, b_f, conv_w, conv_b, conv_ln_g, conv_ln_b, w_o, w_ffn_in, w_ffn_out, mix_pre_g, mix_post_g, ffn_pre_g, ffn_post_g, ada_w, ada_b, loss_target, m_w_in, m_b_f, m_conv_w, m_conv_b, m_conv_ln_g, m_conv_ln_b, m_w_o, m_w_ffn_in, m_w_ffn_out, m_mix_pre_g, m_mix_post_g, m_ffn_pre_g, m_ffn_post_g, m_ada_w, m_ada_b, v_w_in, v_b_f, v_conv_w, v_conv_b, v_conv_ln_g, v_conv_ln_b, v_w_o, v_w_ffn_in, v_w_ffn_out, v_mix_pre_g, v_mix_post_g, v_ffn_pre_g, v_ffn_post_g, v_ada_w, v_ada_b):
    params = dict(w_in=w_in, b_f=b_f, conv_w=conv_w, conv_b=conv_b, conv_ln_g=conv_ln_g, conv_ln_b=conv_ln_b, w_o=w_o,
                  w_ffn_in=w_ffn_in, w_ffn_out=w_ffn_out, mix_pre_g=mix_pre_g, mix_post_g=mix_post_g,
                  ffn_pre_g=ffn_pre_g, ffn_post_g=ffn_post_g, ada_w=ada_w, ada_b=ada_b)
    mom = dict(w_in=m_w_in, b_f=m_b_f, conv_w=m_conv_w, conv_b=m_conv_b, conv_ln_g=m_conv_ln_g, conv_ln_b=m_conv_ln_b,
               w_o=m_w_o, w_ffn_in=m_w_ffn_in, w_ffn_out=m_w_ffn_out, mix_pre_g=m_mix_pre_g, mix_post_g=m_mix_post_g,
               ffn_pre_g=m_ffn_pre_g, ffn_post_g=m_ffn_post_g, ada_w=m_ada_w, ada_b=m_ada_b)
    var = dict(w_in=v_w_in, b_f=v_b_f, conv_w=v_conv_w, conv_b=v_conv_b, conv_ln_g=v_conv_ln_g, conv_ln_b=v_conv_ln_b,
               w_o=v_w_o, w_ffn_in=v_w_ffn_in, w_ffn_out=v_w_ffn_out, mix_pre_g=v_mix_pre_g, mix_post_g=v_mix_post_g,
               ffn_pre_g=v_ffn_pre_g, ffn_post_g=v_ffn_post_g, ada_w=v_ada_w, ada_b=v_ada_b)

    S, D = x.shape[1], x.shape[2]
    L = w_in.shape[0]
    A = D // 2
    C = D - A
    H = A // HEAD_DIM
    F = w_ffn_out.shape[1] * N_CHIPS
    d_in = w_in.shape[2] * N_CHIPS
    NP = 3 * A + 2 * C + LANES
    dm = Dims(S=S, D=D, A=A, C=C, H=H, F=F, L=L, NP=NP, ts=min(ROW_TILE, S))
    assert H <= 8 and A == C and d_in == 3 * A + H + 2 * C

    ix, iy, ic = _place()
    chip = 2 * ix + iy
    dev = 4 * ix + 2 * iy + ic
    x2 = x.reshape(S, D)
    tgt = loss_target.reshape(S, D)

    gather = []
    token = c
    for l in range(L):
        gather.append(_exchange_start([w_in[l].astype(BF16), w_o[l].astype(BF16), w_ffn_in[l].astype(BF16),
                                       w_ffn_out[l].astype(BF16)], False, token, f"gather_w_start_{l}"))
        token = gather[l]["token"]

    c_all =_all_gather_devices(c.reshape(D // LANES, LANES), "gather_c").reshape(N_DEV, D)
    c16 = jnp.pad(c_all, ((0, 16 - N_DEV), (0, 0)))
    n_ada = ada_w.shape[2]
    ada_b_cols = lax.dynamic_slice_in_dim(ada_b, chip * n_ada, n_ada, axis=1).reshape(L, 1, n_ada)
    mod_cols, act16 = _ada_fwd(c16, ada_w, ada_b_cols, dm)
    conv_w_all, mod_all = _all_gather_chips([conv_w.reshape(L * CONV_K, -1), mod_cols.reshape(L * 16, n_ada)], "gather_mod")
    cwc = conv_w.shape[2]
    conv_w_full = conv_w_all.reshape(N_CHIPS, L, CONV_K, cwc).transpose(1, 2, 0, 3).reshape(L, CONV_K, C)
    conv_w_full = jnp.pad(conv_w_full, ((0, 0), (0, HALO - CONV_K), (0, 0)))
    mod_all = mod_all.reshape(N_CHIPS, L, 16, n_ada)
    mod_me = lax.dynamic_index_in_dim(mod_all, dev, axis=2, keepdims=False)
    mod_me = mod_me.transpose(1, 0, 2).reshape(L, N_MOD, 1, D)

    def weights_of(l, after):
        srcs, lands = _exchange_wait(gather[l], after, f"gather_w_wait_{l}")
        g_in, g_o, g_fi, g_fo = [lax.dynamic_update_index_in_dim(land, src, chip, 0) for land, src in zip(lands, srcs)]
        w_nat = g_in.transpose(1, 0, 2).reshape(D, d_in)
        w_p = jnp.concatenate([w_nat[:, :3 * A], w_nat[:, 3 * A + H:], w_nat[:, 3 * A:3 * A + H],
                               jnp.zeros((D, LANES - H), BF16)], axis=1)
        return w_p, g_o.reshape(D, D), g_fi, g_fo.reshape(F, D)

    gathered = [None] * L
    vec = lambda p, l: p[l].reshape(1, -1)
    bf_pad = jnp.pad(b_f, ((0, 0), (0, LANES - H)))

    saved = []
    xin = x2
    h = _pre_norm(xin, vec(mix_pre_g, 0), mod_me[0, 1], mod_me[0, 0], dm)
    dx = loss_part = None
    for l in range(L):
        gathered[l] = weights_of(l, h)
        w_p, wo, wfi, wfo = gathered[l]
        qkv = _mm_nn(h, w_p, BF16, "mm_qkv", 0, 3 * A)
        cf = _mm_nn(h, w_p, F32, "mm_cf", 3 * A, 2 * C + LANES)
        cum, cumt = _gates_fwd(cf, vec(bf_pad, l), dm)
        o, lse = _attn_fwd(qkv, cum, cumt, dm)
        u3, u1 = _conv_fwd(cf, conv_w_full[l], vec(conv_b, l), vec(conv_ln_g, l), vec(conv_ln_b, l), dm)
        cat = jnp.concatenate([o, u3], axis=1)
        y = _mm_nn(cat, wo, F32, "mm_o")
        x1, h2 = _res_norm(xin, y, vec(mix_post_g, l), mod_me[l, 2], vec(ffn_pre_g, l), mod_me[l, 4], mod_me[l, 3], dm)
        gu = _mm_nn_blocked(h2, wfi, F32, "mm_ffn_in")
        a = _swiglu_fwd(gu, dm)
        y2 = _mm_nn(a, wfo, F32, "mm_ffn_out")
        saved.append(dict(xin=xin, h=h, qkv=qkv, cf=cf, cum=cum, cumt=cumt, o=o, lse=lse, u1=u1, cat=cat, y=y,
                          x1=x1, h2=h2, gu=gu, a=a, y2=y2))
        if l + 1 < L:
            xin, h = _res_norm(x1, y2, vec(ffn_post_g, l), mod_me[l, 5], vec(mix_pre_g, l + 1),
                               mod_me[l + 1, 1], mod_me[l + 1, 0], dm)
        else:
            dx, loss_part = _res_loss(x1, y2, vec(ffn_post_g, l), mod_me[l, 5], tgt, dm)
    loss = lax.psum(loss_part[0, 0], ("x", "y", "c"))

    small = [None] * L
    big = [None] * L
    for l in reversed(range(L)):
        w_p, wo, wfi, wfo = gathered[l]
        sv = saved[l]
        dy2, d_gfpost, d_g2 = _post_bwd(dx, sv["y2"], vec(ffn_post_g, l), mod_me[l, 5], dm)
        da = _mm_nt(dy2, wfo, F32, "mm_da")
        dgu = _swiglu_bwd(da, sv["gu"], dm)
        g_wfo = _mm_tn(sv["a"], dy2, BF16, "mm_dwfo")
        g_wfi = _mm_tn(sv["h2"], dgu, BF16, "mm_dwfi", blocked=True)
        dh2 = _mm_nt_blocked(dgu, wfi, F32, "mm_dh2")
        dx1, d_sh2, d_sc2, d_gfpre = _pre_bwd(dh2, dx, sv["x1"], vec(ffn_pre_g, l), mod_me[l, 4], dm)
        dy, d_gpost, d_g1 = _post_bwd(dx1, sv["y"], vec(mix_post_g, l), mod_me[l, 2], dm)
        dcat = _mm_nt(dy, wo, F32, "mm_dcat")
        g_wo = _mm_tn(sv["cat"], dy, BF16, "mm_dwo")
        dcfc, d_cw, d_cb, d_lg, d_lb = _conv_bwd(dcat, sv["u1"], sv["cf"], conv_w_full[l], vec(conv_ln_g, l),
                                                 vec(conv_ln_b, l), dm)
        delta, do = _attn_delta(dcat, sv["o"], dm)
        dq, dkv, dcq, dckt = _attn_bwd(sv["qkv"], do, sv["lse"], delta, sv["cum"], sv["cumt"], dm)
        dfl, d_bf = _gates_bwd(dcq, dckt, sv["cf"], vec(bf_pad, l), dm)
        dproj = jnp.concatenate([dq.astype(BF16), dkv, dcfc, dfl], axis=1)
        dh = _mm_nt(dproj, w_p, F32, "mm_dh")
        g_wp = _mm_tn(sv["h"], dproj, BF16, "mm_dwp")
        dx, d_sh1, d_sc1, d_gpre = _pre_bwd(dh, dx1, sv["xin"], vec(mix_pre_g, l), mod_me[l, 1], dm)
        g_nat = jnp.concatenate([g_wp[:, :3 * A], g_wp[:, 3 * A + 2 * C:3 * A + 2 * C + H], g_wp[:, 3 * A:3 * A + 2 * C]], axis=1)
        g_win = g_nat.reshape(D, N_CHIPS, d_in // N_CHIPS).transpose(1, 0, 2)
        big[l] = _exchange_start([g_win, g_wo.reshape(N_CHIPS, D // N_CHIPS, D), g_wfi,
                                  g_wfo.reshape(N_CHIPS, F // N_CHIPS, D)], True, dx, f"scatter_g_start_{l}")
        small[l] = dict(b_f=d_bf[0, :H], conv_b=d_cb[0], conv_ln_g=d_lg[0], conv_ln_b=d_lb[0], mix_pre_g=d_gpre[0],
                        mix_post_g=d_gpost[0], ffn_pre_g=d_gfpre[0], ffn_post_g=d_gfpost[0],
                        dmod=jnp.concatenate([d_sh1, d_sc1, d_g1, d_sh2, d_sc2, d_g2], axis=1)[0],
                        conv_w=d_cw[:CONV_K])
    grad_x = dx.reshape(1, S, D)

    names_big = ["w_in", "w_o", "w_ffn_in", "w_ffn_out"]
    res_big = {n: None for n in names_big}
    after = dx
    for l in reversed(range(L)):
        srcs, lands = _exchange_wait(big[l], after, f"scatter_g_wait_{l}")
        near = [lax.dynamic_update_index_in_dim(land, lax.dynamic_index_in_dim(src, chip, 0, keepdims=False), chip, 0)
                for land, src in zip(lands, srcs)]
        far = _sibling_forward(near, "forward_grads")
        for t, n in enumerate(names_big):
            res_big[n] = _adamw_shard(params[n], mom[n], var[n], near[t], far[t], l, res_big[n], f"adamw_{n}_{l}")
        after = res_big[names_big[-1]][1]

    keys_small = ["b_f", "conv_b", "conv_ln_g", "conv_ln_b", "mix_pre_g", "mix_post_g", "ffn_pre_g", "ffn_post_g",
                  "dmod", "conv_w"]
    stacked = [jnp.stack([small[l][k] for l in range(L)]) for k in keys_small]
    shapes = [s.shape for s in stacked]
    pack = _pack(stacked)
    pack8 = _all_gather_devices(pack, "gather_small")
    summed = dict(zip(keys_small, _unpack(_sum_devices(pack8), shapes)))
    dmod_all = jnp.stack([_unpack(pack8[d], shapes)[keys_small.index("dmod")] for d in range(N_DEV)])
    grads = {k: summed[k] for k in keys_small[:8]}
    grads["ada_b"] = summed["dmod"]
    grads["conv_w"] = lax.dynamic_slice_in_dim(summed["conv_w"], chip * cwc, cwc, axis=2)

    dmod_cols = lax.dynamic_slice_in_dim(dmod_all.reshape(N_DEV, L, N_CHIPS, n_ada), chip, 1, axis=2)
    dmod16 = jnp.pad(dmod_cols.reshape(N_DEV, L, n_ada).transpose(1, 0, 2), ((0, 0), (0, 16 - N_DEV), (0, 0))).astype(BF16)
    grads["ada_w"] = _ada_bwd(act16, dmod16)

    d_aw, m_aw, v_aw = _adamw_dense(ada_w.reshape(L * D, n_ada), m_ada_w.reshape(L * D, n_ada),
                                    v_ada_w.reshape(L * D, n_ada), grads["ada_w"].reshape(L * D, n_ada), "adamw_ada_w")
    names_small = ["b_f", "conv_w", "conv_b", "conv_ln_g", "conv_ln_b", "mix_pre_g", "mix_post_g", "ffn_pre_g",
                   "ffn_post_g", "ada_b"]
    shapes_small = [params[n].shape for n in names_small]
    d_s, m_s, v_s = _adamw_dense(_pack([params[n] for n in names_small]), _pack([mom[n] for n in names_small]),
                                 _pack([var[n] for n in names_small]), _pack([grads[n] for n in names_small]),
                                 "adamw_small")
    delta_w = dict(zip(names_small, _unpack(d_s, shapes_small)))
    new_m = dict(zip(names_small, _unpack(m_s, shapes_small)))
    new_v = dict(zip(names_small, _unpack(v_s, shapes_small)))
    delta_w["ada_w"], new_m["ada_w"], new_v["ada_w"] = (t.reshape(L, D, n_ada) for t in (d_aw, m_aw, v_aw))
    for n in names_big:
        grads[n], delta_w[n], new_m[n], new_v[n] = res_big[n]

    return (loss, grad_x, *[grads[n] for n in WEIGHTS], *[delta_w[n] for n in WEIGHTS],
            *[new_m[n] for n in WEIGHTS], *[new_v[n] for n in WEIGHTS])
```

```python
import collections
import functools

import jax
import jax.numpy as jnp
from jax import lax
from jax.experimental import pallas as pl
from jax.experimental.pallas import tpu as pltpu

F32 = jnp.float32
BF16 = jnp.bfloat16
MESH = pl.DeviceIdType.MESH

HEAD_DIM = 64
CONV_K = 31
N_MOD = 6
EPS = 1e-6
N_CHIPS = 4
N_DEV = 8
LANES = 128
HALO = 32
ROW_TILE = 256
MM_TM = 512
MM_TN_MAX = 1408
TN_TM = 256
VMEM_LIMIT = 56 * 1024 * 1024

ADAM_LR = 0.001
ADAM_B1 = 0.9
ADAM_B2 = 0.999
ADAM_EPS = 1e-08
ADAM_WD = 0.01
ADAM_STEP = 10

WEIGHTS = ['w_in', 'b_f', 'conv_w', 'conv_b', 'conv_ln_g', 'conv_ln_b', 'w_o', 'w_ffn_in', 'w_ffn_out',
           'mix_pre_g', 'mix_post_g', 'ffn_pre_g', 'ffn_post_g', 'ada_w', 'ada_b']

Dims = collections.namedtuple("Dims", "S D A C H F L NP ts")


def _cp(n_grid=0):
    if n_grid:
        return pltpu.CompilerParams(dimension_semantics=("arbitrary",) * n_grid, vmem_limit_bytes=VMEM_LIMIT)
    return pltpu.CompilerParams(vmem_limit_bytes=VMEM_LIMIT)


def _tile(n, cap, also=None):
    best = None
    t = LANES
    while t <= min(n, cap):
        if n % t == 0 and (also is None or also % t == 0):
            best = t
        t += LANES
    assert best is not None, (n, cap, also)
    return best


def _bf(v):
    return v if v.dtype == BF16 else v.astype(BF16)


def _place():
    return lax.axis_index("x"), lax.axis_index("y"), lax.axis_index("c")


def _flip(v, d):
    return 1 - v if d else v


def _all_gather_devices(a, name):
    def body(a_ref, o_ref, send, recv, lsem):
        x, y, c = _place()
        me = 4 * x + 2 * y + c
        local = pltpu.make_async_copy(a_ref, o_ref.at[me], lsem)
        local.start()
        copies = []
        for k in range(1, N_DEV):
            peer = (_flip(x, (k >> 2) & 1), _flip(y, (k >> 1) & 1), _flip(c, k & 1))
            cp = pltpu.make_async_remote_copy(src_ref=a_ref, dst_ref=o_ref.at[me], send_sem=send.at[k - 1],
                                              recv_sem=recv.at[k - 1], device_id=peer, device_id_type=MESH)
            cp.start()
            copies.append(cp)
        for cp in copies:
            cp.wait()
        local.wait()

    return pl.pallas_call(
        body, name=name,
        out_shape=jax.ShapeDtypeStruct((N_DEV,) + a.shape, a.dtype),
        in_specs=[pl.BlockSpec(memory_space=pl.ANY)],
        out_specs=pl.BlockSpec(memory_space=pl.ANY),
        scratch_shapes=[pltpu.SemaphoreType.DMA((N_DEV - 1,)), pltpu.SemaphoreType.DMA((N_DEV - 1,)),
                        pltpu.SemaphoreType.DMA],
    )(a)


def _all_gather_chips(arrays, name):
    n = len(arrays)

    def body(*refs):
        a_refs, o_refs = refs[:n], refs[n:2 * n]
        send, recv, lsem = refs[2 * n:]
        x, y, c = _place()
        me = 2 * x + y
        copies = []
        for i in range(n):
            local = pltpu.make_async_copy(a_refs[i], o_refs[i].at[me], lsem.at[i])
            local.start()
            copies.append(local)
            for k in range(1, N_CHIPS):
                peer = (_flip(x, (k >> 1) & 1), _flip(y, k & 1), c)
                cp = pltpu.make_async_remote_copy(src_ref=a_refs[i], dst_ref=o_refs[i].at[me],
                                                  send_sem=send.at[i, k - 1], recv_sem=recv.at[i, k - 1],
                                                  device_id=peer, device_id_type=MESH)
                cp.start()
                copies.append(cp)
        for cp in copies:
            cp.wait()

    return pl.pallas_call(
        body, name=name,
        out_shape=[jax.ShapeDtypeStruct((N_CHIPS,) + a.shape, a.dtype) for a in arrays],
        in_specs=[pl.BlockSpec(memory_space=pl.ANY)] * n,
        out_specs=[pl.BlockSpec(memory_space=pl.ANY)] * n,
        scratch_shapes=[pltpu.SemaphoreType.DMA((n, N_CHIPS - 1)), pltpu.SemaphoreType.DMA((n, N_CHIPS - 1)),
                        pltpu.SemaphoreType.DMA((n,))],
    )(*arrays)


_HBM = pl.BlockSpec(memory_space=pltpu.HBM)
_SEM = pl.BlockSpec(memory_space=pltpu.SEMAPHORE)
_EFFECT = pltpu.SideEffectType.DATAFLOW_SIDE_EFFECTING


def _chip_copies(srcs, lands, send, recv, scatter):
    x, y, c = _place()
    me = 2 * x + y
    copies = []
    for i in range(len(srcs)):
        for k in range(1, N_CHIPS):
            px, py = _flip(x, (k >> 1) & 1), _flip(y, k & 1)
            src = srcs[i].at[2 * px + py] if scatter else srcs[i]
            s = i * (N_CHIPS - 1) + k - 1
            copies.append(pltpu.make_async_remote_copy(src_ref=src, dst_ref=lands[i].at[me], send_sem=send.at[s],
                                                       recv_sem=recv.at[s], device_id=(px, py, c), device_id_type=MESH))
    return copies


def _exchange_start(arrays, scatter, after, name):
    n = len(arrays)

    def body(*refs):
        srcs, lands = refs[:n], refs[n:2 * n]
        send, recv = refs[2 * n + 1], refs[2 * n + 2]
        token = refs[-1]
        for cp in _chip_copies(srcs, lands, send, recv, scatter):
            cp.start()
        token[...] = jnp.zeros_like(token)

    land_shapes = [a.shape if scatter else (N_CHIPS,) + a.shape for a in arrays]
    outs = pl.pallas_call(
        body, name=name,
        out_shape=(pltpu.SemaphoreType.DMA((n * (N_CHIPS - 1),)), pltpu.SemaphoreType.DMA((n * (N_CHIPS - 1),)),
                   *[pltpu.HBM(a.shape, a.dtype) for a in arrays],
                   *[pltpu.HBM(s, a.dtype) for s, a in zip(land_shapes, arrays)],
                   jax.ShapeDtypeStruct((8, LANES), F32)),
        in_specs=[_HBM] * (2 * n) + [pl.BlockSpec(memory_space=pl.ANY)],
        out_specs=(_SEM, _SEM, *[_HBM] * (2 * n), pl.BlockSpec(memory_space=pltpu.VMEM)),
        input_output_aliases={i: 2 + i for i in range(2 * n)},
        compiler_params=pltpu.CompilerParams(has_side_effects=_EFFECT),
    )(*[pltpu.with_memory_space_constraint(a, pltpu.HBM) for a in arrays],
      *[pltpu.with_memory_space_constraint(lax.empty(s, a.dtype), pltpu.HBM) for s, a in zip(land_shapes, arrays)],
      after)
    return dict(send=outs[0], recv=outs[1], srcs=outs[2:2 + n], lands=outs[2 + n:2 + 2 * n], token=outs[-1],
                scatter=scatter)


def _exchange_wait(st, after, name):
    n = len(st["srcs"])
    scatter = st["scatter"]

    def body(*refs):
        srcs, lands = refs[:n], refs[n:2 * n]
        send, recv = refs[2 * n], refs[2 * n + 1]
        for cp in _chip_copies(srcs, lands, send, recv, scatter):
            cp.wait_send()
            cp.wait_recv()

    outs = pl.pallas_call(
        body, name=name,
        out_shape=tuple(pltpu.HBM(a.shape, a.dtype) for a in (*st["srcs"], *st["lands"])),
        in_specs=[_HBM] * (2 * n) + [_SEM, _SEM, pl.BlockSpec(memory_space=pl.ANY)],
        out_specs=tuple([_HBM] * (2 * n)),
        input_output_aliases={i: i for i in range(2 * n)},
        compiler_params=pltpu.CompilerParams(has_side_effects=_EFFECT),
    )(*st["srcs"], *st["lands"], st["send"], st["recv"], after)
    return outs[:n], outs[n:]


def _sibling_forward(arrays, name):
    n = len(arrays)

    def body(*refs):
        srcs, outs = refs[:n], refs[n:2 * n]
        send, recv = refs[2 * n:]
        x, y, c = _place()
        copies = []
        for i in range(n):
            cp = pltpu.make_async_remote_copy(src_ref=srcs[i], dst_ref=outs[i], send_sem=send.at[i], recv_sem=recv.at[i],
                                              device_id=(x, y, 1 - c), device_id_type=MESH)
            cp.start()
            copies.append(cp)
        for cp in copies:
            cp.wait()

    return pl.pallas_call(
        body, name=name,
        out_shape=[jax.ShapeDtypeStruct(a.shape, a.dtype) for a in arrays],
        in_specs=[pl.BlockSpec(memory_space=pl.ANY)] * n,
        out_specs=[pl.BlockSpec(memory_space=pl.ANY)] * n,
        scratch_shapes=[pltpu.SemaphoreType.DMA((n,)), pltpu.SemaphoreType.DMA((n,))],
    )(*arrays)


def _matmul(a, b, contract, grid, a_spec, b_spec, o_spec, out_shape, name, nk=1, acc_shape=None):
    def body(a_ref, b_ref, o_ref, *acc):
        r = lax.dot_general(_bf(a_ref[...]), _bf(b_ref[...]), (contract, ((), ())), preferred_element_type=F32)
        if nk == 1:
            o_ref[...] = r.astype(o_ref.dtype)
        else:
            k = pl.program_id(len(grid) - 1)

            @pl.when(k == 0)
            def _():
                acc[0][...] = r

            @pl.when(k > 0)
            def _():
                acc[0][...] += r

            @pl.when(k == nk - 1)
            def _():
                o_ref[...] = acc[0][...].astype(o_ref.dtype)

    return pl.pallas_call(
        body, name=name, grid=grid, in_specs=[a_spec, b_spec], out_specs=o_spec, out_shape=out_shape,
        scratch_shapes=[pltpu.VMEM(acc_shape, F32)] if nk > 1 else [],
        compiler_params=_cp(len(grid)),
    )(a, b)


def _mm_nn(a, b, out_dtype, name, col0=0, n=None):
    m, k = a.shape
    n = b.shape[1] - col0 if n is None else n
    tm = min(MM_TM, m)
    tn = _tile(n, MM_TN_MAX, also=col0 if col0 else None)
    off = col0 // tn
    return _matmul(a, b, ((1,), (0,)), (n // tn, m // tm),
                   pl.BlockSpec((tm, k), lambda j, i: (i, 0)),
                   pl.BlockSpec((k, tn), lambda j, i: (0, j + off)),
                   pl.BlockSpec((tm, tn), lambda j, i: (i, j)),
                   jax.ShapeDtypeStruct((m, n), out_dtype), name)


def _mm_nn_blocked(a, b3, out_dtype, name):
    m, k = a.shape
    nj, _, nb = b3.shape
    tm = min(MM_TM, m)
    return _matmul(a, b3, ((1,), (0,)), (nj, m // tm),
                   pl.BlockSpec((tm, k), lambda j, i: (i, 0)),
                   pl.BlockSpec((None, k, nb), lambda j, i: (j, 0, 0)),
                   pl.BlockSpec((tm, nb), lambda j, i: (i, j)),
                   jax.ShapeDtypeStruct((m, nj * nb), out_dtype), name)


def _mm_nt(a, b, out_dtype, name):
    m, k = a.shape
    n = b.shape[0]
    tm = min(MM_TM, m)
    tn = _tile(n, MM_TN_MAX)
    return _matmul(a, b, ((1,), (1,)), (n // tn, m // tm),
                   pl.BlockSpec((tm, k), lambda j, i: (i, 0)),
                   pl.BlockSpec((tn, k), lambda j, i: (j, 0)),
                   pl.BlockSpec((tm, tn), lambda j, i: (i, j)),
                   jax.ShapeDtypeStruct((m, n), out_dtype), name)


def _mm_nt_blocked(a, b3, out_dtype, name):
    m = a.shape[0]
    nj, n, nb = b3.shape
    tm = min(MM_TM, m)
    tn = _tile(n, 512)
    return _matmul(a, b3, ((1,), (1,)), (n // tn, m // tm, nj),
                   pl.BlockSpec((tm, nb), lambda j, i, k: (i, k)),
                   pl.BlockSpec((None, tn, nb), lambda j, i, k: (k, j, 0)),
                   pl.BlockSpec((tm, tn), lambda j, i, k: (i, j)),
                   jax.ShapeDtypeStruct((m, n), out_dtype), name, nk=nj, acc_shape=(tm, tn))


def _mm_tn(a, b, out_dtype, name, blocked=False):
    k, m = a.shape
    n = b.shape[1]
    tm = min(TN_TM, m)
    tn = n // N_CHIPS if blocked else _tile(n, MM_TN_MAX)
    if blocked:
        o_spec = pl.BlockSpec((None, tm, tn), lambda j, i: (j, i, 0))
        out_shape = jax.ShapeDtypeStruct((n // tn, m, tn), out_dtype)
    else:
        o_spec = pl.BlockSpec((tm, tn), lambda j, i: (i, j))
        out_shape = jax.ShapeDtypeStruct((m, n), out_dtype)
    return _matmul(a, b, ((0,), (0,)), (n // tn, m // tm),
                   pl.BlockSpec((k, tm), lambda j, i: (0, i)),
                   pl.BlockSpec((k, tn), lambda j, i: (0, j)),
                   o_spec, out_shape, name)


def _vec_spec(d):
    return pl.BlockSpec((1, d), lambda i: (0, 0))


def _row_spec(ts, d, col=0):
    return pl.BlockSpec((ts, d), lambda i: (i, col))


def _rms(x):
    return lax.rsqrt(jnp.mean(x * x, axis=-1, keepdims=True) + EPS)


def _pre_norm(x, gain, scale, shift, dm):
    def body(x_ref, g_ref, sc_ref, sh_ref, h_ref):
        xv = x_ref[...]
        h_ref[...] = (((xv * _rms(xv)) * g_ref[...]) * (1.0 + sc_ref[...]) + sh_ref[...]).astype(h_ref.dtype)

    return pl.pallas_call(
        body, name="pre_norm", grid=(dm.S // dm.ts,),
        in_specs=[_row_spec(dm.ts, dm.D)] + [_vec_spec(dm.D)] * 3,
        out_specs=_row_spec(dm.ts, dm.D),
        out_shape=jax.ShapeDtypeStruct((dm.S, dm.D), BF16), compiler_params=_cp(1),
    )(x, gain, scale, shift)


def _res_norm(x, y, gpost, gate, gain, scale, shift, dm):
    def body(x_ref, y_ref, gp_ref, gt_ref, g_ref, sc_ref, sh_ref, xo_ref, h_ref):
        yv = y_ref[...]
        xn = x_ref[...] + gt_ref[...] * ((yv * _rms(yv)) * gp_ref[...])
        xo_ref[...] = xn
        h_ref[...] = (((xn * _rms(xn)) * g_ref[...]) * (1.0 + sc_ref[...]) + sh_ref[...]).astype(h_ref.dtype)

    return pl.pallas_call(
        body, name="res_norm", grid=(dm.S // dm.ts,),
        in_specs=[_row_spec(dm.ts, dm.D)] * 2 + [_vec_spec(dm.D)] * 5,
        out_specs=[_row_spec(dm.ts, dm.D)] * 2,
        out_shape=[jax.ShapeDtypeStruct((dm.S, dm.D), F32), jax.ShapeDtypeStruct((dm.S, dm.D), BF16)],
        compiler_params=_cp(1),
    )(x, y, gpost, gate, gain, scale, shift)


def _res_loss(x, y, gpost, gate, target, dm):
    def body(x_ref, y_ref, gp_ref, gt_ref, t_ref, dx_ref, loss_ref):
        i = pl.program_id(0)

        @pl.when(i == 0)
        def _():
            loss_ref[...] = jnp.zeros_like(loss_ref)
        yv = y_ref[...]
        err = x_ref[...] + gt_ref[...] * ((yv * _rms(yv)) * gp_ref[...]) - t_ref[...]
        dx_ref[...] = err * (1.0 / dm.D)
        per_row = jnp.mean(err * err, axis=-1, keepdims=True)
        loss_ref[...] += 0.5 * jnp.sum(per_row, axis=0, keepdims=True)

    return pl.pallas_call(
        body, name="res_loss", grid=(dm.S // dm.ts,),
        in_specs=[_row_spec(dm.ts, dm.D)] * 2 + [_vec_spec(dm.D)] * 2 + [_row_spec(dm.ts, dm.D)],
        out_specs=[_row_spec(dm.ts, dm.D), _vec_spec(LANES)],
        out_shape=[jax.ShapeDtypeStruct((dm.S, dm.D), F32), jax.ShapeDtypeStruct((1, LANES), F32)],
        compiler_params=_cp(1),
    )(x, y, gpost, gate, target)


def _post_bwd(dxo, y, gpost, gate, dm):
    def body(dx_ref, y_ref, gp_ref, gt_ref, dy_ref, dgp_ref, dgt_ref):
        i = pl.program_id(0)

        @pl.when(i == 0)
        def _():
            dgp_ref[...] = jnp.zeros_like(dgp_ref)
            dgt_ref[...] = jnp.zeros_like(dgt_ref)
        yv, dx = y_ref[...], dx_ref[...]
        r = _rms(yv)
        t = yv * r
        dgp_ref[...] += jnp.sum(dx * gt_ref[...] * t, axis=0, keepdims=True)
        dgt_ref[...] += jnp.sum(dx * (t * gp_ref[...]), axis=0, keepdims=True)
        dt = dx * (gt_ref[...] * gp_ref[...])
        dy_ref[...] = (r * (dt - t * jnp.mean(dt * t, axis=-1, keepdims=True))).astype(dy_ref.dtype)

    return pl.pallas_call(
        body, name="post_bwd", grid=(dm.S // dm.ts,),
        in_specs=[_row_spec(dm.ts, dm.D)] * 2 + [_vec_spec(dm.D)] * 2,
        out_specs=[_row_spec(dm.ts, dm.D), _vec_spec(dm.D), _vec_spec(dm.D)],
        out_shape=[jax.ShapeDtypeStruct((dm.S, dm.D), BF16)] + [jax.ShapeDtypeStruct((1, dm.D), F32)] * 2,
        compiler_params=_cp(1),
    )(dxo, y, gpost, gate)


def _pre_bwd(dh, dxo, x, gain, scale, dm):
    def body(dh_ref, dxo_ref, x_ref, g_ref, sc_ref, dx_ref, dsh_ref, dsc_ref, dg_ref):
        i = pl.program_id(0)

        @pl.when(i == 0)
        def _():
            dsh_ref[...] = jnp.zeros_like(dsh_ref)
            dsc_ref[...] = jnp.zeros_like(dsc_ref)
            dg_ref[...] = jnp.zeros_like(dg_ref)
        xv, dh_ = x_ref[...], dh_ref[...]
        r = _rms(xv)
        nrm = xv * r
        one_sc = 1.0 + sc_ref[...]
        dsh_ref[...] += jnp.sum(dh_, axis=0, keepdims=True)
        dsc_ref[...] += jnp.sum(dh_ * (nrm * g_ref[...]), axis=0, keepdims=True)
        dg_ref[...] += jnp.sum(dh_ * nrm * one_sc, axis=0, keepdims=True)
        dn = dh_ * (g_ref[...] * one_sc)
        dx_ref[...] = dxo_ref[...] + r * (dn - nrm * jnp.mean(dn * nrm, axis=-1, keepdims=True))

    return pl.pallas_call(
        body, name="pre_bwd", grid=(dm.S // dm.ts,),
        in_specs=[_row_spec(dm.ts, dm.D)] * 3 + [_vec_spec(dm.D)] * 2,
        out_specs=[_row_spec(dm.ts, dm.D)] + [_vec_spec(dm.D)] * 3,
        out_shape=[jax.ShapeDtypeStruct((dm.S, dm.D), F32)] + [jax.ShapeDtypeStruct((1, dm.D), F32)] * 3,
        compiler_params=_cp(1),
    )(dh, dxo, x, gain, scale)


def _sigmoid(z):
    return 1.0 / (1.0 + jnp.exp(-z))


def _swiglu_fwd(gu, dm):
    def body(g_ref, u_ref, a_ref):
        g = g_ref[...]
        a_ref[...] = (g * _sigmoid(g) * u_ref[...]).astype(a_ref.dtype)

    return pl.pallas_call(
        body, name="swiglu_fwd", grid=(dm.S // dm.ts,),
        in_specs=[_row_spec(dm.ts, dm.F, 0), _row_spec(dm.ts, dm.F, 1)],
        out_specs=_row_spec(dm.ts, dm.F),
        out_shape=jax.ShapeDtypeStruct((dm.S, dm.F), BF16), compiler_params=_cp(1),
    )(gu, gu)


def _swiglu_bwd(da, gu, dm):
    def body(da_ref, g_ref, u_ref, d_ref):
        g, u, da_ = g_ref[...], u_ref[...], da_ref[...]
        sg = _sigmoid(g)
        d_ref[:, :dm.F] = (da_ * u * (sg * (1.0 + g * (1.0 - sg)))).astype(d_ref.dtype)
        d_ref[:, dm.F:] = (da_ * (g * sg)).astype(d_ref.dtype)

    return pl.pallas_call(
        body, name="swiglu_bwd", grid=(dm.S // dm.ts,),
        in_specs=[_row_spec(dm.ts, dm.F), _row_spec(dm.ts, dm.F, 0), _row_spec(dm.ts, dm.F, 1)],
        out_specs=_row_spec(dm.ts, 2 * dm.F),
        out_shape=jax.ShapeDtypeStruct((dm.S, 2 * dm.F), BF16), compiler_params=_cp(1),
    )(da, gu, gu)


def _tri(n, upper):
    r = lax.broadcasted_iota(jnp.int32, (n, n), 0)
    c = lax.broadcasted_iota(jnp.int32, (n, n), 1)
    return (c >= r if upper else r >= c).astype(F32)


def _gates_fwd(cf, bf, dm):
    ts = dm.ts
    fcol = 2 * dm.C // LANES

    def body(f_ref, b_ref, cum_ref, cumt_ref, carry):
        i = pl.program_id(0)

        @pl.when(i == 0)
        def _():
            carry[...] = jnp.zeros_like(carry)
        z = f_ref[...] + b_ref[...]
        lf = jnp.minimum(z, 0.0) - jnp.log(1.0 + jnp.exp(-jnp.abs(z)))
        cs = jnp.dot(_tri(ts, False), lf, precision=lax.Precision.HIGHEST, preferred_element_type=F32) + carry[...]
        cum_ref[...] = cs
        cumt_ref[...] = cs.T[:8, :]
        carry[...] = cs[ts - 1:ts, :]

    return pl.pallas_call(
        body, name="gates_fwd", grid=(dm.S // ts,),
        in_specs=[pl.BlockSpec((ts, LANES), lambda i: (i, fcol)), _vec_spec(LANES)],
        out_specs=[_row_spec(ts, LANES), pl.BlockSpec((None, 8, ts), lambda i: (i, 0, 0))],
        out_shape=[jax.ShapeDtypeStruct((dm.S, LANES), F32), jax.ShapeDtypeStruct((dm.S // ts, 8, ts), F32)],
        scratch_shapes=[pltpu.VMEM((1, LANES), F32)], compiler_params=_cp(1),
    )(cf, bf)


def _gates_bwd(dcq, dckt, cf, bf, dm):
    ts = dm.ts
    nb = dm.S // ts
    fcol = 2 * dm.C // LANES

    def body(dcq_ref, dck_ref, f_ref, b_ref, df_ref, db_ref, carry):
        i = pl.program_id(0)

        @pl.when(i == 0)
        def _():
            carry[...] = jnp.zeros_like(carry)
            db_ref[...] = jnp.zeros_like(db_ref)
        dck = jnp.concatenate([dck_ref[...], jnp.zeros((LANES - 8, ts), F32)], axis=0).T
        dc = dcq_ref[...] + dck
        dlf = jnp.dot(_tri(ts, True), dc, precision=lax.Precision.HIGHEST, preferred_element_type=F32) + carry[...]
        carry[...] = dlf[0:1, :]
        dz = dlf * (1.0 - _sigmoid(f_ref[...] + b_ref[...]))
        df_ref[...] = dz.astype(df_ref.dtype)
        db_ref[...] += jnp.sum(dz, axis=0, keepdims=True)

    return pl.pallas_call(
        body, name="gates_bwd", grid=(nb,),
        in_specs=[pl.BlockSpec((ts, LANES), lambda i: (nb - 1 - i, 0)),
                  pl.BlockSpec((None, 8, ts), lambda i: (nb - 1 - i, 0, 0)),
                  pl.BlockSpec((ts, LANES), lambda i: (nb - 1 - i, fcol)), _vec_spec(LANES)],
        out_specs=[pl.BlockSpec((ts, LANES), lambda i: (nb - 1 - i, 0)), _vec_spec(LANES)],
        out_shape=[jax.ShapeDtypeStruct((dm.S, LANES), BF16), jax.ShapeDtypeStruct((1, LANES), F32)],
        scratch_shapes=[pltpu.VMEM((1, LANES), F32)], compiler_params=_cp(1),
    )(dcq, dckt, cf, bf)


def _dot_nt(a, b):
    return lax.dot_general(a, b, (((1,), (1,)), ((), ())), preferred_element_type=F32)


def _dot_tn(a, b):
    return lax.dot_general(a, b, (((0,), (0,)), ((), ())), preferred_element_type=F32)


def _attn_fwd(qkv, cum, cumt, dm):
    tq = dm.ts
    A, H = dm.A, dm.H
    scale = HEAD_DIM ** -0.5

    def body(q_ref, kv_ref, cq_ref, ckt_ref, o_ref, lse_ref):
        i = pl.program_id(0)
        row = lax.broadcasted_iota(jnp.int32, (tq, tq), 0)
        col = lax.broadcasted_iota(jnp.int32, (tq, tq), 1)
        lane = lax.broadcasted_iota(jnp.int32, (tq, LANES), 1)
        lse_all = jnp.zeros((tq, LANES), F32)
        for h in range(H):
            lo, hi = h * HEAD_DIM, (h + 1) * HEAD_DIM
            q = q_ref[:, lo:hi]
            cq = cq_ref[:, h:h + 1]

            def step(j, carry, lo=lo, hi=hi, q=q, cq=cq, h=h):
                m, l, acc = carry
                r0 = pl.multiple_of(j * tq, tq)
                k = kv_ref[pl.ds(r0, tq), A + lo:A + hi]
                v = kv_ref[pl.ds(r0, tq), 2 * A + lo:2 * A + hi]
                ck = ckt_ref[j, h:h + 1, :]
                s = _dot_nt(q, k) * scale + cq - ck
                s = jnp.where(row + (i - j) * tq >= col, s, -1e30)
                m_new = jnp.maximum(m, jnp.max(s, axis=1, keepdims=True))
                p = jnp.exp(s - m_new)
                alpha = jnp.exp(m - m_new)
                l = alpha * l + jnp.sum(p, axis=1, keepdims=True)
                acc = alpha * acc + jnp.dot(p.astype(BF16), v, preferred_element_type=F32)
                return m_new, l, acc

            m, l, acc = lax.fori_loop(0, i + 1, step, (jnp.full((tq, 1), -1e30, F32), jnp.zeros((tq, 1), F32),
                                                       jnp.zeros((tq, HEAD_DIM), F32)))
            o_ref[:, lo:hi] = (acc / l).astype(o_ref.dtype)
            lse_all = jnp.where(lane == h, m + jnp.log(l), lse_all)
        lse_ref[...] = lse_all

    nq = dm.S // tq
    return pl.pallas_call(
        body, name="attn_fwd", grid=(nq,),
        in_specs=[pl.BlockSpec((tq, A), lambda i: (i, 0)), pl.BlockSpec((dm.S, 3 * A), lambda i: (0, 0)),
                  _row_spec(tq, LANES), pl.BlockSpec((nq, 8, tq), lambda i: (0, 0, 0))],
        out_specs=[pl.BlockSpec((tq, A), lambda i: (i, 0)), _row_spec(tq, LANES)],
        out_shape=[jax.ShapeDtypeStruct((dm.S, A), BF16), jax.ShapeDtypeStruct((dm.S, LANES), F32)],
        compiler_params=_cp(1),
    )(qkv, qkv, cum, cumt)


def _attn_delta(dcat, o, dm):
    ts = dm.ts

    def body(do_ref, o_ref, dl_ref, dob_ref):
        lane = lax.broadcasted_iota(jnp.int32, (ts, LANES), 1)
        prod = do_ref[...] * o_ref[...].astype(F32)
        dl = jnp.zeros((ts, LANES), F32)
        for h in range(dm.H):
            dl = jnp.where(lane == h, jnp.sum(prod[:, h * HEAD_DIM:(h + 1) * HEAD_DIM], axis=1, keepdims=True), dl)
        dl_ref[...] = dl
        dob_ref[...] = do_ref[...].astype(dob_ref.dtype)

    return pl.pallas_call(
        body, name="attn_delta", grid=(dm.S // ts,),
        in_specs=[_row_spec(ts, dm.A, 0), _row_spec(ts, dm.A)],
        out_specs=[_row_spec(ts, LANES), _row_spec(ts, dm.A)],
        out_shape=[jax.ShapeDtypeStruct((dm.S, LANES), F32), jax.ShapeDtypeStruct((dm.S, dm.A), BF16)],
        compiler_params=_cp(1),
    )(dcat, o)


def _attn_bwd(qkv, do, lse, delta, cum, cumt, dm):
    tq = dm.ts
    A, H = dm.A, dm.H
    nq = dm.S // tq
    scale = HEAD_DIM ** -0.5

    def body(kv_ref, q_ref, do_ref, lse_ref, dl_ref, cq_ref, ckt_ref, dq_ref, dkv_ref, dcq_ref, dckt_ref):
        j = pl.program_id(0)

        @pl.when(j == 0)
        def _():
            dq_ref[...] = jnp.zeros_like(dq_ref)
            dcq_ref[...] = jnp.zeros_like(dcq_ref)
        row = lax.broadcasted_iota(jnp.int32, (tq, tq), 0)
        col = lax.broadcasted_iota(jnp.int32, (tq, tq), 1)
        lane = lax.broadcasted_iota(jnp.int32, (tq, LANES), 1)
        sub = lax.broadcasted_iota(jnp.int32, (8, tq), 0)
        dck_all = jnp.zeros((8, tq), F32)
        for h in range(H):
            lo, hi = h * HEAD_DIM, (h + 1) * HEAD_DIM
            k = kv_ref[:, A + lo:A + hi]
            v = kv_ref[:, 2 * A + lo:2 * A + hi]
            ck = ckt_ref[h:h + 1, :]

            def step(i, carry, lo=lo, hi=hi, k=k, v=v, ck=ck, h=h):
                dk, dv, dck = carry
                r0 = pl.multiple_of(i * tq, tq)
                rows = pl.ds(r0, tq)
                q = q_ref[rows, lo:hi]
                do_ = do_ref[rows, lo:hi]
                s = _dot_nt(q, k) * scale + cq_ref[rows, h:h + 1] - ck
                s = jnp.where(row + (i - j) * tq >= col, s, -1e30)
                p = jnp.exp(s - lse_ref[rows, h:h + 1])
                ds = p * (_dot_nt(do_, v) - dl_ref[rows, h:h + 1])
                dsb = ds.astype(BF16)
                dv = dv + _dot_tn(p.astype(BF16), do_)
                dk = dk + _dot_tn(dsb, q)
                dq_ref[rows, lo:hi] += jnp.dot(dsb, k, preferred_element_type=F32)
                dcq_ref[rows, :] += jnp.where(lane == h, jnp.sum(ds, axis=1, keepdims=True), 0.0)
                dck = dck - jnp.sum(ds, axis=0, keepdims=True)
                return dk, dv, dck

            dk, dv, dck = lax.fori_loop(j, nq, step, (jnp.zeros((tq, HEAD_DIM), F32), jnp.zeros((tq, HEAD_DIM), F32),
                                                      jnp.zeros((1, tq), F32)))
            dkv_ref[:, lo:hi] = (dk * scale).astype(dkv_ref.dtype)
            dkv_ref[:, A + lo:A + hi] = dv.astype(dkv_ref.dtype)
            dck_all = jnp.where(sub == h, dck, dck_all)
        dckt_ref[...] = dck_all

        @pl.when(j == nq - 1)
        def _():
            dq_ref[...] = dq_ref[...] * scale

    full = lambda w: pl.BlockSpec((dm.S, w), lambda j: (0, 0))
    return pl.pallas_call(
        body, name="attn_bwd", grid=(nq,),
        in_specs=[pl.BlockSpec((tq, 3 * A), lambda j: (j, 0)), full(A), full(A), full(LANES), full(LANES), full(LANES),
                  pl.BlockSpec((None, 8, tq), lambda j: (j, 0, 0))],
        out_specs=[full(A), pl.BlockSpec((tq, 2 * A), lambda j: (j, 0)), full(LANES),
                   pl.BlockSpec((None, 8, tq), lambda j: (j, 0, 0))],
        out_shape=[jax.ShapeDtypeStruct((dm.S, A), F32), jax.ShapeDtypeStruct((dm.S, 2 * A), BF16),
                   jax.ShapeDtypeStruct((dm.S, LANES), F32), jax.ShapeDtypeStruct((nq, 8, tq), F32)],
        compiler_params=_cp(1),
    )(qkv, qkv, do, lse, delta, cum, cumt)


def _glu(cf_rows, c):
    return cf_rows[:, :c] * _sigmoid(cf_rows[:, c:2 * c])


def _conv_fwd(cf, cw, cb, lg, lb, dm):
    ts, C = dm.ts, dm.C
    per = ts // HALO

    def body(cf_ref, halo_ref, w_ref, cb_ref, lg_ref, lb_ref, u3_ref, u1_ref):
        i = pl.program_id(0)
        prev = jnp.where(i > 0, _glu(halo_ref[...], C), 0.0)
        win = jnp.concatenate([prev, _glu(cf_ref[...], C)], axis=0)
        u1 = jnp.zeros((ts, C), F32) + cb_ref[...]
        off = HALO - (CONV_K - 1)
        for k in range(CONV_K):
            u1 = u1 + w_ref[k:k + 1, :] * win[off + k:off + k + ts, :]
        u1_ref[...] = u1
        mu = jnp.mean(u1, axis=-1, keepdims=True)
        cen = u1 - mu
        rstd = lax.rsqrt(jnp.mean(cen * cen, axis=-1, keepdims=True) + EPS)
        u2 = cen * rstd * lg_ref[...] + lb_ref[...]
        u3_ref[...] = (u2 * _sigmoid(u2)).astype(u3_ref.dtype)

    return pl.pallas_call(
        body, name="conv_fwd", grid=(dm.S // ts,),
        in_specs=[pl.BlockSpec((ts, 2 * C), lambda i: (i, 0)),
                  pl.BlockSpec((HALO, 2 * C), lambda i: (jnp.maximum(i * per - 1, 0), 0)),
                  pl.BlockSpec((HALO, C), lambda i: (0, 0))] + [_vec_spec(C)] * 3,
        out_specs=[_row_spec(ts, C)] * 2,
        out_shape=[jax.ShapeDtypeStruct((dm.S, C), BF16), jax.ShapeDtypeStruct((dm.S, C), F32)],
        compiler_params=_cp(1),
    )(cf, cf, cw, cb, lg, lb)


def _conv_bwd(dcat, u1, cf, cw, lg, lb, dm):
    ts, C = dm.ts, dm.C
    per = ts // HALO
    nt = dm.S // ts
    last_halo = dm.S // HALO - 1

    def ln_bwd(du3, u1v, lg_v, lb_v):
        mu = jnp.mean(u1v, axis=-1, keepdims=True)
        cen = u1v - mu
        rstd = lax.rsqrt(jnp.mean(cen * cen, axis=-1, keepdims=True) + EPS)
        uhat = cen * rstd
        u2 = uhat * lg_v + lb_v
        sg = _sigmoid(u2)
        du2 = du3 * (sg * (1.0 + u2 * (1.0 - sg)))
        duh = du2 * lg_v
        du1 = rstd * (duh - jnp.mean(duh, axis=-1, keepdims=True) - uhat * jnp.mean(duh * uhat, axis=-1, keepdims=True))
        return du1, du2, uhat

    def body(d_ref, dn_ref, u1_ref, u1n_ref, cf_ref, halo_ref, w_ref, lg_ref, lb_ref,
             dcf_ref, dw_ref, dcb_ref, dlg_ref, dlb_ref):
        i = pl.program_id(0)

        @pl.when(i == 0)
        def _():
            dw_ref[...] = jnp.zeros_like(dw_ref)
            dcb_ref[...] = jnp.zeros_like(dcb_ref)
            dlg_ref[...] = jnp.zeros_like(dlg_ref)
            dlb_ref[...] = jnp.zeros_like(dlb_ref)
        lg_v, lb_v = lg_ref[...], lb_ref[...]
        du1, du2, uhat = ln_bwd(d_ref[...], u1_ref[...], lg_v, lb_v)
        du1n, _, _ = ln_bwd(dn_ref[...], u1n_ref[...], lg_v, lb_v)
        du1n = jnp.where(i < nt - 1, du1n, 0.0)
        dlg_ref[...] += jnp.sum(du2 * uhat, axis=0, keepdims=True)
        dlb_ref[...] += jnp.sum(du2, axis=0, keepdims=True)
        dcb_ref[...] += jnp.sum(du1, axis=0, keepdims=True)
        dwin = jnp.concatenate([du1, du1n], axis=0)
        cfv = cf_ref[...]
        cv, sg = cfv[:, :C], _sigmoid(cfv[:, C:2 * C])
        prev = jnp.where(i > 0, _glu(halo_ref[...], C), 0.0)
        uwin = jnp.concatenate([prev, cv * sg], axis=0)
        du0 = jnp.zeros((ts, C), F32)
        off = HALO - (CONV_K - 1)
        for k in range(CONV_K):
            back = CONV_K - 1 - k
            du0 = du0 + w_ref[k:k + 1, :] * dwin[back:back + ts, :]
            dw_ref[k:k + 1, :] += jnp.sum(du1 * uwin[off + k:off + k + ts, :], axis=0, keepdims=True)
        dcf_ref[:, :C] = (du0 * sg).astype(dcf_ref.dtype)
        dcf_ref[:, C:] = (du0 * cv * sg * (1.0 - sg)).astype(dcf_ref.dtype)

    ucol = dm.A // C
    return pl.pallas_call(
        body, name="conv_bwd", grid=(nt,),
        in_specs=[pl.BlockSpec((ts, C), lambda i: (i, ucol)),
                  pl.BlockSpec((HALO, C), lambda i: (jnp.minimum((i + 1) * per, last_halo), ucol)),
                  pl.BlockSpec((ts, C), lambda i: (i, 0)),
                  pl.BlockSpec((HALO, C), lambda i: (jnp.minimum((i + 1) * per, last_halo), 0)),
                  pl.BlockSpec((ts, 2 * C), lambda i: (i, 0)),
                  pl.BlockSpec((HALO, 2 * C), lambda i: (jnp.maximum(i * per - 1, 0), 0)),
                  pl.BlockSpec((HALO, C), lambda i: (0, 0)), _vec_spec(C), _vec_spec(C)],
        out_specs=[_row_spec(ts, 2 * C), pl.BlockSpec((HALO, C), lambda i: (0, 0))] + [_vec_spec(C)] * 3,
        out_shape=[jax.ShapeDtypeStruct((dm.S, 2 * C), BF16), jax.ShapeDtypeStruct((HALO, C), F32)]
        + [jax.ShapeDtypeStruct((1, C), F32)] * 3,
        compiler_params=_cp(1),
    )(dcat, dcat, u1, u1, cf, cf, cw, lg, lb)


def _ada_fwd(c16, ada_w, ada_b_cols, dm):
    L, D, n = ada_w.shape
    tn = _tile(n, 512)

    def body(c_ref, w_ref, b_ref, o_ref, a_ref):
        cv = c_ref[...]
        act = (cv * _sigmoid(cv)).astype(BF16)
        a_ref[...] = act
        o_ref[...] = jnp.dot(act, w_ref[...].astype(BF16), preferred_element_type=F32) + b_ref[...]

    return pl.pallas_call(
        body, name="ada_fwd", grid=(L, n // tn),
        in_specs=[pl.BlockSpec((16, D), lambda l, j: (0, 0)), pl.BlockSpec((None, D, tn), lambda l, j: (l, 0, j)),
                  pl.BlockSpec((None, 1, tn), lambda l, j: (l, 0, j))],
        out_specs=[pl.BlockSpec((None, 16, tn), lambda l, j: (l, 0, j)), pl.BlockSpec((16, D), lambda l, j: (0, 0))],
        out_shape=[jax.ShapeDtypeStruct((L, 16, n), F32), jax.ShapeDtypeStruct((16, D), BF16)],
        compiler_params=_cp(2),
    )(c16, ada_w, ada_b_cols)


def _ada_bwd(act16, dmod16):
    L, _, n = dmod16.shape
    D = act16.shape[1]
    tm = min(TN_TM, D)

    def body(a_ref, d_ref, o_ref):
        o_ref[...] = _dot_tn(a_ref[...], d_ref[...])

    return pl.pallas_call(
        body, name="ada_bwd", grid=(L, D // tm),
        in_specs=[pl.BlockSpec((16, tm), lambda l, i: (0, i)), pl.BlockSpec((None, 16, n), lambda l, i: (l, 0, 0))],
        out_specs=pl.BlockSpec((None, tm, n), lambda l, i: (l, i, 0)),
        out_shape=jax.ShapeDtypeStruct((L, D, n), F32), compiler_params=_cp(2),
    )(act16, dmod16)


def _sum_devices(g8):
    _, R, _ = g8.shape
    tr = _rows_tile(R)

    def body(g_ref, o_ref):
        acc = g_ref[0]
        for d in range(1, N_DEV):
            acc = acc + g_ref[d]
        o_ref[...] = acc

    return pl.pallas_call(
        body, name="sum_devices", grid=(R // tr,),
        in_specs=[pl.BlockSpec((N_DEV, tr, LANES), lambda i: (0, i, 0))],
        out_specs=pl.BlockSpec((tr, LANES), lambda i: (i, 0)),
        out_shape=jax.ShapeDtypeStruct((R, LANES), F32), compiler_params=_cp(1),
    )(g8)


def _rows_tile(r, cap=512):
    for t in (512, 256, 128, 64, 32, 16, 8):
        if t <= cap and r % t == 0:
            return t
    return r


def _adam_math(w, g, m, v):
    m = ADAM_B1 * m + (1.0 - ADAM_B1) * g
    v = ADAM_B2 * v + (1.0 - ADAM_B2) * (g * g)
    m_hat = m / (1.0 - ADAM_B1 ** ADAM_STEP)
    v_hat = v / (1.0 - ADAM_B2 ** ADAM_STEP)
    delta = -ADAM_LR * (m_hat / (jnp.sqrt(v_hat) + ADAM_EPS) + ADAM_WD * w)
    return delta, m, v


def _adamw_dense(w, m, v, g, name):
    R, Cc = w.shape
    tr = _rows_tile(R, 128)

    def body(w_ref, m_ref, v_ref, g_ref, d_ref, mo_ref, vo_ref):
        d, mn, vn = _adam_math(w_ref[...], g_ref[...], m_ref[...], v_ref[...])
        d_ref[...] = d
        mo_ref[...] = mn
        vo_ref[...] = vn

    spec = pl.BlockSpec((tr, Cc), lambda i: (i, 0))
    return pl.pallas_call(
        body, name=name, grid=(R // tr,), in_specs=[spec] * 4, out_specs=[spec] * 3,
        out_shape=[jax.ShapeDtypeStruct((R, Cc), F32)] * 3, compiler_params=_cp(1),
    )(w, m, v, g)


def _adamw_shard(w, m, v, near, far, layer, prev, name):
    L, r, cc = w.shape
    tr = _rows_tile(r, 128)

    def body(w_ref, m_ref, v_ref, n_ref, f_ref, *rest):
        g_ref, d_ref, mo_ref, vo_ref = rest[-4:]
        g = n_ref[0].astype(F32) + f_ref[0].astype(F32)
        for k in range(1, N_CHIPS):
            g = g + (n_ref[k].astype(F32) + f_ref[k].astype(F32))
        d, mn, vn = _adam_math(w_ref[...], g, m_ref[...], v_ref[...])
        g_ref[...] = g
        d_ref[...] = d
        mo_ref[...] = mn
        vo_ref[...] = vn

    wspec = pl.BlockSpec((None, tr, cc), lambda i: (layer, i, 0))
    sspec = pl.BlockSpec((N_CHIPS, tr, cc), lambda i: (0, i, 0))
    n_prev = 0 if prev is None else 4
    return pl.pallas_call(
        body, name=name, grid=(r // tr,),
        in_specs=[wspec] * 3 + [sspec] * 2 + [pl.BlockSpec(memory_space=pl.ANY)] * n_prev,
        out_specs=[wspec] * 4,
        out_shape=[jax.ShapeDtypeStruct((L, r, cc), F32)] * 4,
        input_output_aliases={5 + t: t for t in range(n_prev)},
        compiler_params=_cp(1),
    )(w, m, v, near, far, *(prev or ()))


def _pack(vs):
    flat = jnp.concatenate([v.reshape(-1).astype(F32) for v in vs])
    pad = (-flat.shape[0]) % (8 * LANES)
    return jnp.pad(flat, (0, pad)).reshape(-1, LANES)


def _unpack(packed, shapes):
    flat = packed.reshape(-1)
    out, pos = [], 0
    for s in shapes:
        n = 1
        for d in s:
            n *= d
        out.append(flat[pos:pos + n].reshape(s))
        pos += n
    return out


def kernel(x, c, w_in, b_f, conv_w, conv_b, conv_ln_g, conv_ln_b, w_o, w_ffn_in, w_ffn_out, mix_pre_g, mix_post_g, ffn_pre_g, ffn_post_g, ada_w, ada_b, loss_target, m_w_in, m_b_f, m_conv_w, m_conv_b, m_conv_ln_g, m_conv_ln_b, m_w_o, m_w_ffn_in, m_w_ffn_out, m_mix_pre_g, m_mix_post_g, m_ffn_pre_g, m_ffn_post_g, m_ada_w, m_ada_b, v_w_in, v_b_f, v_conv_w, v_conv_b, v_conv_ln_g, v_conv_ln_b, v_w_o, v_w_ffn_in, v_w_ffn_out, v_mix_pre_g, v_mix_post_g, v_ffn_pre_g, v_ffn_post_g, v_ada_w, v_ada_b):
    params = dict(w_in=w_in, b_f=b_f, conv_w=conv_w, conv_b=conv_b, conv_ln_g=conv_ln_g, conv_ln_b=conv_ln_b, w_o=w_o,
                  w_ffn_in=w_ffn_in, w_ffn_out=w_ffn_out, mix_pre_g=mix_pre_g, mix_post_g=mix_post_g,
                  ffn_pre_g=ffn_pre_g, ffn_post_g=ffn_post_g, ada_w=ada_w, ada_b=ada_b)
    mom = dict(w_in=m_w_in, b_f=m_b_f, conv_w=m_conv_w, conv_b=m_conv_b, conv_ln_g=m_conv_ln_g, conv_ln_b=m_conv_ln_b,
               w_o=m_w_o, w_ffn_in=m_w_ffn_in, w_ffn_out=m_w_ffn_out, mix_pre_g=m_mix_pre_g, mix_post_g=m_mix_post_g,
               ffn_pre_g=m_ffn_pre_g, ffn_post_g=m_ffn_post_g, ada_w=m_ada_w, ada_b=m_ada_b)
    var = dict(w_in=v_w_in, b_f=v_b_f, conv_w=v_conv_w, conv_b=v_conv_b, conv_ln_g=v_conv_ln_g, conv_ln_b=v_conv_ln_b,
               w_o=v_w_o, w_ffn_in=v_w_ffn_in, w_ffn_out=v_w_ffn_out, mix_pre_g=v_mix_pre_g, mix_post_g=v_mix_post_g,
               ffn_pre_g=v_ffn_pre_g, ffn_post_g=v_ffn_post_g, ada_w=v_ada_w, ada_b=v_ada_b)

    S, D = x.shape[1], x.shape[2]
    L = w_in.shape[0]
    A = D // 2
    C = D - A
    H = A // HEAD_DIM
    F = w_ffn_out.shape[1] * N_CHIPS
    d_in = w_in.shape[2] * N_CHIPS
    NP = 3 * A + 2 * C + LANES
    dm = Dims(S=S, D=D, A=A, C=C, H=H, F=F, L=L, NP=NP, ts=min(ROW_TILE, S))
    assert H <= 8 and A == C and d_in == 3 * A + H + 2 * C

    ix, iy, ic = _place()
    chip = 2 * ix + iy
    dev = 4 * ix + 2 * iy + ic
    x2 = x.reshape(S, D)
    tgt = loss_target.reshape(S, D)

    def tie(v, token):
        return lax.optimization_barrier((v, token))[0]

    def gather_start(l, after):
        first = _exchange_start([w_in[l].astype(BF16)], False, after, f"gather_a_start_{l}")
        rest = _exchange_start([w_o[l].astype(BF16), w_ffn_in[l].astype(BF16), w_ffn_out[l].astype(BF16)], False,
                               first["token"], f"gather_b_start_{l}")
        return first, rest

    def gather_wait(st, after, name):
        srcs, lands = _exchange_wait(st, after, name)
        return [lax.dynamic_update_index_in_dim(land, src, chip, 0) for land, src in zip(lands, srcs)]

    gather = [None] * L
    gather[0] = gather_start(0, c)

    c_all =_all_gather_devices(c.reshape(D // LANES, LANES), "gather_c").reshape(N_DEV, D)
    c16 = jnp.pad(c_all, ((0, 16 - N_DEV), (0, 0)))
    n_ada = ada_w.shape[2]
    ada_b_cols = lax.dynamic_slice_in_dim(ada_b, chip * n_ada, n_ada, axis=1).reshape(L, 1, n_ada)
    mod_cols, act16 = _ada_fwd(c16, ada_w, ada_b_cols, dm)
    conv_w_all, mod_all = _all_gather_chips([conv_w.reshape(L * CONV_K, -1), mod_cols.reshape(L * 16, n_ada)], "gather_mod")
    cwc = conv_w.shape[2]
    conv_w_full = conv_w_all.reshape(N_CHIPS, L, CONV_K, cwc).transpose(1, 2, 0, 3).reshape(L, CONV_K, C)
    conv_w_full = jnp.pad(conv_w_full, ((0, 0), (0, HALO - CONV_K), (0, 0)))
    mod_all = mod_all.reshape(N_CHIPS, L, 16, n_ada)
    mod_me = lax.dynamic_index_in_dim(mod_all, dev, axis=2, keepdims=False)
    mod_me = mod_me.transpose(1, 0, 2).reshape(L, N_MOD, 1, D)

    def projection_of(g_in):
        w_nat = g_in.transpose(1, 0, 2).reshape(D, d_in)
        return jnp.concatenate([w_nat[:, :3 * A], w_nat[:, 3 * A + H:], w_nat[:, 3 * A:3 * A + H],
                                jnp.zeros((D, LANES - H), BF16)], axis=1)

    gathered = [None] * L
    vec = lambda p, l: p[l].reshape(1, -1)
    bf_pad = jnp.pad(b_f, ((0, 0), (0, LANES - H)))

    saved = []
    xin = x2
    h = _pre_norm(xin, vec(mix_pre_g, 0), mod_me[0, 1], mod_me[0, 0], dm)
    dx = loss_part = None
    for l in range(L):
        (g_in,) = gather_wait(gather[l][0], h, f"gather_a_wait_{l}")
        w_p = projection_of(g_in)
        if l + 1 < L:
            gather[l + 1] = gather_start(l + 1, g_in)
            h = tie(h, gather[l + 1][1]["token"])
        qkv = _mm_nn(h, w_p, BF16, "mm_qkv", 0, 3 * A)
        cf = _mm_nn(h, w_p, F32, "mm_cf", 3 * A, 2 * C + LANES)
        cum, cumt = _gates_fwd(cf, vec(bf_pad, l), dm)
        o, lse = _attn_fwd(qkv, cum, cumt, dm)
        u3, u1 = _conv_fwd(cf, conv_w_full[l], vec(conv_b, l), vec(conv_ln_g, l), vec(conv_ln_b, l), dm)
        cat = jnp.concatenate([o, u3], axis=1)
        g_o, wfi, g_fo = gather_wait(gather[l][1], cat, f"gather_b_wait_{l}")
        wo, wfo = g_o.reshape(D, D), g_fo.reshape(F, D)
        gathered[l] = (w_p, wo, wfi, wfo)
        y = _mm_nn(cat, wo, F32, "mm_o")
        x1, h2 = _res_norm(xin, y, vec(mix_post_g, l), mod_me[l, 2], vec(ffn_pre_g, l), mod_me[l, 4], mod_me[l, 3], dm)
        gu = _mm_nn_blocked(h2, wfi, F32, "mm_ffn_in")
        a = _swiglu_fwd(gu, dm)
        y2 = _mm_nn(a, wfo, F32, "mm_ffn_out")
        saved.append(dict(xin=xin, h=h, qkv=qkv, cf=cf, cum=cum, cumt=cumt, o=o, lse=lse, u1=u1, cat=cat, y=y,
                          x1=x1, h2=h2, gu=gu, a=a, y2=y2))
        if l + 1 < L:
            xin, h = _res_norm(x1, y2, vec(ffn_post_g, l), mod_me[l, 5], vec(mix_pre_g, l + 1),
                               mod_me[l + 1, 1], mod_me[l + 1, 0], dm)
        else:
            dx, loss_part = _res_loss(x1, y2, vec(ffn_post_g, l), mod_me[l, 5], tgt, dm)
    loss = lax.psum(loss_part[0, 0], ("x", "y", "c"))

    small = [None] * L
    big = [None] * L
    for l in reversed(range(L)):
        w_p, wo, wfi, wfo = gathered[l]
        sv = saved[l]
        dy2, d_gfpost, d_g2 = _post_bwd(dx, sv["y2"], vec(ffn_post_g, l), mod_me[l, 5], dm)
        da = _mm_nt(dy2, wfo, F32, "mm_da")
        dgu = _swiglu_bwd(da, sv["gu"], dm)
        g_wfo = _mm_tn(sv["a"], dy2, BF16, "mm_dwfo")
        g_wfi = _mm_tn(sv["h2"], dgu, BF16, "mm_dwfi", blocked=True)
        dh2 = _mm_nt_blocked(dgu, wfi, F32, "mm_dh2")
        scatter_ffn = _exchange_start([g_wfi, g_wfo.reshape(N_CHIPS, F // N_CHIPS, D)], True, dh2, f"scatter_b_start_{l}")
        dh2 = tie(dh2, scatter_ffn["token"])
        dx1, d_sh2, d_sc2, d_gfpre = _pre_bwd(dh2, dx, sv["x1"], vec(ffn_pre_g, l), mod_me[l, 4], dm)
        dy, d_gpost, d_g1 = _post_bwd(dx1, sv["y"], vec(mix_post_g, l), mod_me[l, 2], dm)
        dcat = _mm_nt(dy, wo, F32, "mm_dcat")
        g_wo = _mm_tn(sv["cat"], dy, BF16, "mm_dwo")
        dcfc, d_cw, d_cb, d_lg, d_lb = _conv_bwd(dcat, sv["u1"], sv["cf"], conv_w_full[l], vec(conv_ln_g, l),
                                                 vec(conv_ln_b, l), dm)
        delta, do = _attn_delta(dcat, sv["o"], dm)
        dq, dkv, dcq, dckt = _attn_bwd(sv["qkv"], do, sv["lse"], delta, sv["cum"], sv["cumt"], dm)
        dfl, d_bf = _gates_bwd(dcq, dckt, sv["cf"], vec(bf_pad, l), dm)
        dproj = jnp.concatenate([dq.astype(BF16), dkv, dcfc, dfl], axis=1)
        dh = _mm_nt(dproj, w_p, F32, "mm_dh")
        g_wp = _mm_tn(sv["h"], dproj, BF16, "mm_dwp")
        dx, d_sh1, d_sc1, d_gpre = _pre_bwd(dh, dx1, sv["xin"], vec(mix_pre_g, l), mod_me[l, 1], dm)
        g_nat = jnp.concatenate([g_wp[:, :3 * A], g_wp[:, 3 * A + 2 * C:3 * A + 2 * C + H], g_wp[:, 3 * A:3 * A + 2 * C]], axis=1)
        g_win = g_nat.reshape(D, N_CHIPS, d_in // N_CHIPS).transpose(1, 0, 2)
        scatter_mix = _exchange_start([g_win, g_wo.reshape(N_CHIPS, D // N_CHIPS, D)], True, dx, f"scatter_a_start_{l}")
        dx = tie(dx, scatter_mix["token"])
        big[l] = (scatter_mix, scatter_ffn)
        small[l] = dict(b_f=d_bf[0, :H], conv_b=d_cb[0], conv_ln_g=d_lg[0], conv_ln_b=d_lb[0], mix_pre_g=d_gpre[0],
                        mix_post_g=d_gpost[0], ffn_pre_g=d_gfpre[0], ffn_post_g=d_gfpost[0],
                        dmod=jnp.concatenate([d_sh1, d_sc1, d_g1, d_sh2, d_sc2, d_g2], axis=1)[0],
                        conv_w=d_cw[:CONV_K])
    grad_x = dx.reshape(1, S, D)

    keys_small = ["b_f", "conv_b", "conv_ln_g", "conv_ln_b", "mix_pre_g", "mix_post_g", "ffn_pre_g", "ffn_post_g",
                  "dmod", "conv_w"]
    stacked = [jnp.stack([small[l][k] for l in range(L)]) for k in keys_small]
    shapes = [s.shape for s in stacked]
    pack = _pack(stacked)
    pack8 = _all_gather_devices(pack, "gather_small")
    summed = dict(zip(keys_small, _unpack(_sum_devices(pack8), shapes)))
    dmod_all = jnp.stack([_unpack(pack8[d], shapes)[keys_small.index("dmod")] for d in range(N_DEV)])
    grads = {k: summed[k] for k in keys_small[:8]}
    grads["ada_b"] = summed["dmod"]
    grads["conv_w"] = lax.dynamic_slice_in_dim(summed["conv_w"], chip * cwc, cwc, axis=2)

    dmod_cols = lax.dynamic_slice_in_dim(dmod_all.reshape(N_DEV, L, N_CHIPS, n_ada), chip, 1, axis=2)
    dmod16 = jnp.pad(dmod_cols.reshape(N_DEV, L, n_ada).transpose(1, 0, 2), ((0, 0), (0, 16 - N_DEV), (0, 0))).astype(BF16)
    grads["ada_w"] = _ada_bwd(act16, dmod16)

    d_aw, m_aw, v_aw = _adamw_dense(ada_w.reshape(L * D, n_ada), m_ada_w.reshape(L * D, n_ada),
                                    v_ada_w.reshape(L * D, n_ada), grads["ada_w"].reshape(L * D, n_ada), "adamw_ada_w")
    names_small = ["b_f", "conv_w", "conv_b", "conv_ln_g", "conv_ln_b", "mix_pre_g", "mix_post_g", "ffn_pre_g",
                   "ffn_post_g", "ada_b"]
    shapes_small = [params[n].shape for n in names_small]
    d_s, m_s, v_s = _adamw_dense(_pack([params[n] for n in names_small]), _pack([mom[n] for n in names_small]),
                                 _pack([var[n] for n in names_small]), _pack([grads[n] for n in names_small]),
                                 "adamw_small")
    delta_w = dict(zip(names_small, _unpack(d_s, shapes_small)))
    new_m = dict(zip(names_small, _unpack(m_s, shapes_small)))
    new_v = dict(zip(names_small, _unpack(v_s, shapes_small)))
    delta_w["ada_w"], new_m["ada_w"], new_v["ada_w"] = (t.reshape(L, D, n_ada) for t in (d_aw, m_aw, v_aw))

    names_big = ["w_in", "w_o", "w_ffn_in", "w_ffn_out"]
    res_big = {n: None for n in names_big}
    after = dx
    for l in reversed(range(L)):
        if l == 0:
            after = tie(tie(after, d_s), d_aw)
        near = []
        for st, nm in zip(big[l], ("a", "b")):
            srcs, lands = _exchange_wait(st, after, f"scatter_{nm}_wait_{l}")
            near += [lax.dynamic_update_index_in_dim(land, lax.dynamic_index_in_dim(src, chip, 0, keepdims=False), chip, 0)
                     for land, src in zip(lands, srcs)]
        far = _sibling_forward(near, "forward_grads")
        for t, n in enumerate(names_big):
            res_big[n] = _adamw_shard(params[n], mom[n], var[n], near[t], far[t], l, res_big[n], f"adamw_{n}_{l}")
        after = res_big[names_big[-1]][1]
    for n in names_big:
        grads[n], delta_w[n], new_m[n], new_v[n] = res_big[n]

    return (loss, grad_x, *[grads[n] for n in WEIGHTS], *[delta_w[n] for n in WEIGHTS],
            *[new_m[n] for n in WEIGHTS], *[new_v[n] for n in WEIGHTS])
```

```python
import collections
import functools

import jax
import jax.numpy as jnp
from jax import lax
from jax.experimental import pallas as pl
from jax.experimental.pallas import tpu as pltpu

F32 = jnp.float32
BF16 = jnp.bfloat16
MESH = pl.DeviceIdType.MESH

HEAD_DIM = 64
CONV_K = 31
N_MOD = 6
EPS = 1e-6
N_CHIPS = 4
N_DEV = 8
LANES = 128
HALO = 32
ROW_TILE = 256
MM_TM = 512
MM_TN_MAX = 1408
TN_TM = 256
VMEM_LIMIT = 56 * 1024 * 1024

ADAM_LR = 0.001
ADAM_B1 = 0.9
ADAM_B2 = 0.999
ADAM_EPS = 1e-08
ADAM_WD = 0.01
ADAM_STEP = 10

WEIGHTS = ['w_in', 'b_f', 'conv_w', 'conv_b', 'conv_ln_g', 'conv_ln_b', 'w_o', 'w_ffn_in', 'w_ffn_out',
           'mix_pre_g', 'mix_post_g', 'ffn_pre_g', 'ffn_post_g', 'ada_w', 'ada_b']

Dims = collections.namedtuple("Dims", "S D A C H F L NP ts")


def _cp(n_grid=0):
    if n_grid:
        return pltpu.CompilerParams(dimension_semantics=("arbitrary",) * n_grid, vmem_limit_bytes=VMEM_LIMIT)
    return pltpu.CompilerParams(vmem_limit_bytes=VMEM_LIMIT)


def _tile(n, cap, also=None):
    best = None
    t = LANES
    while t <= min(n, cap):
        if n % t == 0 and (also is None or also % t == 0):
            best = t
        t += LANES
    assert best is not None, (n, cap, also)
    return best


def _bf(v):
    return v if v.dtype == BF16 else v.astype(BF16)


def _place():
    return lax.axis_index("x"), lax.axis_index("y"), lax.axis_index("c")


def _flip(v, d):
    return 1 - v if d else v


def _all_gather_devices(a, name, after=()):
    def body(a_ref, *rest):
        o_ref, send, recv, lsem = rest[len(after):]
        x, y, c = _place()
        me = 4 * x + 2 * y + c
        local = pltpu.make_async_copy(a_ref, o_ref.at[me], lsem)
        local.start()
        copies = []
        for k in range(1, N_DEV):
            peer = (_flip(x, (k >> 2) & 1), _flip(y, (k >> 1) & 1), _flip(c, k & 1))
            cp = pltpu.make_async_remote_copy(src_ref=a_ref, dst_ref=o_ref.at[me], send_sem=send.at[k - 1],
                                              recv_sem=recv.at[k - 1], device_id=peer, device_id_type=MESH)
            cp.start()
            copies.append(cp)
        for cp in copies:
            cp.wait()
        local.wait()

    return pl.pallas_call(
        body, name=name,
        out_shape=jax.ShapeDtypeStruct((N_DEV,) + a.shape, a.dtype),
        in_specs=[pl.BlockSpec(memory_space=pl.ANY)] * (1 + len(after)),
        out_specs=pl.BlockSpec(memory_space=pl.ANY),
        scratch_shapes=[pltpu.SemaphoreType.DMA((N_DEV - 1,)), pltpu.SemaphoreType.DMA((N_DEV - 1,)),
                        pltpu.SemaphoreType.DMA],
    )(a, *after)


def _all_gather_chips(arrays, name):
    n = len(arrays)

    def body(*refs):
        a_refs, o_refs = refs[:n], refs[n:2 * n]
        send, recv, lsem = refs[2 * n:]
        x, y, c = _place()
        me = 2 * x + y
        copies = []
        for i in range(n):
            local = pltpu.make_async_copy(a_refs[i], o_refs[i].at[me], lsem.at[i])
            local.start()
            copies.append(local)
            for k in range(1, N_CHIPS):
                peer = (_flip(x, (k >> 1) & 1), _flip(y, k & 1), c)
                cp = pltpu.make_async_remote_copy(src_ref=a_refs[i], dst_ref=o_refs[i].at[me],
                                                  send_sem=send.at[i, k - 1], recv_sem=recv.at[i, k - 1],
                                                  device_id=peer, device_id_type=MESH)
                cp.start()
                copies.append(cp)
        for cp in copies:
            cp.wait()

    return pl.pallas_call(
        body, name=name,
        out_shape=[jax.ShapeDtypeStruct((N_CHIPS,) + a.shape, a.dtype) for a in arrays],
        in_specs=[pl.BlockSpec(memory_space=pl.ANY)] * n,
        out_specs=[pl.BlockSpec(memory_space=pl.ANY)] * n,
        scratch_shapes=[pltpu.SemaphoreType.DMA((n, N_CHIPS - 1)), pltpu.SemaphoreType.DMA((n, N_CHIPS - 1)),
                        pltpu.SemaphoreType.DMA((n,))],
    )(*arrays)


_HBM = pl.BlockSpec(memory_space=pltpu.HBM)
_SEM = pl.BlockSpec(memory_space=pltpu.SEMAPHORE)
_EFFECT = pltpu.SideEffectType.DATAFLOW_SIDE_EFFECTING


def _chip_copies(srcs, lands, send, recv, scatter):
    x, y, c = _place()
    me = 2 * x + y
    copies = []
    for i in range(len(srcs)):
        for k in range(1, N_CHIPS):
            px, py = _flip(x, (k >> 1) & 1), _flip(y, k & 1)
            src = srcs[i].at[2 * px + py] if scatter else srcs[i]
            s = i * (N_CHIPS - 1) + k - 1
            copies.append(pltpu.make_async_remote_copy(src_ref=src, dst_ref=lands[i].at[me], send_sem=send.at[s],
                                                       recv_sem=recv.at[s], device_id=(px, py, c), device_id_type=MESH))
    return copies


_ORDER = pl.BlockSpec(memory_space=pl.ANY)


def _exchange_start(arrays, scatter, after, name):
    n = len(arrays)

    def body(*refs):
        srcs, lands = refs[:n], refs[n:2 * n]
        send, recv = refs[2 * n + len(after)], refs[2 * n + len(after) + 1]
        token = refs[-1]
        for cp in _chip_copies(srcs, lands, send, recv, scatter):
            cp.start()
        token[...] = jnp.zeros_like(token)

    land_shapes = [a.shape if scatter else (N_CHIPS,) + a.shape for a in arrays]
    outs = pl.pallas_call(
        body, name=name,
        out_shape=(pltpu.SemaphoreType.DMA((n * (N_CHIPS - 1),)), pltpu.SemaphoreType.DMA((n * (N_CHIPS - 1),)),
                   *[pltpu.HBM(a.shape, a.dtype) for a in arrays],
                   *[pltpu.HBM(s, a.dtype) for s, a in zip(land_shapes, arrays)],
                   jax.ShapeDtypeStruct((8, LANES), F32)),
        in_specs=[_HBM] * (2 * n) + [_ORDER] * len(after),
        out_specs=(_SEM, _SEM, *[_HBM] * (2 * n), pl.BlockSpec(memory_space=pltpu.VMEM)),
        input_output_aliases={i: 2 + i for i in range(2 * n)},
        compiler_params=pltpu.CompilerParams(has_side_effects=_EFFECT),
    )(*[pltpu.with_memory_space_constraint(a, pltpu.HBM) for a in arrays],
      *[pltpu.with_memory_space_constraint(lax.empty(s, a.dtype), pltpu.HBM) for s, a in zip(land_shapes, arrays)],
      *after)
    return dict(send=outs[0], recv=outs[1], srcs=outs[2:2 + n], lands=outs[2 + n:2 + 2 * n], token=outs[-1],
                scatter=scatter)


def _exchange_wait(st, after, name):
    n = len(st["srcs"])
    scatter = st["scatter"]

    def body(*refs):
        srcs, lands = refs[:n], refs[n:2 * n]
        send, recv = refs[2 * n], refs[2 * n + 1]
        for cp in _chip_copies(srcs, lands, send, recv, scatter):
            cp.wait_send()
            cp.wait_recv()

    outs = pl.pallas_call(
        body, name=name,
        out_shape=tuple(pltpu.HBM(a.shape, a.dtype) for a in (*st["srcs"], *st["lands"])),
        in_specs=[_HBM] * (2 * n) + [_SEM, _SEM] + [_ORDER] * len(after),
        out_specs=tuple([_HBM] * (2 * n)),
        input_output_aliases={i: i for i in range(2 * n)},
        compiler_params=pltpu.CompilerParams(has_side_effects=_EFFECT),
    )(*st["srcs"], *st["lands"], st["send"], st["recv"], *after)
    return outs[:n], outs[n:]


def _sibling_forward(arrays, name):
    n = len(arrays)

    def body(*refs):
        srcs, outs = refs[:n], refs[n:2 * n]
        send, recv = refs[2 * n:]
        x, y, c = _place()
        copies = []
        for i in range(n):
            cp = pltpu.make_async_remote_copy(src_ref=srcs[i], dst_ref=outs[i], send_sem=send.at[i], recv_sem=recv.at[i],
                                              device_id=(x, y, 1 - c), device_id_type=MESH)
            cp.start()
            copies.append(cp)
        for cp in copies:
            cp.wait()

    return pl.pallas_call(
        body, name=name,
        out_shape=[jax.ShapeDtypeStruct(a.shape, a.dtype) for a in arrays],
        in_specs=[pl.BlockSpec(memory_space=pl.ANY)] * n,
        out_specs=[pl.BlockSpec(memory_space=pl.ANY)] * n,
        scratch_shapes=[pltpu.SemaphoreType.DMA((n,)), pltpu.SemaphoreType.DMA((n,))],
    )(*arrays)


def _matmul(a, b, contract, grid, a_spec, b_spec, o_spec, out_shape, name, nk=1, acc_shape=None, after=()):
    def body(a_ref, b_ref, *rest):
        o_ref, acc = rest[len(after)], rest[len(after) + 1:]
        r = lax.dot_general(_bf(a_ref[...]), _bf(b_ref[...]), (contract, ((), ())), preferred_element_type=F32)
        if nk == 1:
            o_ref[...] = r.astype(o_ref.dtype)
        else:
            k = pl.program_id(len(grid) - 1)

            @pl.when(k == 0)
            def _():
                acc[0][...] = r

            @pl.when(k > 0)
            def _():
                acc[0][...] += r

            @pl.when(k == nk - 1)
            def _():
                o_ref[...] = acc[0][...].astype(o_ref.dtype)

    return pl.pallas_call(
        body, name=name, grid=grid, in_specs=[a_spec, b_spec] + [_ORDER] * len(after), out_specs=o_spec,
        out_shape=out_shape, scratch_shapes=[pltpu.VMEM(acc_shape, F32)] if nk > 1 else [],
        compiler_params=_cp(len(grid)),
    )(a, b, *after)


def _mm_nn(a, b, out_dtype, name, col0=0, n=None, after=()):
    m, k = a.shape
    n = b.shape[1] - col0 if n is None else n
    tm = min(MM_TM, m)
    tn = _tile(n, MM_TN_MAX, also=col0 if col0 else None)
    off = col0 // tn
    return _matmul(a, b, ((1,), (0,)), (n // tn, m // tm),
                   pl.BlockSpec((tm, k), lambda j, i: (i, 0)),
                   pl.BlockSpec((k, tn), lambda j, i: (0, j + off)),
                   pl.BlockSpec((tm, tn), lambda j, i: (i, j)),
                   jax.ShapeDtypeStruct((m, n), out_dtype), name, after=after)


def _mm_nn_blocked(a, b3, out_dtype, name):
    m, k = a.shape
    nj, _, nb = b3.shape
    tm = min(MM_TM, m)
    return _matmul(a, b3, ((1,), (0,)), (nj, m // tm),
                   pl.BlockSpec((tm, k), lambda j, i: (i, 0)),
                   pl.BlockSpec((None, k, nb), lambda j, i: (j, 0, 0)),
                   pl.BlockSpec((tm, nb), lambda j, i: (i, j)),
                   jax.ShapeDtypeStruct((m, nj * nb), out_dtype), name)


def _mm_nt(a, b, out_dtype, name):
    m, k = a.shape
    n = b.shape[0]
    tm = min(MM_TM, m)
    tn = _tile(n, MM_TN_MAX)
    return _matmul(a, b, ((1,), (1,)), (n // tn, m // tm),
                   pl.BlockSpec((tm, k), lambda j, i: (i, 0)),
                   pl.BlockSpec((tn, k), lambda j, i: (j, 0)),
                   pl.BlockSpec((tm, tn), lambda j, i: (i, j)),
                   jax.ShapeDtypeStruct((m, n), out_dtype), name)


def _mm_nt_blocked(a, b3, out_dtype, name):
    m = a.shape[0]
    nj, n, nb = b3.shape
    tm = min(MM_TM, m)
    tn = _tile(n, 512)
    return _matmul(a, b3, ((1,), (1,)), (n // tn, m // tm, nj),
                   pl.BlockSpec((tm, nb), lambda j, i, k: (i, k)),
                   pl.BlockSpec((None, tn, nb), lambda j, i, k: (k, j, 0)),
                   pl.BlockSpec((tm, tn), lambda j, i, k: (i, j)),
                   jax.ShapeDtypeStruct((m, n), out_dtype), name, nk=nj, acc_shape=(tm, tn))


def _mm_tn(a, b, out_dtype, name, blocked=False):
    k, m = a.shape
    n = b.shape[1]
    tm = min(TN_TM, m)
    tn = n // N_CHIPS if blocked else _tile(n, MM_TN_MAX)
    if blocked:
        o_spec = pl.BlockSpec((None, tm, tn), lambda j, i: (j, i, 0))
        out_shape = jax.ShapeDtypeStruct((n // tn, m, tn), out_dtype)
    else:
        o_spec = pl.BlockSpec((tm, tn), lambda j, i: (i, j))
        out_shape = jax.ShapeDtypeStruct((m, n), out_dtype)
    return _matmul(a, b, ((0,), (0,)), (n // tn, m // tm),
                   pl.BlockSpec((k, tm), lambda j, i: (0, i)),
                   pl.BlockSpec((k, tn), lambda j, i: (0, j)),
                   o_spec, out_shape, name)


def _vec_spec(d):
    return pl.BlockSpec((1, d), lambda i: (0, 0))


def _row_spec(ts, d, col=0):
    return pl.BlockSpec((ts, d), lambda i: (i, col))


def _rms(x):
    return lax.rsqrt(jnp.mean(x * x, axis=-1, keepdims=True) + EPS)


def _pre_norm(x, gain, scale, shift, dm):
    def body(x_ref, g_ref, sc_ref, sh_ref, h_ref):
        xv = x_ref[...]
        h_ref[...] = (((xv * _rms(xv)) * g_ref[...]) * (1.0 + sc_ref[...]) + sh_ref[...]).astype(h_ref.dtype)

    return pl.pallas_call(
        body, name="pre_norm", grid=(dm.S // dm.ts,),
        in_specs=[_row_spec(dm.ts, dm.D)] + [_vec_spec(dm.D)] * 3,
        out_specs=_row_spec(dm.ts, dm.D),
        out_shape=jax.ShapeDtypeStruct((dm.S, dm.D), BF16), compiler_params=_cp(1),
    )(x, gain, scale, shift)


def _res_norm(x, y, gpost, gate, gain, scale, shift, dm):
    def body(x_ref, y_ref, gp_ref, gt_ref, g_ref, sc_ref, sh_ref, xo_ref, h_ref):
        yv = y_ref[...]
        xn = x_ref[...] + gt_ref[...] * ((yv * _rms(yv)) * gp_ref[...])
        xo_ref[...] = xn
        h_ref[...] = (((xn * _rms(xn)) * g_ref[...]) * (1.0 + sc_ref[...]) + sh_ref[...]).astype(h_ref.dtype)

    return pl.pallas_call(
        body, name="res_norm", grid=(dm.S // dm.ts,),
        in_specs=[_row_spec(dm.ts, dm.D)] * 2 + [_vec_spec(dm.D)] * 5,
        out_specs=[_row_spec(dm.ts, dm.D)] * 2,
        out_shape=[jax.ShapeDtypeStruct((dm.S, dm.D), F32), jax.ShapeDtypeStruct((dm.S, dm.D), BF16)],
        compiler_params=_cp(1),
    )(x, y, gpost, gate, gain, scale, shift)


def _res_loss(x, y, gpost, gate, target, dm):
    def body(x_ref, y_ref, gp_ref, gt_ref, t_ref, dx_ref, loss_ref):
        i = pl.program_id(0)

        @pl.when(i == 0)
        def _():
            loss_ref[...] = jnp.zeros_like(loss_ref)
        yv = y_ref[...]
        err = x_ref[...] + gt_ref[...] * ((yv * _rms(yv)) * gp_ref[...]) - t_ref[...]
        dx_ref[...] = err * (1.0 / dm.D)
        per_row = jnp.mean(err * err, axis=-1, keepdims=True)
        loss_ref[...] += 0.5 * jnp.sum(per_row, axis=0, keepdims=True)

    return pl.pallas_call(
        body, name="res_loss", grid=(dm.S // dm.ts,),
        in_specs=[_row_spec(dm.ts, dm.D)] * 2 + [_vec_spec(dm.D)] * 2 + [_row_spec(dm.ts, dm.D)],
        out_specs=[_row_spec(dm.ts, dm.D), _vec_spec(LANES)],
        out_shape=[jax.ShapeDtypeStruct((dm.S, dm.D), F32), jax.ShapeDtypeStruct((1, LANES), F32)],
        compiler_params=_cp(1),
    )(x, y, gpost, gate, target)


def _post_bwd(dxo, y, gpost, gate, dm, after=()):
    def body(dx_ref, y_ref, gp_ref, gt_ref, *rest):
        dy_ref, dgp_ref, dgt_ref = rest[len(after):]
        i = pl.program_id(0)

        @pl.when(i == 0)
        def _():
            dgp_ref[...] = jnp.zeros_like(dgp_ref)
            dgt_ref[...] = jnp.zeros_like(dgt_ref)
        yv, dx = y_ref[...], dx_ref[...]
        r = _rms(yv)
        t = yv * r
        dgp_ref[...] += jnp.sum(dx * gt_ref[...] * t, axis=0, keepdims=True)
        dgt_ref[...] += jnp.sum(dx * (t * gp_ref[...]), axis=0, keepdims=True)
        dt = dx * (gt_ref[...] * gp_ref[...])
        dy_ref[...] = (r * (dt - t * jnp.mean(dt * t, axis=-1, keepdims=True))).astype(dy_ref.dtype)

    return pl.pallas_call(
        body, name="post_bwd", grid=(dm.S // dm.ts,),
        in_specs=[_row_spec(dm.ts, dm.D)] * 2 + [_vec_spec(dm.D)] * 2 + [_ORDER] * len(after),
        out_specs=[_row_spec(dm.ts, dm.D), _vec_spec(dm.D), _vec_spec(dm.D)],
        out_shape=[jax.ShapeDtypeStruct((dm.S, dm.D), BF16)] + [jax.ShapeDtypeStruct((1, dm.D), F32)] * 2,
        compiler_params=_cp(1),
    )(dxo, y, gpost, gate, *after)


def _pre_bwd(dh, dxo, x, gain, scale, dm, after=()):
    def body(dh_ref, dxo_ref, x_ref, g_ref, sc_ref, *rest):
        dx_ref, dsh_ref, dsc_ref, dg_ref = rest[len(after):]
        i = pl.program_id(0)

        @pl.when(i == 0)
        def _():
            dsh_ref[...] = jnp.zeros_like(dsh_ref)
            dsc_ref[...] = jnp.zeros_like(dsc_ref)
            dg_ref[...] = jnp.zeros_like(dg_ref)
        xv, dh_ = x_ref[...], dh_ref[...]
        r = _rms(xv)
        nrm = xv * r
        one_sc = 1.0 + sc_ref[...]
        dsh_ref[...] += jnp.sum(dh_, axis=0, keepdims=True)
        dsc_ref[...] += jnp.sum(dh_ * (nrm * g_ref[...]), axis=0, keepdims=True)
        dg_ref[...] += jnp.sum(dh_ * nrm * one_sc, axis=0, keepdims=True)
        dn = dh_ * (g_ref[...] * one_sc)
        dx_ref[...] = dxo_ref[...] + r * (dn - nrm * jnp.mean(dn * nrm, axis=-1, keepdims=True))

    return pl.pallas_call(
        body, name="pre_bwd", grid=(dm.S // dm.ts,),
        in_specs=[_row_spec(dm.ts, dm.D)] * 3 + [_vec_spec(dm.D)] * 2 + [_ORDER] * len(after),
        out_specs=[_row_spec(dm.ts, dm.D)] + [_vec_spec(dm.D)] * 3,
        out_shape=[jax.ShapeDtypeStruct((dm.S, dm.D), F32)] + [jax.ShapeDtypeStruct((1, dm.D), F32)] * 3,
        compiler_params=_cp(1),
    )(dh, dxo, x, gain, scale, *after)


def _sigmoid(z):
    return 1.0 / (1.0 + jnp.exp(-z))


def _swiglu_fwd(gu, dm):
    def body(g_ref, u_ref, a_ref):
        g = g_ref[...]
        a_ref[...] = (g * _sigmoid(g) * u_ref[...]).astype(a_ref.dtype)

    return pl.pallas_call(
        body, name="swiglu_fwd", grid=(dm.S // dm.ts,),
        in_specs=[_row_spec(dm.ts, dm.F, 0), _row_spec(dm.ts, dm.F, 1)],
        out_specs=_row_spec(dm.ts, dm.F),
        out_shape=jax.ShapeDtypeStruct((dm.S, dm.F), BF16), compiler_params=_cp(1),
    )(gu, gu)


def _swiglu_bwd(da, gu, dm):
    def body(da_ref, g_ref, u_ref, d_ref):
        g, u, da_ = g_ref[...], u_ref[...], da_ref[...]
        sg = _sigmoid(g)
        d_ref[:, :dm.F] = (da_ * u * (sg * (1.0 + g * (1.0 - sg)))).astype(d_ref.dtype)
        d_ref[:, dm.F:] = (da_ * (g * sg)).astype(d_ref.dtype)

    return pl.pallas_call(
        body, name="swiglu_bwd", grid=(dm.S // dm.ts,),
        in_specs=[_row_spec(dm.ts, dm.F), _row_spec(dm.ts, dm.F, 0), _row_spec(dm.ts, dm.F, 1)],
        out_specs=_row_spec(dm.ts, 2 * dm.F),
        out_shape=jax.ShapeDtypeStruct((dm.S, 2 * dm.F), BF16), compiler_params=_cp(1),
    )(da, gu, gu)


def _tri(n, upper):
    r = lax.broadcasted_iota(jnp.int32, (n, n), 0)
    c = lax.broadcasted_iota(jnp.int32, (n, n), 1)
    return (c >= r if upper else r >= c).astype(F32)


def _gates_fwd(cf, bf, dm):
    ts = dm.ts
    fcol = 2 * dm.C // LANES

    def body(f_ref, b_ref, cum_ref, cumt_ref, carry):
        i = pl.program_id(0)

        @pl.when(i == 0)
        def _():
            carry[...] = jnp.zeros_like(carry)
        z = f_ref[...] + b_ref[...]
        lf = jnp.minimum(z, 0.0) - jnp.log(1.0 + jnp.exp(-jnp.abs(z)))
        cs = jnp.dot(_tri(ts, False), lf, precision=lax.Precision.HIGHEST, preferred_element_type=F32) + carry[...]
        cum_ref[...] = cs
        cumt_ref[...] = cs.T[:8, :]
        carry[...] = cs[ts - 1:ts, :]

    return pl.pallas_call(
        body, name="gates_fwd", grid=(dm.S // ts,),
        in_specs=[pl.BlockSpec((ts, LANES), lambda i: (i, fcol)), _vec_spec(LANES)],
        out_specs=[_row_spec(ts, LANES), pl.BlockSpec((None, 8, ts), lambda i: (i, 0, 0))],
        out_shape=[jax.ShapeDtypeStruct((dm.S, LANES), F32), jax.ShapeDtypeStruct((dm.S // ts, 8, ts), F32)],
        scratch_shapes=[pltpu.VMEM((1, LANES), F32)], compiler_params=_cp(1),
    )(cf, bf)


def _gates_bwd(dcq, dckt, cf, bf, dm):
    ts = dm.ts
    nb = dm.S // ts
    fcol = 2 * dm.C // LANES

    def body(dcq_ref, dck_ref, f_ref, b_ref, df_ref, db_ref, carry):
        i = pl.program_id(0)

        @pl.when(i == 0)
        def _():
            carry[...] = jnp.zeros_like(carry)
            db_ref[...] = jnp.zeros_like(db_ref)
        dck = jnp.concatenate([dck_ref[...], jnp.zeros((LANES - 8, ts), F32)], axis=0).T
        dc = dcq_ref[...] + dck
        dlf = jnp.dot(_tri(ts, True), dc, precision=lax.Precision.HIGHEST, preferred_element_type=F32) + carry[...]
        carry[...] = dlf[0:1, :]
        dz = dlf * (1.0 - _sigmoid(f_ref[...] + b_ref[...]))
        df_ref[...] = dz.astype(df_ref.dtype)
        db_ref[...] += jnp.sum(dz, axis=0, keepdims=True)

    return pl.pallas_call(
        body, name="gates_bwd", grid=(nb,),
        in_specs=[pl.BlockSpec((ts, LANES), lambda i: (nb - 1 - i, 0)),
                  pl.BlockSpec((None, 8, ts), lambda i: (nb - 1 - i, 0, 0)),
                  pl.BlockSpec((ts, LANES), lambda i: (nb - 1 - i, fcol)), _vec_spec(LANES)],
        out_specs=[pl.BlockSpec((ts, LANES), lambda i: (nb - 1 - i, 0)), _vec_spec(LANES)],
        out_shape=[jax.ShapeDtypeStruct((dm.S, LANES), BF16), jax.ShapeDtypeStruct((1, LANES), F32)],
        scratch_shapes=[pltpu.VMEM((1, LANES), F32)], compiler_params=_cp(1),
    )(dcq, dckt, cf, bf)


def _dot_nt(a, b):
    return lax.dot_general(a, b, (((1,), (1,)), ((), ())), preferred_element_type=F32)


def _dot_tn(a, b):
    return lax.dot_general(a, b, (((0,), (0,)), ((), ())), preferred_element_type=F32)


def _attn_fwd(qkv, cum, cumt, dm):
    tq = dm.ts
    A, H = dm.A, dm.H
    scale = HEAD_DIM ** -0.5

    def body(q_ref, kv_ref, cq_ref, ckt_ref, o_ref, lse_ref):
        i = pl.program_id(0)
        row = lax.broadcasted_iota(jnp.int32, (tq, tq), 0)
        col = lax.broadcasted_iota(jnp.int32, (tq, tq), 1)
        lane = lax.broadcasted_iota(jnp.int32, (tq, LANES), 1)
        lse_all = jnp.zeros((tq, LANES), F32)
        for h in range(H):
            lo, hi = h * HEAD_DIM, (h + 1) * HEAD_DIM
            q = q_ref[:, lo:hi]
            cq = cq_ref[:, h:h + 1]

            def step(j, carry, lo=lo, hi=hi, q=q, cq=cq, h=h):
                m, l, acc = carry
                r0 = pl.multiple_of(j * tq, tq)
                k = kv_ref[pl.ds(r0, tq), A + lo:A + hi]
                v = kv_ref[pl.ds(r0, tq), 2 * A + lo:2 * A + hi]
                ck = ckt_ref[j, h:h + 1, :]
                s = _dot_nt(q, k) * scale + cq - ck
                s = jnp.where(row + (i - j) * tq >= col, s, -1e30)
                m_new = jnp.maximum(m, jnp.max(s, axis=1, keepdims=True))
                p = jnp.exp(s - m_new)
                alpha = jnp.exp(m - m_new)
                l = alpha * l + jnp.sum(p, axis=1, keepdims=True)
                acc = alpha * acc + jnp.dot(p.astype(BF16), v, preferred_element_type=F32)
                return m_new, l, acc

            m, l, acc = lax.fori_loop(0, i + 1, step, (jnp.full((tq, 1), -1e30, F32), jnp.zeros((tq, 1), F32),
                                                       jnp.zeros((tq, HEAD_DIM), F32)))
            o_ref[:, lo:hi] = (acc / l).astype(o_ref.dtype)
            lse_all = jnp.where(lane == h, m + jnp.log(l), lse_all)
        lse_ref[...] = lse_all

    nq = dm.S // tq
    return pl.pallas_call(
        body, name="attn_fwd", grid=(nq,),
        in_specs=[pl.BlockSpec((tq, A), lambda i: (i, 0)), pl.BlockSpec((dm.S, 3 * A), lambda i: (0, 0)),
                  _row_spec(tq, LANES), pl.BlockSpec((nq, 8, tq), lambda i: (0, 0, 0))],
        out_specs=[pl.BlockSpec((tq, A), lambda i: (i, 0)), _row_spec(tq, LANES)],
        out_shape=[jax.ShapeDtypeStruct((dm.S, A), BF16), jax.ShapeDtypeStruct((dm.S, LANES), F32)],
        compiler_params=_cp(1),
    )(qkv, qkv, cum, cumt)


def _attn_delta(dcat, o, dm):
    ts = dm.ts

    def body(do_ref, o_ref, dl_ref, dob_ref):
        lane = lax.broadcasted_iota(jnp.int32, (ts, LANES), 1)
        prod = do_ref[...] * o_ref[...].astype(F32)
        dl = jnp.zeros((ts, LANES), F32)
        for h in range(dm.H):
            dl = jnp.where(lane == h, jnp.sum(prod[:, h * HEAD_DIM:(h + 1) * HEAD_DIM], axis=1, keepdims=True), dl)
        dl_ref[...] = dl
        dob_ref[...] = do_ref[...].astype(dob_ref.dtype)

    return pl.pallas_call(
        body, name="attn_delta", grid=(dm.S // ts,),
        in_specs=[_row_spec(ts, dm.A, 0), _row_spec(ts, dm.A)],
        out_specs=[_row_spec(ts, LANES), _row_spec(ts, dm.A)],
        out_shape=[jax.ShapeDtypeStruct((dm.S, LANES), F32), jax.ShapeDtypeStruct((dm.S, dm.A), BF16)],
        compiler_params=_cp(1),
    )(dcat, o)


def _attn_bwd(qkv, do, lse, delta, cum, cumt, dm):
    tq = dm.ts
    A, H = dm.A, dm.H
    nq = dm.S // tq
    scale = HEAD_DIM ** -0.5

    def body(kv_ref, q_ref, do_ref, lse_ref, dl_ref, cq_ref, ckt_ref, dq_ref, dkv_ref, dcq_ref, dckt_ref):
        j = pl.program_id(0)

        @pl.when(j == 0)
        def _():
            dq_ref[...] = jnp.zeros_like(dq_ref)
            dcq_ref[...] = jnp.zeros_like(dcq_ref)
        row = lax.broadcasted_iota(jnp.int32, (tq, tq), 0)
        col = lax.broadcasted_iota(jnp.int32, (tq, tq), 1)
        lane = lax.broadcasted_iota(jnp.int32, (tq, LANES), 1)
        sub = lax.broadcasted_iota(jnp.int32, (8, tq), 0)
        dck_all = jnp.zeros((8, tq), F32)
        for h in range(H):
            lo, hi = h * HEAD_DIM, (h + 1) * HEAD_DIM
            k = kv_ref[:, A + lo:A + hi]
            v = kv_ref[:, 2 * A + lo:2 * A + hi]
            ck = ckt_ref[h:h + 1, :]

            def step(i, carry, lo=lo, hi=hi, k=k, v=v, ck=ck, h=h):
                dk, dv, dck = carry
                r0 = pl.multiple_of(i * tq, tq)
                rows = pl.ds(r0, tq)
                q = q_ref[rows, lo:hi]
                do_ = do_ref[rows, lo:hi]
                s = _dot_nt(q, k) * scale + cq_ref[rows, h:h + 1] - ck
                s = jnp.where(row + (i - j) * tq >= col, s, -1e30)
                p = jnp.exp(s - lse_ref[rows, h:h + 1])
                ds = p * (_dot_nt(do_, v) - dl_ref[rows, h:h + 1])
                dsb = ds.astype(BF16)
                dv = dv + _dot_tn(p.astype(BF16), do_)
                dk = dk + _dot_tn(dsb, q)
                dq_ref[rows, lo:hi] += jnp.dot(dsb, k, preferred_element_type=F32)
                dcq_ref[rows, :] += jnp.where(lane == h, jnp.sum(ds, axis=1, keepdims=True), 0.0)
                dck = dck - jnp.sum(ds, axis=0, keepdims=True)
                return dk, dv, dck

            dk, dv, dck = lax.fori_loop(j, nq, step, (jnp.zeros((tq, HEAD_DIM), F32), jnp.zeros((tq, HEAD_DIM), F32),
                                                      jnp.zeros((1, tq), F32)))
            dkv_ref[:, lo:hi] = (dk * scale).astype(dkv_ref.dtype)
            dkv_ref[:, A + lo:A + hi] = dv.astype(dkv_ref.dtype)
            dck_all = jnp.where(sub == h, dck, dck_all)
        dckt_ref[...] = dck_all

        @pl.when(j == nq - 1)
        def _():
            dq_ref[...] = dq_ref[...] * scale

    full = lambda w: pl.BlockSpec((dm.S, w), lambda j: (0, 0))
    return pl.pallas_call(
        body, name="attn_bwd", grid=(nq,),
        in_specs=[pl.BlockSpec((tq, 3 * A), lambda j: (j, 0)), full(A), full(A), full(LANES), full(LANES), full(LANES),
                  pl.BlockSpec((None, 8, tq), lambda j: (j, 0, 0))],
        out_specs=[full(A), pl.BlockSpec((tq, 2 * A), lambda j: (j, 0)), full(LANES),
                   pl.BlockSpec((None, 8, tq), lambda j: (j, 0, 0))],
        out_shape=[jax.ShapeDtypeStruct((dm.S, A), F32), jax.ShapeDtypeStruct((dm.S, 2 * A), BF16),
                   jax.ShapeDtypeStruct((dm.S, LANES), F32), jax.ShapeDtypeStruct((nq, 8, tq), F32)],
        compiler_params=_cp(1),
    )(qkv, qkv, do, lse, delta, cum, cumt)


def _glu(cf_rows, c):
    return cf_rows[:, :c] * _sigmoid(cf_rows[:, c:2 * c])


def _conv_fwd(cf, cw, cb, lg, lb, dm):
    ts, C = dm.ts, dm.C
    per = ts // HALO

    def body(cf_ref, halo_ref, w_ref, cb_ref, lg_ref, lb_ref, u3_ref, u1_ref):
        i = pl.program_id(0)
        prev = jnp.where(i > 0, _glu(halo_ref[...], C), 0.0)
        win = jnp.concatenate([prev, _glu(cf_ref[...], C)], axis=0)
        u1 = jnp.zeros((ts, C), F32) + cb_ref[...]
        off = HALO - (CONV_K - 1)
        for k in range(CONV_K):
            u1 = u1 + w_ref[k:k + 1, :] * win[off + k:off + k + ts, :]
        u1_ref[...] = u1
        mu = jnp.mean(u1, axis=-1, keepdims=True)
        cen = u1 - mu
        rstd = lax.rsqrt(jnp.mean(cen * cen, axis=-1, keepdims=True) + EPS)
        u2 = cen * rstd * lg_ref[...] + lb_ref[...]
        u3_ref[...] = (u2 * _sigmoid(u2)).astype(u3_ref.dtype)

    return pl.pallas_call(
        body, name="conv_fwd", grid=(dm.S // ts,),
        in_specs=[pl.BlockSpec((ts, 2 * C), lambda i: (i, 0)),
                  pl.BlockSpec((HALO, 2 * C), lambda i: (jnp.maximum(i * per - 1, 0), 0)),
                  pl.BlockSpec((HALO, C), lambda i: (0, 0))] + [_vec_spec(C)] * 3,
        out_specs=[_row_spec(ts, C)] * 2,
        out_shape=[jax.ShapeDtypeStruct((dm.S, C), BF16), jax.ShapeDtypeStruct((dm.S, C), F32)],
        compiler_params=_cp(1),
    )(cf, cf, cw, cb, lg, lb)


def _conv_bwd(dcat, u1, cf, cw, lg, lb, dm):
    ts, C = dm.ts, dm.C
    per = ts // HALO
    nt = dm.S // ts
    last_halo = dm.S // HALO - 1

    def ln_bwd(du3, u1v, lg_v, lb_v):
        mu = jnp.mean(u1v, axis=-1, keepdims=True)
        cen = u1v - mu
        rstd = lax.rsqrt(jnp.mean(cen * cen, axis=-1, keepdims=True) + EPS)
        uhat = cen * rstd
        u2 = uhat * lg_v + lb_v
        sg = _sigmoid(u2)
        du2 = du3 * (sg * (1.0 + u2 * (1.0 - sg)))
        duh = du2 * lg_v
        du1 = rstd * (duh - jnp.mean(duh, axis=-1, keepdims=True) - uhat * jnp.mean(duh * uhat, axis=-1, keepdims=True))
        return du1, du2, uhat

    def body(d_ref, dn_ref, u1_ref, u1n_ref, cf_ref, halo_ref, w_ref, lg_ref, lb_ref,
             dcf_ref, dw_ref, dcb_ref, dlg_ref, dlb_ref):
        i = pl.program_id(0)

        @pl.when(i == 0)
        def _():
            dw_ref[...] = jnp.zeros_like(dw_ref)
            dcb_ref[...] = jnp.zeros_like(dcb_ref)
            dlg_ref[...] = jnp.zeros_like(dlg_ref)
            dlb_ref[...] = jnp.zeros_like(dlb_ref)
        lg_v, lb_v = lg_ref[...], lb_ref[...]
        du1, du2, uhat = ln_bwd(d_ref[...], u1_ref[...], lg_v, lb_v)
        du1n, _, _ = ln_bwd(dn_ref[...], u1n_ref[...], lg_v, lb_v)
        du1n = jnp.where(i < nt - 1, du1n, 0.0)
        dlg_ref[...] += jnp.sum(du2 * uhat, axis=0, keepdims=True)
        dlb_ref[...] += jnp.sum(du2, axis=0, keepdims=True)
        dcb_ref[...] += jnp.sum(du1, axis=0, keepdims=True)
        dwin = jnp.concatenate([du1, du1n], axis=0)
        cfv = cf_ref[...]
        cv, sg = cfv[:, :C], _sigmoid(cfv[:, C:2 * C])
        prev = jnp.where(i > 0, _glu(halo_ref[...], C), 0.0)
        uwin = jnp.concatenate([prev, cv * sg], axis=0)
        du0 = jnp.zeros((ts, C), F32)
        off = HALO - (CONV_K - 1)
        for k in range(CONV_K):
            back = CONV_K - 1 - k
            du0 = du0 + w_ref[k:k + 1, :] * dwin[back:back + ts, :]
            dw_ref[k:k + 1, :] += jnp.sum(du1 * uwin[off + k:off + k + ts, :], axis=0, keepdims=True)
        dcf_ref[:, :C] = (du0 * sg).astype(dcf_ref.dtype)
        dcf_ref[:, C:] = (du0 * cv * sg * (1.0 - sg)).astype(dcf_ref.dtype)

    ucol = dm.A // C
    return pl.pallas_call(
        body, name="conv_bwd", grid=(nt,),
        in_specs=[pl.BlockSpec((ts, C), lambda i: (i, ucol)),
                  pl.BlockSpec((HALO, C), lambda i: (jnp.minimum((i + 1) * per, last_halo), ucol)),
                  pl.BlockSpec((ts, C), lambda i: (i, 0)),
                  pl.BlockSpec((HALO, C), lambda i: (jnp.minimum((i + 1) * per, last_halo), 0)),
                  pl.BlockSpec((ts, 2 * C), lambda i: (i, 0)),
                  pl.BlockSpec((HALO, 2 * C), lambda i: (jnp.maximum(i * per - 1, 0), 0)),
                  pl.BlockSpec((HALO, C), lambda i: (0, 0)), _vec_spec(C), _vec_spec(C)],
        out_specs=[_row_spec(ts, 2 * C), pl.BlockSpec((HALO, C), lambda i: (0, 0))] + [_vec_spec(C)] * 3,
        out_shape=[jax.ShapeDtypeStruct((dm.S, 2 * C), BF16), jax.ShapeDtypeStruct((HALO, C), F32)]
        + [jax.ShapeDtypeStruct((1, C), F32)] * 3,
        compiler_params=_cp(1),
    )(dcat, dcat, u1, u1, cf, cf, cw, lg, lb)


def _ada_fwd(c16, ada_w, ada_b_cols, dm):
    L, D, n = ada_w.shape
    tn = _tile(n, 512)

    def body(c_ref, w_ref, b_ref, o_ref, a_ref):
        cv = c_ref[...]
        act = (cv * _sigmoid(cv)).astype(BF16)
        a_ref[...] = act
        o_ref[...] = jnp.dot(act, w_ref[...].astype(BF16), preferred_element_type=F32) + b_ref[...]

    return pl.pallas_call(
        body, name="ada_fwd", grid=(L, n // tn),
        in_specs=[pl.BlockSpec((16, D), lambda l, j: (0, 0)), pl.BlockSpec((None, D, tn), lambda l, j: (l, 0, j)),
                  pl.BlockSpec((None, 1, tn), lambda l, j: (l, 0, j))],
        out_specs=[pl.BlockSpec((None, 16, tn), lambda l, j: (l, 0, j)), pl.BlockSpec((16, D), lambda l, j: (0, 0))],
        out_shape=[jax.ShapeDtypeStruct((L, 16, n), F32), jax.ShapeDtypeStruct((16, D), BF16)],
        compiler_params=_cp(2),
    )(c16, ada_w, ada_b_cols)


def _ada_bwd(act16, dmod16):
    L, _, n = dmod16.shape
    D = act16.shape[1]
    tm = min(TN_TM, D)

    def body(a_ref, d_ref, o_ref):
        o_ref[...] = _dot_tn(a_ref[...], d_ref[...])

    return pl.pallas_call(
        body, name="ada_bwd", grid=(L, D // tm),
        in_specs=[pl.BlockSpec((16, tm), lambda l, i: (0, i)), pl.BlockSpec((None, 16, n), lambda l, i: (l, 0, 0))],
        out_specs=pl.BlockSpec((None, tm, n), lambda l, i: (l, i, 0)),
        out_shape=jax.ShapeDtypeStruct((L, D, n), F32), compiler_params=_cp(2),
    )(act16, dmod16)


def _sum_devices(g8):
    _, R, _ = g8.shape
    tr = _rows_tile(R)

    def body(g_ref, o_ref):
        acc = g_ref[0]
        for d in range(1, N_DEV):
            acc = acc + g_ref[d]
        o_ref[...] = acc

    return pl.pallas_call(
        body, name="sum_devices", grid=(R // tr,),
        in_specs=[pl.BlockSpec((N_DEV, tr, LANES), lambda i: (0, i, 0))],
        out_specs=pl.BlockSpec((tr, LANES), lambda i: (i, 0)),
        out_shape=jax.ShapeDtypeStruct((R, LANES), F32), compiler_params=_cp(1),
    )(g8)


def _rows_tile(r, cap=512):
    for t in (512, 256, 128, 64, 32, 16, 8):
        if t <= cap and r % t == 0:
            return t
    return r


def _adam_math(w, g, m, v):
    m = ADAM_B1 * m + (1.0 - ADAM_B1) * g
    v = ADAM_B2 * v + (1.0 - ADAM_B2) * (g * g)
    m_hat = m / (1.0 - ADAM_B1 ** ADAM_STEP)
    v_hat = v / (1.0 - ADAM_B2 ** ADAM_STEP)
    delta = -ADAM_LR * (m_hat / (jnp.sqrt(v_hat) + ADAM_EPS) + ADAM_WD * w)
    return delta, m, v


def _adamw_dense(w, m, v, g, name):
    R, Cc = w.shape
    tr = _rows_tile(R, 128)

    def body(w_ref, m_ref, v_ref, g_ref, d_ref, mo_ref, vo_ref):
        d, mn, vn = _adam_math(w_ref[...], g_ref[...], m_ref[...], v_ref[...])
        d_ref[...] = d
        mo_ref[...] = mn
        vo_ref[...] = vn

    spec = pl.BlockSpec((tr, Cc), lambda i: (i, 0))
    return pl.pallas_call(
        body, name=name, grid=(R // tr,), in_specs=[spec] * 4, out_specs=[spec] * 3,
        out_shape=[jax.ShapeDtypeStruct((R, Cc), F32)] * 3, compiler_params=_cp(1),
    )(w, m, v, g)


def _adamw_shard(w, m, v, near, far, layer, prev, name):
    L, r, cc = w.shape
    tr = _rows_tile(r, 128)

    def body(w_ref, m_ref, v_ref, n_ref, f_ref, *rest):
        g_ref, d_ref, mo_ref, vo_ref = rest[-4:]
        g = n_ref[0].astype(F32) + f_ref[0].astype(F32)
        for k in range(1, N_CHIPS):
            g = g + (n_ref[k].astype(F32) + f_ref[k].astype(F32))
        d, mn, vn = _adam_math(w_ref[...], g, m_ref[...], v_ref[...])
        g_ref[...] = g
        d_ref[...] = d
        mo_ref[...] = mn
        vo_ref[...] = vn

    wspec = pl.BlockSpec((None, tr, cc), lambda i: (layer, i, 0))
    sspec = pl.BlockSpec((N_CHIPS, tr, cc), lambda i: (0, i, 0))
    n_prev = 0 if prev is None else 4
    return pl.pallas_call(
        body, name=name, grid=(r // tr,),
        in_specs=[wspec] * 3 + [sspec] * 2 + [pl.BlockSpec(memory_space=pl.ANY)] * n_prev,
        out_specs=[wspec] * 4,
        out_shape=[jax.ShapeDtypeStruct((L, r, cc), F32)] * 4,
        input_output_aliases={5 + t: t for t in range(n_prev)},
        compiler_params=_cp(1),
    )(w, m, v, near, far, *(prev or ()))


def _pack(vs):
    flat = jnp.concatenate([v.reshape(-1).astype(F32) for v in vs])
    pad = (-flat.shape[0]) % (8 * LANES)
    return jnp.pad(flat, (0, pad)).reshape(-1, LANES)


def _unpack(packed, shapes):
    flat = packed.reshape(-1)
    out, pos = [], 0
    for s in shapes:
        n = 1
        for d in s:
            n *= d
        out.append(flat[pos:pos + n].reshape(s))
        pos += n
    return out


def kernel(x, c, w_in, b_f, conv_w, conv_b, conv_ln_g, conv_ln_b, w_o, w_ffn_in, w_ffn_out, mix_pre_g, mix_post_g, ffn_pre_g, ffn_post_g, ada_w, ada_b, loss_target, m_w_in, m_b_f, m_conv_w, m_conv_b, m_conv_ln_g, m_conv_ln_b, m_w_o, m_w_ffn_in, m_w_ffn_out, m_mix_pre_g, m_mix_post_g, m_ffn_pre_g, m_ffn_post_g, m_ada_w, m_ada_b, v_w_in, v_b_f, v_conv_w, v_conv_b, v_conv_ln_g, v_conv_ln_b, v_w_o, v_w_ffn_in, v_w_ffn_out, v_mix_pre_g, v_mix_post_g, v_ffn_pre_g, v_ffn_post_g, v_ada_w, v_ada_b):
    params = dict(w_in=w_in, b_f=b_f, conv_w=conv_w, conv_b=conv_b, conv_ln_g=conv_ln_g, conv_ln_b=conv_ln_b, w_o=w_o,
                  w_ffn_in=w_ffn_in, w_ffn_out=w_ffn_out, mix_pre_g=mix_pre_g, mix_post_g=mix_post_g,
                  ffn_pre_g=ffn_pre_g, ffn_post_g=ffn_post_g, ada_w=ada_w, ada_b=ada_b)
    mom = dict(w_in=m_w_in, b_f=m_b_f, conv_w=m_conv_w, conv_b=m_conv_b, conv_ln_g=m_conv_ln_g, conv_ln_b=m_conv_ln_b,
               w_o=m_w_o, w_ffn_in=m_w_ffn_in, w_ffn_out=m_w_ffn_out, mix_pre_g=m_mix_pre_g, mix_post_g=m_mix_post_g,
               ffn_pre_g=m_ffn_pre_g, ffn_post_g=m_ffn_post_g, ada_w=m_ada_w, ada_b=m_ada_b)
    var = dict(w_in=v_w_in, b_f=v_b_f, conv_w=v_conv_w, conv_b=v_conv_b, conv_ln_g=v_conv_ln_g, conv_ln_b=v_conv_ln_b,
               w_o=v_w_o, w_ffn_in=v_w_ffn_in, w_ffn_out=v_w_ffn_out, mix_pre_g=v_mix_pre_g, mix_post_g=v_mix_post_g,
               ffn_pre_g=v_ffn_pre_g, ffn_post_g=v_ffn_post_g, ada_w=v_ada_w, ada_b=v_ada_b)

    S, D = x.shape[1], x.shape[2]
    L = w_in.shape[0]
    A = D // 2
    C = D - A
    H = A // HEAD_DIM
    F = w_ffn_out.shape[1] * N_CHIPS
    d_in = w_in.shape[2] * N_CHIPS
    NP = 3 * A + 2 * C + LANES
    dm = Dims(S=S, D=D, A=A, C=C, H=H, F=F, L=L, NP=NP, ts=min(ROW_TILE, S))
    assert H <= 8 and A == C and d_in == 3 * A + H + 2 * C

    ix, iy, ic = _place()
    chip = 2 * ix + iy
    dev = 4 * ix + 2 * iy + ic
    x2 = x.reshape(S, D)
    tgt = loss_target.reshape(S, D)

    def gather_start(l, after):
        first = _exchange_start([w_in[l].astype(BF16)], False, after, f"gather_a_start_{l}")
        rest = _exchange_start([w_o[l].astype(BF16), w_ffn_in[l].astype(BF16), w_ffn_out[l].astype(BF16)], False,
                               [first["token"]], f"gather_b_start_{l}")
        return first, rest

    def gather_wait(st, after, name):
        srcs, lands = _exchange_wait(st, after, name)
        return [lax.dynamic_update_index_in_dim(land, src, chip, 0) for land, src in zip(lands, srcs)]

    gather = [None] * L
    gather[0] = gather_start(0, [])

    c_all = _all_gather_devices(c.reshape(D // LANES, LANES), "gather_c", after=[gather[0][1]["token"]]).reshape(N_DEV, D)
    c16 = jnp.pad(c_all, ((0, 16 - N_DEV), (0, 0)))
    n_ada = ada_w.shape[2]
    ada_b_cols = lax.dynamic_slice_in_dim(ada_b, chip * n_ada, n_ada, axis=1).reshape(L, 1, n_ada)
    mod_cols, act16 = _ada_fwd(c16, ada_w, ada_b_cols, dm)
    conv_w_all, mod_all = _all_gather_chips([conv_w.reshape(L * CONV_K, -1), mod_cols.reshape(L * 16, n_ada)], "gather_mod")
    cwc = conv_w.shape[2]
    conv_w_full = conv_w_all.reshape(N_CHIPS, L, CONV_K, cwc).transpose(1, 2, 0, 3).reshape(L, CONV_K, C)
    conv_w_full = jnp.pad(conv_w_full, ((0, 0), (0, HALO - CONV_K), (0, 0)))
    mod_all = mod_all.reshape(N_CHIPS, L, 16, n_ada)
    mod_me = lax.dynamic_index_in_dim(mod_all, dev, axis=2, keepdims=False)
    mod_me = mod_me.transpose(1, 0, 2).reshape(L, N_MOD, 1, D)

    def projection_of(g_in):
        w_nat = g_in.transpose(1, 0, 2).reshape(D, d_in)
        return jnp.concatenate([w_nat[:, :3 * A], w_nat[:, 3 * A + H:], w_nat[:, 3 * A:3 * A + H],
                                jnp.zeros((D, LANES - H), BF16)], axis=1)

    gathered = [None] * L
    vec = lambda p, l: p[l].reshape(1, -1)
    bf_pad = jnp.pad(b_f, ((0, 0), (0, LANES - H)))

    saved = []
    xin = x2
    h = _pre_norm(xin, vec(mix_pre_g, 0), mod_me[0, 1], mod_me[0, 0], dm)
    dx = loss_part = None
    for l in range(L):
        (g_in,) = gather_wait(gather[l][0], [h], f"gather_a_wait_{l}")
        w_p = projection_of(g_in)
        order = []
        if l + 1 < L:
            gather[l + 1] = gather_start(l + 1, [g_in])
            order = [gather[l + 1][1]["token"]]
        qkv = _mm_nn(h, w_p, BF16, "mm_qkv", 0, 3 * A, after=order)
        cf = _mm_nn(h, w_p, F32, "mm_cf", 3 * A, 2 * C + LANES)
        cum, cumt = _gates_fwd(cf, vec(bf_pad, l), dm)
        o, lse = _attn_fwd(qkv, cum, cumt, dm)
        u3, u1 = _conv_fwd(cf, conv_w_full[l], vec(conv_b, l), vec(conv_ln_g, l), vec(conv_ln_b, l), dm)
        cat = jnp.concatenate([o, u3], axis=1)
        g_o, wfi, g_fo = gather_wait(gather[l][1], [cat], f"gather_b_wait_{l}")
        wo, wfo = g_o.reshape(D, D), g_fo.reshape(F, D)
        gathered[l] = (w_p, wo, wfi, wfo)
        y = _mm_nn(cat, wo, F32, "mm_o")
        x1, h2 = _res_norm(xin, y, vec(mix_post_g, l), mod_me[l, 2], vec(ffn_pre_g, l), mod_me[l, 4], mod_me[l, 3], dm)
        gu = _mm_nn_blocked(h2, wfi, F32, "mm_ffn_in")
        a = _swiglu_fwd(gu, dm)
        y2 = _mm_nn(a, wfo, F32, "mm_ffn_out")
        saved.append(dict(xin=xin, h=h, qkv=qkv, cf=cf, cum=cum, cumt=cumt, o=o, lse=lse, u1=u1, cat=cat, y=y,
                          x1=x1, h2=h2, gu=gu, a=a, y2=y2))
        if l + 1 < L:
            xin, h = _res_norm(x1, y2, vec(ffn_post_g, l), mod_me[l, 5], vec(mix_pre_g, l + 1),
                               mod_me[l + 1, 1], mod_me[l + 1, 0], dm)
        else:
            dx, loss_part = _res_loss(x1, y2, vec(ffn_post_g, l), mod_me[l, 5], tgt, dm)
    loss = lax.psum(loss_part[0, 0], ("x", "y", "c"))

    small = [None] * L
    big = [None] * L
    order = []
    for l in reversed(range(L)):
        w_p, wo, wfi, wfo = gathered[l]
        sv = saved[l]
        dy2, d_gfpost, d_g2 = _post_bwd(dx, sv["y2"], vec(ffn_post_g, l), mod_me[l, 5], dm, after=order)
        da = _mm_nt(dy2, wfo, F32, "mm_da")
        dgu = _swiglu_bwd(da, sv["gu"], dm)
        g_wfo = _mm_tn(sv["a"], dy2, BF16, "mm_dwfo")
        g_wfi = _mm_tn(sv["h2"], dgu, BF16, "mm_dwfi", blocked=True)
        dh2 = _mm_nt_blocked(dgu, wfi, F32, "mm_dh2")
        scatter_ffn = _exchange_start([g_wfi, g_wfo.reshape(N_CHIPS, F // N_CHIPS, D)], True, [], f"scatter_b_start_{l}")
        dx1, d_sh2, d_sc2, d_gfpre = _pre_bwd(dh2, dx, sv["x1"], vec(ffn_pre_g, l), mod_me[l, 4], dm,
                                              after=[scatter_ffn["token"]])
        dy, d_gpost, d_g1 = _post_bwd(dx1, sv["y"], vec(mix_post_g, l), mod_me[l, 2], dm)
        dcat = _mm_nt(dy, wo, F32, "mm_dcat")
        g_wo = _mm_tn(sv["cat"], dy, BF16, "mm_dwo")
        dcfc, d_cw, d_cb, d_lg, d_lb = _conv_bwd(dcat, sv["u1"], sv["cf"], conv_w_full[l], vec(conv_ln_g, l),
                                                 vec(conv_ln_b, l), dm)
        delta, do = _attn_delta(dcat, sv["o"], dm)
        dq, dkv, dcq, dckt = _attn_bwd(sv["qkv"], do, sv["lse"], delta, sv["cum"], sv["cumt"], dm)
        dfl, d_bf = _gates_bwd(dcq, dckt, sv["cf"], vec(bf_pad, l), dm)
        dproj = jnp.concatenate([dq.astype(BF16), dkv, dcfc, dfl], axis=1)
        dh = _mm_nt(dproj, w_p, F32, "mm_dh")
        g_wp = _mm_tn(sv["h"], dproj, BF16, "mm_dwp")
        dx, d_sh1, d_sc1, d_gpre = _pre_bwd(dh, dx1, sv["xin"], vec(mix_pre_g, l), mod_me[l, 1], dm)
        g_nat = jnp.concatenate([g_wp[:, :3 * A], g_wp[:, 3 * A + 2 * C:3 * A + 2 * C + H], g_wp[:, 3 * A:3 * A + 2 * C]], axis=1)
        g_win = g_nat.reshape(D, N_CHIPS, d_in // N_CHIPS).transpose(1, 0, 2)
        scatter_mix = _exchange_start([g_win, g_wo.reshape(N_CHIPS, D // N_CHIPS, D)], True, [], f"scatter_a_start_{l}")
        order = [scatter_mix["token"]]
        big[l] = (scatter_mix, scatter_ffn)
        small[l] = dict(b_f=d_bf[0, :H], conv_b=d_cb[0], conv_ln_g=d_lg[0], conv_ln_b=d_lb[0], mix_pre_g=d_gpre[0],
                        mix_post_g=d_gpost[0], ffn_pre_g=d_gfpre[0], ffn_post_g=d_gfpost[0],
                        dmod=jnp.concatenate([d_sh1, d_sc1, d_g1, d_sh2, d_sc2, d_g2], axis=1)[0],
                        conv_w=d_cw[:CONV_K])
    grad_x = dx.reshape(1, S, D)

    keys_small = ["b_f", "conv_b", "conv_ln_g", "conv_ln_b", "mix_pre_g", "mix_post_g", "ffn_pre_g", "ffn_post_g",
                  "dmod", "conv_w"]
    stacked = [jnp.stack([small[l][k] for l in range(L)]) for k in keys_small]
    shapes = [s.shape for s in stacked]
    pack = _pack(stacked)
    pack8 = _all_gather_devices(pack, "gather_small", after=order)
    summed = dict(zip(keys_small, _unpack(_sum_devices(pack8), shapes)))
    dmod_all = jnp.stack([_unpack(pack8[d], shapes)[keys_small.index("dmod")] for d in range(N_DEV)])
    grads = {k: summed[k] for k in keys_small[:8]}
    grads["ada_b"] = summed["dmod"]
    grads["conv_w"] = lax.dynamic_slice_in_dim(summed["conv_w"], chip * cwc, cwc, axis=2)

    dmod_cols = lax.dynamic_slice_in_dim(dmod_all.reshape(N_DEV, L, N_CHIPS, n_ada), chip, 1, axis=2)
    dmod16 = jnp.pad(dmod_cols.reshape(N_DEV, L, n_ada).transpose(1, 0, 2), ((0, 0), (0, 16 - N_DEV), (0, 0))).astype(BF16)
    grads["ada_w"] = _ada_bwd(act16, dmod16)

    d_aw, m_aw, v_aw = _adamw_dense(ada_w.reshape(L * D, n_ada), m_ada_w.reshape(L * D, n_ada),
                                    v_ada_w.reshape(L * D, n_ada), grads["ada_w"].reshape(L * D, n_ada), "adamw_ada_w")
    names_small = ["b_f", "conv_w", "conv_b", "conv_ln_g", "conv_ln_b", "mix_pre_g", "mix_post_g", "ffn_pre_g",
                   "ffn_post_g", "ada_b"]
    shapes_small = [params[n].shape for n in names_small]
    d_s, m_s, v_s = _adamw_dense(_pack([params[n] for n in names_small]), _pack([mom[n] for n in names_small]),
                                 _pack([var[n] for n in names_small]), _pack([grads[n] for n in names_small]),
                                 "adamw_small")
    delta_w = dict(zip(names_small, _unpack(d_s, shapes_small)))
    new_m = dict(zip(names_small, _unpack(m_s, shapes_small)))
    new_v = dict(zip(names_small, _unpack(v_s, shapes_small)))
    delta_w["ada_w"], new_m["ada_w"], new_v["ada_w"] = (t.reshape(L, D, n_ada) for t in (d_aw, m_aw, v_aw))

    names_big = ["w_in", "w_o", "w_ffn_in", "w_ffn_out"]
    res_big = {n: None for n in names_big}
    after = [dx]
    for l in reversed(range(L)):
        if l == 0:
            after = after + [d_s, d_aw]
        near = []
        for st, nm in zip(big[l], ("a", "b")):
            srcs, lands = _exchange_wait(st, after, f"scatter_{nm}_wait_{l}")
            near += [lax.dynamic_update_index_in_dim(land, lax.dynamic_index_in_dim(src, chip, 0, keepdims=False), chip, 0)
                     for land, src in zip(lands, srcs)]
        far = _sibling_forward(near, "forward_grads")
        for t, n in enumerate(names_big):
            res_big[n] = _adamw_shard(params[n], mom[n], var[n], near[t], far[t], l, res_big[n], f"adamw_{n}_{l}")
        after = [res_big[names_big[-1]][1]]
    for n in names_big:
        grads[n], delta_w[n], new_m[n], new_v[n] = res_big[n]

    return (loss, grad_x, *[grads[n] for n in WEIGHTS], *[delta_w[n] for n in WEIGHTS],
            *[new_m[n] for n in WEIGHTS], *[new_v[n] for n in WEIGHTS])
```

```python
import collections
import functools

import jax
import jax.numpy as jnp
from jax import lax
from jax.experimental import pallas as pl
from jax.experimental.pallas import tpu as pltpu

F32 = jnp.float32
BF16 = jnp.bfloat16
MESH = pl.DeviceIdType.MESH

HEAD_DIM = 64
CONV_K = 31
N_MOD = 6
EPS = 1e-6
N_CHIPS = 4
N_DEV = 8
LANES = 128
HALO = 32
ROW_TILE = 256
MM_TM = 512
MM_TN_MAX = 1408
TN_TM = 256
VMEM_LIMIT = 56 * 1024 * 1024

ADAM_LR = 0.001
ADAM_B1 = 0.9
ADAM_B2 = 0.999
ADAM_EPS = 1e-08
ADAM_WD = 0.01
ADAM_STEP = 10

WEIGHTS = ['w_in', 'b_f', 'conv_w', 'conv_b', 'conv_ln_g', 'conv_ln_b', 'w_o', 'w_ffn_in', 'w_ffn_out',
           'mix_pre_g', 'mix_post_g', 'ffn_pre_g', 'ffn_post_g', 'ada_w', 'ada_b']

Dims = collections.namedtuple("Dims", "S D A C H F L NP ts")


def _cp(n_grid=0):
    if n_grid:
        return pltpu.CompilerParams(dimension_semantics=("arbitrary",) * n_grid, vmem_limit_bytes=VMEM_LIMIT)
    return pltpu.CompilerParams(vmem_limit_bytes=VMEM_LIMIT)


def _tile(n, cap, also=None):
    best = None
    t = LANES
    while t <= min(n, cap):
        if n % t == 0 and (also is None or also % t == 0):
            best = t
        t += LANES
    assert best is not None, (n, cap, also)
    return best


def _bf(v):
    return v if v.dtype == BF16 else v.astype(BF16)


def _place():
    return lax.axis_index("x"), lax.axis_index("y"), lax.axis_index("c")


def _flip(v, d):
    return 1 - v if d else v


def _all_gather_devices(a, name, after=()):
    def body(a_ref, *rest):
        o_ref, send, recv, lsem = rest[len(after):]
        x, y, c = _place()
        me = 4 * x + 2 * y + c
        local = pltpu.make_async_copy(a_ref, o_ref.at[me], lsem)
        local.start()
        copies = []
        for k in range(1, N_DEV):
            peer = (_flip(x, (k >> 2) & 1), _flip(y, (k >> 1) & 1), _flip(c, k & 1))
            cp = pltpu.make_async_remote_copy(src_ref=a_ref, dst_ref=o_ref.at[me], send_sem=send.at[k - 1],
                                              recv_sem=recv.at[k - 1], device_id=peer, device_id_type=MESH)
            cp.start()
            copies.append(cp)
        for cp in copies:
            cp.wait()
        local.wait()

    return pl.pallas_call(
        body, name=name,
        out_shape=jax.ShapeDtypeStruct((N_DEV,) + a.shape, a.dtype),
        in_specs=[pl.BlockSpec(memory_space=pl.ANY)] * (1 + len(after)),
        out_specs=pl.BlockSpec(memory_space=pl.ANY),
        scratch_shapes=[pltpu.SemaphoreType.DMA((N_DEV - 1,)), pltpu.SemaphoreType.DMA((N_DEV - 1,)),
                        pltpu.SemaphoreType.DMA],
    )(a, *after)


def _all_gather_chips(arrays, name):
    n = len(arrays)

    def body(*refs):
        a_refs, o_refs = refs[:n], refs[n:2 * n]
        send, recv, lsem = refs[2 * n:]
        x, y, c = _place()
        me = 2 * x + y
        copies = []
        for i in range(n):
            local = pltpu.make_async_copy(a_refs[i], o_refs[i].at[me], lsem.at[i])
            local.start()
            copies.append(local)
            for k in range(1, N_CHIPS):
                peer = (_flip(x, (k >> 1) & 1), _flip(y, k & 1), c)
                cp = pltpu.make_async_remote_copy(src_ref=a_refs[i], dst_ref=o_refs[i].at[me],
                                                  send_sem=send.at[i, k - 1], recv_sem=recv.at[i, k - 1],
                                                  device_id=peer, device_id_type=MESH)
                cp.start()
                copies.append(cp)
        for cp in copies:
            cp.wait()

    return pl.pallas_call(
        body, name=name,
        out_shape=[jax.ShapeDtypeStruct((N_CHIPS,) + a.shape, a.dtype) for a in arrays],
        in_specs=[pl.BlockSpec(memory_space=pl.ANY)] * n,
        out_specs=[pl.BlockSpec(memory_space=pl.ANY)] * n,
        scratch_shapes=[pltpu.SemaphoreType.DMA((n, N_CHIPS - 1)), pltpu.SemaphoreType.DMA((n, N_CHIPS - 1)),
                        pltpu.SemaphoreType.DMA((n,))],
    )(*arrays)


_HBM = pl.BlockSpec(memory_space=pltpu.HBM)
_SEM = pl.BlockSpec(memory_space=pltpu.SEMAPHORE)
_EFFECT = pltpu.SideEffectType.DATAFLOW_SIDE_EFFECTING


def _n_copies(mode):
    return 1 if mode == "sibling" else N_CHIPS - 1


def _chip_copies(srcs, lands, send, recv, mode):
    x, y, c = _place()
    me = 2 * x + y
    copies = []
    for i in range(len(srcs)):
        if mode == "sibling":
            copies.append(pltpu.make_async_remote_copy(src_ref=srcs[i], dst_ref=lands[i], send_sem=send.at[i],
                                                       recv_sem=recv.at[i], device_id=(x, y, 1 - c), device_id_type=MESH))
            continue
        for k in range(1, N_CHIPS):
            px, py = _flip(x, (k >> 1) & 1), _flip(y, k & 1)
            src = srcs[i].at[2 * px + py] if mode == "scatter" else srcs[i]
            s = i * (N_CHIPS - 1) + k - 1
            copies.append(pltpu.make_async_remote_copy(src_ref=src, dst_ref=lands[i].at[me], send_sem=send.at[s],
                                                       recv_sem=recv.at[s], device_id=(px, py, c), device_id_type=MESH))
    return copies


_ORDER = pl.BlockSpec(memory_space=pl.ANY)


def _exchange_start(arrays, mode, after, name):
    n = len(arrays)

    def body(*refs):
        srcs, lands = refs[:n], refs[n:2 * n]
        send, recv = refs[2 * n + len(after)], refs[2 * n + len(after) + 1]
        token = refs[-1]
        for cp in _chip_copies(srcs, lands, send, recv, mode):
            cp.start()
        token[...] = jnp.zeros_like(token)

    land_shapes = [(N_CHIPS,) + a.shape if mode == "gather" else a.shape for a in arrays]
    n_sem = n * _n_copies(mode)
    outs = pl.pallas_call(
        body, name=name,
        out_shape=(pltpu.SemaphoreType.DMA((n_sem,)), pltpu.SemaphoreType.DMA((n_sem,)),
                   *[pltpu.HBM(a.shape, a.dtype) for a in arrays],
                   *[pltpu.HBM(s, a.dtype) for s, a in zip(land_shapes, arrays)],
                   jax.ShapeDtypeStruct((8, LANES), F32)),
        in_specs=[_HBM] * (2 * n) + [_ORDER] * len(after),
        out_specs=(_SEM, _SEM, *[_HBM] * (2 * n), pl.BlockSpec(memory_space=pltpu.VMEM)),
        input_output_aliases={i: 2 + i for i in range(2 * n)},
        compiler_params=pltpu.CompilerParams(has_side_effects=_EFFECT),
    )(*[pltpu.with_memory_space_constraint(a, pltpu.HBM) for a in arrays],
      *[pltpu.with_memory_space_constraint(lax.empty(s, a.dtype), pltpu.HBM) for s, a in zip(land_shapes, arrays)],
      *after)
    return dict(send=outs[0], recv=outs[1], srcs=outs[2:2 + n], lands=outs[2 + n:2 + 2 * n], token=outs[-1], mode=mode)


def _exchange_wait(st, after, name):
    n = len(st["srcs"])
    mode = st["mode"]

    def body(*refs):
        srcs, lands = refs[:n], refs[n:2 * n]
        send, recv = refs[2 * n], refs[2 * n + 1]
        for cp in _chip_copies(srcs, lands, send, recv, mode):
            cp.wait_send()
            cp.wait_recv()

    outs = pl.pallas_call(
        body, name=name,
        out_shape=tuple(pltpu.HBM(a.shape, a.dtype) for a in (*st["srcs"], *st["lands"])),
        in_specs=[_HBM] * (2 * n) + [_SEM, _SEM] + [_ORDER] * len(after),
        out_specs=tuple([_HBM] * (2 * n)),
        input_output_aliases={i: i for i in range(2 * n)},
        compiler_params=pltpu.CompilerParams(has_side_effects=_EFFECT),
    )(*st["srcs"], *st["lands"], st["send"], st["recv"], *after)
    return outs[:n], outs[n:]


def _matmul(a, b, contract, grid, a_spec, b_spec, o_spec, out_shape, name, nk=1, acc_shape=None, after=()):
    def body(a_ref, b_ref, *rest):
        o_ref, acc = rest[len(after)], rest[len(after) + 1:]
        r = lax.dot_general(_bf(a_ref[...]), _bf(b_ref[...]), (contract, ((), ())), preferred_element_type=F32)
        if nk == 1:
            o_ref[...] = r.astype(o_ref.dtype)
        else:
            k = pl.program_id(len(grid) - 1)

            @pl.when(k == 0)
            def _():
                acc[0][...] = r

            @pl.when(k > 0)
            def _():
                acc[0][...] += r

            @pl.when(k == nk - 1)
            def _():
                o_ref[...] = acc[0][...].astype(o_ref.dtype)

    return pl.pallas_call(
        body, name=name, grid=grid, in_specs=[a_spec, b_spec] + [_ORDER] * len(after), out_specs=o_spec,
        out_shape=out_shape, scratch_shapes=[pltpu.VMEM(acc_shape, F32)] if nk > 1 else [],
        compiler_params=_cp(len(grid)),
    )(a, b, *after)


def _mm_nn(a, b, out_dtype, name, col0=0, n=None, after=()):
    m, k = a.shape
    n = b.shape[1] - col0 if n is None else n
    tm = min(MM_TM, m)
    tn = _tile(n, MM_TN_MAX, also=col0 if col0 else None)
    off = col0 // tn
    return _matmul(a, b, ((1,), (0,)), (n // tn, m // tm),
                   pl.BlockSpec((tm, k), lambda j, i: (i, 0)),
                   pl.BlockSpec((k, tn), lambda j, i: (0, j + off)),
                   pl.BlockSpec((tm, tn), lambda j, i: (i, j)),
                   jax.ShapeDtypeStruct((m, n), out_dtype), name, after=after)


def _mm_nn_blocked(a, b3, out_dtype, name):
    m, k = a.shape
    nj, _, nb = b3.shape
    tm = min(MM_TM, m)
    return _matmul(a, b3, ((1,), (0,)), (nj, m // tm),
                   pl.BlockSpec((tm, k), lambda j, i: (i, 0)),
                   pl.BlockSpec((None, k, nb), lambda j, i: (j, 0, 0)),
                   pl.BlockSpec((tm, nb), lambda j, i: (i, j)),
                   jax.ShapeDtypeStruct((m, nj * nb), out_dtype), name)


def _mm_nt(a, b, out_dtype, name):
    m, k = a.shape
    n = b.shape[0]
    tm = min(MM_TM, m)
    tn = _tile(n, MM_TN_MAX)
    return _matmul(a, b, ((1,), (1,)), (n // tn, m // tm),
                   pl.BlockSpec((tm, k), lambda j, i: (i, 0)),
                   pl.BlockSpec((tn, k), lambda j, i: (j, 0)),
                   pl.BlockSpec((tm, tn), lambda j, i: (i, j)),
                   jax.ShapeDtypeStruct((m, n), out_dtype), name)


def _mm_nt_blocked(a, b3, out_dtype, name):
    m = a.shape[0]
    nj, n, nb = b3.shape
    tm = min(MM_TM, m)
    tn = _tile(n, 512)
    return _matmul(a, b3, ((1,), (1,)), (n // tn, m // tm, nj),
                   pl.BlockSpec((tm, nb), lambda j, i, k: (i, k)),
                   pl.BlockSpec((None, tn, nb), lambda j, i, k: (k, j, 0)),
                   pl.BlockSpec((tm, tn), lambda j, i, k: (i, j)),
                   jax.ShapeDtypeStruct((m, n), out_dtype), name, nk=nj, acc_shape=(tm, tn))


def _mm_tn(a, b, out_dtype, name, blocked=False):
    k, m = a.shape
    n = b.shape[1]
    tm = min(TN_TM, m)
    tn = n // N_CHIPS if blocked else _tile(n, MM_TN_MAX)
    if blocked:
        o_spec = pl.BlockSpec((None, tm, tn), lambda j, i: (j, i, 0))
        out_shape = jax.ShapeDtypeStruct((n // tn, m, tn), out_dtype)
    else:
        o_spec = pl.BlockSpec((tm, tn), lambda j, i: (i, j))
        out_shape = jax.ShapeDtypeStruct((m, n), out_dtype)
    return _matmul(a, b, ((0,), (0,)), (n // tn, m // tm),
                   pl.BlockSpec((k, tm), lambda j, i: (0, i)),
                   pl.BlockSpec((k, tn), lambda j, i: (0, j)),
                   o_spec, out_shape, name)


def _vec_spec(d):
    return pl.BlockSpec((1, d), lambda i: (0, 0))


def _row_spec(ts, d, col=0):
    return pl.BlockSpec((ts, d), lambda i: (i, col))


def _rms(x):
    return lax.rsqrt(jnp.mean(x * x, axis=-1, keepdims=True) + EPS)


def _pre_norm(x, gain, scale, shift, dm):
    def body(x_ref, g_ref, sc_ref, sh_ref, h_ref):
        xv = x_ref[...]
        h_ref[...] = (((xv * _rms(xv)) * g_ref[...]) * (1.0 + sc_ref[...]) + sh_ref[...]).astype(h_ref.dtype)

    return pl.pallas_call(
        body, name="pre_norm", grid=(dm.S // dm.ts,),
        in_specs=[_row_spec(dm.ts, dm.D)] + [_vec_spec(dm.D)] * 3,
        out_specs=_row_spec(dm.ts, dm.D),
        out_shape=jax.ShapeDtypeStruct((dm.S, dm.D), BF16), compiler_params=_cp(1),
    )(x, gain, scale, shift)


def _res_norm(x, y, gpost, gate, gain, scale, shift, dm):
    def body(x_ref, y_ref, gp_ref, gt_ref, g_ref, sc_ref, sh_ref, xo_ref, h_ref):
        yv = y_ref[...]
        xn = x_ref[...] + gt_ref[...] * ((yv * _rms(yv)) * gp_ref[...])
        xo_ref[...] = xn
        h_ref[...] = (((xn * _rms(xn)) * g_ref[...]) * (1.0 + sc_ref[...]) + sh_ref[...]).astype(h_ref.dtype)

    return pl.pallas_call(
        body, name="res_norm", grid=(dm.S // dm.ts,),
        in_specs=[_row_spec(dm.ts, dm.D)] * 2 + [_vec_spec(dm.D)] * 5,
        out_specs=[_row_spec(dm.ts, dm.D)] * 2,
        out_shape=[jax.ShapeDtypeStruct((dm.S, dm.D), F32), jax.ShapeDtypeStruct((dm.S, dm.D), BF16)],
        compiler_params=_cp(1),
    )(x, y, gpost, gate, gain, scale, shift)


def _res_loss(x, y, gpost, gate, target, dm):
    def body(x_ref, y_ref, gp_ref, gt_ref, t_ref, dx_ref, loss_ref):
        i = pl.program_id(0)

        @pl.when(i == 0)
        def _():
            loss_ref[...] = jnp.zeros_like(loss_ref)
        yv = y_ref[...]
        err = x_ref[...] + gt_ref[...] * ((yv * _rms(yv)) * gp_ref[...]) - t_ref[...]
        dx_ref[...] = err * (1.0 / dm.D)
        per_row = jnp.mean(err * err, axis=-1, keepdims=True)
        loss_ref[...] += 0.5 * jnp.sum(per_row, axis=0, keepdims=True)

    return pl.pallas_call(
        body, name="res_loss", grid=(dm.S // dm.ts,),
        in_specs=[_row_spec(dm.ts, dm.D)] * 2 + [_vec_spec(dm.D)] * 2 + [_row_spec(dm.ts, dm.D)],
        out_specs=[_row_spec(dm.ts, dm.D), _vec_spec(LANES)],
        out_shape=[jax.ShapeDtypeStruct((dm.S, dm.D), F32), jax.ShapeDtypeStruct((1, LANES), F32)],
        compiler_params=_cp(1),
    )(x, y, gpost, gate, target)


def _post_bwd(dxo, y, gpost, gate, dm, after=()):
    def body(dx_ref, y_ref, gp_ref, gt_ref, *rest):
        dy_ref, dgp_ref, dgt_ref = rest[len(after):]
        i = pl.program_id(0)

        @pl.when(i == 0)
        def _():
            dgp_ref[...] = jnp.zeros_like(dgp_ref)
            dgt_ref[...] = jnp.zeros_like(dgt_ref)
        yv, dx = y_ref[...], dx_ref[...]
        r = _rms(yv)
        t = yv * r
        dgp_ref[...] += jnp.sum(dx * gt_ref[...] * t, axis=0, keepdims=True)
        dgt_ref[...] += jnp.sum(dx * (t * gp_ref[...]), axis=0, keepdims=True)
        dt = dx * (gt_ref[...] * gp_ref[...])
        dy_ref[...] = (r * (dt - t * jnp.mean(dt * t, axis=-1, keepdims=True))).astype(dy_ref.dtype)

    return pl.pallas_call(
        body, name="post_bwd", grid=(dm.S // dm.ts,),
        in_specs=[_row_spec(dm.ts, dm.D)] * 2 + [_vec_spec(dm.D)] * 2 + [_ORDER] * len(after),
        out_specs=[_row_spec(dm.ts, dm.D), _vec_spec(dm.D), _vec_spec(dm.D)],
        out_shape=[jax.ShapeDtypeStruct((dm.S, dm.D), BF16)] + [jax.ShapeDtypeStruct((1, dm.D), F32)] * 2,
        compiler_params=_cp(1),
    )(dxo, y, gpost, gate, *after)


def _pre_bwd(dh, dxo, x, gain, scale, dm, after=()):
    def body(dh_ref, dxo_ref, x_ref, g_ref, sc_ref, *rest):
        dx_ref, dsh_ref, dsc_ref, dg_ref = rest[len(after):]
        i = pl.program_id(0)

        @pl.when(i == 0)
        def _():
            dsh_ref[...] = jnp.zeros_like(dsh_ref)
            dsc_ref[...] = jnp.zeros_like(dsc_ref)
            dg_ref[...] = jnp.zeros_like(dg_ref)
        xv, dh_ = x_ref[...], dh_ref[...]
        r = _rms(xv)
        nrm = xv * r
        one_sc = 1.0 + sc_ref[...]
        dsh_ref[...] += jnp.sum(dh_, axis=0, keepdims=True)
        dsc_ref[...] += jnp.sum(dh_ * (nrm * g_ref[...]), axis=0, keepdims=True)
        dg_ref[...] += jnp.sum(dh_ * nrm * one_sc, axis=0, keepdims=True)
        dn = dh_ * (g_ref[...] * one_sc)
        dx_ref[...] = dxo_ref[...] + r * (dn - nrm * jnp.mean(dn * nrm, axis=-1, keepdims=True))

    return pl.pallas_call(
        body, name="pre_bwd", grid=(dm.S // dm.ts,),
        in_specs=[_row_spec(dm.ts, dm.D)] * 3 + [_vec_spec(dm.D)] * 2 + [_ORDER] * len(after),
        out_specs=[_row_spec(dm.ts, dm.D)] + [_vec_spec(dm.D)] * 3,
        out_shape=[jax.ShapeDtypeStruct((dm.S, dm.D), F32)] + [jax.ShapeDtypeStruct((1, dm.D), F32)] * 3,
        compiler_params=_cp(1),
    )(dh, dxo, x, gain, scale, *after)


def _sigmoid(z):
    return 1.0 / (1.0 + jnp.exp(-z))


def _swiglu_fwd(gu, dm):
    def body(g_ref, u_ref, a_ref):
        g = g_ref[...]
        a_ref[...] = (g * _sigmoid(g) * u_ref[...]).astype(a_ref.dtype)

    return pl.pallas_call(
        body, name="swiglu_fwd", grid=(dm.S // dm.ts,),
        in_specs=[_row_spec(dm.ts, dm.F, 0), _row_spec(dm.ts, dm.F, 1)],
        out_specs=_row_spec(dm.ts, dm.F),
        out_shape=jax.ShapeDtypeStruct((dm.S, dm.F), BF16), compiler_params=_cp(1),
    )(gu, gu)


def _swiglu_bwd(da, gu, dm):
    def body(da_ref, g_ref, u_ref, d_ref):
        g, u, da_ = g_ref[...], u_ref[...], da_ref[...]
        sg = _sigmoid(g)
        d_ref[:, :dm.F] = (da_ * u * (sg * (1.0 + g * (1.0 - sg)))).astype(d_ref.dtype)
        d_ref[:, dm.F:] = (da_ * (g * sg)).astype(d_ref.dtype)

    return pl.pallas_call(
        body, name="swiglu_bwd", grid=(dm.S // dm.ts,),
        in_specs=[_row_spec(dm.ts, dm.F), _row_spec(dm.ts, dm.F, 0), _row_spec(dm.ts, dm.F, 1)],
        out_specs=_row_spec(dm.ts, 2 * dm.F),
        out_shape=jax.ShapeDtypeStruct((dm.S, 2 * dm.F), BF16), compiler_params=_cp(1),
    )(da, gu, gu)


def _tri(n, upper):
    r = lax.broadcasted_iota(jnp.int32, (n, n), 0)
    c = lax.broadcasted_iota(jnp.int32, (n, n), 1)
    return (c >= r if upper else r >= c).astype(F32)


def _gates_fwd(cf, bf, dm):
    ts = dm.ts
    fcol = 2 * dm.C // LANES

    def body(f_ref, b_ref, cum_ref, cumt_ref, carry):
        i = pl.program_id(0)

        @pl.when(i == 0)
        def _():
            carry[...] = jnp.zeros_like(carry)
        z = f_ref[...] + b_ref[...]
        lf = jnp.minimum(z, 0.0) - jnp.log(1.0 + jnp.exp(-jnp.abs(z)))
        cs = jnp.dot(_tri(ts, False), lf, precision=lax.Precision.HIGHEST, preferred_element_type=F32) + carry[...]
        cum_ref[...] = cs
        cumt_ref[...] = cs.T[:8, :]
        carry[...] = cs[ts - 1:ts, :]

    return pl.pallas_call(
        body, name="gates_fwd", grid=(dm.S // ts,),
        in_specs=[pl.BlockSpec((ts, LANES), lambda i: (i, fcol)), _vec_spec(LANES)],
        out_specs=[_row_spec(ts, LANES), pl.BlockSpec((None, 8, ts), lambda i: (i, 0, 0))],
        out_shape=[jax.ShapeDtypeStruct((dm.S, LANES), F32), jax.ShapeDtypeStruct((dm.S // ts, 8, ts), F32)],
        scratch_shapes=[pltpu.VMEM((1, LANES), F32)], compiler_params=_cp(1),
    )(cf, bf)


def _gates_bwd(dcq, dckt, cf, bf, dm):
    ts = dm.ts
    nb = dm.S // ts
    fcol = 2 * dm.C // LANES

    def body(dcq_ref, dck_ref, f_ref, b_ref, df_ref, db_ref, carry):
        i = pl.program_id(0)

        @pl.when(i == 0)
        def _():
            carry[...] = jnp.zeros_like(carry)
            db_ref[...] = jnp.zeros_like(db_ref)
        dck = jnp.concatenate([dck_ref[...], jnp.zeros((LANES - 8, ts), F32)], axis=0).T
        dc = dcq_ref[...] + dck
        dlf = jnp.dot(_tri(ts, True), dc, precision=lax.Precision.HIGHEST, preferred_element_type=F32) + carry[...]
        carry[...] = dlf[0:1, :]
        dz = dlf * (1.0 - _sigmoid(f_ref[...] + b_ref[...]))
        df_ref[...] = dz.astype(df_ref.dtype)
        db_ref[...] += jnp.sum(dz, axis=0, keepdims=True)

    return pl.pallas_call(
        body, name="gates_bwd", grid=(nb,),
        in_specs=[pl.BlockSpec((ts, LANES), lambda i: (nb - 1 - i, 0)),
                  pl.BlockSpec((None, 8, ts), lambda i: (nb - 1 - i, 0, 0)),
                  pl.BlockSpec((ts, LANES), lambda i: (nb - 1 - i, fcol)), _vec_spec(LANES)],
        out_specs=[pl.BlockSpec((ts, LANES), lambda i: (nb - 1 - i, 0)), _vec_spec(LANES)],
        out_shape=[jax.ShapeDtypeStruct((dm.S, LANES), BF16), jax.ShapeDtypeStruct((1, LANES), F32)],
        scratch_shapes=[pltpu.VMEM((1, LANES), F32)], compiler_params=_cp(1),
    )(dcq, dckt, cf, bf)


def _dot_nt(a, b):
    return lax.dot_general(a, b, (((1,), (1,)), ((), ())), preferred_element_type=F32)


def _dot_tn(a, b):
    return lax.dot_general(a, b, (((0,), (0,)), ((), ())), preferred_element_type=F32)


def _attn_fwd(qkv, cum, cumt, dm):
    tq = dm.ts
    A, H = dm.A, dm.H
    scale = HEAD_DIM ** -0.5

    def body(q_ref, kv_ref, cq_ref, ckt_ref, o_ref, lse_ref):
        i = pl.program_id(0)
        row = lax.broadcasted_iota(jnp.int32, (tq, tq), 0)
        col = lax.broadcasted_iota(jnp.int32, (tq, tq), 1)
        lane = lax.broadcasted_iota(jnp.int32, (tq, LANES), 1)
        lse_all = jnp.zeros((tq, LANES), F32)
        for h in range(H):
            lo, hi = h * HEAD_DIM, (h + 1) * HEAD_DIM
            q = q_ref[:, lo:hi]
            cq = cq_ref[:, h:h + 1]

            def step(j, carry, lo=lo, hi=hi, q=q, cq=cq, h=h):
                m, l, acc = carry
                r0 = pl.multiple_of(j * tq, tq)
                k = kv_ref[pl.ds(r0, tq), A + lo:A + hi]
                v = kv_ref[pl.ds(r0, tq), 2 * A + lo:2 * A + hi]
                ck = ckt_ref[j, h:h + 1, :]
                s = _dot_nt(q, k) * scale + cq - ck
                s = jnp.where(row + (i - j) * tq >= col, s, -1e30)
                m_new = jnp.maximum(m, jnp.max(s, axis=1, keepdims=True))
                p = jnp.exp(s - m_new)
                alpha = jnp.exp(m - m_new)
                l = alpha * l + jnp.sum(p, axis=1, keepdims=True)
                acc = alpha * acc + jnp.dot(p.astype(BF16), v, preferred_element_type=F32)
                return m_new, l, acc

            m, l, acc = lax.fori_loop(0, i + 1, step, (jnp.full((tq, 1), -1e30, F32), jnp.zeros((tq, 1), F32),
                                                       jnp.zeros((tq, HEAD_DIM), F32)))
            o_ref[:, lo:hi] = (acc / l).astype(o_ref.dtype)
            lse_all = jnp.where(lane == h, m + jnp.log(l), lse_all)
        lse_ref[...] = lse_all

    nq = dm.S // tq
    return pl.pallas_call(
        body, name="attn_fwd", grid=(nq,),
        in_specs=[pl.BlockSpec((tq, A), lambda i: (i, 0)), pl.BlockSpec((dm.S, 3 * A), lambda i: (0, 0)),
                  _row_spec(tq, LANES), pl.BlockSpec((nq, 8, tq), lambda i: (0, 0, 0))],
        out_specs=[pl.BlockSpec((tq, A), lambda i: (i, 0)), _row_spec(tq, LANES)],
        out_shape=[jax.ShapeDtypeStruct((dm.S, A), BF16), jax.ShapeDtypeStruct((dm.S, LANES), F32)],
        compiler_params=_cp(1),
    )(qkv, qkv, cum, cumt)


def _attn_delta(dcat, o, dm):
    ts = dm.ts

    def body(do_ref, o_ref, dl_ref, dob_ref):
        lane = lax.broadcasted_iota(jnp.int32, (ts, LANES), 1)
        prod = do_ref[...] * o_ref[...].astype(F32)
        dl = jnp.zeros((ts, LANES), F32)
        for h in range(dm.H):
            dl = jnp.where(lane == h, jnp.sum(prod[:, h * HEAD_DIM:(h + 1) * HEAD_DIM], axis=1, keepdims=True), dl)
        dl_ref[...] = dl
        dob_ref[...] = do_ref[...].astype(dob_ref.dtype)

    return pl.pallas_call(
        body, name="attn_delta", grid=(dm.S // ts,),
        in_specs=[_row_spec(ts, dm.A, 0), _row_spec(ts, dm.A)],
        out_specs=[_row_spec(ts, LANES), _row_spec(ts, dm.A)],
        out_shape=[jax.ShapeDtypeStruct((dm.S, LANES), F32), jax.ShapeDtypeStruct((dm.S, dm.A), BF16)],
        compiler_params=_cp(1),
    )(dcat, o)


def _attn_bwd(qkv, do, lse, delta, cum, cumt, dm):
    tq = dm.ts
    A, H = dm.A, dm.H
    nq = dm.S // tq
    scale = HEAD_DIM ** -0.5

    def body(kv_ref, q_ref, do_ref, lse_ref, dl_ref, cq_ref, ckt_ref, dq_ref, dkv_ref, dcq_ref, dckt_ref):
        j = pl.program_id(0)

        @pl.when(j == 0)
        def _():
            dq_ref[...] = jnp.zeros_like(dq_ref)
            dcq_ref[...] = jnp.zeros_like(dcq_ref)
        row = lax.broadcasted_iota(jnp.int32, (tq, tq), 0)
        col = lax.broadcasted_iota(jnp.int32, (tq, tq), 1)
        lane = lax.broadcasted_iota(jnp.int32, (tq, LANES), 1)
        sub = lax.broadcasted_iota(jnp.int32, (8, tq), 0)
        dck_all = jnp.zeros((8, tq), F32)
        for h in range(H):
            lo, hi = h * HEAD_DIM, (h + 1) * HEAD_DIM
            k = kv_ref[:, A + lo:A + hi]
            v = kv_ref[:, 2 * A + lo:2 * A + hi]
            ck = ckt_ref[h:h + 1, :]

            def step(i, carry, lo=lo, hi=hi, k=k, v=v, ck=ck, h=h):
                dk, dv, dck = carry
                r0 = pl.multiple_of(i * tq, tq)
                rows = pl.ds(r0, tq)
                q = q_ref[rows, lo:hi]
                do_ = do_ref[rows, lo:hi]
                s = _dot_nt(q, k) * scale + cq_ref[rows, h:h + 1] - ck
                s = jnp.where(row + (i - j) * tq >= col, s, -1e30)
                p = jnp.exp(s - lse_ref[rows, h:h + 1])
                ds = p * (_dot_nt(do_, v) - dl_ref[rows, h:h + 1])
                dsb = ds.astype(BF16)
                dv = dv + _dot_tn(p.astype(BF16), do_)
                dk = dk + _dot_tn(dsb, q)
                dq_ref[rows, lo:hi] += jnp.dot(dsb, k, preferred_element_type=F32)
                dcq_ref[rows, :] += jnp.where(lane == h, jnp.sum(ds, axis=1, keepdims=True), 0.0)
                dck = dck - jnp.sum(ds, axis=0, keepdims=True)
                return dk, dv, dck

            dk, dv, dck = lax.fori_loop(j, nq, step, (jnp.zeros((tq, HEAD_DIM), F32), jnp.zeros((tq, HEAD_DIM), F32),
                                                      jnp.zeros((1, tq), F32)))
            dkv_ref[:, lo:hi] = (dk * scale).astype(dkv_ref.dtype)
            dkv_ref[:, A + lo:A + hi] = dv.astype(dkv_ref.dtype)
            dck_all = jnp.where(sub == h, dck, dck_all)
        dckt_ref[...] = dck_all

        @pl.when(j == nq - 1)
        def _():
            dq_ref[...] = dq_ref[...] * scale

    full = lambda w: pl.BlockSpec((dm.S, w), lambda j: (0, 0))
    return pl.pallas_call(
        body, name="attn_bwd", grid=(nq,),
        in_specs=[pl.BlockSpec((tq, 3 * A), lambda j: (j, 0)), full(A), full(A), full(LANES), full(LANES), full(LANES),
                  pl.BlockSpec((None, 8, tq), lambda j: (j, 0, 0))],
        out_specs=[full(A), pl.BlockSpec((tq, 2 * A), lambda j: (j, 0)), full(LANES),
                   pl.BlockSpec((None, 8, tq), lambda j: (j, 0, 0))],
        out_shape=[jax.ShapeDtypeStruct((dm.S, A), F32), jax.ShapeDtypeStruct((dm.S, 2 * A), BF16),
                   jax.ShapeDtypeStruct((dm.S, LANES), F32), jax.ShapeDtypeStruct((nq, 8, tq), F32)],
        compiler_params=_cp(1),
    )(qkv, qkv, do, lse, delta, cum, cumt)


def _glu(cf_rows, c):
    return cf_rows[:, :c] * _sigmoid(cf_rows[:, c:2 * c])


def _conv_fwd(cf, cw, cb, lg, lb, dm):
    ts, C = dm.ts, dm.C
    per = ts // HALO

    def body(cf_ref, halo_ref, w_ref, cb_ref, lg_ref, lb_ref, u3_ref, u1_ref):
        i = pl.program_id(0)
        prev = jnp.where(i > 0, _glu(halo_ref[...], C), 0.0)
        win = jnp.concatenate([prev, _glu(cf_ref[...], C)], axis=0)
        u1 = jnp.zeros((ts, C), F32) + cb_ref[...]
        off = HALO - (CONV_K - 1)
        for k in range(CONV_K):
            u1 = u1 + w_ref[k:k + 1, :] * win[off + k:off + k + ts, :]
        u1_ref[...] = u1
        mu = jnp.mean(u1, axis=-1, keepdims=True)
        cen = u1 - mu
        rstd = lax.rsqrt(jnp.mean(cen * cen, axis=-1, keepdims=True) + EPS)
        u2 = cen * rstd * lg_ref[...] + lb_ref[...]
        u3_ref[...] = (u2 * _sigmoid(u2)).astype(u3_ref.dtype)

    return pl.pallas_call(
        body, name="conv_fwd", grid=(dm.S // ts,),
        in_specs=[pl.BlockSpec((ts, 2 * C), lambda i: (i, 0)),
                  pl.BlockSpec((HALO, 2 * C), lambda i: (jnp.maximum(i * per - 1, 0), 0)),
                  pl.BlockSpec((HALO, C), lambda i: (0, 0))] + [_vec_spec(C)] * 3,
        out_specs=[_row_spec(ts, C)] * 2,
        out_shape=[jax.ShapeDtypeStruct((dm.S, C), BF16), jax.ShapeDtypeStruct((dm.S, C), F32)],
        compiler_params=_cp(1),
    )(cf, cf, cw, cb, lg, lb)


def _conv_bwd(dcat, u1, cf, cw, lg, lb, dm):
    ts, C = dm.ts, dm.C
    per = ts // HALO
    nt = dm.S // ts
    last_halo = dm.S // HALO - 1

    def ln_bwd(du3, u1v, lg_v, lb_v):
        mu = jnp.mean(u1v, axis=-1, keepdims=True)
        cen = u1v - mu
        rstd = lax.rsqrt(jnp.mean(cen * cen, axis=-1, keepdims=True) + EPS)
        uhat = cen * rstd
        u2 = uhat * lg_v + lb_v
        sg = _sigmoid(u2)
        du2 = du3 * (sg * (1.0 + u2 * (1.0 - sg)))
        duh = du2 * lg_v
        du1 = rstd * (duh - jnp.mean(duh, axis=-1, keepdims=True) - uhat * jnp.mean(duh * uhat, axis=-1, keepdims=True))
        return du1, du2, uhat

    def body(d_ref, dn_ref, u1_ref, u1n_ref, cf_ref, halo_ref, w_ref, lg_ref, lb_ref,
             dcf_ref, dw_ref, dcb_ref, dlg_ref, dlb_ref):
        i = pl.program_id(0)

        @pl.when(i == 0)
        def _():
            dw_ref[...] = jnp.zeros_like(dw_ref)
            dcb_ref[...] = jnp.zeros_like(dcb_ref)
            dlg_ref[...] = jnp.zeros_like(dlg_ref)
            dlb_ref[...] = jnp.zeros_like(dlb_ref)
        lg_v, lb_v = lg_ref[...], lb_ref[...]
        du1, du2, uhat = ln_bwd(d_ref[...], u1_ref[...], lg_v, lb_v)
        du1n, _, _ = ln_bwd(dn_ref[...], u1n_ref[...], lg_v, lb_v)
        du1n = jnp.where(i < nt - 1, du1n, 0.0)
        dlg_ref[...] += jnp.sum(du2 * uhat, axis=0, keepdims=True)
        dlb_ref[...] += jnp.sum(du2, axis=0, keepdims=True)
        dcb_ref[...] += jnp.sum(du1, axis=0, keepdims=True)
        dwin = jnp.concatenate([du1, du1n], axis=0)
        cfv = cf_ref[...]
        cv, sg = cfv[:, :C], _sigmoid(cfv[:, C:2 * C])
        prev = jnp.where(i > 0, _glu(halo_ref[...], C), 0.0)
        uwin = jnp.concatenate([prev, cv * sg], axis=0)
        du0 = jnp.zeros((ts, C), F32)
        off = HALO - (CONV_K - 1)
        for k in range(CONV_K):
            back = CONV_K - 1 - k
            du0 = du0 + w_ref[k:k + 1, :] * dwin[back:back + ts, :]
            dw_ref[k:k + 1, :] += jnp.sum(du1 * uwin[off + k:off + k + ts, :], axis=0, keepdims=True)
        dcf_ref[:, :C] = (du0 * sg).astype(dcf_ref.dtype)
        dcf_ref[:, C:] = (du0 * cv * sg * (1.0 - sg)).astype(dcf_ref.dtype)

    ucol = dm.A // C
    return pl.pallas_call(
        body, name="conv_bwd", grid=(nt,),
        in_specs=[pl.BlockSpec((ts, C), lambda i: (i, ucol)),
                  pl.BlockSpec((HALO, C), lambda i: (jnp.minimum((i + 1) * per, last_halo), ucol)),
                  pl.BlockSpec((ts, C), lambda i: (i, 0)),
                  pl.BlockSpec((HALO, C), lambda i: (jnp.minimum((i + 1) * per, last_halo), 0)),
                  pl.BlockSpec((ts, 2 * C), lambda i: (i, 0)),
                  pl.BlockSpec((HALO, 2 * C), lambda i: (jnp.maximum(i * per - 1, 0), 0)),
                  pl.BlockSpec((HALO, C), lambda i: (0, 0)), _vec_spec(C), _vec_spec(C)],
        out_specs=[_row_spec(ts, 2 * C), pl.BlockSpec((HALO, C), lambda i: (0, 0))] + [_vec_spec(C)] * 3,
        out_shape=[jax.ShapeDtypeStruct((dm.S, 2 * C), BF16), jax.ShapeDtypeStruct((HALO, C), F32)]
        + [jax.ShapeDtypeStruct((1, C), F32)] * 3,
        compiler_params=_cp(1),
    )(dcat, dcat, u1, u1, cf, cf, cw, lg, lb)


def _ada_fwd(c16, ada_w, ada_b_cols, dm):
    L, D, n = ada_w.shape
    tn = _tile(n, 512)

    def body(c_ref, w_ref, b_ref, o_ref, a_ref):
        cv = c_ref[...]
        act = (cv * _sigmoid(cv)).astype(BF16)
        a_ref[...] = act
        o_ref[...] = jnp.dot(act, w_ref[...].astype(BF16), preferred_element_type=F32) + b_ref[...]

    return pl.pallas_call(
        body, name="ada_fwd", grid=(L, n // tn),
        in_specs=[pl.BlockSpec((16, D), lambda l, j: (0, 0)), pl.BlockSpec((None, D, tn), lambda l, j: (l, 0, j)),
                  pl.BlockSpec((None, 1, tn), lambda l, j: (l, 0, j))],
        out_specs=[pl.BlockSpec((None, 16, tn), lambda l, j: (l, 0, j)), pl.BlockSpec((16, D), lambda l, j: (0, 0))],
        out_shape=[jax.ShapeDtypeStruct((L, 16, n), F32), jax.ShapeDtypeStruct((16, D), BF16)],
        compiler_params=_cp(2),
    )(c16, ada_w, ada_b_cols)


def _ada_bwd(act16, dmod16):
    L, _, n = dmod16.shape
    D = act16.shape[1]
    tm = min(TN_TM, D)

    def body(a_ref, d_ref, o_ref):
        o_ref[...] = _dot_tn(a_ref[...], d_ref[...])

    return pl.pallas_call(
        body, name="ada_bwd", grid=(L, D // tm),
        in_specs=[pl.BlockSpec((16, tm), lambda l, i: (0, i)), pl.BlockSpec((None, 16, n), lambda l, i: (l, 0, 0))],
        out_specs=pl.BlockSpec((None, tm, n), lambda l, i: (l, i, 0)),
        out_shape=jax.ShapeDtypeStruct((L, D, n), F32), compiler_params=_cp(2),
    )(act16, dmod16)


def _sum_devices(g8):
    _, R, _ = g8.shape
    tr = _rows_tile(R)

    def body(g_ref, o_ref):
        acc = g_ref[0]
        for d in range(1, N_DEV):
            acc = acc + g_ref[d]
        o_ref[...] = acc

    return pl.pallas_call(
        body, name="sum_devices", grid=(R // tr,),
        in_specs=[pl.BlockSpec((N_DEV, tr, LANES), lambda i: (0, i, 0))],
        out_specs=pl.BlockSpec((tr, LANES), lambda i: (i, 0)),
        out_shape=jax.ShapeDtypeStruct((R, LANES), F32), compiler_params=_cp(1),
    )(g8)


def _rows_tile(r, cap=512):
    for t in (512, 256, 128, 64, 32, 16, 8):
        if t <= cap and r % t == 0:
            return t
    return r


def _adam_math(w, g, m, v):
    m = ADAM_B1 * m + (1.0 - ADAM_B1) * g
    v = ADAM_B2 * v + (1.0 - ADAM_B2) * (g * g)
    m_hat = m / (1.0 - ADAM_B1 ** ADAM_STEP)
    v_hat = v / (1.0 - ADAM_B2 ** ADAM_STEP)
    delta = -ADAM_LR * (m_hat / (jnp.sqrt(v_hat) + ADAM_EPS) + ADAM_WD * w)
    return delta, m, v


def _adamw_dense(w, m, v, g, name):
    R, Cc = w.shape
    tr = _rows_tile(R, 128)

    def body(w_ref, m_ref, v_ref, g_ref, d_ref, mo_ref, vo_ref):
        d, mn, vn = _adam_math(w_ref[...], g_ref[...], m_ref[...], v_ref[...])
        d_ref[...] = d
        mo_ref[...] = mn
        vo_ref[...] = vn

    spec = pl.BlockSpec((tr, Cc), lambda i: (i, 0))
    return pl.pallas_call(
        body, name=name, grid=(R // tr,), in_specs=[spec] * 4, out_specs=[spec] * 3,
        out_shape=[jax.ShapeDtypeStruct((R, Cc), F32)] * 3, compiler_params=_cp(1),
    )(w, m, v, g)


def _adamw_shard(w, m, v, near, far, layer, prev, name):
    L, r, cc = w.shape
    tr = _rows_tile(r, 128)

    def body(w_ref, m_ref, v_ref, n_ref, f_ref, *rest):
        g_ref, d_ref, mo_ref, vo_ref = rest[-4:]
        g = n_ref[0].astype(F32) + f_ref[0].astype(F32)
        for k in range(1, N_CHIPS):
            g = g + (n_ref[k].astype(F32) + f_ref[k].astype(F32))
        d, mn, vn = _adam_math(w_ref[...], g, m_ref[...], v_ref[...])
        g_ref[...] = g
        d_ref[...] = d
        mo_ref[...] = mn
        vo_ref[...] = vn

    wspec = pl.BlockSpec((None, tr, cc), lambda i: (layer, i, 0))
    sspec = pl.BlockSpec((N_CHIPS, tr, cc), lambda i: (0, i, 0))
    n_prev = 0 if prev is None else 4
    return pl.pallas_call(
        body, name=name, grid=(r // tr,),
        in_specs=[wspec] * 3 + [sspec] * 2 + [pl.BlockSpec(memory_space=pl.ANY)] * n_prev,
        out_specs=[wspec] * 4,
        out_shape=[jax.ShapeDtypeStruct((L, r, cc), F32)] * 4,
        input_output_aliases={5 + t: t for t in range(n_prev)},
        compiler_params=_cp(1),
    )(w, m, v, near, far, *(prev or ()))


def _pack(vs):
    flat = jnp.concatenate([v.reshape(-1).astype(F32) for v in vs])
    pad = (-flat.shape[0]) % (64 * LANES)
    return jnp.pad(flat, (0, pad)).reshape(-1, LANES)


def _unpack(packed, shapes):
    flat = packed.reshape(-1)
    out, pos = [], 0
    for s in shapes:
        n = 1
        for d in s:
            n *= d
        out.append(flat[pos:pos + n].reshape(s))
        pos += n
    return out


def kernel(x, c, w_in, b_f, conv_w, conv_b, conv_ln_g, conv_ln_b, w_o, w_ffn_in, w_ffn_out, mix_pre_g, mix_post_g, ffn_pre_g, ffn_post_g, ada_w, ada_b, loss_target, m_w_in, m_b_f, m_conv_w, m_conv_b, m_conv_ln_g, m_conv_ln_b, m_w_o, m_w_ffn_in, m_w_ffn_out, m_mix_pre_g, m_mix_post_g, m_ffn_pre_g, m_ffn_post_g, m_ada_w, m_ada_b, v_w_in, v_b_f, v_conv_w, v_conv_b, v_conv_ln_g, v_conv_ln_b, v_w_o, v_w_ffn_in, v_w_ffn_out, v_mix_pre_g, v_mix_post_g, v_ffn_pre_g, v_ffn_post_g, v_ada_w, v_ada_b):
    params = dict(w_in=w_in, b_f=b_f, conv_w=conv_w, conv_b=conv_b, conv_ln_g=conv_ln_g, conv_ln_b=conv_ln_b, w_o=w_o,
                  w_ffn_in=w_ffn_in, w_ffn_out=w_ffn_out, mix_pre_g=mix_pre_g, mix_post_g=mix_post_g,
                  ffn_pre_g=ffn_pre_g, ffn_post_g=ffn_post_g, ada_w=ada_w, ada_b=ada_b)
    mom = dict(w_in=m_w_in, b_f=m_b_f, conv_w=m_conv_w, conv_b=m_conv_b, conv_ln_g=m_conv_ln_g, conv_ln_b=m_conv_ln_b,
               w_o=m_w_o, w_ffn_in=m_w_ffn_in, w_ffn_out=m_w_ffn_out, mix_pre_g=m_mix_pre_g, mix_post_g=m_mix_post_g,
               ffn_pre_g=m_ffn_pre_g, ffn_post_g=m_ffn_post_g, ada_w=m_ada_w, ada_b=m_ada_b)
    var = dict(w_in=v_w_in, b_f=v_b_f, conv_w=v_conv_w, conv_b=v_conv_b, conv_ln_g=v_conv_ln_g, conv_ln_b=v_conv_ln_b,
               w_o=v_w_o, w_ffn_in=v_w_ffn_in, w_ffn_out=v_w_ffn_out, mix_pre_g=v_mix_pre_g, mix_post_g=v_mix_post_g,
               ffn_pre_g=v_ffn_pre_g, ffn_post_g=v_ffn_post_g, ada_w=v_ada_w, ada_b=v_ada_b)

    S, D = x.shape[1], x.shape[2]
    L = w_in.shape[0]
    A = D // 2
    C = D - A
    H = A // HEAD_DIM
    F = w_ffn_out.shape[1] * N_CHIPS
    d_in = w_in.shape[2] * N_CHIPS
    NP = 3 * A + 2 * C + LANES
    dm = Dims(S=S, D=D, A=A, C=C, H=H, F=F, L=L, NP=NP, ts=min(ROW_TILE, S))
    assert H <= 8 and A == C and d_in == 3 * A + H + 2 * C

    ix, iy, ic = _place()
    chip = 2 * ix + iy
    dev = 4 * ix + 2 * iy + ic
    x2 = x.reshape(S, D)
    tgt = loss_target.reshape(S, D)

    def gather_start(l, after):
        first = _exchange_start([w_in[l].astype(BF16)], "gather", after, f"gather_a_start_{l}")
        rest = _exchange_start([w_o[l].astype(BF16), w_ffn_in[l].astype(BF16), w_ffn_out[l].astype(BF16)], "gather",
                               [first["token"]], f"gather_b_start_{l}")
        return first, rest

    def gather_wait(st, after, name):
        srcs, lands = _exchange_wait(st, after, name)
        return [lax.dynamic_update_index_in_dim(land, src, chip, 0) for land, src in zip(lands, srcs)]

    c_all = _all_gather_devices(c.reshape(D // LANES, LANES), "gather_c").reshape(N_DEV, D)
    c16 = jnp.pad(c_all, ((0, 16 - N_DEV), (0, 0)))
    n_ada = ada_w.shape[2]
    ada_b_cols = lax.dynamic_slice_in_dim(ada_b, chip * n_ada, n_ada, axis=1).reshape(L, 1, n_ada)
    mod_cols, act16 = _ada_fwd(c16, ada_w, ada_b_cols, dm)
    conv_w_all, mod_all = _all_gather_chips([conv_w.reshape(L * CONV_K, -1), mod_cols.reshape(L * 16, n_ada)], "gather_mod")
    cwc = conv_w.shape[2]
    conv_w_full = conv_w_all.reshape(N_CHIPS, L, CONV_K, cwc).transpose(1, 2, 0, 3).reshape(L, CONV_K, C)
    conv_w_full = jnp.pad(conv_w_full, ((0, 0), (0, HALO - CONV_K), (0, 0)))
    mod_all = mod_all.reshape(N_CHIPS, L, 16, n_ada)
    mod_me = lax.dynamic_index_in_dim(mod_all, dev, axis=2, keepdims=False)
    mod_me = mod_me.transpose(1, 0, 2).reshape(L, N_MOD, 1, D)

    gather = [None] * L
    gather[0] = gather_start(0, [mod_all])

    def projection_of(g_in):
        w_nat = g_in.transpose(1, 0, 2).reshape(D, d_in)
        return jnp.concatenate([w_nat[:, :3 * A], w_nat[:, 3 * A + H:], w_nat[:, 3 * A:3 * A + H],
                                jnp.zeros((D, LANES - H), BF16)], axis=1)

    gathered = [None] * L
    vec = lambda p, l: p[l].reshape(1, -1)
    bf_pad = jnp.pad(b_f, ((0, 0), (0, LANES - H)))

    saved = []
    xin = x2
    h = _pre_norm(xin, vec(mix_pre_g, 0), mod_me[0, 1], mod_me[0, 0], dm)
    dx = loss_part = None
    for l in range(L):
        (g_in,) = gather_wait(gather[l][0], [h], f"gather_a_wait_{l}")
        w_p = projection_of(g_in)
        order = [gather[l][1]["token"]]
        if l + 1 < L:
            gather[l + 1] = gather_start(l + 1, [g_in])
            order.append(gather[l + 1][1]["token"])
        qkv = _mm_nn(h, w_p, BF16, "mm_qkv", 0, 3 * A, after=order)
        cf = _mm_nn(h, w_p, F32, "mm_cf", 3 * A, 2 * C + LANES)
        cum, cumt = _gates_fwd(cf, vec(bf_pad, l), dm)
        o, lse = _attn_fwd(qkv, cum, cumt, dm)
        u3, u1 = _conv_fwd(cf, conv_w_full[l], vec(conv_b, l), vec(conv_ln_g, l), vec(conv_ln_b, l), dm)
        cat = jnp.concatenate([o, u3], axis=1)
        g_o, wfi, g_fo = gather_wait(gather[l][1], [cat], f"gather_b_wait_{l}")
        wo, wfo = g_o.reshape(D, D), g_fo.reshape(F, D)
        gathered[l] = (w_p, wo, wfi, wfo)
        y = _mm_nn(cat, wo, F32, "mm_o")
        x1, h2 = _res_norm(xin, y, vec(mix_post_g, l), mod_me[l, 2], vec(ffn_pre_g, l), mod_me[l, 4], mod_me[l, 3], dm)
        gu = _mm_nn_blocked(h2, wfi, F32, "mm_ffn_in")
        a = _swiglu_fwd(gu, dm)
        y2 = _mm_nn(a, wfo, F32, "mm_ffn_out")
        saved.append(dict(xin=xin, h=h, qkv=qkv, cf=cf, cum=cum, cumt=cumt, o=o, lse=lse, u1=u1, cat=cat, y=y,
                          x1=x1, h2=h2, gu=gu, a=a, y2=y2))
        if l + 1 < L:
            xin, h = _res_norm(x1, y2, vec(ffn_post_g, l), mod_me[l, 5], vec(mix_pre_g, l + 1),
                               mod_me[l + 1, 1], mod_me[l + 1, 0], dm)
        else:
            dx, loss_part = _res_loss(x1, y2, vec(ffn_post_g, l), mod_me[l, 5], tgt, dm)
    loss = lax.psum(loss_part[0, 0], ("x", "y", "c"))

    small = [None] * L
    big = [None] * L
    order = []
    for l in reversed(range(L)):
        w_p, wo, wfi, wfo = gathered[l]
        sv = saved[l]
        dy2, d_gfpost, d_g2 = _post_bwd(dx, sv["y2"], vec(ffn_post_g, l), mod_me[l, 5], dm, after=order)
        da = _mm_nt(dy2, wfo, F32, "mm_da")
        dgu = _swiglu_bwd(da, sv["gu"], dm)
        g_wfo = _mm_tn(sv["a"], dy2, BF16, "mm_dwfo")
        g_wfi = _mm_tn(sv["h2"], dgu, BF16, "mm_dwfi", blocked=True)
        dh2 = _mm_nt_blocked(dgu, wfi, F32, "mm_dh2")
        scatter_ffn = _exchange_start([g_wfi, g_wfo.reshape(N_CHIPS, F // N_CHIPS, D)], "scatter", [], f"scatter_b_start_{l}")
        dx1, d_sh2, d_sc2, d_gfpre = _pre_bwd(dh2, dx, sv["x1"], vec(ffn_pre_g, l), mod_me[l, 4], dm,
                                              after=[scatter_ffn["token"]])
        dy, d_gpost, d_g1 = _post_bwd(dx1, sv["y"], vec(mix_post_g, l), mod_me[l, 2], dm)
        dcat = _mm_nt(dy, wo, F32, "mm_dcat")
        g_wo = _mm_tn(sv["cat"], dy, BF16, "mm_dwo")
        dcfc, d_cw, d_cb, d_lg, d_lb = _conv_bwd(dcat, sv["u1"], sv["cf"], conv_w_full[l], vec(conv_ln_g, l),
                                                 vec(conv_ln_b, l), dm)
        delta, do = _attn_delta(dcat, sv["o"], dm)
        dq, dkv, dcq, dckt = _attn_bwd(sv["qkv"], do, sv["lse"], delta, sv["cum"], sv["cumt"], dm)
        dfl, d_bf = _gates_bwd(dcq, dckt, sv["cf"], vec(bf_pad, l), dm)
        dproj = jnp.concatenate([dq.astype(BF16), dkv, dcfc, dfl], axis=1)
        dh = _mm_nt(dproj, w_p, F32, "mm_dh")
        g_wp = _mm_tn(sv["h"], dproj, BF16, "mm_dwp")
        dx, d_sh1, d_sc1, d_gpre = _pre_bwd(dh, dx1, sv["xin"], vec(mix_pre_g, l), mod_me[l, 1], dm)
        g_nat = jnp.concatenate([g_wp[:, :3 * A], g_wp[:, 3 * A + 2 * C:3 * A + 2 * C + H], g_wp[:, 3 * A:3 * A + 2 * C]], axis=1)
        g_win = g_nat.reshape(D, N_CHIPS, d_in // N_CHIPS).transpose(1, 0, 2)
        g_mix = [g_win, g_wo.reshape(N_CHIPS, D // N_CHIPS, D)]
        if l > 0:
            scatter_mix = _exchange_start(g_mix, "scatter", [], f"scatter_a_start_{l}")
            order = [scatter_mix["token"]]
            big[l] = (scatter_mix, scatter_ffn)
        small[l] = dict(b_f=d_bf[0, :H], conv_b=d_cb[0], conv_ln_g=d_lg[0], conv_ln_b=d_lb[0], mix_pre_g=d_gpre[0],
                        mix_post_g=d_gpost[0], ffn_pre_g=d_gfpre[0], ffn_post_g=d_gfpost[0],
                        dmod=jnp.concatenate([d_sh1, d_sc1, d_g1, d_sh2, d_sc2, d_g2], axis=1)[0],
                        conv_w=d_cw[:CONV_K])
    grad_x = dx.reshape(1, S, D)

    keys_small = ["b_f", "conv_b", "conv_ln_g", "conv_ln_b", "mix_pre_g", "mix_post_g", "ffn_pre_g", "ffn_post_g",
                  "dmod", "conv_w"]
    stacked = [jnp.stack([small[l][k] for l in range(L)]) for k in keys_small]
    shapes = [s.shape for s in stacked]
    pack = _pack(stacked)
    pack8 = _all_gather_devices(pack, "gather_small")
    big[0] = (_exchange_start(g_mix, "scatter", [pack8], "scatter_a_start_0"), scatter_ffn)
    summed = dict(zip(keys_small, _unpack(_sum_devices(pack8), shapes)))
    dmod_all = jnp.stack([_unpack(pack8[d], shapes)[keys_small.index("dmod")] for d in range(N_DEV)])
    grads = {k: summed[k] for k in keys_small[:8]}
    grads["ada_b"] = summed["dmod"]
    grads["conv_w"] = lax.dynamic_slice_in_dim(summed["conv_w"], chip * cwc, cwc, axis=2)

    dmod_cols = lax.dynamic_slice_in_dim(dmod_all.reshape(N_DEV, L, N_CHIPS, n_ada), chip, 1, axis=2)
    dmod16 = jnp.pad(dmod_cols.reshape(N_DEV, L, n_ada).transpose(1, 0, 2), ((0, 0), (0, 16 - N_DEV), (0, 0))).astype(BF16)
    grads["ada_w"] = _ada_bwd(act16, dmod16)

    d_aw, m_aw, v_aw = _adamw_dense(ada_w.reshape(L * D, n_ada), m_ada_w.reshape(L * D, n_ada),
                                    v_ada_w.reshape(L * D, n_ada), grads["ada_w"].reshape(L * D, n_ada), "adamw_ada_w")
    names_small = ["b_f", "conv_w", "conv_b", "conv_ln_g", "conv_ln_b", "mix_pre_g", "mix_post_g", "ffn_pre_g",
                   "ffn_post_g", "ada_b"]
    shapes_small = [params[n].shape for n in names_small]
    d_s, m_s, v_s = _adamw_dense(_pack([params[n] for n in names_small]), _pack([mom[n] for n in names_small]),
                                 _pack([var[n] for n in names_small]), _pack([grads[n] for n in names_small]),
                                 "adamw_small")
    delta_w = dict(zip(names_small, _unpack(d_s, shapes_small)))
    new_m = dict(zip(names_small, _unpack(m_s, shapes_small)))
    new_v = dict(zip(names_small, _unpack(v_s, shapes_small)))
    delta_w["ada_w"], new_m["ada_w"], new_v["ada_w"] = (t.reshape(L, D, n_ada) for t in (d_aw, m_aw, v_aw))

    names_big = ["w_in", "w_o", "w_ffn_in", "w_ffn_out"]
    res_big = {n: None for n in names_big}
    forward = [None] * L

    def update(l, after):
        near, far = _exchange_wait(forward[l], after, f"forward_wait_{l}")
        for t, n in enumerate(names_big):
            res_big[n] = _adamw_shard(params[n], mom[n], var[n], near[t], far[t], l, res_big[n], f"adamw_{n}_{l}")
        return [res_big[names_big[-1]][1]]

    done = [dx]
    for l in reversed(range(L)):
        after = done + [d_s, d_aw] if l == 0 else done
        near = []
        for st, nm in zip(big[l], ("a", "b")):
            srcs, lands = _exchange_wait(st, after, f"scatter_{nm}_wait_{l}")
            near += [lax.dynamic_update_index_in_dim(land, lax.dynamic_index_in_dim(src, chip, 0, keepdims=False), chip, 0)
                     for land, src in zip(lands, srcs)]
        forward[l] = _exchange_start(near, "sibling", [], f"forward_start_{l}")
        done = update(l + 1, [forward[l]["token"]]) if l + 1 < L else [forward[l]["token"]]
    update(0, [])
    for n in names_big:
        grads[n], delta_w[n], new_m[n], new_v[n] = res_big[n]

    return (loss, grad_x, *[grads[n] for n in WEIGHTS], *[delta_w[n] for n in WEIGHTS],
            *[new_m[n] for n in WEIGHTS], *[new_v[n] for n in WEIGHTS])
```

```python
import collections
import functools

import jax
import jax.numpy as jnp
from jax import lax
from jax.experimental import pallas as pl
from jax.experimental.pallas import tpu as pltpu

F32 = jnp.float32
BF16 = jnp.bfloat16
MESH = pl.DeviceIdType.MESH

HEAD_DIM = 64
CONV_K = 31
N_MOD = 6
EPS = 1e-6
N_CHIPS = 4
N_DEV = 8
LANES = 128
HALO = 32
ROW_TILE = 256
MM_TM = 512
MM_TN_MAX = 1408
TN_TM = 256
VMEM_LIMIT = 56 * 1024 * 1024

ADAM_LR = 0.001
ADAM_B1 = 0.9
ADAM_B2 = 0.999
ADAM_EPS = 1e-08
ADAM_WD = 0.01
ADAM_STEP = 10

WEIGHTS = ['w_in', 'b_f', 'conv_w', 'conv_b', 'conv_ln_g', 'conv_ln_b', 'w_o', 'w_ffn_in', 'w_ffn_out',
           'mix_pre_g', 'mix_post_g', 'ffn_pre_g', 'ffn_post_g', 'ada_w', 'ada_b']

Dims = collections.namedtuple("Dims", "S D A C H F L NP ts")


def _cp(n_grid=0):
    if n_grid:
        return pltpu.CompilerParams(dimension_semantics=("arbitrary",) * n_grid, vmem_limit_bytes=VMEM_LIMIT)
    return pltpu.CompilerParams(vmem_limit_bytes=VMEM_LIMIT)


def _tile(n, cap, also=None):
    best = None
    t = LANES
    while t <= min(n, cap):
        if n % t == 0 and (also is None or also % t == 0):
            best = t
        t += LANES
    assert best is not None, (n, cap, also)
    return best


def _bf(v):
    return v if v.dtype == BF16 else v.astype(BF16)


def _place():
    return lax.axis_index("x"), lax.axis_index("y"), lax.axis_index("c")


def _flip(v, d):
    return 1 - v if d else v


def _all_gather_devices(a, name, after=()):
    def body(a_ref, *rest):
        o_ref, send, recv, lsem = rest[len(after):]
        x, y, c = _place()
        me = 4 * x + 2 * y + c
        local = pltpu.make_async_copy(a_ref, o_ref.at[me], lsem)
        local.start()
        copies = []
        for k in range(1, N_DEV):
            peer = (_flip(x, (k >> 2) & 1), _flip(y, (k >> 1) & 1), _flip(c, k & 1))
            cp = pltpu.make_async_remote_copy(src_ref=a_ref, dst_ref=o_ref.at[me], send_sem=send.at[k - 1],
                                              recv_sem=recv.at[k - 1], device_id=peer, device_id_type=MESH)
            cp.start()
            copies.append(cp)
        for cp in copies:
            cp.wait()
        local.wait()

    return pl.pallas_call(
        body, name=name,
        out_shape=jax.ShapeDtypeStruct((N_DEV,) + a.shape, a.dtype),
        in_specs=[pl.BlockSpec(memory_space=pl.ANY)] * (1 + len(after)),
        out_specs=pl.BlockSpec(memory_space=pl.ANY),
        scratch_shapes=[pltpu.SemaphoreType.DMA((N_DEV - 1,)), pltpu.SemaphoreType.DMA((N_DEV - 1,)),
                        pltpu.SemaphoreType.DMA],
    )(a, *after)


def _all_gather_chips(arrays, name):
    n = len(arrays)

    def body(*refs):
        a_refs, o_refs = refs[:n], refs[n:2 * n]
        send, recv, lsem = refs[2 * n:]
        x, y, c = _place()
        me = 2 * x + y
        copies = []
        for i in range(n):
            local = pltpu.make_async_copy(a_refs[i], o_refs[i].at[me], lsem.at[i])
            local.start()
            copies.append(local)
            for k in range(1, N_CHIPS):
                peer = (_flip(x, (k >> 1) & 1), _flip(y, k & 1), c)
                cp = pltpu.make_async_remote_copy(src_ref=a_refs[i], dst_ref=o_refs[i].at[me],
                                                  send_sem=send.at[i, k - 1], recv_sem=recv.at[i, k - 1],
                                                  device_id=peer, device_id_type=MESH)
                cp.start()
                copies.append(cp)
        for cp in copies:
            cp.wait()

    return pl.pallas_call(
        body, name=name,
        out_shape=[jax.ShapeDtypeStruct((N_CHIPS,) + a.shape, a.dtype) for a in arrays],
        in_specs=[pl.BlockSpec(memory_space=pl.ANY)] * n,
        out_specs=[pl.BlockSpec(memory_space=pl.ANY)] * n,
        scratch_shapes=[pltpu.SemaphoreType.DMA((n, N_CHIPS - 1)), pltpu.SemaphoreType.DMA((n, N_CHIPS - 1)),
                        pltpu.SemaphoreType.DMA((n,))],
    )(*arrays)


_HBM = pl.BlockSpec(memory_space=pltpu.HBM)
_SEM = pl.BlockSpec(memory_space=pltpu.SEMAPHORE)
_EFFECT = pltpu.SideEffectType.DATAFLOW_SIDE_EFFECTING


def _n_copies(mode):
    return 1 if mode == "sibling" else N_CHIPS - 1


def _chip_copies(srcs, lands, send, recv, mode):
    x, y, c = _place()
    me = 2 * x + y
    copies = []
    for i in range(len(srcs)):
        if mode == "sibling":
            copies.append(pltpu.make_async_remote_copy(src_ref=srcs[i], dst_ref=lands[i], send_sem=send.at[i],
                                                       recv_sem=recv.at[i], device_id=(x, y, 1 - c), device_id_type=MESH))
            continue
        for k in range(1, N_CHIPS):
            px, py = _flip(x, (k >> 1) & 1), _flip(y, k & 1)
            src = srcs[i].at[2 * px + py] if mode == "scatter" else srcs[i]
            s = i * (N_CHIPS - 1) + k - 1
            copies.append(pltpu.make_async_remote_copy(src_ref=src, dst_ref=lands[i].at[me], send_sem=send.at[s],
                                                       recv_sem=recv.at[s], device_id=(px, py, c), device_id_type=MESH))
    return copies


_ORDER = pl.BlockSpec(memory_space=pl.ANY)


def _exchange_start(arrays, mode, after, name):
    n = len(arrays)

    def body(*refs):
        srcs, lands = refs[:n], refs[n:2 * n]
        send, recv = refs[2 * n + len(after)], refs[2 * n + len(after) + 1]
        token = refs[-1]
        for cp in _chip_copies(srcs, lands, send, recv, mode):
            cp.start()
        token[...] = jnp.zeros_like(token)

    land_shapes = [(N_CHIPS,) + a.shape if mode == "gather" else a.shape for a in arrays]
    n_sem = n * _n_copies(mode)
    outs = pl.pallas_call(
        body, name=name,
        out_shape=(pltpu.SemaphoreType.DMA((n_sem,)), pltpu.SemaphoreType.DMA((n_sem,)),
                   *[pltpu.HBM(a.shape, a.dtype) for a in arrays],
                   *[pltpu.HBM(s, a.dtype) for s, a in zip(land_shapes, arrays)],
                   jax.ShapeDtypeStruct((8, LANES), F32)),
        in_specs=[_HBM] * (2 * n) + [_ORDER] * len(after),
        out_specs=(_SEM, _SEM, *[_HBM] * (2 * n), pl.BlockSpec(memory_space=pltpu.VMEM)),
        input_output_aliases={i: 2 + i for i in range(2 * n)},
        compiler_params=pltpu.CompilerParams(has_side_effects=_EFFECT),
    )(*[pltpu.with_memory_space_constraint(a, pltpu.HBM) for a in arrays],
      *[pltpu.with_memory_space_constraint(lax.empty(s, a.dtype), pltpu.HBM) for s, a in zip(land_shapes, arrays)],
      *after)
    return dict(send=outs[0], recv=outs[1], srcs=outs[2:2 + n], lands=outs[2 + n:2 + 2 * n], token=outs[-1], mode=mode)


def _exchange_wait(st, after, name):
    n = len(st["srcs"])
    mode = st["mode"]

    def body(*refs):
        srcs, lands = refs[:n], refs[n:2 * n]
        send, recv = refs[2 * n], refs[2 * n + 1]
        for cp in _chip_copies(srcs, lands, send, recv, mode):
            cp.wait_send()
            cp.wait_recv()

    outs = pl.pallas_call(
        body, name=name,
        out_shape=tuple(pltpu.HBM(a.shape, a.dtype) for a in (*st["srcs"], *st["lands"])),
        in_specs=[_HBM] * (2 * n) + [_SEM, _SEM] + [_ORDER] * len(after),
        out_specs=tuple([_HBM] * (2 * n)),
        input_output_aliases={i: i for i in range(2 * n)},
        compiler_params=pltpu.CompilerParams(has_side_effects=_EFFECT),
    )(*st["srcs"], *st["lands"], st["send"], st["recv"], *after)
    return outs[:n], outs[n:]


def _matmul(a, b, contract, grid, a_spec, b_spec, o_spec, out_shape, name, nk=1, acc_shape=None, after=()):
    def body(a_ref, b_ref, *rest):
        o_ref, acc = rest[len(after)], rest[len(after) + 1:]
        r = lax.dot_general(_bf(a_ref[...]), _bf(b_ref[...]), (contract, ((), ())), preferred_element_type=F32)
        if nk == 1:
            o_ref[...] = r.astype(o_ref.dtype)
        else:
            k = pl.program_id(len(grid) - 1)

            @pl.when(k == 0)
            def _():
                acc[0][...] = r

            @pl.when(k > 0)
            def _():
                acc[0][...] += r

            @pl.when(k == nk - 1)
            def _():
                o_ref[...] = acc[0][...].astype(o_ref.dtype)

    return pl.pallas_call(
        body, name=name, grid=grid, in_specs=[a_spec, b_spec] + [_ORDER] * len(after), out_specs=o_spec,
        out_shape=out_shape, scratch_shapes=[pltpu.VMEM(acc_shape, F32)] if nk > 1 else [],
        compiler_params=_cp(len(grid)),
    )(a, b, *after)


def _mm_nn(a, b, out_dtype, name, col0=0, n=None, after=()):
    m, k = a.shape
    n = b.shape[1] - col0 if n is None else n
    tm = min(MM_TM, m)
    tn = _tile(n, MM_TN_MAX, also=col0 if col0 else None)
    off = col0 // tn
    return _matmul(a, b, ((1,), (0,)), (n // tn, m // tm),
                   pl.BlockSpec((tm, k), lambda j, i: (i, 0)),
                   pl.BlockSpec((k, tn), lambda j, i: (0, j + off)),
                   pl.BlockSpec((tm, tn), lambda j, i: (i, j)),
                   jax.ShapeDtypeStruct((m, n), out_dtype), name, after=after)


def _mm_nn_blocked(a, b3, out_dtype, name):
    m, k = a.shape
    nj, _, nb = b3.shape
    tm = min(MM_TM, m)
    return _matmul(a, b3, ((1,), (0,)), (nj, m // tm),
                   pl.BlockSpec((tm, k), lambda j, i: (i, 0)),
                   pl.BlockSpec((None, k, nb), lambda j, i: (j, 0, 0)),
                   pl.BlockSpec((tm, nb), lambda j, i: (i, j)),
                   jax.ShapeDtypeStruct((m, nj * nb), out_dtype), name)


def _mm_nt(a, b, out_dtype, name):
    m, k = a.shape
    n = b.shape[0]
    tm = min(MM_TM, m)
    tn = _tile(n, MM_TN_MAX)
    return _matmul(a, b, ((1,), (1,)), (n // tn, m // tm),
                   pl.BlockSpec((tm, k), lambda j, i: (i, 0)),
                   pl.BlockSpec((tn, k), lambda j, i: (j, 0)),
                   pl.BlockSpec((tm, tn), lambda j, i: (i, j)),
                   jax.ShapeDtypeStruct((m, n), out_dtype), name)


def _mm_nt_blocked(a, b3, out_dtype, name):
    m = a.shape[0]
    nj, n, nb = b3.shape
    tm = min(MM_TM, m)
    tn = _tile(n, 512)
    return _matmul(a, b3, ((1,), (1,)), (n // tn, m // tm, nj),
                   pl.BlockSpec((tm, nb), lambda j, i, k: (i, k)),
                   pl.BlockSpec((None, tn, nb), lambda j, i, k: (k, j, 0)),
                   pl.BlockSpec((tm, tn), lambda j, i, k: (i, j)),
                   jax.ShapeDtypeStruct((m, n), out_dtype), name, nk=nj, acc_shape=(tm, tn))


def _mm_tn(a, b, out_dtype, name, blocked=False):
    k, m = a.shape
    n = b.shape[1]
    tm = min(TN_TM, m)
    tn = n // N_CHIPS if blocked else _tile(n, MM_TN_MAX)
    if blocked:
        o_spec = pl.BlockSpec((None, tm, tn), lambda j, i: (j, i, 0))
        out_shape = jax.ShapeDtypeStruct((n // tn, m, tn), out_dtype)
    else:
        o_spec = pl.BlockSpec((tm, tn), lambda j, i: (i, j))
        out_shape = jax.ShapeDtypeStruct((m, n), out_dtype)
    return _matmul(a, b, ((0,), (0,)), (n // tn, m // tm),
                   pl.BlockSpec((k, tm), lambda j, i: (0, i)),
                   pl.BlockSpec((k, tn), lambda j, i: (0, j)),
                   o_spec, out_shape, name)


def _vec_spec(d):
    return pl.BlockSpec((1, d), lambda i: (0, 0))


def _row_spec(ts, d, col=0):
    return pl.BlockSpec((ts, d), lambda i: (i, col))


def _rms(x):
    return lax.rsqrt(jnp.mean(x * x, axis=-1, keepdims=True) + EPS)


def _pre_norm(x, gain, scale, shift, dm):
    def body(x_ref, g_ref, sc_ref, sh_ref, h_ref):
        xv = x_ref[...]
        h_ref[...] = (((xv * _rms(xv)) * g_ref[...]) * (1.0 + sc_ref[...]) + sh_ref[...]).astype(h_ref.dtype)

    return pl.pallas_call(
        body, name="pre_norm", grid=(dm.S // dm.ts,),
        in_specs=[_row_spec(dm.ts, dm.D)] + [_vec_spec(dm.D)] * 3,
        out_specs=_row_spec(dm.ts, dm.D),
        out_shape=jax.ShapeDtypeStruct((dm.S, dm.D), BF16), compiler_params=_cp(1),
    )(x, gain, scale, shift)


def _res_norm(x, y, gpost, gate, gain, scale, shift, dm):
    def body(x_ref, y_ref, gp_ref, gt_ref, g_ref, sc_ref, sh_ref, xo_ref, h_ref):
        yv = y_ref[...]
        xn = x_ref[...] + gt_ref[...] * ((yv * _rms(yv)) * gp_ref[...])
        xo_ref[...] = xn
        h_ref[...] = (((xn * _rms(xn)) * g_ref[...]) * (1.0 + sc_ref[...]) + sh_ref[...]).astype(h_ref.dtype)

    return pl.pallas_call(
        body, name="res_norm", grid=(dm.S // dm.ts,),
        in_specs=[_row_spec(dm.ts, dm.D)] * 2 + [_vec_spec(dm.D)] * 5,
        out_specs=[_row_spec(dm.ts, dm.D)] * 2,
        out_shape=[jax.ShapeDtypeStruct((dm.S, dm.D), F32), jax.ShapeDtypeStruct((dm.S, dm.D), BF16)],
        compiler_params=_cp(1),
    )(x, y, gpost, gate, gain, scale, shift)


def _res_loss(x, y, gpost, gate, target, dm):
    def body(x_ref, y_ref, gp_ref, gt_ref, t_ref, dx_ref, loss_ref):
        i = pl.program_id(0)

        @pl.when(i == 0)
        def _():
            loss_ref[...] = jnp.zeros_like(loss_ref)
        yv = y_ref[...]
        err = x_ref[...] + gt_ref[...] * ((yv * _rms(yv)) * gp_ref[...]) - t_ref[...]
        dx_ref[...] = err * (1.0 / dm.D)
        per_row = jnp.mean(err * err, axis=-1, keepdims=True)
        loss_ref[...] += 0.5 * jnp.sum(per_row, axis=0, keepdims=True)

    return pl.pallas_call(
        body, name="res_loss", grid=(dm.S // dm.ts,),
        in_specs=[_row_spec(dm.ts, dm.D)] * 2 + [_vec_spec(dm.D)] * 2 + [_row_spec(dm.ts, dm.D)],
        out_specs=[_row_spec(dm.ts, dm.D), _vec_spec(LANES)],
        out_shape=[jax.ShapeDtypeStruct((dm.S, dm.D), F32), jax.ShapeDtypeStruct((1, LANES), F32)],
        compiler_params=_cp(1),
    )(x, y, gpost, gate, target)


def _post_bwd(dxo, y, gpost, gate, dm, after=()):
    def body(dx_ref, y_ref, gp_ref, gt_ref, *rest):
        dy_ref, dgp_ref, dgt_ref = rest[len(after):]
        i = pl.program_id(0)

        @pl.when(i == 0)
        def _():
            dgp_ref[...] = jnp.zeros_like(dgp_ref)
            dgt_ref[...] = jnp.zeros_like(dgt_ref)
        yv, dx = y_ref[...], dx_ref[...]
        r = _rms(yv)
        t = yv * r
        dgp_ref[...] += jnp.sum(dx * gt_ref[...] * t, axis=0, keepdims=True)
        dgt_ref[...] += jnp.sum(dx * (t * gp_ref[...]), axis=0, keepdims=True)
        dt = dx * (gt_ref[...] * gp_ref[...])
        dy_ref[...] = (r * (dt - t * jnp.mean(dt * t, axis=-1, keepdims=True))).astype(dy_ref.dtype)

    return pl.pallas_call(
        body, name="post_bwd", grid=(dm.S // dm.ts,),
        in_specs=[_row_spec(dm.ts, dm.D)] * 2 + [_vec_spec(dm.D)] * 2 + [_ORDER] * len(after),
        out_specs=[_row_spec(dm.ts, dm.D), _vec_spec(dm.D), _vec_spec(dm.D)],
        out_shape=[jax.ShapeDtypeStruct((dm.S, dm.D), BF16)] + [jax.ShapeDtypeStruct((1, dm.D), F32)] * 2,
        compiler_params=_cp(1),
    )(dxo, y, gpost, gate, *after)


def _pre_bwd(dh, dxo, x, gain, scale, dm, after=()):
    def body(dh_ref, dxo_ref, x_ref, g_ref, sc_ref, *rest):
        dx_ref, dsh_ref, dsc_ref, dg_ref = rest[len(after):]
        i = pl.program_id(0)

        @pl.when(i == 0)
        def _():
            dsh_ref[...] = jnp.zeros_like(dsh_ref)
            dsc_ref[...] = jnp.zeros_like(dsc_ref)
            dg_ref[...] = jnp.zeros_like(dg_ref)
        xv, dh_ = x_ref[...], dh_ref[...]
        r = _rms(xv)
        nrm = xv * r
        one_sc = 1.0 + sc_ref[...]
        dsh_ref[...] += jnp.sum(dh_, axis=0, keepdims=True)
        dsc_ref[...] += jnp.sum(dh_ * (nrm * g_ref[...]), axis=0, keepdims=True)
        dg_ref[...] += jnp.sum(dh_ * nrm * one_sc, axis=0, keepdims=True)
        dn = dh_ * (g_ref[...] * one_sc)
        dx_ref[...] = dxo_ref[...] + r * (dn - nrm * jnp.mean(dn * nrm, axis=-1, keepdims=True))

    return pl.pallas_call(
        body, name="pre_bwd", grid=(dm.S // dm.ts,),
        in_specs=[_row_spec(dm.ts, dm.D)] * 3 + [_vec_spec(dm.D)] * 2 + [_ORDER] * len(after),
        out_specs=[_row_spec(dm.ts, dm.D)] + [_vec_spec(dm.D)] * 3,
        out_shape=[jax.ShapeDtypeStruct((dm.S, dm.D), F32)] + [jax.ShapeDtypeStruct((1, dm.D), F32)] * 3,
        compiler_params=_cp(1),
    )(dh, dxo, x, gain, scale, *after)


def _sigmoid(z):
    return 1.0 / (1.0 + jnp.exp(-z))


def _swiglu_fwd(gu, dm):
    def body(g_ref, u_ref, a_ref):
        g = g_ref[...]
        a_ref[...] = (g * _sigmoid(g) * u_ref[...]).astype(a_ref.dtype)

    return pl.pallas_call(
        body, name="swiglu_fwd", grid=(dm.S // dm.ts,),
        in_specs=[_row_spec(dm.ts, dm.F, 0), _row_spec(dm.ts, dm.F, 1)],
        out_specs=_row_spec(dm.ts, dm.F),
        out_shape=jax.ShapeDtypeStruct((dm.S, dm.F), BF16), compiler_params=_cp(1),
    )(gu, gu)


def _swiglu_bwd(da, gu, dm):
    def body(da_ref, g_ref, u_ref, d_ref):
        g, u, da_ = g_ref[...], u_ref[...], da_ref[...]
        sg = _sigmoid(g)
        d_ref[:, :dm.F] = (da_ * u * (sg * (1.0 + g * (1.0 - sg)))).astype(d_ref.dtype)
        d_ref[:, dm.F:] = (da_ * (g * sg)).astype(d_ref.dtype)

    return pl.pallas_call(
        body, name="swiglu_bwd", grid=(dm.S // dm.ts,),
        in_specs=[_row_spec(dm.ts, dm.F), _row_spec(dm.ts, dm.F, 0), _row_spec(dm.ts, dm.F, 1)],
        out_specs=_row_spec(dm.ts, 2 * dm.F),
        out_shape=jax.ShapeDtypeStruct((dm.S, 2 * dm.F), BF16), compiler_params=_cp(1),
    )(da, gu, gu)


def _tri(n, upper):
    r = lax.broadcasted_iota(jnp.int32, (n, n), 0)
    c = lax.broadcasted_iota(jnp.int32, (n, n), 1)
    return (c >= r if upper else r >= c).astype(F32)


def _gates_fwd(cf, bf, dm):
    ts = dm.ts
    fcol = 2 * dm.C // LANES

    def body(f_ref, b_ref, cum_ref, carry):
        i = pl.program_id(0)

        @pl.when(i == 0)
        def _():
            carry[...] = jnp.zeros_like(carry)
        z = f_ref[...] + b_ref[...]
        lf = jnp.minimum(z, 0.0) - jnp.log(1.0 + jnp.exp(-jnp.abs(z)))
        cs = jnp.dot(_tri(ts, False), lf, precision=lax.Precision.HIGHEST, preferred_element_type=F32) + carry[...]
        cum_ref[...] = cs
        carry[...] = cs[ts - 1:ts, :]

    return pl.pallas_call(
        body, name="gates_fwd", grid=(dm.S // ts,),
        in_specs=[pl.BlockSpec((ts, LANES), lambda i: (i, fcol)), _vec_spec(LANES)],
        out_specs=_row_spec(ts, LANES),
        out_shape=jax.ShapeDtypeStruct((dm.S, LANES), F32),
        scratch_shapes=[pltpu.VMEM((1, LANES), F32)], compiler_params=_cp(1),
    )(cf, bf)


def _gates_bwd(dc, cf, bf, dm):
    ts = dm.ts
    nb = dm.S // ts
    fcol = 2 * dm.C // LANES

    def body(dc_ref, f_ref, b_ref, df_ref, db_ref, carry):
        i = pl.program_id(0)

        @pl.when(i == 0)
        def _():
            carry[...] = jnp.zeros_like(carry)
            db_ref[...] = jnp.zeros_like(db_ref)
        dlf = jnp.dot(_tri(ts, True), dc_ref[...], precision=lax.Precision.HIGHEST, preferred_element_type=F32) + carry[...]
        carry[...] = dlf[0:1, :]
        dz = dlf * (1.0 - _sigmoid(f_ref[...] + b_ref[...]))
        df_ref[...] = dz.astype(df_ref.dtype)
        db_ref[...] += jnp.sum(dz, axis=0, keepdims=True)

    return pl.pallas_call(
        body, name="gates_bwd", grid=(nb,),
        in_specs=[pl.BlockSpec((ts, LANES), lambda i: (nb - 1 - i, 0)),
                  pl.BlockSpec((ts, LANES), lambda i: (nb - 1 - i, fcol)), _vec_spec(LANES)],
        out_specs=[pl.BlockSpec((ts, LANES), lambda i: (nb - 1 - i, 0)), _vec_spec(LANES)],
        out_shape=[jax.ShapeDtypeStruct((dm.S, LANES), BF16), jax.ShapeDtypeStruct((1, LANES), F32)],
        scratch_shapes=[pltpu.VMEM((1, LANES), F32)], compiler_params=_cp(1),
    )(dc, cf, bf)


def _dot_nt(a, b):
    return lax.dot_general(a, b, (((1,), (1,)), ((), ())), preferred_element_type=F32)


def _dot_tn(a, b):
    return lax.dot_general(a, b, (((0,), (0,)), ((), ())), preferred_element_type=F32)


_C0, _C1, _C2 = HEAD_DIM, HEAD_DIM + 3, HEAD_DIM + 6


def _split3(c):
    hi = c.astype(BF16).astype(F32)
    mid = (c - hi).astype(BF16).astype(F32)
    return hi, mid, c - hi - mid


def _put3(base, lane, start, pieces, sign=1.0):
    out = base
    for t, piece in enumerate(pieces):
        out = jnp.where(lane == start + t, sign * piece, out)
    return out


def _head_lanes(pair, odd):
    v = pair.astype(F32)
    return pltpu.roll(v, HEAD_DIM, axis=1) if odd else v


def _attn_prep(qkv, cum, dm):
    ts, A, H = dm.ts, dm.A, dm.H
    scale = HEAD_DIM ** -0.5

    def body(x_ref, c_ref, qa_ref, ka_ref, va_ref):
        lane = lax.broadcasted_iota(jnp.int32, (ts, LANES), 1)
        data = lane < HEAD_DIM
        for h in range(H):
            e, odd = h // 2, h % 2
            pieces = _split3(c_ref[:, h:h + 1])
            q = _head_lanes(x_ref[:, e * LANES:(e + 1) * LANES], odd) * scale
            k = _head_lanes(x_ref[:, A + e * LANES:A + (e + 1) * LANES], odd)
            v = _head_lanes(x_ref[:, 2 * A + e * LANES:2 * A + (e + 1) * LANES], odd)
            qa = jnp.where(data, q, jnp.where((lane >= _C1) & (lane < _C2), 1.0, 0.0))
            qa_ref[h] = _put3(qa, lane, _C0, pieces).astype(BF16)
            ka = jnp.where(data, k, jnp.where((lane < _C1) | ((lane >= _C2) & (lane < _C2 + 3)), 1.0, 0.0))
            ka_ref[h] = _put3(ka, lane, _C1, pieces, -1.0).astype(BF16)
            va_ref[h] = jnp.where(data, v, jnp.where(lane < _C1, 1.0, 0.0)).astype(BF16)

    spec = pl.BlockSpec((H, ts, LANES), lambda i: (0, i, 0))
    return pl.pallas_call(
        body, name="attn_prep", grid=(dm.S // ts,),
        in_specs=[_row_spec(ts, 3 * A), _row_spec(ts, LANES)], out_specs=[spec] * 3,
        out_shape=[jax.ShapeDtypeStruct((H, dm.S, LANES), BF16)] * 3, compiler_params=_cp(1),
    )(qkv, cum)


def _attn_fwd(qa, ka, va, dm):
    tq, A, H, S = dm.ts, dm.A, dm.H, dm.S
    nq = S // tq

    def body(qa_ref, ka_ref, va_ref, o_ref, lse_ref, m_scr, acc_scr):
        i = pl.program_id(0)
        m_scr[...] = jnp.full(m_scr.shape, -1e30, F32)
        acc_scr[...] = jnp.zeros(acc_scr.shape, F32)
        row = lax.broadcasted_iota(jnp.int32, (tq, tq), 0)
        col = lax.broadcasted_iota(jnp.int32, (tq, tq), 1)

        def block(j, masked):
            rows = pl.ds(pl.multiple_of(j * tq, tq), tq)
            for h in range(H):
                s = _dot_nt(qa_ref[h], ka_ref[h, rows, :])
                if masked:
                    s = jnp.where(row >= col, s, -1e30)
                m_old = m_scr[h]
                m_new = jnp.maximum(m_old, jnp.max(s, axis=1, keepdims=True))
                p = jnp.exp(s - m_new)
                acc_scr[h] = jnp.exp(m_old - m_new) * acc_scr[h] + jnp.dot(p.astype(BF16), va_ref[h, rows, :],
                                                                           preferred_element_type=F32)
                m_scr[h] = m_new

        def step(j, carry):
            block(j, False)
            return carry

        lax.fori_loop(0, i, step, 0)
        block(i, True)
        lane = lax.broadcasted_iota(jnp.int32, (tq, LANES), 1)
        lse_all = jnp.zeros((tq, LANES), F32)
        for h in range(H):
            acc = acc_scr[h]
            l = acc[:, _C0:_C0 + 1]
            o_ref[:, h * HEAD_DIM:(h + 1) * HEAD_DIM] = (acc[:, :HEAD_DIM] / l).astype(o_ref.dtype)
            lse_all = jnp.where(lane == h, m_scr[h] + jnp.log(l), lse_all)
        lse_ref[...] = lse_all

    full = pl.BlockSpec((H, S, LANES), lambda i: (0, 0, 0))
    return pl.pallas_call(
        body, name="attn_fwd", grid=(nq,),
        in_specs=[pl.BlockSpec((H, tq, LANES), lambda i: (0, i, 0)), full, full],
        out_specs=[pl.BlockSpec((tq, A), lambda i: (i, 0)), _row_spec(tq, LANES)],
        out_shape=[jax.ShapeDtypeStruct((S, A), BF16), jax.ShapeDtypeStruct((S, LANES), F32)],
        scratch_shapes=[pltpu.VMEM((H, tq, 1), F32), pltpu.VMEM((H, tq, LANES), F32)],
        compiler_params=_cp(1),
    )(qa, ka, va)


def _attn_prep_bwd(qa, lse, dcat, o, dm):
    ts, A, H = dm.ts, dm.A, dm.H

    def body(qa_ref, lse_ref, do_ref, o_ref, qb_ref, doa_ref):
        lane = lax.broadcasted_iota(jnp.int32, (ts, LANES), 1)
        data = lane < HEAD_DIM
        for h in range(H):
            e, odd = h // 2, h % 2
            qb_ref[h] = _put3(qa_ref[h].astype(F32), lane, _C2, _split3(lse_ref[:, h:h + 1]), -1.0).astype(BF16)
            do_pair = do_ref[:, e * LANES:(e + 1) * LANES]
            prod = do_pair * o_ref[:, e * LANES:(e + 1) * LANES].astype(F32)
            mine = (lane >= HEAD_DIM) if odd else data
            delta = jnp.sum(jnp.where(mine, prod, 0.0), axis=1, keepdims=True)
            doa = jnp.where(data, _head_lanes(do_pair, odd), 0.0)
            doa_ref[h] = _put3(doa, lane, _C0, _split3(delta), -1.0).astype(BF16)

    spec = pl.BlockSpec((H, ts, LANES), lambda i: (0, i, 0))
    return pl.pallas_call(
        body, name="attn_prep_bwd", grid=(dm.S // ts,),
        in_specs=[spec, _row_spec(ts, LANES), _row_spec(ts, A, 0), _row_spec(ts, A)], out_specs=[spec] * 2,
        out_shape=[jax.ShapeDtypeStruct((H, dm.S, LANES), BF16)] * 2, compiler_params=_cp(1),
    )(qa, lse, dcat, o)


def _attn_bwd(qb, ka, va, doa, dm):
    tq, H, S = dm.ts, dm.H, dm.S
    nq = S // tq

    def body(ka_ref, va_ref, qb_ref, doa_ref, dq_ref, dk_ref, dv_ref):
        j = pl.program_id(0)

        @pl.when(j == 0)
        def _():
            dq_ref[...] = jnp.zeros(dq_ref.shape, F32)
        dk_ref[...] = jnp.zeros(dk_ref.shape, F32)
        dv_ref[...] = jnp.zeros(dv_ref.shape, F32)
        row = lax.broadcasted_iota(jnp.int32, (tq, tq), 0)
        col = lax.broadcasted_iota(jnp.int32, (tq, tq), 1)

        def block(i, masked):
            rows = pl.ds(pl.multiple_of(i * tq, tq), tq)
            for h in range(H):
                q, do_ = qb_ref[h, rows, :], doa_ref[h, rows, :]
                k, v = ka_ref[h], va_ref[h]
                s = _dot_nt(q, k)
                if masked:
                    s = jnp.where(row >= col, s, -1e30)
                p = jnp.exp(s)
                dsb = (p * _dot_nt(do_, v)).astype(BF16)
                dv_ref[h] += _dot_tn(p.astype(BF16), do_)
                dk_ref[h] += _dot_tn(dsb, q)
                dq_ref[h, rows, :] += jnp.dot(dsb, k, preferred_element_type=F32)

        block(j, True)

        def step(i, carry):
            block(i, False)
            return carry

        lax.fori_loop(j + 1, nq, step, 0)

    blk = pl.BlockSpec((H, tq, LANES), lambda j: (0, j, 0))
    full = pl.BlockSpec((H, S, LANES), lambda j: (0, 0, 0))
    return pl.pallas_call(
        body, name="attn_bwd", grid=(nq,),
        in_specs=[blk, blk, full, full], out_specs=[full, blk, blk],
        out_shape=[jax.ShapeDtypeStruct((H, S, LANES), F32)] * 3, compiler_params=_cp(1),
    )(ka, va, qb, doa)


def _attn_post(dqa, dka, dva, dm):
    ts, A, H = dm.ts, dm.A, dm.H
    scale = HEAD_DIM ** -0.5

    def body(dq_ref, dk_ref, dv_ref, o_ref, dc_ref):
        lane = lax.broadcasted_iota(jnp.int32, (ts, LANES), 1)
        data = lane < HEAD_DIM
        dc = jnp.zeros((ts, LANES), F32)
        for h in range(H):
            dc = jnp.where(lane == h, dq_ref[h][:, _C0:_C0 + 1] - dk_ref[h][:, _C1:_C1 + 1], dc)
        dc_ref[...] = dc
        for part, (ref, mul) in enumerate(((dq_ref, scale), (dk_ref, 1.0), (dv_ref, 1.0))):
            for e in range(H // 2):
                pair = jnp.where(data, ref[2 * e], pltpu.roll(ref[2 * e + 1], HEAD_DIM, axis=1))
                o_ref[:, part * A + e * LANES:part * A + (e + 1) * LANES] = (pair * mul).astype(o_ref.dtype)

    spec = pl.BlockSpec((H, ts, LANES), lambda i: (0, i, 0))
    return pl.pallas_call(
        body, name="attn_post", grid=(dm.S // ts,),
        in_specs=[spec] * 3, out_specs=[_row_spec(ts, 3 * A), _row_spec(ts, LANES)],
        out_shape=[jax.ShapeDtypeStruct((dm.S, 3 * A), BF16), jax.ShapeDtypeStruct((dm.S, LANES), F32)],
        compiler_params=_cp(1),
    )(dqa, dka, dva)


def _glu(cf_rows, c):
    return cf_rows[:, :c] * _sigmoid(cf_rows[:, c:2 * c])


def _conv_fwd(cf, cw, cb, lg, lb, dm):
    ts, C = dm.ts, dm.C
    per = ts // HALO

    def body(cf_ref, halo_ref, w_ref, cb_ref, lg_ref, lb_ref, u3_ref, u1_ref):
        i = pl.program_id(0)
        prev = jnp.where(i > 0, _glu(halo_ref[...], C), 0.0)
        win = jnp.concatenate([prev, _glu(cf_ref[...], C)], axis=0)
        u1 = jnp.zeros((ts, C), F32) + cb_ref[...]
        off = HALO - (CONV_K - 1)
        for k in range(CONV_K):
            u1 = u1 + w_ref[k:k + 1, :] * win[off + k:off + k + ts, :]
        u1_ref[...] = u1
        mu = jnp.mean(u1, axis=-1, keepdims=True)
        cen = u1 - mu
        rstd = lax.rsqrt(jnp.mean(cen * cen, axis=-1, keepdims=True) + EPS)
        u2 = cen * rstd * lg_ref[...] + lb_ref[...]
        u3_ref[...] = (u2 * _sigmoid(u2)).astype(u3_ref.dtype)

    return pl.pallas_call(
        body, name="conv_fwd", grid=(dm.S // ts,),
        in_specs=[pl.BlockSpec((ts, 2 * C), lambda i: (i, 0)),
                  pl.BlockSpec((HALO, 2 * C), lambda i: (jnp.maximum(i * per - 1, 0), 0)),
                  pl.BlockSpec((HALO, C), lambda i: (0, 0))] + [_vec_spec(C)] * 3,
        out_specs=[_row_spec(ts, C)] * 2,
        out_shape=[jax.ShapeDtypeStruct((dm.S, C), BF16), jax.ShapeDtypeStruct((dm.S, C), F32)],
        compiler_params=_cp(1),
    )(cf, cf, cw, cb, lg, lb)


def _conv_bwd(dcat, u1, cf, cw, lg, lb, dm):
    ts, C = dm.ts, dm.C
    per = ts // HALO
    nt = dm.S // ts
    last_halo = dm.S // HALO - 1

    def ln_bwd(du3, u1v, lg_v, lb_v):
        mu = jnp.mean(u1v, axis=-1, keepdims=True)
        cen = u1v - mu
        rstd = lax.rsqrt(jnp.mean(cen * cen, axis=-1, keepdims=True) + EPS)
        uhat = cen * rstd
        u2 = uhat * lg_v + lb_v
        sg = _sigmoid(u2)
        du2 = du3 * (sg * (1.0 + u2 * (1.0 - sg)))
        duh = du2 * lg_v
        du1 = rstd * (duh - jnp.mean(duh, axis=-1, keepdims=True) - uhat * jnp.mean(duh * uhat, axis=-1, keepdims=True))
        return du1, du2, uhat

    def body(d_ref, dn_ref, u1_ref, u1n_ref, cf_ref, halo_ref, w_ref, lg_ref, lb_ref,
             dcf_ref, dw_ref, dcb_ref, dlg_ref, dlb_ref):
        i = pl.program_id(0)

        @pl.when(i == 0)
        def _():
            dw_ref[...] = jnp.zeros_like(dw_ref)
            dcb_ref[...] = jnp.zeros_like(dcb_ref)
            dlg_ref[...] = jnp.zeros_like(dlg_ref)
            dlb_ref[...] = jnp.zeros_like(dlb_ref)
        lg_v, lb_v = lg_ref[...], lb_ref[...]
        du1, du2, uhat = ln_bwd(d_ref[...], u1_ref[...], lg_v, lb_v)
        du1n, _, _ = ln_bwd(dn_ref[...], u1n_ref[...], lg_v, lb_v)
        du1n = jnp.where(i < nt - 1, du1n, 0.0)
        dlg_ref[...] += jnp.sum(du2 * uhat, axis=0, keepdims=True)
        dlb_ref[...] += jnp.sum(du2, axis=0, keepdims=True)
        dcb_ref[...] += jnp.sum(du1, axis=0, keepdims=True)
        dwin = jnp.concatenate([du1, du1n], axis=0)
        cfv = cf_ref[...]
        cv, sg = cfv[:, :C], _sigmoid(cfv[:, C:2 * C])
        prev = jnp.where(i > 0, _glu(halo_ref[...], C), 0.0)
        uwin = jnp.concatenate([prev, cv * sg], axis=0)
        du0 = jnp.zeros((ts, C), F32)
        off = HALO - (CONV_K - 1)
        for k in range(CONV_K):
            back = CONV_K - 1 - k
            du0 = du0 + w_ref[k:k + 1, :] * dwin[back:back + ts, :]
            dw_ref[k:k + 1, :] += jnp.sum(du1 * uwin[off + k:off + k + ts, :], axis=0, keepdims=True)
        dcf_ref[:, :C] = (du0 * sg).astype(dcf_ref.dtype)
        dcf_ref[:, C:] = (du0 * cv * sg * (1.0 - sg)).astype(dcf_ref.dtype)

    ucol = dm.A // C
    return pl.pallas_call(
        body, name="conv_bwd", grid=(nt,),
        in_specs=[pl.BlockSpec((ts, C), lambda i: (i, ucol)),
                  pl.BlockSpec((HALO, C), lambda i: (jnp.minimum((i + 1) * per, last_halo), ucol)),
                  pl.BlockSpec((ts, C), lambda i: (i, 0)),
                  pl.BlockSpec((HALO, C), lambda i: (jnp.minimum((i + 1) * per, last_halo), 0)),
                  pl.BlockSpec((ts, 2 * C), lambda i: (i, 0)),
                  pl.BlockSpec((HALO, 2 * C), lambda i: (jnp.maximum(i * per - 1, 0), 0)),
                  pl.BlockSpec((HALO, C), lambda i: (0, 0)), _vec_spec(C), _vec_spec(C)],
        out_specs=[_row_spec(ts, 2 * C), pl.BlockSpec((HALO, C), lambda i: (0, 0))] + [_vec_spec(C)] * 3,
        out_shape=[jax.ShapeDtypeStruct((dm.S, 2 * C), BF16), jax.ShapeDtypeStruct((HALO, C), F32)]
        + [jax.ShapeDtypeStruct((1, C), F32)] * 3,
        compiler_params=_cp(1),
    )(dcat, dcat, u1, u1, cf, cf, cw, lg, lb)


def _ada_fwd(c16, ada_w, ada_b_cols, dm):
    L, D, n = ada_w.shape
    tn = _tile(n, 512)

    def body(c_ref, w_ref, b_ref, o_ref, a_ref):
        cv = c_ref[...]
        act = (cv * _sigmoid(cv)).astype(BF16)
        a_ref[...] = act
        o_ref[...] = jnp.dot(act, w_ref[...].astype(BF16), preferred_element_type=F32) + b_ref[...]

    return pl.pallas_call(
        body, name="ada_fwd", grid=(L, n // tn),
        in_specs=[pl.BlockSpec((16, D), lambda l, j: (0, 0)), pl.BlockSpec((None, D, tn), lambda l, j: (l, 0, j)),
                  pl.BlockSpec((None, 1, tn), lambda l, j: (l, 0, j))],
        out_specs=[pl.BlockSpec((None, 16, tn), lambda l, j: (l, 0, j)), pl.BlockSpec((16, D), lambda l, j: (0, 0))],
        out_shape=[jax.ShapeDtypeStruct((L, 16, n), F32), jax.ShapeDtypeStruct((16, D), BF16)],
        compiler_params=_cp(2),
    )(c16, ada_w, ada_b_cols)


def _ada_bwd(act16, dmod16):
    L, _, n = dmod16.shape
    D = act16.shape[1]
    tm = min(TN_TM, D)

    def body(a_ref, d_ref, o_ref):
        o_ref[...] = _dot_tn(a_ref[...], d_ref[...])

    return pl.pallas_call(
        body, name="ada_bwd", grid=(L, D // tm),
        in_specs=[pl.BlockSpec((16, tm), lambda l, i: (0, i)), pl.BlockSpec((None, 16, n), lambda l, i: (l, 0, 0))],
        out_specs=pl.BlockSpec((None, tm, n), lambda l, i: (l, i, 0)),
        out_shape=jax.ShapeDtypeStruct((L, D, n), F32), compiler_params=_cp(2),
    )(act16, dmod16)


def _sum_devices(g8, after=()):
    _, R, _ = g8.shape
    tr = _rows_tile(R)

    def body(g_ref, *rest):
        o_ref = rest[len(after)]
        acc = g_ref[0]
        for d in range(1, N_DEV):
            acc = acc + g_ref[d]
        o_ref[...] = acc

    return pl.pallas_call(
        body, name="sum_devices", grid=(R // tr,),
        in_specs=[pl.BlockSpec((N_DEV, tr, LANES), lambda i: (0, i, 0))] + [_ORDER] * len(after),
        out_specs=pl.BlockSpec((tr, LANES), lambda i: (i, 0)),
        out_shape=jax.ShapeDtypeStruct((R, LANES), F32), compiler_params=_cp(1),
    )(g8, *after)


def _rows_tile(r, cap=512):
    for t in (512, 256, 128, 64, 32, 16, 8):
        if t <= cap and r % t == 0:
            return t
    return r


def _adam_math(w, g, m, v):
    m = ADAM_B1 * m + (1.0 - ADAM_B1) * g
    v = ADAM_B2 * v + (1.0 - ADAM_B2) * (g * g)
    m_hat = m / (1.0 - ADAM_B1 ** ADAM_STEP)
    v_hat = v / (1.0 - ADAM_B2 ** ADAM_STEP)
    delta = -ADAM_LR * (m_hat / (jnp.sqrt(v_hat) + ADAM_EPS) + ADAM_WD * w)
    return delta, m, v


def _adamw_dense(w, m, v, g, name):
    R, Cc = w.shape
    tr = _rows_tile(R, 128)

    def body(w_ref, m_ref, v_ref, g_ref, d_ref, mo_ref, vo_ref):
        d, mn, vn = _adam_math(w_ref[...], g_ref[...], m_ref[...], v_ref[...])
        d_ref[...] = d
        mo_ref[...] = mn
        vo_ref[...] = vn

    spec = pl.BlockSpec((tr, Cc), lambda i: (i, 0))
    return pl.pallas_call(
        body, name=name, grid=(R // tr,), in_specs=[spec] * 4, out_specs=[spec] * 3,
        out_shape=[jax.ShapeDtypeStruct((R, Cc), F32)] * 3, compiler_params=_cp(1),
    )(w, m, v, g)


def _adamw_shard(w, m, v, near, far, layer, prev, name):
    L, r, cc = w.shape
    tr = _rows_tile(r, 128)

    def body(w_ref, m_ref, v_ref, n_ref, f_ref, *rest):
        g_ref, d_ref, mo_ref, vo_ref = rest[-4:]
        g = n_ref[0].astype(F32) + f_ref[0].astype(F32)
        for k in range(1, N_CHIPS):
            g = g + (n_ref[k].astype(F32) + f_ref[k].astype(F32))
        d, mn, vn = _adam_math(w_ref[...], g, m_ref[...], v_ref[...])
        g_ref[...] = g
        d_ref[...] = d
        mo_ref[...] = mn
        vo_ref[...] = vn

    wspec = pl.BlockSpec((None, tr, cc), lambda i: (layer, i, 0))
    sspec = pl.BlockSpec((N_CHIPS, tr, cc), lambda i: (0, i, 0))
    n_prev = 0 if prev is None else 4
    return pl.pallas_call(
        body, name=name, grid=(r // tr,),
        in_specs=[wspec] * 3 + [sspec] * 2 + [pl.BlockSpec(memory_space=pl.ANY)] * n_prev,
        out_specs=[wspec] * 4,
        out_shape=[jax.ShapeDtypeStruct((L, r, cc), F32)] * 4,
        input_output_aliases={5 + t: t for t in range(n_prev)},
        compiler_params=_cp(1),
    )(w, m, v, near, far, *(prev or ()))


def _pack(vs):
    flat = jnp.concatenate([v.reshape(-1).astype(F32) for v in vs])
    pad = (-flat.shape[0]) % (64 * LANES)
    return jnp.pad(flat, (0, pad)).reshape(-1, LANES)


def _unpack(packed, shapes):
    flat = packed.reshape(-1)
    out, pos = [], 0
    for s in shapes:
        n = 1
        for d in s:
            n *= d
        out.append(flat[pos:pos + n].reshape(s))
        pos += n
    return out


def kernel(x, c, w_in, b_f, conv_w, conv_b, conv_ln_g, conv_ln_b, w_o, w_ffn_in, w_ffn_out, mix_pre_g, mix_post_g, ffn_pre_g, ffn_post_g, ada_w, ada_b, loss_target, m_w_in, m_b_f, m_conv_w, m_conv_b, m_conv_ln_g, m_conv_ln_b, m_w_o, m_w_ffn_in, m_w_ffn_out, m_mix_pre_g, m_mix_post_g, m_ffn_pre_g, m_ffn_post_g, m_ada_w, m_ada_b, v_w_in, v_b_f, v_conv_w, v_conv_b, v_conv_ln_g, v_conv_ln_b, v_w_o, v_w_ffn_in, v_w_ffn_out, v_mix_pre_g, v_mix_post_g, v_ffn_pre_g, v_ffn_post_g, v_ada_w, v_ada_b):
    params = dict(w_in=w_in, b_f=b_f, conv_w=conv_w, conv_b=conv_b, conv_ln_g=conv_ln_g, conv_ln_b=conv_ln_b, w_o=w_o,
                  w_ffn_in=w_ffn_in, w_ffn_out=w_ffn_out, mix_pre_g=mix_pre_g, mix_post_g=mix_post_g,
                  ffn_pre_g=ffn_pre_g, ffn_post_g=ffn_post_g, ada_w=ada_w, ada_b=ada_b)
    mom = dict(w_in=m_w_in, b_f=m_b_f, conv_w=m_conv_w, conv_b=m_conv_b, conv_ln_g=m_conv_ln_g, conv_ln_b=m_conv_ln_b,
               w_o=m_w_o, w_ffn_in=m_w_ffn_in, w_ffn_out=m_w_ffn_out, mix_pre_g=m_mix_pre_g, mix_post_g=m_mix_post_g,
               ffn_pre_g=m_ffn_pre_g, ffn_post_g=m_ffn_post_g, ada_w=m_ada_w, ada_b=m_ada_b)
    var = dict(w_in=v_w_in, b_f=v_b_f, conv_w=v_conv_w, conv_b=v_conv_b, conv_ln_g=v_conv_ln_g, conv_ln_b=v_conv_ln_b,
               w_o=v_w_o, w_ffn_in=v_w_ffn_in, w_ffn_out=v_w_ffn_out, mix_pre_g=v_mix_pre_g, mix_post_g=v_mix_post_g,
               ffn_pre_g=v_ffn_pre_g, ffn_post_g=v_ffn_post_g, ada_w=v_ada_w, ada_b=v_ada_b)

    S, D = x.shape[1], x.shape[2]
    L = w_in.shape[0]
    A = D // 2
    C = D - A
    H = A // HEAD_DIM
    F = w_ffn_out.shape[1] * N_CHIPS
    d_in = w_in.shape[2] * N_CHIPS
    NP = 3 * A + 2 * C + LANES
    dm = Dims(S=S, D=D, A=A, C=C, H=H, F=F, L=L, NP=NP, ts=min(ROW_TILE, S))
    assert H <= 8 and A == C and d_in == 3 * A + H + 2 * C

    ix, iy, ic = _place()
    chip = 2 * ix + iy
    dev = 4 * ix + 2 * iy + ic
    x2 = x.reshape(S, D)
    tgt = loss_target.reshape(S, D)

    def gather_start(l, after):
        first = _exchange_start([w_in[l].astype(BF16)], "gather", after, f"gather_a_start_{l}")
        rest = _exchange_start([w_o[l].astype(BF16), w_ffn_in[l].astype(BF16), w_ffn_out[l].astype(BF16)], "gather",
                               [first["token"]], f"gather_b_start_{l}")
        return first, rest

    def gather_wait(st, after, name):
        srcs, lands = _exchange_wait(st, after, name)
        return [lax.dynamic_update_index_in_dim(land, src, chip, 0) for land, src in zip(lands, srcs)]

    c_all = _all_gather_devices(c.reshape(D // LANES, LANES), "gather_c").reshape(N_DEV, D)
    c16 = jnp.pad(c_all, ((0, 16 - N_DEV), (0, 0)))
    n_ada = ada_w.shape[2]
    ada_b_cols = lax.dynamic_slice_in_dim(ada_b, chip * n_ada, n_ada, axis=1).reshape(L, 1, n_ada)
    mod_cols, act16 = _ada_fwd(c16, ada_w, ada_b_cols, dm)
    conv_w_all, mod_all = _all_gather_chips([conv_w.reshape(L * CONV_K, -1), mod_cols.reshape(L * 16, n_ada)], "gather_mod")
    cwc = conv_w.shape[2]
    conv_w_full = conv_w_all.reshape(N_CHIPS, L, CONV_K, cwc).transpose(1, 2, 0, 3).reshape(L, CONV_K, C)
    conv_w_full = jnp.pad(conv_w_full, ((0, 0), (0, HALO - CONV_K), (0, 0)))
    mod_all = mod_all.reshape(N_CHIPS, L, 16, n_ada)
    mod_me = lax.dynamic_index_in_dim(mod_all, dev, axis=2, keepdims=False)
    mod_me = mod_me.transpose(1, 0, 2).reshape(L, N_MOD, 1, D)

    gather = [None] * L
    gather[0] = gather_start(0, [mod_all])

    def projection_of(g_in):
        w_nat = g_in.transpose(1, 0, 2).reshape(D, d_in)
        return jnp.concatenate([w_nat[:, :3 * A], w_nat[:, 3 * A + H:], w_nat[:, 3 * A:3 * A + H],
                                jnp.zeros((D, LANES - H), BF16)], axis=1)

    gathered = [None] * L
    vec = lambda p, l: p[l].reshape(1, -1)
    bf_pad = jnp.pad(b_f, ((0, 0), (0, LANES - H)))

    saved = []
    xin = x2
    h = _pre_norm(xin, vec(mix_pre_g, 0), mod_me[0, 1], mod_me[0, 0], dm)
    dx = loss_part = None
    for l in range(L):
        (g_in,) = gather_wait(gather[l][0], [h], f"gather_a_wait_{l}")
        w_p = projection_of(g_in)
        order = [gather[l][1]["token"]]
        if l + 1 < L:
            gather[l + 1] = gather_start(l + 1, [g_in, gather[l][1]["token"]])
            order.append(gather[l + 1][1]["token"])
        qkv = _mm_nn(h, w_p, BF16, "mm_qkv", 0, 3 * A, after=order)
        cf = _mm_nn(h, w_p, F32, "mm_cf", 3 * A, 2 * C + LANES)
        cum = _gates_fwd(cf, vec(bf_pad, l), dm)
        qa, ka, va = _attn_prep(qkv, cum, dm)
        o, lse = _attn_fwd(qa, ka, va, dm)
        u3, u1 = _conv_fwd(cf, conv_w_full[l], vec(conv_b, l), vec(conv_ln_g, l), vec(conv_ln_b, l), dm)
        cat = jnp.concatenate([o, u3], axis=1)
        g_o, wfi, g_fo = gather_wait(gather[l][1], [cat], f"gather_b_wait_{l}")
        wo, wfo = g_o.reshape(D, D), g_fo.reshape(F, D)
        gathered[l] = (w_p, wo, wfi, wfo)
        y = _mm_nn(cat, wo, F32, "mm_o")
        x1, h2 = _res_norm(xin, y, vec(mix_post_g, l), mod_me[l, 2], vec(ffn_pre_g, l), mod_me[l, 4], mod_me[l, 3], dm)
        gu = _mm_nn_blocked(h2, wfi, F32, "mm_ffn_in")
        a = _swiglu_fwd(gu, dm)
        y2 = _mm_nn(a, wfo, F32, "mm_ffn_out")
        saved.append(dict(xin=xin, h=h, qa=qa, ka=ka, va=va, cf=cf, o=o, lse=lse, u1=u1, cat=cat, y=y,
                          x1=x1, h2=h2, gu=gu, a=a, y2=y2))
        if l + 1 < L:
            xin, h = _res_norm(x1, y2, vec(ffn_post_g, l), mod_me[l, 5], vec(mix_pre_g, l + 1),
                               mod_me[l + 1, 1], mod_me[l + 1, 0], dm)
        else:
            dx, loss_part = _res_loss(x1, y2, vec(ffn_post_g, l), mod_me[l, 5], tgt, dm)
    loss = lax.psum(loss_part[0, 0], ("x", "y", "c"))

    small = [None] * L
    big = [None] * L
    order = []
    for l in reversed(range(L)):
        w_p, wo, wfi, wfo = gathered[l]
        sv = saved[l]
        dy2, d_gfpost, d_g2 = _post_bwd(dx, sv["y2"], vec(ffn_post_g, l), mod_me[l, 5], dm, after=order)
        da = _mm_nt(dy2, wfo, F32, "mm_da")
        dgu = _swiglu_bwd(da, sv["gu"], dm)
        g_wfo = _mm_tn(sv["a"], dy2, BF16, "mm_dwfo")
        g_wfi = _mm_tn(sv["h2"], dgu, BF16, "mm_dwfi", blocked=True)
        dh2 = _mm_nt_blocked(dgu, wfi, F32, "mm_dh2")
        scatter_ffn = _exchange_start([g_wfi, g_wfo.reshape(N_CHIPS, F // N_CHIPS, D)], "scatter", [], f"scatter_b_start_{l}")
        dx1, d_sh2, d_sc2, d_gfpre = _pre_bwd(dh2, dx, sv["x1"], vec(ffn_pre_g, l), mod_me[l, 4], dm,
                                              after=[scatter_ffn["token"]])
        dy, d_gpost, d_g1 = _post_bwd(dx1, sv["y"], vec(mix_post_g, l), mod_me[l, 2], dm)
        dcat = _mm_nt(dy, wo, F32, "mm_dcat")
        g_wo = _mm_tn(sv["cat"], dy, BF16, "mm_dwo")
        dcfc, d_cw, d_cb, d_lg, d_lb = _conv_bwd(dcat, sv["u1"], sv["cf"], conv_w_full[l], vec(conv_ln_g, l),
                                                 vec(conv_ln_b, l), dm)
        qb, doa = _attn_prep_bwd(sv["qa"], sv["lse"], dcat, sv["o"], dm)
        dqkv, dcum = _attn_post(*_attn_bwd(qb, sv["ka"], sv["va"], doa, dm), dm)
        dfl, d_bf = _gates_bwd(dcum, sv["cf"], vec(bf_pad, l), dm)
        dproj = jnp.concatenate([dqkv, dcfc, dfl], axis=1)
        dh = _mm_nt(dproj, w_p, F32, "mm_dh")
        g_wp = _mm_tn(sv["h"], dproj, BF16, "mm_dwp")
        dx, d_sh1, d_sc1, d_gpre = _pre_bwd(dh, dx1, sv["xin"], vec(mix_pre_g, l), mod_me[l, 1], dm)
        g_nat = jnp.concatenate([g_wp[:, :3 * A], g_wp[:, 3 * A + 2 * C:3 * A + 2 * C + H], g_wp[:, 3 * A:3 * A + 2 * C]], axis=1)
        g_win = g_nat.reshape(D, N_CHIPS, d_in // N_CHIPS).transpose(1, 0, 2)
        g_mix = [g_win, g_wo.reshape(N_CHIPS, D // N_CHIPS, D)]
        if l > 0:
            scatter_mix = _exchange_start(g_mix, "scatter", [], f"scatter_a_start_{l}")
            order = [scatter_mix["token"]]
            big[l] = (scatter_mix, scatter_ffn)
        small[l] = dict(b_f=d_bf[0, :H], conv_b=d_cb[0], conv_ln_g=d_lg[0], conv_ln_b=d_lb[0], mix_pre_g=d_gpre[0],
                        mix_post_g=d_gpost[0], ffn_pre_g=d_gfpre[0], ffn_post_g=d_gfpost[0],
                        dmod=jnp.concatenate([d_sh1, d_sc1, d_g1, d_sh2, d_sc2, d_g2], axis=1)[0],
                        conv_w=d_cw[:CONV_K])
    grad_x = dx.reshape(1, S, D)

    keys_small = ["b_f", "conv_b", "conv_ln_g", "conv_ln_b", "mix_pre_g", "mix_post_g", "ffn_pre_g", "ffn_post_g",
                  "dmod", "conv_w"]
    stacked = [jnp.stack([small[l][k] for l in range(L)]) for k in keys_small]
    shapes = [s.shape for s in stacked]
    pack = _pack(stacked)
    pack8 = _all_gather_devices(pack, "gather_small")
    big[0] = (_exchange_start(g_mix, "scatter", [pack8], "scatter_a_start_0"), scatter_ffn)
    summed = dict(zip(keys_small, _unpack(_sum_devices(pack8, after=[big[0][0]["token"]]), shapes)))
    dmod_all = jnp.stack([_unpack(pack8[d], shapes)[keys_small.index("dmod")] for d in range(N_DEV)])
    grads = {k: summed[k] for k in keys_small[:8]}
    grads["ada_b"] = summed["dmod"]
    grads["conv_w"] = lax.dynamic_slice_in_dim(summed["conv_w"], chip * cwc, cwc, axis=2)

    dmod_cols = lax.dynamic_slice_in_dim(dmod_all.reshape(N_DEV, L, N_CHIPS, n_ada), chip, 1, axis=2)
    dmod16 = jnp.pad(dmod_cols.reshape(N_DEV, L, n_ada).transpose(1, 0, 2), ((0, 0), (0, 16 - N_DEV), (0, 0))).astype(BF16)
    grads["ada_w"] = _ada_bwd(act16, dmod16)

    d_aw, m_aw, v_aw = _adamw_dense(ada_w.reshape(L * D, n_ada), m_ada_w.reshape(L * D, n_ada),
                                    v_ada_w.reshape(L * D, n_ada), grads["ada_w"].reshape(L * D, n_ada), "adamw_ada_w")
    names_small = ["b_f", "conv_w", "conv_b", "conv_ln_g", "conv_ln_b", "mix_pre_g", "mix_post_g", "ffn_pre_g",
                   "ffn_post_g", "ada_b"]
    shapes_small = [params[n].shape for n in names_small]
    d_s, m_s, v_s = _adamw_dense(_pack([params[n] for n in names_small]), _pack([mom[n] for n in names_small]),
                                 _pack([var[n] for n in names_small]), _pack([grads[n] for n in names_small]),
                                 "adamw_small")
    delta_w = dict(zip(names_small, _unpack(d_s, shapes_small)))
    new_m = dict(zip(names_small, _unpack(m_s, shapes_small)))
    new_v = dict(zip(names_small, _unpack(v_s, shapes_small)))
    delta_w["ada_w"], new_m["ada_w"], new_v["ada_w"] = (t.reshape(L, D, n_ada) for t in (d_aw, m_aw, v_aw))

    names_big = ["w_in", "w_o", "w_ffn_in", "w_ffn_out"]
    res_big = {n: None for n in names_big}
    forward = [None] * L

    def update(l, after):
        near, far = _exchange_wait(forward[l], after, f"forward_wait_{l}")
        for t, n in enumerate(names_big):
            res_big[n] = _adamw_shard(params[n], mom[n], var[n], near[t], far[t], l, res_big[n], f"adamw_{n}_{l}")
        return [res_big[names_big[-1]][1]]

    done = [dx]
    for l in reversed(range(L)):
        after = done + [d_s, d_aw] if l == 0 else done
        near = []
        for st, nm in zip(big[l], ("a", "b")):
            srcs, lands = _exchange_wait(st, after, f"scatter_{nm}_wait_{l}")
            near += [lax.dynamic_update_index_in_dim(land, lax.dynamic_index_in_dim(src, chip, 0, keepdims=False), chip, 0)
                     for land, src in zip(lands, srcs)]
        forward[l] = _exchange_start(near, "sibling", [], f"forward_start_{l}")
        done = update(l + 1, [forward[l]["token"]]) if l + 1 < L else [forward[l]["token"]]
    update(0, [])
    for n in names_big:
        grads[n], delta_w[n], new_m[n], new_v[n] = res_big[n]

    return (loss, grad_x, *[grads[n] for n in WEIGHTS], *[delta_w[n] for n in WEIGHTS],
            *[new_m[n] for n in WEIGHTS], *[new_v[n] for n in WEIGHTS])
```

```python
import collections
import functools

import jax
import jax.numpy as jnp
from jax import lax
from jax.experimental import pallas as pl
from jax.experimental.pallas import tpu as pltpu

F32 = jnp.float32
BF16 = jnp.bfloat16
MESH = pl.DeviceIdType.MESH

HEAD_DIM = 64
CONV_K = 31
N_MOD = 6
EPS = 1e-6
N_CHIPS = 4
N_DEV = 8
LANES = 128
HALO = 32
ROW_TILE = 256
MM_TM = 512
MM_TN_MAX = 1408
TN_TM = 256
VMEM_LIMIT = 56 * 1024 * 1024

ADAM_LR = 0.001
ADAM_B1 = 0.9
ADAM_B2 = 0.999
ADAM_EPS = 1e-08
ADAM_WD = 0.01
ADAM_STEP = 10

WEIGHTS = ['w_in', 'b_f', 'conv_w', 'conv_b', 'conv_ln_g', 'conv_ln_b', 'w_o', 'w_ffn_in', 'w_ffn_out',
           'mix_pre_g', 'mix_post_g', 'ffn_pre_g', 'ffn_post_g', 'ada_w', 'ada_b']

Dims = collections.namedtuple("Dims", "S D A C H F L NP ts")


def _cp(n_grid=0):
    if n_grid:
        return pltpu.CompilerParams(dimension_semantics=("arbitrary",) * n_grid, vmem_limit_bytes=VMEM_LIMIT)
    return pltpu.CompilerParams(vmem_limit_bytes=VMEM_LIMIT)


def _tile(n, cap, also=None):
    best = None
    t = LANES
    while t <= min(n, cap):
        if n % t == 0 and (also is None or also % t == 0):
            best = t
        t += LANES
    assert best is not None, (n, cap, also)
    return best


def _bf(v):
    return v if v.dtype == BF16 else v.astype(BF16)


def _place():
    return lax.axis_index("x"), lax.axis_index("y"), lax.axis_index("c")


def _flip(v, d):
    return 1 - v if d else v


def _all_gather_devices(a, name, after=()):
    def body(a_ref, *rest):
        o_ref, send, recv, lsem = rest[len(after):]
        x, y, c = _place()
        me = 4 * x + 2 * y + c
        local = pltpu.make_async_copy(a_ref, o_ref.at[me], lsem)
        local.start()
        copies = []
        for k in range(1, N_DEV):
            peer = (_flip(x, (k >> 2) & 1), _flip(y, (k >> 1) & 1), _flip(c, k & 1))
            cp = pltpu.make_async_remote_copy(src_ref=a_ref, dst_ref=o_ref.at[me], send_sem=send.at[k - 1],
                                              recv_sem=recv.at[k - 1], device_id=peer, device_id_type=MESH)
            cp.start()
            copies.append(cp)
        for cp in copies:
            cp.wait()
        local.wait()

    return pl.pallas_call(
        body, name=name,
        out_shape=jax.ShapeDtypeStruct((N_DEV,) + a.shape, a.dtype),
        in_specs=[pl.BlockSpec(memory_space=pl.ANY)] * (1 + len(after)),
        out_specs=pl.BlockSpec(memory_space=pl.ANY),
        scratch_shapes=[pltpu.SemaphoreType.DMA((N_DEV - 1,)), pltpu.SemaphoreType.DMA((N_DEV - 1,)),
                        pltpu.SemaphoreType.DMA],
    )(a, *after)


def _all_gather_chips(arrays, name):
    n = len(arrays)

    def body(*refs):
        a_refs, o_refs = refs[:n], refs[n:2 * n]
        send, recv, lsem = refs[2 * n:]
        x, y, c = _place()
        me = 2 * x + y
        copies = []
        for i in range(n):
            local = pltpu.make_async_copy(a_refs[i], o_refs[i].at[me], lsem.at[i])
            local.start()
            copies.append(local)
            for k in range(1, N_CHIPS):
                peer = (_flip(x, (k >> 1) & 1), _flip(y, k & 1), c)
                cp = pltpu.make_async_remote_copy(src_ref=a_refs[i], dst_ref=o_refs[i].at[me],
                                                  send_sem=send.at[i, k - 1], recv_sem=recv.at[i, k - 1],
                                                  device_id=peer, device_id_type=MESH)
                cp.start()
                copies.append(cp)
        for cp in copies:
            cp.wait()

    return pl.pallas_call(
        body, name=name,
        out_shape=[jax.ShapeDtypeStruct((N_CHIPS,) + a.shape, a.dtype) for a in arrays],
        in_specs=[pl.BlockSpec(memory_space=pl.ANY)] * n,
        out_specs=[pl.BlockSpec(memory_space=pl.ANY)] * n,
        scratch_shapes=[pltpu.SemaphoreType.DMA((n, N_CHIPS - 1)), pltpu.SemaphoreType.DMA((n, N_CHIPS - 1)),
                        pltpu.SemaphoreType.DMA((n,))],
    )(*arrays)


_HBM = pl.BlockSpec(memory_space=pltpu.HBM)
_SEM = pl.BlockSpec(memory_space=pltpu.SEMAPHORE)
_EFFECT = pltpu.SideEffectType.DATAFLOW_SIDE_EFFECTING


def _n_copies(mode):
    return 1 if mode == "sibling" else N_CHIPS - 1


def _chip_copies(srcs, lands, send, recv, mode):
    x, y, c = _place()
    me = 2 * x + y
    copies = []
    for i in range(len(srcs)):
        if mode == "sibling":
            copies.append(pltpu.make_async_remote_copy(src_ref=srcs[i], dst_ref=lands[i], send_sem=send.at[i],
                                                       recv_sem=recv.at[i], device_id=(x, y, 1 - c), device_id_type=MESH))
            continue
        for k in range(1, N_CHIPS):
            px, py = _flip(x, (k >> 1) & 1), _flip(y, k & 1)
            src = srcs[i].at[2 * px + py] if mode == "scatter" else srcs[i]
            s = i * (N_CHIPS - 1) + k - 1
            copies.append(pltpu.make_async_remote_copy(src_ref=src, dst_ref=lands[i].at[me], send_sem=send.at[s],
                                                       recv_sem=recv.at[s], device_id=(px, py, c), device_id_type=MESH))
    return copies


def _own_copies(srcs, lands, own, mode):
    x, y, _ = _place()
    me = 2 * x + y
    return [pltpu.make_async_copy(srcs[i].at[me] if mode == "scatter" else srcs[i], lands[i].at[me], own.at[i])
            for i in range(len(srcs))]


_ORDER = pl.BlockSpec(memory_space=pl.ANY)


def _exchange_start(arrays, mode, after, name):
    n = len(arrays)
    n_sems = 2 if mode == "sibling" else 3

    def body(*refs):
        srcs, lands = refs[:n], refs[n:2 * n]
        sems = refs[2 * n + len(after):2 * n + len(after) + n_sems]
        token = refs[-1]
        for cp in _chip_copies(srcs, lands, sems[0], sems[1], mode):
            cp.start()
        if mode != "sibling":
            for cp in _own_copies(srcs, lands, sems[2], mode):
                cp.start()
        token[...] = jnp.zeros_like(token)

    land_shapes = [(N_CHIPS,) + a.shape if mode == "gather" else a.shape for a in arrays]
    n_sem = n * _n_copies(mode)
    sem_shapes = [pltpu.SemaphoreType.DMA((n_sem,)), pltpu.SemaphoreType.DMA((n_sem,)), pltpu.SemaphoreType.DMA((n,))]
    outs = pl.pallas_call(
        body, name=name,
        out_shape=(*sem_shapes[:n_sems],
                   *[pltpu.HBM(a.shape, a.dtype) for a in arrays],
                   *[pltpu.HBM(s, a.dtype) for s, a in zip(land_shapes, arrays)],
                   jax.ShapeDtypeStruct((8, LANES), F32)),
        in_specs=[_HBM] * (2 * n) + [_ORDER] * len(after),
        out_specs=(*[_SEM] * n_sems, *[_HBM] * (2 * n), pl.BlockSpec(memory_space=pltpu.VMEM)),
        input_output_aliases={i: n_sems + i for i in range(2 * n)},
        compiler_params=pltpu.CompilerParams(has_side_effects=_EFFECT),
    )(*[pltpu.with_memory_space_constraint(a, pltpu.HBM) for a in arrays],
      *[pltpu.with_memory_space_constraint(lax.empty(s, a.dtype), pltpu.HBM) for s, a in zip(land_shapes, arrays)],
      *after)
    return dict(sems=outs[:n_sems], srcs=outs[n_sems:n_sems + n], lands=outs[n_sems + n:n_sems + 2 * n], token=outs[-1],
                mode=mode)


def _exchange_wait(st, after, name):
    n = len(st["srcs"])
    mode = st["mode"]
    n_sems = len(st["sems"])

    def body(*refs):
        srcs, lands = refs[:n], refs[n:2 * n]
        sems = refs[2 * n:2 * n + n_sems]
        for cp in _chip_copies(srcs, lands, sems[0], sems[1], mode):
            cp.wait_send()
            cp.wait_recv()
        if mode != "sibling":
            for cp in _own_copies(srcs, lands, sems[2], mode):
                cp.wait()

    outs = pl.pallas_call(
        body, name=name,
        out_shape=tuple(pltpu.HBM(a.shape, a.dtype) for a in (*st["srcs"], *st["lands"])),
        in_specs=[_HBM] * (2 * n) + [_SEM] * n_sems + [_ORDER] * len(after),
        out_specs=tuple([_HBM] * (2 * n)),
        input_output_aliases={i: i for i in range(2 * n)},
        compiler_params=pltpu.CompilerParams(has_side_effects=_EFFECT),
    )(*st["srcs"], *st["lands"], *st["sems"], *after)
    return outs[:n], outs[n:]


def _matmul(a, b, contract, grid, a_spec, b_spec, o_spec, out_shape, name, after=()):
    def body(a_ref, b_ref, *rest):
        o_ref = rest[len(after)]
        r = lax.dot_general(_bf(a_ref[...]), _bf(b_ref[...]), (contract, ((), ())), preferred_element_type=F32)
        o_ref[...] = r.astype(o_ref.dtype)

    return pl.pallas_call(
        body, name=name, grid=grid, in_specs=[a_spec, b_spec] + [_ORDER] * len(after), out_specs=o_spec,
        out_shape=out_shape, compiler_params=_cp(len(grid)),
    )(a, b, *after)


def _mm_nn(a, b, out_dtype, name, col0=0, n=None, after=()):
    m, k = a.shape
    n = b.shape[1] - col0 if n is None else n
    tm = min(MM_TM, m)
    tn = _tile(n, MM_TN_MAX, also=col0 if col0 else None)
    off = col0 // tn
    return _matmul(a, b, ((1,), (0,)), (n // tn, m // tm),
                   pl.BlockSpec((tm, k), lambda j, i: (i, 0)),
                   pl.BlockSpec((k, tn), lambda j, i: (0, j + off)),
                   pl.BlockSpec((tm, tn), lambda j, i: (i, j)),
                   jax.ShapeDtypeStruct((m, n), out_dtype), name, after=after)


def _mm_nn_blocked(a, b3, out_dtype, name):
    m, k = a.shape
    nj, _, nb = b3.shape
    tm = min(MM_TM, m)
    return _matmul(a, b3, ((1,), (0,)), (nj, m // tm),
                   pl.BlockSpec((tm, k), lambda j, i: (i, 0)),
                   pl.BlockSpec((None, k, nb), lambda j, i: (j, 0, 0)),
                   pl.BlockSpec((tm, nb), lambda j, i: (i, j)),
                   jax.ShapeDtypeStruct((m, nj * nb), out_dtype), name)


def _mm_nt(a, b, out_dtype, name):
    m, k = a.shape
    n = b.shape[0]
    tm = min(MM_TM, m)
    tn = _tile(n, MM_TN_MAX)
    return _matmul(a, b, ((1,), (1,)), (n // tn, m // tm),
                   pl.BlockSpec((tm, k), lambda j, i: (i, 0)),
                   pl.BlockSpec((tn, k), lambda j, i: (j, 0)),
                   pl.BlockSpec((tm, tn), lambda j, i: (i, j)),
                   jax.ShapeDtypeStruct((m, n), out_dtype), name)


def _mm_nt_blocked(a, b3, out_dtype, name):
    m = a.shape[0]
    nj, n, nb = b3.shape
    tm = min(MM_TM, m)
    tn = _tile(n, 512)

    def body(a_ref, b_ref, o_ref):
        acc = _dot_nt(a_ref[:, :nb], b_ref[0])
        for k in range(1, nj):
            acc = acc + _dot_nt(a_ref[:, k * nb:(k + 1) * nb], b_ref[k])
        o_ref[...] = acc.astype(o_ref.dtype)

    return pl.pallas_call(
        body, name=name, grid=(m // tm, n // tn),
        in_specs=[pl.BlockSpec((tm, nj * nb), lambda i, j: (i, 0)), pl.BlockSpec((nj, tn, nb), lambda i, j: (0, j, 0))],
        out_specs=pl.BlockSpec((tm, tn), lambda i, j: (i, j)),
        out_shape=jax.ShapeDtypeStruct((m, n), out_dtype), compiler_params=_cp(2),
    )(a, b3)


def _mm_tn(a, b, out_dtype, name, blocked=False):
    k, m = a.shape
    n = b.shape[1]
    tm = min(TN_TM, m)
    tn = n // N_CHIPS if blocked else _tile(n, MM_TN_MAX)
    if blocked:
        o_spec = pl.BlockSpec((None, tm, tn), lambda j, i: (j, i, 0))
        out_shape = jax.ShapeDtypeStruct((n // tn, m, tn), out_dtype)
    else:
        o_spec = pl.BlockSpec((tm, tn), lambda j, i: (i, j))
        out_shape = jax.ShapeDtypeStruct((m, n), out_dtype)
    return _matmul(a, b, ((0,), (0,)), (n // tn, m // tm),
                   pl.BlockSpec((k, tm), lambda j, i: (0, i)),
                   pl.BlockSpec((k, tn), lambda j, i: (0, j)),
                   o_spec, out_shape, name)


def _vec_spec(d):
    return pl.BlockSpec((1, d), lambda i: (0, 0))


def _row_spec(ts, d, col=0):
    return pl.BlockSpec((ts, d), lambda i: (i, col))


def _rms(x):
    return lax.rsqrt(jnp.mean(x * x, axis=-1, keepdims=True) + EPS)


def _pre_norm(x, gain, scale, shift, dm):
    def body(x_ref, g_ref, sc_ref, sh_ref, h_ref):
        xv = x_ref[...]
        h_ref[...] = (((xv * _rms(xv)) * g_ref[...]) * (1.0 + sc_ref[...]) + sh_ref[...]).astype(h_ref.dtype)

    return pl.pallas_call(
        body, name="pre_norm", grid=(dm.S // dm.ts,),
        in_specs=[_row_spec(dm.ts, dm.D)] + [_vec_spec(dm.D)] * 3,
        out_specs=_row_spec(dm.ts, dm.D),
        out_shape=jax.ShapeDtypeStruct((dm.S, dm.D), BF16), compiler_params=_cp(1),
    )(x, gain, scale, shift)


def _res_norm(x, y, gpost, gate, gain, scale, shift, dm):
    def body(x_ref, y_ref, gp_ref, gt_ref, g_ref, sc_ref, sh_ref, xo_ref, h_ref):
        yv = y_ref[...]
        xn = x_ref[...] + gt_ref[...] * ((yv * _rms(yv)) * gp_ref[...])
        xo_ref[...] = xn
        h_ref[...] = (((xn * _rms(xn)) * g_ref[...]) * (1.0 + sc_ref[...]) + sh_ref[...]).astype(h_ref.dtype)

    return pl.pallas_call(
        body, name="res_norm", grid=(dm.S // dm.ts,),
        in_specs=[_row_spec(dm.ts, dm.D)] * 2 + [_vec_spec(dm.D)] * 5,
        out_specs=[_row_spec(dm.ts, dm.D)] * 2,
        out_shape=[jax.ShapeDtypeStruct((dm.S, dm.D), F32), jax.ShapeDtypeStruct((dm.S, dm.D), BF16)],
        compiler_params=_cp(1),
    )(x, y, gpost, gate, gain, scale, shift)


def _res_loss(x, y, gpost, gate, target, dm):
    def body(x_ref, y_ref, gp_ref, gt_ref, t_ref, dx_ref, loss_ref):
        i = pl.program_id(0)

        @pl.when(i == 0)
        def _():
            loss_ref[...] = jnp.zeros_like(loss_ref)
        yv = y_ref[...]
        err = x_ref[...] + gt_ref[...] * ((yv * _rms(yv)) * gp_ref[...]) - t_ref[...]
        dx_ref[...] = err * (1.0 / dm.D)
        per_row = jnp.mean(err * err, axis=-1, keepdims=True)
        loss_ref[...] += 0.5 * jnp.sum(per_row, axis=0, keepdims=True)

    return pl.pallas_call(
        body, name="res_loss", grid=(dm.S // dm.ts,),
        in_specs=[_row_spec(dm.ts, dm.D)] * 2 + [_vec_spec(dm.D)] * 2 + [_row_spec(dm.ts, dm.D)],
        out_specs=[_row_spec(dm.ts, dm.D), _vec_spec(LANES)],
        out_shape=[jax.ShapeDtypeStruct((dm.S, dm.D), F32), jax.ShapeDtypeStruct((1, LANES), F32)],
        compiler_params=_cp(1),
    )(x, y, gpost, gate, target)


def _post_bwd(dxo, y, gpost, gate, dm, after=()):
    def body(dx_ref, y_ref, gp_ref, gt_ref, *rest):
        dy_ref, dgp_ref, dgt_ref = rest[len(after):]
        i = pl.program_id(0)

        @pl.when(i == 0)
        def _():
            dgp_ref[...] = jnp.zeros_like(dgp_ref)
            dgt_ref[...] = jnp.zeros_like(dgt_ref)
        yv, dx = y_ref[...], dx_ref[...]
        r = _rms(yv)
        t = yv * r
        dgp_ref[...] += jnp.sum(dx * gt_ref[...] * t, axis=0, keepdims=True)
        dgt_ref[...] += jnp.sum(dx * (t * gp_ref[...]), axis=0, keepdims=True)
        dt = dx * (gt_ref[...] * gp_ref[...])
        dy_ref[...] = (r * (dt - t * jnp.mean(dt * t, axis=-1, keepdims=True))).astype(dy_ref.dtype)

    return pl.pallas_call(
        body, name="post_bwd", grid=(dm.S // dm.ts,),
        in_specs=[_row_spec(dm.ts, dm.D)] * 2 + [_vec_spec(dm.D)] * 2 + [_ORDER] * len(after),
        out_specs=[_row_spec(dm.ts, dm.D), _vec_spec(dm.D), _vec_spec(dm.D)],
        out_shape=[jax.ShapeDtypeStruct((dm.S, dm.D), BF16)] + [jax.ShapeDtypeStruct((1, dm.D), F32)] * 2,
        compiler_params=_cp(1),
    )(dxo, y, gpost, gate, *after)


def _pre_bwd(dh, dxo, x, gain, scale, dm, after=()):
    def body(dh_ref, dxo_ref, x_ref, g_ref, sc_ref, *rest):
        dx_ref, dsh_ref, dsc_ref, dg_ref = rest[len(after):]
        i = pl.program_id(0)

        @pl.when(i == 0)
        def _():
            dsh_ref[...] = jnp.zeros_like(dsh_ref)
            dsc_ref[...] = jnp.zeros_like(dsc_ref)
            dg_ref[...] = jnp.zeros_like(dg_ref)
        xv, dh_ = x_ref[...], dh_ref[...]
        r = _rms(xv)
        nrm = xv * r
        one_sc = 1.0 + sc_ref[...]
        dsh_ref[...] += jnp.sum(dh_, axis=0, keepdims=True)
        dsc_ref[...] += jnp.sum(dh_ * (nrm * g_ref[...]), axis=0, keepdims=True)
        dg_ref[...] += jnp.sum(dh_ * nrm * one_sc, axis=0, keepdims=True)
        dn = dh_ * (g_ref[...] * one_sc)
        dx_ref[...] = dxo_ref[...] + r * (dn - nrm * jnp.mean(dn * nrm, axis=-1, keepdims=True))

    return pl.pallas_call(
        body, name="pre_bwd", grid=(dm.S // dm.ts,),
        in_specs=[_row_spec(dm.ts, dm.D)] * 3 + [_vec_spec(dm.D)] * 2 + [_ORDER] * len(after),
        out_specs=[_row_spec(dm.ts, dm.D)] + [_vec_spec(dm.D)] * 3,
        out_shape=[jax.ShapeDtypeStruct((dm.S, dm.D), F32)] + [jax.ShapeDtypeStruct((1, dm.D), F32)] * 3,
        compiler_params=_cp(1),
    )(dh, dxo, x, gain, scale, *after)


def _sigmoid(z):
    return 1.0 / (1.0 + jnp.exp(-z))


def _swiglu_fwd(gu, dm):
    def body(g_ref, u_ref, a_ref):
        g = g_ref[...]
        a_ref[...] = (g * _sigmoid(g) * u_ref[...]).astype(a_ref.dtype)

    return pl.pallas_call(
        body, name="swiglu_fwd", grid=(dm.S // dm.ts,),
        in_specs=[_row_spec(dm.ts, dm.F, 0), _row_spec(dm.ts, dm.F, 1)],
        out_specs=_row_spec(dm.ts, dm.F),
        out_shape=jax.ShapeDtypeStruct((dm.S, dm.F), BF16), compiler_params=_cp(1),
    )(gu, gu)


def _swiglu_bwd(da, gu, dm):
    def body(da_ref, g_ref, u_ref, d_ref):
        g, u, da_ = g_ref[...], u_ref[...], da_ref[...]
        sg = _sigmoid(g)
        d_ref[:, :dm.F] = (da_ * u * (sg * (1.0 + g * (1.0 - sg)))).astype(d_ref.dtype)
        d_ref[:, dm.F:] = (da_ * (g * sg)).astype(d_ref.dtype)

    return pl.pallas_call(
        body, name="swiglu_bwd", grid=(dm.S // dm.ts,),
        in_specs=[_row_spec(dm.ts, dm.F), _row_spec(dm.ts, dm.F, 0), _row_spec(dm.ts, dm.F, 1)],
        out_specs=_row_spec(dm.ts, 2 * dm.F),
        out_shape=jax.ShapeDtypeStruct((dm.S, 2 * dm.F), BF16), compiler_params=_cp(1),
    )(da, gu, gu)


def _tri(n, upper):
    r = lax.broadcasted_iota(jnp.int32, (n, n), 0)
    c = lax.broadcasted_iota(jnp.int32, (n, n), 1)
    return (c >= r if upper else r >= c).astype(F32)


def _gates_fwd(cf, bf, dm):
    ts = dm.ts
    fcol = 2 * dm.C // LANES

    def body(f_ref, b_ref, cum_ref, carry):
        i = pl.program_id(0)

        @pl.when(i == 0)
        def _():
            carry[...] = jnp.zeros_like(carry)
        z = f_ref[...] + b_ref[...]
        lf = jnp.minimum(z, 0.0) - jnp.log(1.0 + jnp.exp(-jnp.abs(z)))
        cs = jnp.dot(_tri(ts, False), lf, precision=lax.Precision.HIGHEST, preferred_element_type=F32) + carry[...]
        cum_ref[...] = cs
        carry[...] = cs[ts - 1:ts, :]

    return pl.pallas_call(
        body, name="gates_fwd", grid=(dm.S // ts,),
        in_specs=[pl.BlockSpec((ts, LANES), lambda i: (i, fcol)), _vec_spec(LANES)],
        out_specs=_row_spec(ts, LANES),
        out_shape=jax.ShapeDtypeStruct((dm.S, LANES), F32),
        scratch_shapes=[pltpu.VMEM((1, LANES), F32)], compiler_params=_cp(1),
    )(cf, bf)


def _gates_bwd(dc, cf, bf, dm):
    ts = dm.ts
    nb = dm.S // ts
    fcol = 2 * dm.C // LANES

    def body(dc_ref, f_ref, b_ref, df_ref, db_ref, carry):
        i = pl.program_id(0)

        @pl.when(i == 0)
        def _():
            carry[...] = jnp.zeros_like(carry)
            db_ref[...] = jnp.zeros_like(db_ref)
        dlf = jnp.dot(_tri(ts, True), dc_ref[...], precision=lax.Precision.HIGHEST, preferred_element_type=F32) + carry[...]
        carry[...] = dlf[0:1, :]
        dz = dlf * (1.0 - _sigmoid(f_ref[...] + b_ref[...]))
        df_ref[...] = dz.astype(df_ref.dtype)
        db_ref[...] += jnp.sum(dz, axis=0, keepdims=True)

    return pl.pallas_call(
        body, name="gates_bwd", grid=(nb,),
        in_specs=[pl.BlockSpec((ts, LANES), lambda i: (nb - 1 - i, 0)),
                  pl.BlockSpec((ts, LANES), lambda i: (nb - 1 - i, fcol)), _vec_spec(LANES)],
        out_specs=[pl.BlockSpec((ts, LANES), lambda i: (nb - 1 - i, 0)), _vec_spec(LANES)],
        out_shape=[jax.ShapeDtypeStruct((dm.S, LANES), BF16), jax.ShapeDtypeStruct((1, LANES), F32)],
        scratch_shapes=[pltpu.VMEM((1, LANES), F32)], compiler_params=_cp(1),
    )(dc, cf, bf)


def _dot_nt(a, b):
    return lax.dot_general(a, b, (((1,), (1,)), ((), ())), preferred_element_type=F32)


def _dot_tn(a, b):
    return lax.dot_general(a, b, (((0,), (0,)), ((), ())), preferred_element_type=F32)


_C0, _C1, _C2 = HEAD_DIM, HEAD_DIM + 3, HEAD_DIM + 6


def _split3(c):
    hi = c.astype(BF16).astype(F32)
    mid = (c - hi).astype(BF16).astype(F32)
    return hi, mid, c - hi - mid


def _put3(base, lane, start, pieces, sign=1.0):
    out = base
    for t, piece in enumerate(pieces):
        out = jnp.where(lane == start + t, sign * piece, out)
    return out


def _head_lanes(pair, odd):
    v = pair.astype(F32)
    return pltpu.roll(v, HEAD_DIM, axis=1) if odd else v


def _attn_prep(qkv, cum, dm):
    ts, A, H = dm.ts, dm.A, dm.H
    scale = HEAD_DIM ** -0.5

    def body(x_ref, c_ref, qa_ref, ka_ref, va_ref):
        lane = lax.broadcasted_iota(jnp.int32, (ts, LANES), 1)
        data = lane < HEAD_DIM
        for h in range(H):
            e, odd = h // 2, h % 2
            pieces = _split3(c_ref[:, h:h + 1])
            q = _head_lanes(x_ref[:, e * LANES:(e + 1) * LANES], odd) * scale
            k = _head_lanes(x_ref[:, A + e * LANES:A + (e + 1) * LANES], odd)
            v = _head_lanes(x_ref[:, 2 * A + e * LANES:2 * A + (e + 1) * LANES], odd)
            qa = jnp.where(data, q, jnp.where((lane >= _C1) & (lane < _C2), 1.0, 0.0))
            qa_ref[h] = _put3(qa, lane, _C0, pieces).astype(BF16)
            ka = jnp.where(data, k, jnp.where((lane < _C1) | ((lane >= _C2) & (lane < _C2 + 3)), 1.0, 0.0))
            ka_ref[h] = _put3(ka, lane, _C1, pieces, -1.0).astype(BF16)
            va_ref[h] = jnp.where(data, v, jnp.where(lane < _C1, 1.0, 0.0)).astype(BF16)

    spec = pl.BlockSpec((H, ts, LANES), lambda i: (0, i, 0))
    return pl.pallas_call(
        body, name="attn_prep", grid=(dm.S // ts,),
        in_specs=[_row_spec(ts, 3 * A), _row_spec(ts, LANES)], out_specs=[spec] * 3,
        out_shape=[jax.ShapeDtypeStruct((H, dm.S, LANES), BF16)] * 3, compiler_params=_cp(1),
    )(qkv, cum)


def _attn_fwd(qa, ka, va, dm):
    tq, A, H, S = dm.ts, dm.A, dm.H, dm.S
    nq = S // tq

    def body(qa_ref, ka_ref, va_ref, o_ref, lse_ref, top_scr, qb_scr, m_scr, acc_scr):
        i = pl.program_id(0)
        row = lax.broadcasted_iota(jnp.int32, (tq, tq), 0)
        col = lax.broadcasted_iota(jnp.int32, (tq, tq), 1)
        lane = lax.broadcasted_iota(jnp.int32, (tq, LANES), 1)

        def logits(q_ref, j, h, diagonal):
            s = _dot_nt(q_ref[h], ka_ref[h, pl.ds(pl.multiple_of(j * tq, tq), tq), :])
            return jnp.where(row >= col, s, -1e30) if diagonal else s

        def maxima(j, carry):
            for h in range(H):
                top_scr[h] = jnp.maximum(top_scr[h], logits(qa_ref, j, h, False))
            return carry

        for h in range(H):
            top_scr[h] = logits(qa_ref, i, h, True)
        lax.fori_loop(0, i, maxima, 0)
        for h in range(H):
            m = jnp.max(top_scr[h], axis=1, keepdims=True)
            m_scr[h] = m
            qb_scr[h] = _put3(qa_ref[h].astype(F32), lane, _C2, _split3(m), -1.0).astype(BF16)

        def weigh(j, carry):
            rows = pl.ds(pl.multiple_of(j * tq, tq), tq)
            for h in range(H):
                p = jnp.exp(logits(qb_scr, j, h, False)).astype(BF16)
                acc_scr[h] += jnp.dot(p, va_ref[h, rows, :], preferred_element_type=F32)
            return carry

        for h in range(H):
            p = jnp.exp(logits(qb_scr, i, h, True)).astype(BF16)
            acc_scr[h] = jnp.dot(p, va_ref[h, pl.ds(pl.multiple_of(i * tq, tq), tq), :], preferred_element_type=F32)
        lax.fori_loop(0, i, weigh, 0)
        lse_all = jnp.zeros((tq, LANES), F32)
        for h in range(H):
            acc = acc_scr[h]
            l = acc[:, _C0:_C0 + 1]
            o_ref[:, h * HEAD_DIM:(h + 1) * HEAD_DIM] = (acc[:, :HEAD_DIM] / l).astype(o_ref.dtype)
            lse_all = jnp.where(lane == h, m_scr[h] + jnp.log(l), lse_all)
        lse_ref[...] = lse_all

    full = pl.BlockSpec((H, S, LANES), lambda i: (0, 0, 0))
    return pl.pallas_call(
        body, name="attn_fwd", grid=(nq,),
        in_specs=[pl.BlockSpec((H, tq, LANES), lambda i: (0, i, 0)), full, full],
        out_specs=[pl.BlockSpec((tq, A), lambda i: (i, 0)), _row_spec(tq, LANES)],
        out_shape=[jax.ShapeDtypeStruct((S, A), BF16), jax.ShapeDtypeStruct((S, LANES), F32)],
        scratch_shapes=[pltpu.VMEM((H, tq, tq), F32), pltpu.VMEM((H, tq, LANES), BF16), pltpu.VMEM((H, tq, 1), F32),
                        pltpu.VMEM((H, tq, LANES), F32)],
        compiler_params=_cp(1),
    )(qa, ka, va)


def _attn_prep_bwd(qa, lse, dcat, o, dm):
    ts, A, H = dm.ts, dm.A, dm.H

    def body(qa_ref, lse_ref, do_ref, o_ref, qb_ref, doa_ref):
        lane = lax.broadcasted_iota(jnp.int32, (ts, LANES), 1)
        data = lane < HEAD_DIM
        for h in range(H):
            e, odd = h // 2, h % 2
            qb_ref[h] = _put3(qa_ref[h].astype(F32), lane, _C2, _split3(lse_ref[:, h:h + 1]), -1.0).astype(BF16)
            do_pair = do_ref[:, e * LANES:(e + 1) * LANES]
            prod = do_pair * o_ref[:, e * LANES:(e + 1) * LANES].astype(F32)
            mine = (lane >= HEAD_DIM) if odd else data
            delta = jnp.sum(jnp.where(mine, prod, 0.0), axis=1, keepdims=True)
            doa = jnp.where(data, _head_lanes(do_pair, odd), 0.0)
            doa_ref[h] = _put3(doa, lane, _C0, _split3(delta), -1.0).astype(BF16)

    spec = pl.BlockSpec((H, ts, LANES), lambda i: (0, i, 0))
    return pl.pallas_call(
        body, name="attn_prep_bwd", grid=(dm.S // ts,),
        in_specs=[spec, _row_spec(ts, LANES), _row_spec(ts, A, 0), _row_spec(ts, A)], out_specs=[spec] * 2,
        out_shape=[jax.ShapeDtypeStruct((H, dm.S, LANES), BF16)] * 2, compiler_params=_cp(1),
    )(qa, lse, dcat, o)


def _attn_bwd(qb, ka, va, doa, dm):
    tq, H, S = dm.ts, dm.H, dm.S
    nq = S // tq

    def body(ka_ref, va_ref, qb_ref, doa_ref, dq_ref, dk_ref, dv_ref):
        j = pl.program_id(0)

        @pl.when(j == 0)
        def _():
            dq_ref[...] = jnp.zeros(dq_ref.shape, F32)
        dk_ref[...] = jnp.zeros(dk_ref.shape, F32)
        dv_ref[...] = jnp.zeros(dv_ref.shape, F32)
        row = lax.broadcasted_iota(jnp.int32, (tq, tq), 0)
        col = lax.broadcasted_iota(jnp.int32, (tq, tq), 1)

        def block(i, masked):
            rows = pl.ds(pl.multiple_of(i * tq, tq), tq)
            for h in range(H):
                q, do_ = qb_ref[h, rows, :], doa_ref[h, rows, :]
                k, v = ka_ref[h], va_ref[h]
                s = _dot_nt(q, k)
                if masked:
                    s = jnp.where(row >= col, s, -1e30)
                p = jnp.exp(s)
                dsb = (p * _dot_nt(do_, v)).astype(BF16)
                dv_ref[h] += _dot_tn(p.astype(BF16), do_)
                dk_ref[h] += _dot_tn(dsb, q)
                dq_ref[h, rows, :] += jnp.dot(dsb, k, preferred_element_type=F32)

        block(j, True)

        def step(i, carry):
            block(i, False)
            return carry

        lax.fori_loop(j + 1, nq, step, 0)

    blk = pl.BlockSpec((H, tq, LANES), lambda j: (0, j, 0))
    full = pl.BlockSpec((H, S, LANES), lambda j: (0, 0, 0))
    return pl.pallas_call(
        body, name="attn_bwd", grid=(nq,),
        in_specs=[blk, blk, full, full], out_specs=[full, blk, blk],
        out_shape=[jax.ShapeDtypeStruct((H, S, LANES), F32)] * 3, compiler_params=_cp(1),
    )(ka, va, qb, doa)


def _attn_post(dqa, dka, dva, dm):
    ts, A, H = dm.ts, dm.A, dm.H
    scale = HEAD_DIM ** -0.5

    def body(dq_ref, dk_ref, dv_ref, o_ref, dc_ref):
        lane = lax.broadcasted_iota(jnp.int32, (ts, LANES), 1)
        data = lane < HEAD_DIM
        dc = jnp.zeros((ts, LANES), F32)
        for h in range(H):
            dc = jnp.where(lane == h, dq_ref[h][:, _C0:_C0 + 1] - dk_ref[h][:, _C1:_C1 + 1], dc)
        dc_ref[...] = dc
        for part, (ref, mul) in enumerate(((dq_ref, scale), (dk_ref, 1.0), (dv_ref, 1.0))):
            for e in range(H // 2):
                pair = jnp.where(data, ref[2 * e], pltpu.roll(ref[2 * e + 1], HEAD_DIM, axis=1))
                o_ref[:, part * A + e * LANES:part * A + (e + 1) * LANES] = (pair * mul).astype(o_ref.dtype)

    spec = pl.BlockSpec((H, ts, LANES), lambda i: (0, i, 0))
    return pl.pallas_call(
        body, name="attn_post", grid=(dm.S // ts,),
        in_specs=[spec] * 3, out_specs=[_row_spec(ts, 3 * A), _row_spec(ts, LANES)],
        out_shape=[jax.ShapeDtypeStruct((dm.S, 3 * A), BF16), jax.ShapeDtypeStruct((dm.S, LANES), F32)],
        compiler_params=_cp(1),
    )(dqa, dka, dva)


def _glu(cf_rows, c):
    return cf_rows[:, :c] * _sigmoid(cf_rows[:, c:2 * c])


def _conv_fwd(cf, cw, cb, lg, lb, dm):
    ts, C = dm.ts, dm.C
    per = ts // HALO

    def body(cf_ref, halo_ref, w_ref, cb_ref, lg_ref, lb_ref, u3_ref, u1_ref):
        i = pl.program_id(0)
        prev = jnp.where(i > 0, _glu(halo_ref[...], C), 0.0)
        win = jnp.concatenate([prev, _glu(cf_ref[...], C)], axis=0)
        u1 = jnp.zeros((ts, C), F32) + cb_ref[...]
        off = HALO - (CONV_K - 1)
        for k in range(CONV_K):
            u1 = u1 + w_ref[k:k + 1, :] * win[off + k:off + k + ts, :]
        u1_ref[...] = u1
        mu = jnp.mean(u1, axis=-1, keepdims=True)
        cen = u1 - mu
        rstd = lax.rsqrt(jnp.mean(cen * cen, axis=-1, keepdims=True) + EPS)
        u2 = cen * rstd * lg_ref[...] + lb_ref[...]
        u3_ref[...] = (u2 * _sigmoid(u2)).astype(u3_ref.dtype)

    return pl.pallas_call(
        body, name="conv_fwd", grid=(dm.S // ts,),
        in_specs=[pl.BlockSpec((ts, 2 * C), lambda i: (i, 0)),
                  pl.BlockSpec((HALO, 2 * C), lambda i: (jnp.maximum(i * per - 1, 0), 0)),
                  pl.BlockSpec((HALO, C), lambda i: (0, 0))] + [_vec_spec(C)] * 3,
        out_specs=[_row_spec(ts, C)] * 2,
        out_shape=[jax.ShapeDtypeStruct((dm.S, C), BF16), jax.ShapeDtypeStruct((dm.S, C), F32)],
        compiler_params=_cp(1),
    )(cf, cf, cw, cb, lg, lb)


def _conv_bwd(dcat, u1, cf, cw, lg, lb, dm):
    ts, C = dm.ts, dm.C
    per = ts // HALO
    nt = dm.S // ts
    last_halo = dm.S // HALO - 1

    def ln_bwd(du3, u1v, lg_v, lb_v):
        mu = jnp.mean(u1v, axis=-1, keepdims=True)
        cen = u1v - mu
        rstd = lax.rsqrt(jnp.mean(cen * cen, axis=-1, keepdims=True) + EPS)
        uhat = cen * rstd
        u2 = uhat * lg_v + lb_v
        sg = _sigmoid(u2)
        du2 = du3 * (sg * (1.0 + u2 * (1.0 - sg)))
        duh = du2 * lg_v
        du1 = rstd * (duh - jnp.mean(duh, axis=-1, keepdims=True) - uhat * jnp.mean(duh * uhat, axis=-1, keepdims=True))
        return du1, du2, uhat

    def body(d_ref, dn_ref, u1_ref, u1n_ref, cf_ref, halo_ref, w_ref, lg_ref, lb_ref,
             dcf_ref, dw_ref, dcb_ref, dlg_ref, dlb_ref):
        i = pl.program_id(0)

        @pl.when(i == 0)
        def _():
            dw_ref[...] = jnp.zeros_like(dw_ref)
            dcb_ref[...] = jnp.zeros_like(dcb_ref)
            dlg_ref[...] = jnp.zeros_like(dlg_ref)
            dlb_ref[...] = jnp.zeros_like(dlb_ref)
        lg_v, lb_v = lg_ref[...], lb_ref[...]
        du1, du2, uhat = ln_bwd(d_ref[...], u1_ref[...], lg_v, lb_v)
        du1n, _, _ = ln_bwd(dn_ref[...], u1n_ref[...], lg_v, lb_v)
        du1n = jnp.where(i < nt - 1, du1n, 0.0)
        dlg_ref[...] += jnp.sum(du2 * uhat, axis=0, keepdims=True)
        dlb_ref[...] += jnp.sum(du2, axis=0, keepdims=True)
        dcb_ref[...] += jnp.sum(du1, axis=0, keepdims=True)
        dwin = jnp.concatenate([du1, du1n], axis=0)
        cfv = cf_ref[...]
        cv, sg = cfv[:, :C], _sigmoid(cfv[:, C:2 * C])
        prev = jnp.where(i > 0, _glu(halo_ref[...], C), 0.0)
        uwin = jnp.concatenate([prev, cv * sg], axis=0)
        du0 = jnp.zeros((ts, C), F32)
        off = HALO - (CONV_K - 1)
        for k in range(CONV_K):
            back = CONV_K - 1 - k
            du0 = du0 + w_ref[k:k + 1, :] * dwin[back:back + ts, :]
            dw_ref[k:k + 1, :] += jnp.sum(du1 * uwin[off + k:off + k + ts, :], axis=0, keepdims=True)
        dcf_ref[:, :C] = (du0 * sg).astype(dcf_ref.dtype)
        dcf_ref[:, C:] = (du0 * cv * sg * (1.0 - sg)).astype(dcf_ref.dtype)

    ucol = dm.A // C
    return pl.pallas_call(
        body, name="conv_bwd", grid=(nt,),
        in_specs=[pl.BlockSpec((ts, C), lambda i: (i, ucol)),
                  pl.BlockSpec((HALO, C), lambda i: (jnp.minimum((i + 1) * per, last_halo), ucol)),
                  pl.BlockSpec((ts, C), lambda i: (i, 0)),
                  pl.BlockSpec((HALO, C), lambda i: (jnp.minimum((i + 1) * per, last_halo), 0)),
                  pl.BlockSpec((ts, 2 * C), lambda i: (i, 0)),
                  pl.BlockSpec((HALO, 2 * C), lambda i: (jnp.maximum(i * per - 1, 0), 0)),
                  pl.BlockSpec((HALO, C), lambda i: (0, 0)), _vec_spec(C), _vec_spec(C)],
        out_specs=[_row_spec(ts, 2 * C), pl.BlockSpec((HALO, C), lambda i: (0, 0))] + [_vec_spec(C)] * 3,
        out_shape=[jax.ShapeDtypeStruct((dm.S, 2 * C), BF16), jax.ShapeDtypeStruct((HALO, C), F32)]
        + [jax.ShapeDtypeStruct((1, C), F32)] * 3,
        compiler_params=_cp(1),
    )(dcat, dcat, u1, u1, cf, cf, cw, lg, lb)


def _ada_fwd(c16, ada_w, ada_b_cols, dm):
    L, D, n = ada_w.shape
    tn = _tile(n, 512)

    def body(c_ref, w_ref, b_ref, o_ref, a_ref):
        cv = c_ref[...]
        act = (cv * _sigmoid(cv)).astype(BF16)
        a_ref[...] = act
        o_ref[...] = jnp.dot(act, w_ref[...].astype(BF16), preferred_element_type=F32) + b_ref[...]

    return pl.pallas_call(
        body, name="ada_fwd", grid=(L, n // tn),
        in_specs=[pl.BlockSpec((16, D), lambda l, j: (0, 0)), pl.BlockSpec((None, D, tn), lambda l, j: (l, 0, j)),
                  pl.BlockSpec((None, 1, tn), lambda l, j: (l, 0, j))],
        out_specs=[pl.BlockSpec((None, 16, tn), lambda l, j: (l, 0, j)), pl.BlockSpec((16, D), lambda l, j: (0, 0))],
        out_shape=[jax.ShapeDtypeStruct((L, 16, n), F32), jax.ShapeDtypeStruct((16, D), BF16)],
        compiler_params=_cp(2),
    )(c16, ada_w, ada_b_cols)


def _ada_bwd(act16, dmod16):
    L, _, n = dmod16.shape
    D = act16.shape[1]
    tm = min(TN_TM, D)

    def body(a_ref, d_ref, o_ref):
        o_ref[...] = _dot_tn(a_ref[...], d_ref[...])

    return pl.pallas_call(
        body, name="ada_bwd", grid=(L, D // tm),
        in_specs=[pl.BlockSpec((16, tm), lambda l, i: (0, i)), pl.BlockSpec((None, 16, n), lambda l, i: (l, 0, 0))],
        out_specs=pl.BlockSpec((None, tm, n), lambda l, i: (l, i, 0)),
        out_shape=jax.ShapeDtypeStruct((L, D, n), F32), compiler_params=_cp(2),
    )(act16, dmod16)


def _sum_devices(g8, after=()):
    _, R, _ = g8.shape
    tr = _rows_tile(R)

    def body(g_ref, *rest):
        o_ref = rest[len(after)]
        acc = g_ref[0]
        for d in range(1, N_DEV):
            acc = acc + g_ref[d]
        o_ref[...] = acc

    return pl.pallas_call(
        body, name="sum_devices", grid=(R // tr,),
        in_specs=[pl.BlockSpec((N_DEV, tr, LANES), lambda i: (0, i, 0))] + [_ORDER] * len(after),
        out_specs=pl.BlockSpec((tr, LANES), lambda i: (i, 0)),
        out_shape=jax.ShapeDtypeStruct((R, LANES), F32), compiler_params=_cp(1),
    )(g8, *after)


def _rows_tile(r, cap=512):
    for t in (512, 256, 128, 64, 32, 16, 8):
        if t <= cap and r % t == 0:
            return t
    return r


def _adam_math(w, g, m, v):
    m = ADAM_B1 * m + (1.0 - ADAM_B1) * g
    v = ADAM_B2 * v + (1.0 - ADAM_B2) * (g * g)
    m_hat = m / (1.0 - ADAM_B1 ** ADAM_STEP)
    v_hat = v / (1.0 - ADAM_B2 ** ADAM_STEP)
    delta = -ADAM_LR * (m_hat / (jnp.sqrt(v_hat) + ADAM_EPS) + ADAM_WD * w)
    return delta, m, v


def _adamw_dense(w, m, v, g, name):
    R, Cc = w.shape
    tr = _rows_tile(R, 128)

    def body(w_ref, m_ref, v_ref, g_ref, d_ref, mo_ref, vo_ref):
        d, mn, vn = _adam_math(w_ref[...], g_ref[...], m_ref[...], v_ref[...])
        d_ref[...] = d
        mo_ref[...] = mn
        vo_ref[...] = vn

    spec = pl.BlockSpec((tr, Cc), lambda i: (i, 0))
    return pl.pallas_call(
        body, name=name, grid=(R // tr,), in_specs=[spec] * 4, out_specs=[spec] * 3,
        out_shape=[jax.ShapeDtypeStruct((R, Cc), F32)] * 3, compiler_params=_cp(1),
    )(w, m, v, g)


def _adamw_shard(w, m, v, near, far, layer, prev, name):
    L, r, cc = w.shape
    tr = _rows_tile(r, 128)

    def body(w_ref, m_ref, v_ref, n_ref, f_ref, *rest):
        g_ref, d_ref, mo_ref, vo_ref = rest[-4:]
        g = n_ref[0].astype(F32) + f_ref[0].astype(F32)
        for k in range(1, N_CHIPS):
            g = g + (n_ref[k].astype(F32) + f_ref[k].astype(F32))
        d, mn, vn = _adam_math(w_ref[...], g, m_ref[...], v_ref[...])
        g_ref[...] = g
        d_ref[...] = d
        mo_ref[...] = mn
        vo_ref[...] = vn

    wspec = pl.BlockSpec((None, tr, cc), lambda i: (layer, i, 0))
    sspec = pl.BlockSpec((N_CHIPS, tr, cc), lambda i: (0, i, 0))
    n_prev = 0 if prev is None else 4
    return pl.pallas_call(
        body, name=name, grid=(r // tr,),
        in_specs=[wspec] * 3 + [sspec] * 2 + [pl.BlockSpec(memory_space=pl.ANY)] * n_prev,
        out_specs=[wspec] * 4,
        out_shape=[jax.ShapeDtypeStruct((L, r, cc), F32)] * 4,
        input_output_aliases={5 + t: t for t in range(n_prev)},
        compiler_params=_cp(1),
    )(w, m, v, near, far, *(prev or ()))


def _pack(vs):
    flat = jnp.concatenate([v.reshape(-1).astype(F32) for v in vs])
    pad = (-flat.shape[0]) % (64 * LANES)
    return jnp.pad(flat, (0, pad)).reshape(-1, LANES)


def _unpack(packed, shapes):
    flat = packed.reshape(-1)
    out, pos = [], 0
    for s in shapes:
        n = 1
        for d in s:
            n *= d
        out.append(flat[pos:pos + n].reshape(s))
        pos += n
    return out


def kernel(x, c, w_in, b_f, conv_w, conv_b, conv_ln_g, conv_ln_b, w_o, w_ffn_in, w_ffn_out, mix_pre_g, mix_post_g, ffn_pre_g, ffn_post_g, ada_w, ada_b, loss_target, m_w_in, m_b_f, m_conv_w, m_conv_b, m_conv_ln_g, m_conv_ln_b, m_w_o, m_w_ffn_in, m_w_ffn_out, m_mix_pre_g, m_mix_post_g, m_ffn_pre_g, m_ffn_post_g, m_ada_w, m_ada_b, v_w_in, v_b_f, v_conv_w, v_conv_b, v_conv_ln_g, v_conv_ln_b, v_w_o, v_w_ffn_in, v_w_ffn_out, v_mix_pre_g, v_mix_post_g, v_ffn_pre_g, v_ffn_post_g, v_ada_w, v_ada_b):
    params = dict(w_in=w_in, b_f=b_f, conv_w=conv_w, conv_b=conv_b, conv_ln_g=conv_ln_g, conv_ln_b=conv_ln_b, w_o=w_o,
                  w_ffn_in=w_ffn_in, w_ffn_out=w_ffn_out, mix_pre_g=mix_pre_g, mix_post_g=mix_post_g,
                  ffn_pre_g=ffn_pre_g, ffn_post_g=ffn_post_g, ada_w=ada_w, ada_b=ada_b)
    mom = dict(w_in=m_w_in, b_f=m_b_f, conv_w=m_conv_w, conv_b=m_conv_b, conv_ln_g=m_conv_ln_g, conv_ln_b=m_conv_ln_b,
               w_o=m_w_o, w_ffn_in=m_w_ffn_in, w_ffn_out=m_w_ffn_out, mix_pre_g=m_mix_pre_g, mix_post_g=m_mix_post_g,
               ffn_pre_g=m_ffn_pre_g, ffn_post_g=m_ffn_post_g, ada_w=m_ada_w, ada_b=m_ada_b)
    var = dict(w_in=v_w_in, b_f=v_b_f, conv_w=v_conv_w, conv_b=v_conv_b, conv_ln_g=v_conv_ln_g, conv_ln_b=v_conv_ln_b,
               w_o=v_w_o, w_ffn_in=v_w_ffn_in, w_ffn_out=v_w_ffn_out, mix_pre_g=v_mix_pre_g, mix_post_g=v_mix_post_g,
               ffn_pre_g=v_ffn_pre_g, ffn_post_g=v_ffn_post_g, ada_w=v_ada_w, ada_b=v_ada_b)

    S, D = x.shape[1], x.shape[2]
    L = w_in.shape[0]
    A = D // 2
    C = D - A
    H = A // HEAD_DIM
    F = w_ffn_out.shape[1] * N_CHIPS
    d_in = w_in.shape[2] * N_CHIPS
    NP = 3 * A + 2 * C + LANES
    dm = Dims(S=S, D=D, A=A, C=C, H=H, F=F, L=L, NP=NP, ts=min(ROW_TILE, S))
    assert H <= 8 and A == C and d_in == 3 * A + H + 2 * C

    ix, iy, ic = _place()
    chip = 2 * ix + iy
    dev = 4 * ix + 2 * iy + ic
    x2 = x.reshape(S, D)
    tgt = loss_target.reshape(S, D)

    def gather_start(l, after):
        first = _exchange_start([w_in[l].astype(BF16)], "gather", after, f"gather_a_start_{l}")
        rest = _exchange_start([w_o[l].astype(BF16), w_ffn_in[l].astype(BF16), w_ffn_out[l].astype(BF16)], "gather",
                               [first["token"]], f"gather_b_start_{l}")
        return first, rest

    def gather_wait(st, after, name):
        return _exchange_wait(st, after, name)[1]

    c_all = _all_gather_devices(c.reshape(D // LANES, LANES), "gather_c").reshape(N_DEV, D)
    c16 = jnp.pad(c_all, ((0, 16 - N_DEV), (0, 0)))
    n_ada = ada_w.shape[2]
    ada_b_cols = lax.dynamic_slice_in_dim(ada_b, chip * n_ada, n_ada, axis=1).reshape(L, 1, n_ada)
    mod_cols, act16 = _ada_fwd(c16, ada_w, ada_b_cols, dm)
    conv_w_all, mod_all = _all_gather_chips([conv_w.reshape(L * CONV_K, -1), mod_cols.reshape(L * 16, n_ada)], "gather_mod")
    cwc = conv_w.shape[2]
    conv_w_full = conv_w_all.reshape(N_CHIPS, L, CONV_K, cwc).transpose(1, 2, 0, 3).reshape(L, CONV_K, C)
    conv_w_full = jnp.pad(conv_w_full, ((0, 0), (0, HALO - CONV_K), (0, 0)))
    mod_all = mod_all.reshape(N_CHIPS, L, 16, n_ada)
    mod_me = lax.dynamic_index_in_dim(mod_all, dev, axis=2, keepdims=False)
    mod_me = mod_me.transpose(1, 0, 2).reshape(L, N_MOD, 1, D)

    gather = [None] * L
    gather[0] = gather_start(0, [mod_all])

    def projection_of(g_in):
        w_nat = g_in.transpose(1, 0, 2).reshape(D, d_in)
        return jnp.concatenate([w_nat[:, :3 * A], w_nat[:, 3 * A + H:], w_nat[:, 3 * A:3 * A + H],
                                jnp.zeros((D, LANES - H), BF16)], axis=1)

    gathered = [None] * L
    vec = lambda p, l: p[l].reshape(1, -1)
    bf_pad = jnp.pad(b_f, ((0, 0), (0, LANES - H)))

    saved = []
    xin = x2
    h = _pre_norm(xin, vec(mix_pre_g, 0), mod_me[0, 1], mod_me[0, 0], dm)
    dx = loss_part = None
    for l in range(L):
        (g_in,) = gather_wait(gather[l][0], [h], f"gather_a_wait_{l}")
        w_p = projection_of(g_in)
        order = [gather[l][1]["token"]]
        if l + 1 < L:
            gather[l + 1] = gather_start(l + 1, [g_in, gather[l][1]["token"]])
            order.append(gather[l + 1][1]["token"])
        qkv = _mm_nn(h, w_p, BF16, "mm_qkv", 0, 3 * A, after=order)
        cf = _mm_nn(h, w_p, F32, "mm_cf", 3 * A, 2 * C + LANES)
        cum = _gates_fwd(cf, vec(bf_pad, l), dm)
        qa, ka, va = _attn_prep(qkv, cum, dm)
        o, lse = _attn_fwd(qa, ka, va, dm)
        u3, u1 = _conv_fwd(cf, conv_w_full[l], vec(conv_b, l), vec(conv_ln_g, l), vec(conv_ln_b, l), dm)
        cat = jnp.concatenate([o, u3], axis=1)
        g_o, wfi, g_fo = gather_wait(gather[l][1], [cat], f"gather_b_wait_{l}")
        wo, wfo = g_o.reshape(D, D), g_fo.reshape(F, D)
        gathered[l] = (w_p, wo, wfi, wfo)
        y = _mm_nn(cat, wo, F32, "mm_o")
        x1, h2 = _res_norm(xin, y, vec(mix_post_g, l), mod_me[l, 2], vec(ffn_pre_g, l), mod_me[l, 4], mod_me[l, 3], dm)
        gu = _mm_nn_blocked(h2, wfi, F32, "mm_ffn_in")
        a = _swiglu_fwd(gu, dm)
        y2 = _mm_nn(a, wfo, F32, "mm_ffn_out")
        saved.append(dict(xin=xin, h=h, qa=qa, ka=ka, va=va, cf=cf, o=o, lse=lse, u1=u1, cat=cat, y=y,
                          x1=x1, h2=h2, gu=gu, a=a, y2=y2))
        if l + 1 < L:
            xin, h = _res_norm(x1, y2, vec(ffn_post_g, l), mod_me[l, 5], vec(mix_pre_g, l + 1),
                               mod_me[l + 1, 1], mod_me[l + 1, 0], dm)
        else:
            dx, loss_part = _res_loss(x1, y2, vec(ffn_post_g, l), mod_me[l, 5], tgt, dm)
    loss = lax.psum(loss_part[0, 0], ("x", "y", "c"))

    small = [None] * L
    big = [None] * L
    order = []
    for l in reversed(range(L)):
        w_p, wo, wfi, wfo = gathered[l]
        sv = saved[l]
        dy2, d_gfpost, d_g2 = _post_bwd(dx, sv["y2"], vec(ffn_post_g, l), mod_me[l, 5], dm, after=order)
        da = _mm_nt(dy2, wfo, F32, "mm_da")
        dgu = _swiglu_bwd(da, sv["gu"], dm)
        g_wfo = _mm_tn(sv["a"], dy2, BF16, "mm_dwfo")
        g_wfi = _mm_tn(sv["h2"], dgu, BF16, "mm_dwfi", blocked=True)
        dh2 = _mm_nt_blocked(dgu, wfi, F32, "mm_dh2")
        scatter_ffn = _exchange_start([g_wfi, g_wfo.reshape(N_CHIPS, F // N_CHIPS, D)], "scatter", [], f"scatter_b_start_{l}")
        dx1, d_sh2, d_sc2, d_gfpre = _pre_bwd(dh2, dx, sv["x1"], vec(ffn_pre_g, l), mod_me[l, 4], dm,
                                              after=[scatter_ffn["token"]])
        dy, d_gpost, d_g1 = _post_bwd(dx1, sv["y"], vec(mix_post_g, l), mod_me[l, 2], dm)
        dcat = _mm_nt(dy, wo, F32, "mm_dcat")
        g_wo = _mm_tn(sv["cat"], dy, BF16, "mm_dwo")
        dcfc, d_cw, d_cb, d_lg, d_lb = _conv_bwd(dcat, sv["u1"], sv["cf"], conv_w_full[l], vec(conv_ln_g, l),
                                                 vec(conv_ln_b, l), dm)
        qb, doa = _attn_prep_bwd(sv["qa"], sv["lse"], dcat, sv["o"], dm)
        dqkv, dcum = _attn_post(*_attn_bwd(qb, sv["ka"], sv["va"], doa, dm), dm)
        dfl, d_bf = _gates_bwd(dcum, sv["cf"], vec(bf_pad, l), dm)
        dproj = jnp.concatenate([dqkv, dcfc, dfl], axis=1)
        dh = _mm_nt(dproj, w_p, F32, "mm_dh")
        g_wp = _mm_tn(sv["h"], dproj, BF16, "mm_dwp")
        dx, d_sh1, d_sc1, d_gpre = _pre_bwd(dh, dx1, sv["xin"], vec(mix_pre_g, l), mod_me[l, 1], dm)
        g_nat = jnp.concatenate([g_wp[:, :3 * A], g_wp[:, 3 * A + 2 * C:3 * A + 2 * C + H], g_wp[:, 3 * A:3 * A + 2 * C]], axis=1)
        g_win = g_nat.reshape(D, N_CHIPS, d_in // N_CHIPS).transpose(1, 0, 2)
        g_mix = [g_win, g_wo.reshape(N_CHIPS, D // N_CHIPS, D)]
        if l > 0:
            scatter_mix = _exchange_start(g_mix, "scatter", [], f"scatter_a_start_{l}")
            order = [scatter_mix["token"]]
            big[l] = (scatter_mix, scatter_ffn)
        small[l] = dict(b_f=d_bf[0, :H], conv_b=d_cb[0], conv_ln_g=d_lg[0], conv_ln_b=d_lb[0], mix_pre_g=d_gpre[0],
                        mix_post_g=d_gpost[0], ffn_pre_g=d_gfpre[0], ffn_post_g=d_gfpost[0],
                        dmod=jnp.concatenate([d_sh1, d_sc1, d_g1, d_sh2, d_sc2, d_g2], axis=1)[0],
                        conv_w=d_cw[:CONV_K])
    grad_x = dx.reshape(1, S, D)

    keys_small = ["b_f", "conv_b", "conv_ln_g", "conv_ln_b", "mix_pre_g", "mix_post_g", "ffn_pre_g", "ffn_post_g",
                  "dmod", "conv_w"]
    stacked = [jnp.stack([small[l][k] for l in range(L)]) for k in keys_small]
    shapes = [s.shape for s in stacked]
    pack = _pack(stacked)
    pack8 = _all_gather_devices(pack, "gather_small")
    big[0] = (_exchange_start(g_mix, "scatter", [pack8], "scatter_a_start_0"), scatter_ffn)
    summed = dict(zip(keys_small, _unpack(_sum_devices(pack8, after=[big[0][0]["token"]]), shapes)))
    dmod_all = jnp.stack([_unpack(pack8[d], shapes)[keys_small.index("dmod")] for d in range(N_DEV)])
    grads = {k: summed[k] for k in keys_small[:8]}
    grads["ada_b"] = summed["dmod"]
    grads["conv_w"] = lax.dynamic_slice_in_dim(summed["conv_w"], chip * cwc, cwc, axis=2)

    dmod_cols = lax.dynamic_slice_in_dim(dmod_all.reshape(N_DEV, L, N_CHIPS, n_ada), chip, 1, axis=2)
    dmod16 = jnp.pad(dmod_cols.reshape(N_DEV, L, n_ada).transpose(1, 0, 2), ((0, 0), (0, 16 - N_DEV), (0, 0))).astype(BF16)
    grads["ada_w"] = _ada_bwd(act16, dmod16)

    d_aw, m_aw, v_aw = _adamw_dense(ada_w.reshape(L * D, n_ada), m_ada_w.reshape(L * D, n_ada),
                                    v_ada_w.reshape(L * D, n_ada), grads["ada_w"].reshape(L * D, n_ada), "adamw_ada_w")
    names_small = ["b_f", "conv_w", "conv_b", "conv_ln_g", "conv_ln_b", "mix_pre_g", "mix_post_g", "ffn_pre_g",
                   "ffn_post_g", "ada_b"]
    shapes_small = [params[n].shape for n in names_small]
    d_s, m_s, v_s = _adamw_dense(_pack([params[n] for n in names_small]), _pack([mom[n] for n in names_small]),
                                 _pack([var[n] for n in names_small]), _pack([grads[n] for n in names_small]),
                                 "adamw_small")
    delta_w = dict(zip(names_small, _unpack(d_s, shapes_small)))
    new_m = dict(zip(names_small, _unpack(m_s, shapes_small)))
    new_v = dict(zip(names_small, _unpack(v_s, shapes_small)))
    delta_w["ada_w"], new_m["ada_w"], new_v["ada_w"] = (t.reshape(L, D, n_ada) for t in (d_aw, m_aw, v_aw))

    names_big = ["w_in", "w_o", "w_ffn_in", "w_ffn_out"]
    res_big = {n: None for n in names_big}
    forward = [None] * L

    def update(l, after):
        near, far = _exchange_wait(forward[l], after, f"forward_wait_{l}")
        for t, n in enumerate(names_big):
            res_big[n] = _adamw_shard(params[n], mom[n], var[n], near[t], far[t], l, res_big[n], f"adamw_{n}_{l}")
        return [res_big[names_big[-1]][1]]

    done = [dx]
    for l in reversed(range(L)):
        after = done + [d_s, d_aw] if l == 0 else done
        near = []
        for st, nm in zip(big[l], ("a", "b")):
            near += _exchange_wait(st, after, f"scatter_{nm}_wait_{l}")[1]
        forward[l] = _exchange_start(near, "sibling", [], f"forward_start_{l}")
        done = update(l + 1, [forward[l]["token"]]) if l + 1 < L else [forward[l]["token"]]
    update(0, [])
    for n in names_big:
        grads[n], delta_w[n], new_m[n], new_v[n] = res_big[n]

    return (loss, grad_x, *[grads[n] for n in WEIGHTS], *[delta_w[n] for n in WEIGHTS],
            *[new_m[n] for n in WEIGHTS], *[new_v[n] for n in WEIGHTS])
```

```python
import collections
import functools

import jax
import jax.numpy as jnp
from jax import lax
from jax.experimental import pallas as pl
from jax.experimental.pallas import tpu as pltpu

F32 = jnp.float32
BF16 = jnp.bfloat16
MESH = pl.DeviceIdType.MESH

HEAD_DIM = 64
CONV_K = 31
N_MOD = 6
EPS = 1e-6
N_CHIPS = 4
N_DEV = 8
LANES = 128
HALO = 32
ROW_TILE = 256
MM_TM = 512
MM_TN_MAX = 1408
TN_TM = 256
TN_TM_MAX = 384
VMEM_LIMIT = 56 * 1024 * 1024

ADAM_LR = 0.001
ADAM_B1 = 0.9
ADAM_B2 = 0.999
ADAM_EPS = 1e-08
ADAM_WD = 0.01
ADAM_STEP = 10

WEIGHTS = ['w_in', 'b_f', 'conv_w', 'conv_b', 'conv_ln_g', 'conv_ln_b', 'w_o', 'w_ffn_in', 'w_ffn_out',
           'mix_pre_g', 'mix_post_g', 'ffn_pre_g', 'ffn_post_g', 'ada_w', 'ada_b']

Dims = collections.namedtuple("Dims", "S D A C H F L NP ts")


def _cp(n_grid=0):
    if n_grid:
        return pltpu.CompilerParams(dimension_semantics=("arbitrary",) * n_grid, vmem_limit_bytes=VMEM_LIMIT)
    return pltpu.CompilerParams(vmem_limit_bytes=VMEM_LIMIT)


def _tile(n, cap, also=None):
    best = None
    t = LANES
    while t <= min(n, cap):
        if n % t == 0 and (also is None or also % t == 0):
            best = t
        t += LANES
    assert best is not None, (n, cap, also)
    return best


def _bf(v):
    return v if v.dtype == BF16 else v.astype(BF16)


def _place():
    return lax.axis_index("x"), lax.axis_index("y"), lax.axis_index("c")


def _flip(v, d):
    return 1 - v if d else v


def _all_gather_devices(a, name, after=()):
    def body(a_ref, *rest):
        o_ref, send, recv, lsem = rest[len(after):]
        x, y, c = _place()
        me = 4 * x + 2 * y + c
        local = pltpu.make_async_copy(a_ref, o_ref.at[me], lsem)
        local.start()
        copies = []
        for k in range(1, N_DEV):
            peer = (_flip(x, (k >> 2) & 1), _flip(y, (k >> 1) & 1), _flip(c, k & 1))
            cp = pltpu.make_async_remote_copy(src_ref=a_ref, dst_ref=o_ref.at[me], send_sem=send.at[k - 1],
                                              recv_sem=recv.at[k - 1], device_id=peer, device_id_type=MESH)
            cp.start()
            copies.append(cp)
        for cp in copies:
            cp.wait()
        local.wait()

    return pl.pallas_call(
        body, name=name,
        out_shape=jax.ShapeDtypeStruct((N_DEV,) + a.shape, a.dtype),
        in_specs=[pl.BlockSpec(memory_space=pl.ANY)] * (1 + len(after)),
        out_specs=pl.BlockSpec(memory_space=pl.ANY),
        scratch_shapes=[pltpu.SemaphoreType.DMA((N_DEV - 1,)), pltpu.SemaphoreType.DMA((N_DEV - 1,)),
                        pltpu.SemaphoreType.DMA],
    )(a, *after)


def _all_gather_chips(arrays, name):
    n = len(arrays)

    def body(*refs):
        a_refs, o_refs = refs[:n], refs[n:2 * n]
        send, recv, lsem = refs[2 * n:]
        x, y, c = _place()
        me = 2 * x + y
        copies = []
        for i in range(n):
            local = pltpu.make_async_copy(a_refs[i], o_refs[i].at[me], lsem.at[i])
            local.start()
            copies.append(local)
            for k in range(1, N_CHIPS):
                peer = (_flip(x, (k >> 1) & 1), _flip(y, k & 1), c)
                cp = pltpu.make_async_remote_copy(src_ref=a_refs[i], dst_ref=o_refs[i].at[me],
                                                  send_sem=send.at[i, k - 1], recv_sem=recv.at[i, k - 1],
                                                  device_id=peer, device_id_type=MESH)
                cp.start()
                copies.append(cp)
        for cp in copies:
            cp.wait()

    return pl.pallas_call(
        body, name=name,
        out_shape=[jax.ShapeDtypeStruct((N_CHIPS,) + a.shape, a.dtype) for a in arrays],
        in_specs=[pl.BlockSpec(memory_space=pl.ANY)] * n,
        out_specs=[pl.BlockSpec(memory_space=pl.ANY)] * n,
        scratch_shapes=[pltpu.SemaphoreType.DMA((n, N_CHIPS - 1)), pltpu.SemaphoreType.DMA((n, N_CHIPS - 1)),
                        pltpu.SemaphoreType.DMA((n,))],
    )(*arrays)


_HBM = pl.BlockSpec(memory_space=pltpu.HBM)
_SEM = pl.BlockSpec(memory_space=pltpu.SEMAPHORE)
_EFFECT = pltpu.SideEffectType.DATAFLOW_SIDE_EFFECTING


def _n_copies(mode):
    return 1 if mode == "sibling" else N_CHIPS - 1


def _chip_copies(srcs, lands, send, recv, mode):
    x, y, c = _place()
    me = 2 * x + y
    copies = []
    for i in range(len(srcs)):
        if mode == "sibling":
            copies.append(pltpu.make_async_remote_copy(src_ref=srcs[i], dst_ref=lands[i], send_sem=send.at[i],
                                                       recv_sem=recv.at[i], device_id=(x, y, 1 - c), device_id_type=MESH))
            continue
        for k in range(1, N_CHIPS):
            px, py = _flip(x, (k >> 1) & 1), _flip(y, k & 1)
            src = srcs[i].at[2 * px + py] if mode == "scatter" else srcs[i]
            s = i * (N_CHIPS - 1) + k - 1
            copies.append(pltpu.make_async_remote_copy(src_ref=src, dst_ref=lands[i].at[me], send_sem=send.at[s],
                                                       recv_sem=recv.at[s], device_id=(px, py, c), device_id_type=MESH))
    return copies


def _own_copies(srcs, lands, own, mode):
    x, y, _ = _place()
    me = 2 * x + y
    return [pltpu.make_async_copy(srcs[i].at[me] if mode == "scatter" else srcs[i], lands[i].at[me], own.at[i])
            for i in range(len(srcs))]


_ORDER = pl.BlockSpec(memory_space=pl.ANY)


def _exchange_start(arrays, mode, after, name):
    n = len(arrays)
    n_sems = 2 if mode == "sibling" else 3

    def body(*refs):
        srcs, lands = refs[:n], refs[n:2 * n]
        sems = refs[2 * n + len(after):2 * n + len(after) + n_sems]
        token = refs[-1]
        for cp in _chip_copies(srcs, lands, sems[0], sems[1], mode):
            cp.start()
        if mode != "sibling":
            for cp in _own_copies(srcs, lands, sems[2], mode):
                cp.start()
        token[...] = jnp.zeros_like(token)

    land_shapes = [(N_CHIPS,) + a.shape if mode == "gather" else a.shape for a in arrays]
    n_sem = n * _n_copies(mode)
    sem_shapes = [pltpu.SemaphoreType.DMA((n_sem,)), pltpu.SemaphoreType.DMA((n_sem,)), pltpu.SemaphoreType.DMA((n,))]
    outs = pl.pallas_call(
        body, name=name,
        out_shape=(*sem_shapes[:n_sems],
                   *[pltpu.HBM(a.shape, a.dtype) for a in arrays],
                   *[pltpu.HBM(s, a.dtype) for s, a in zip(land_shapes, arrays)],
                   jax.ShapeDtypeStruct((8, LANES), F32)),
        in_specs=[_HBM] * (2 * n) + [_ORDER] * len(after),
        out_specs=(*[_SEM] * n_sems, *[_HBM] * (2 * n), pl.BlockSpec(memory_space=pltpu.VMEM)),
        input_output_aliases={i: n_sems + i for i in range(2 * n)},
        compiler_params=pltpu.CompilerParams(has_side_effects=_EFFECT),
    )(*[pltpu.with_memory_space_constraint(a, pltpu.HBM) for a in arrays],
      *[pltpu.with_memory_space_constraint(lax.empty(s, a.dtype), pltpu.HBM) for s, a in zip(land_shapes, arrays)],
      *after)
    return dict(sems=outs[:n_sems], srcs=outs[n_sems:n_sems + n], lands=outs[n_sems + n:n_sems + 2 * n], token=outs[-1],
                mode=mode)


def _exchange_wait(st, after, name):
    n = len(st["srcs"])
    mode = st["mode"]
    n_sems = len(st["sems"])

    def body(*refs):
        srcs, lands = refs[:n], refs[n:2 * n]
        sems = refs[2 * n:2 * n + n_sems]
        for cp in _chip_copies(srcs, lands, sems[0], sems[1], mode):
            cp.wait_send()
            cp.wait_recv()
        if mode != "sibling":
            for cp in _own_copies(srcs, lands, sems[2], mode):
                cp.wait()

    outs = pl.pallas_call(
        body, name=name,
        out_shape=tuple(pltpu.HBM(a.shape, a.dtype) for a in (*st["srcs"], *st["lands"])),
        in_specs=[_HBM] * (2 * n) + [_SEM] * n_sems + [_ORDER] * len(after),
        out_specs=tuple([_HBM] * (2 * n)),
        input_output_aliases={i: i for i in range(2 * n)},
        compiler_params=pltpu.CompilerParams(has_side_effects=_EFFECT),
    )(*st["srcs"], *st["lands"], *st["sems"], *after)
    return outs[:n], outs[n:]


def _matmul(a, b, contract, grid, a_spec, b_spec, o_spec, out_shape, name, after=()):
    def body(a_ref, b_ref, *rest):
        o_ref = rest[len(after)]
        r = lax.dot_general(_bf(a_ref[...]), _bf(b_ref[...]), (contract, ((), ())), preferred_element_type=F32)
        o_ref[...] = r.astype(o_ref.dtype)

    return pl.pallas_call(
        body, name=name, grid=grid, in_specs=[a_spec, b_spec] + [_ORDER] * len(after), out_specs=o_spec,
        out_shape=out_shape, compiler_params=_cp(len(grid)),
    )(a, b, *after)


def _mm_nn(a, b, out_dtype, name, col0=0, n=None, after=()):
    m, k = a.shape
    n = b.shape[1] - col0 if n is None else n
    tm = min(MM_TM, m)
    tn = _tile(n, MM_TN_MAX, also=col0 if col0 else None)
    off = col0 // tn
    return _matmul(a, b, ((1,), (0,)), (n // tn, m // tm),
                   pl.BlockSpec((tm, k), lambda j, i: (i, 0)),
                   pl.BlockSpec((k, tn), lambda j, i: (0, j + off)),
                   pl.BlockSpec((tm, tn), lambda j, i: (i, j)),
                   jax.ShapeDtypeStruct((m, n), out_dtype), name, after=after)


def _mm_nt(a, b, out_dtype, name, row0=0, n=None, after=()):
    m, k = a.shape
    n = b.shape[0] - row0 if n is None else n
    tm = min(MM_TM, m)
    tn = _tile(n, MM_TN_MAX, also=row0 if row0 else None)
    off = row0 // tn
    return _matmul(a, b, ((1,), (1,)), (n // tn, m // tm),
                   pl.BlockSpec((tm, k), lambda j, i: (i, 0)),
                   pl.BlockSpec((tn, k), lambda j, i: (j + off, 0)),
                   pl.BlockSpec((tm, tn), lambda j, i: (i, j)),
                   jax.ShapeDtypeStruct((m, n), out_dtype), name, after=after)


def _mm_tn(a, b, out_dtype, name):
    k, m = a.shape
    n = b.shape[1]
    tm = _tile(m, TN_TM_MAX)
    tn = _tile(n, MM_TN_MAX)
    return _matmul(a, b, ((0,), (0,)), (n // tn, m // tm),
                   pl.BlockSpec((k, tm), lambda j, i: (0, i)),
                   pl.BlockSpec((k, tn), lambda j, i: (0, j)),
                   pl.BlockSpec((tm, tn), lambda j, i: (i, j)),
                   jax.ShapeDtypeStruct((m, n), out_dtype), name)


def _ffn_in_swiglu(x, w3, name):
    m, k = x.shape
    half, nb = w3.shape[0] // 2, w3.shape[2]
    tm = min(MM_TM, m)

    def body(x_ref, wg_ref, wu_ref, gu_ref, a_ref):
        xv = x_ref[...]
        g = jnp.dot(xv, wg_ref[...], preferred_element_type=F32)
        u = jnp.dot(xv, wu_ref[...], preferred_element_type=F32)
        gu_ref[0] = g.astype(gu_ref.dtype)
        gu_ref[1] = u.astype(gu_ref.dtype)
        a_ref[...] = (g * _sigmoid(g) * u).astype(a_ref.dtype)

    return pl.pallas_call(
        body, name=name, grid=(half, m // tm),
        in_specs=[pl.BlockSpec((tm, k), lambda j, i: (i, 0)), pl.BlockSpec((None, k, nb), lambda j, i: (j, 0, 0)),
                  pl.BlockSpec((None, k, nb), lambda j, i: (j + half, 0, 0))],
        out_specs=[pl.BlockSpec((2, tm, nb), lambda j, i: (0, i, j)), pl.BlockSpec((tm, nb), lambda j, i: (i, j))],
        out_shape=[jax.ShapeDtypeStruct((2, m, half * nb), BF16), jax.ShapeDtypeStruct((m, half * nb), BF16)],
        compiler_params=_cp(2),
    )(x, w3, w3)


def _ffn_out_bwd_swiglu(dy, w_out, gu, name):
    m, k = dy.shape
    f = w_out.shape[0]
    tm = min(MM_TM, m)
    tn = _tile(f, MM_TN_MAX)

    def body(dy_ref, w_ref, gu_ref, d_ref):
        da = _dot_nt(dy_ref[...], w_ref[...])
        g, u = gu_ref[0].astype(F32), gu_ref[1].astype(F32)
        sg = _sigmoid(g)
        d_ref[0] = (da * u * (sg * (1.0 + g * (1.0 - sg)))).astype(d_ref.dtype)
        d_ref[1] = (da * (g * sg)).astype(d_ref.dtype)

    return pl.pallas_call(
        body, name=name, grid=(f // tn, m // tm),
        in_specs=[pl.BlockSpec((tm, k), lambda j, i: (i, 0)), pl.BlockSpec((tn, k), lambda j, i: (j, 0)),
                  pl.BlockSpec((2, tm, tn), lambda j, i: (0, i, j))],
        out_specs=pl.BlockSpec((2, tm, tn), lambda j, i: (0, i, j)),
        out_shape=jax.ShapeDtypeStruct((2, m, f), BF16), compiler_params=_cp(2),
    )(dy, w_out, gu)


def _ffn_in_bwd_x(dgu, w3, name):
    _, m, f = dgu.shape
    nj, n, nb = w3.shape
    per = f // nb
    tm = min(MM_TM, m)
    tn = _tile(n, 512)

    def body(d_ref, w_ref, o_ref):
        acc = None
        for j in range(nj):
            part = _dot_nt(d_ref[j // per][:, (j % per) * nb:(j % per + 1) * nb], w_ref[j])
            acc = part if acc is None else acc + part
        o_ref[...] = acc.astype(o_ref.dtype)

    return pl.pallas_call(
        body, name=name, grid=(m // tm, n // tn),
        in_specs=[pl.BlockSpec((2, tm, f), lambda i, j: (0, i, 0)), pl.BlockSpec((nj, tn, nb), lambda i, j: (0, j, 0))],
        out_specs=pl.BlockSpec((tm, tn), lambda i, j: (i, j)),
        out_shape=jax.ShapeDtypeStruct((m, n), F32), compiler_params=_cp(2),
    )(dgu, w3)


def _ffn_in_bwd_w(x, dgu, nj, name):
    k, m = x.shape
    f = dgu.shape[2]
    per = nj // 2
    nb = f // per
    tm = min(TN_TM, m)
    return _matmul(x, dgu, ((0,), (0,)), (nj, m // tm),
                   pl.BlockSpec((k, tm), lambda j, i: (0, i)),
                   pl.BlockSpec((None, k, nb), lambda j, i: (j // per, 0, j % per)),
                   pl.BlockSpec((None, tm, nb), lambda j, i: (j, i, 0)),
                   jax.ShapeDtypeStruct((nj, m, nb), BF16), name)


def _vec_spec(d):
    return pl.BlockSpec((1, d), lambda i: (0, 0))


def _row_spec(ts, d, col=0):
    return pl.BlockSpec((ts, d), lambda i: (i, col))


def _rms(x):
    return lax.rsqrt(jnp.mean(x * x, axis=-1, keepdims=True) + EPS)


def _pre_norm(x, gain, scale, shift, dm):
    def body(x_ref, g_ref, sc_ref, sh_ref, h_ref):
        xv = x_ref[...]
        h_ref[...] = (((xv * _rms(xv)) * g_ref[...]) * (1.0 + sc_ref[...]) + sh_ref[...]).astype(h_ref.dtype)

    return pl.pallas_call(
        body, name="pre_norm", grid=(dm.S // dm.ts,),
        in_specs=[_row_spec(dm.ts, dm.D)] + [_vec_spec(dm.D)] * 3,
        out_specs=_row_spec(dm.ts, dm.D),
        out_shape=jax.ShapeDtypeStruct((dm.S, dm.D), BF16), compiler_params=_cp(1),
    )(x, gain, scale, shift)


def _res_norm(x, y, gpost, gate, gain, scale, shift, dm):
    def body(x_ref, y_ref, gp_ref, gt_ref, g_ref, sc_ref, sh_ref, xo_ref, h_ref):
        yv = y_ref[...]
        xn = x_ref[...] + gt_ref[...] * ((yv * _rms(yv)) * gp_ref[...])
        xo_ref[...] = xn
        h_ref[...] = (((xn * _rms(xn)) * g_ref[...]) * (1.0 + sc_ref[...]) + sh_ref[...]).astype(h_ref.dtype)

    return pl.pallas_call(
        body, name="res_norm", grid=(dm.S // dm.ts,),
        in_specs=[_row_spec(dm.ts, dm.D)] * 2 + [_vec_spec(dm.D)] * 5,
        out_specs=[_row_spec(dm.ts, dm.D)] * 2,
        out_shape=[jax.ShapeDtypeStruct((dm.S, dm.D), F32), jax.ShapeDtypeStruct((dm.S, dm.D), BF16)],
        compiler_params=_cp(1),
    )(x, y, gpost, gate, gain, scale, shift)


def _res_loss(x, y, gpost, gate, target, dm):
    def body(x_ref, y_ref, gp_ref, gt_ref, t_ref, dx_ref, loss_ref):
        i = pl.program_id(0)

        @pl.when(i == 0)
        def _():
            loss_ref[...] = jnp.zeros_like(loss_ref)
        yv = y_ref[...]
        err = x_ref[...] + gt_ref[...] * ((yv * _rms(yv)) * gp_ref[...]) - t_ref[...]
        dx_ref[...] = err * (1.0 / dm.D)
        per_row = jnp.mean(err * err, axis=-1, keepdims=True)
        loss_ref[...] += 0.5 * jnp.sum(per_row, axis=0, keepdims=True)

    return pl.pallas_call(
        body, name="res_loss", grid=(dm.S // dm.ts,),
        in_specs=[_row_spec(dm.ts, dm.D)] * 2 + [_vec_spec(dm.D)] * 2 + [_row_spec(dm.ts, dm.D)],
        out_specs=[_row_spec(dm.ts, dm.D), _vec_spec(LANES)],
        out_shape=[jax.ShapeDtypeStruct((dm.S, dm.D), F32), jax.ShapeDtypeStruct((1, LANES), F32)],
        compiler_params=_cp(1),
    )(x, y, gpost, gate, target)


def _post_bwd(dxo, y, gpost, gate, dm, after=()):
    def body(dx_ref, y_ref, gp_ref, gt_ref, *rest):
        dy_ref, dgp_ref, dgt_ref = rest[len(after):]
        i = pl.program_id(0)

        @pl.when(i == 0)
        def _():
            dgp_ref[...] = jnp.zeros_like(dgp_ref)
            dgt_ref[...] = jnp.zeros_like(dgt_ref)
        yv, dx = y_ref[...], dx_ref[...]
        r = _rms(yv)
        t = yv * r
        dgp_ref[...] += jnp.sum(dx * gt_ref[...] * t, axis=0, keepdims=True)
        dgt_ref[...] += jnp.sum(dx * (t * gp_ref[...]), axis=0, keepdims=True)
        dt = dx * (gt_ref[...] * gp_ref[...])
        dy_ref[...] = (r * (dt - t * jnp.mean(dt * t, axis=-1, keepdims=True))).astype(dy_ref.dtype)

    return pl.pallas_call(
        body, name="post_bwd", grid=(dm.S // dm.ts,),
        in_specs=[_row_spec(dm.ts, dm.D)] * 2 + [_vec_spec(dm.D)] * 2 + [_ORDER] * len(after),
        out_specs=[_row_spec(dm.ts, dm.D), _vec_spec(dm.D), _vec_spec(dm.D)],
        out_shape=[jax.ShapeDtypeStruct((dm.S, dm.D), BF16)] + [jax.ShapeDtypeStruct((1, dm.D), F32)] * 2,
        compiler_params=_cp(1),
    )(dxo, y, gpost, gate, *after)


def _pre_bwd(dh, dxo, x, gain, scale, dm, after=()):
    def body(dh_ref, dxo_ref, x_ref, g_ref, sc_ref, *rest):
        dx_ref, dsh_ref, dsc_ref, dg_ref = rest[len(after):]
        i = pl.program_id(0)

        @pl.when(i == 0)
        def _():
            dsh_ref[...] = jnp.zeros_like(dsh_ref)
            dsc_ref[...] = jnp.zeros_like(dsc_ref)
            dg_ref[...] = jnp.zeros_like(dg_ref)
        xv, dh_ = x_ref[...], dh_ref[...]
        r = _rms(xv)
        nrm = xv * r
        one_sc = 1.0 + sc_ref[...]
        dsh_ref[...] += jnp.sum(dh_, axis=0, keepdims=True)
        dsc_ref[...] += jnp.sum(dh_ * (nrm * g_ref[...]), axis=0, keepdims=True)
        dg_ref[...] += jnp.sum(dh_ * nrm * one_sc, axis=0, keepdims=True)
        dn = dh_ * (g_ref[...] * one_sc)
        dx_ref[...] = dxo_ref[...] + r * (dn - nrm * jnp.mean(dn * nrm, axis=-1, keepdims=True))

    return pl.pallas_call(
        body, name="pre_bwd", grid=(dm.S // dm.ts,),
        in_specs=[_row_spec(dm.ts, dm.D)] * 3 + [_vec_spec(dm.D)] * 2 + [_ORDER] * len(after),
        out_specs=[_row_spec(dm.ts, dm.D)] + [_vec_spec(dm.D)] * 3,
        out_shape=[jax.ShapeDtypeStruct((dm.S, dm.D), F32)] + [jax.ShapeDtypeStruct((1, dm.D), F32)] * 3,
        compiler_params=_cp(1),
    )(dh, dxo, x, gain, scale, *after)


def _sigmoid(z):
    return 1.0 / (1.0 + jnp.exp(-z))


def _tri(n, upper):
    r = lax.broadcasted_iota(jnp.int32, (n, n), 0)
    c = lax.broadcasted_iota(jnp.int32, (n, n), 1)
    return (c >= r if upper else r >= c).astype(F32)


def _gates_fwd(cf, bf, dm):
    ts = dm.ts
    fcol = 2 * dm.C // LANES

    def body(f_ref, b_ref, cum_ref, carry):
        i = pl.program_id(0)

        @pl.when(i == 0)
        def _():
            carry[...] = jnp.zeros_like(carry)
        z = f_ref[...] + b_ref[...]
        lf = jnp.minimum(z, 0.0) - jnp.log(1.0 + jnp.exp(-jnp.abs(z)))
        cs = jnp.dot(_tri(ts, False), lf, precision=lax.Precision.HIGHEST, preferred_element_type=F32) + carry[...]
        cum_ref[...] = cs
        carry[...] = cs[ts - 1:ts, :]

    return pl.pallas_call(
        body, name="gates_fwd", grid=(dm.S // ts,),
        in_specs=[pl.BlockSpec((ts, LANES), lambda i: (i, fcol)), _vec_spec(LANES)],
        out_specs=_row_spec(ts, LANES),
        out_shape=jax.ShapeDtypeStruct((dm.S, LANES), F32),
        scratch_shapes=[pltpu.VMEM((1, LANES), F32)], compiler_params=_cp(1),
    )(cf, bf)


def _gates_bwd(dc, cf, bf, dm):
    ts = dm.ts
    nb = dm.S // ts
    fcol = 2 * dm.C // LANES

    def body(dc_ref, f_ref, b_ref, df_ref, db_ref, carry):
        i = pl.program_id(0)

        @pl.when(i == 0)
        def _():
            carry[...] = jnp.zeros_like(carry)
            db_ref[...] = jnp.zeros_like(db_ref)
        dlf = jnp.dot(_tri(ts, True), dc_ref[...], precision=lax.Precision.HIGHEST, preferred_element_type=F32) + carry[...]
        carry[...] = dlf[0:1, :]
        dz = dlf * (1.0 - _sigmoid(f_ref[...] + b_ref[...]))
        df_ref[...] = dz.astype(df_ref.dtype)
        db_ref[...] += jnp.sum(dz, axis=0, keepdims=True)

    return pl.pallas_call(
        body, name="gates_bwd", grid=(nb,),
        in_specs=[pl.BlockSpec((ts, LANES), lambda i: (nb - 1 - i, 0)),
                  pl.BlockSpec((ts, LANES), lambda i: (nb - 1 - i, fcol)), _vec_spec(LANES)],
        out_specs=[pl.BlockSpec((ts, LANES), lambda i: (nb - 1 - i, 0)), _vec_spec(LANES)],
        out_shape=[jax.ShapeDtypeStruct((dm.S, LANES), BF16), jax.ShapeDtypeStruct((1, LANES), F32)],
        scratch_shapes=[pltpu.VMEM((1, LANES), F32)], compiler_params=_cp(1),
    )(dc, cf, bf)


def _dot_nt(a, b):
    return lax.dot_general(a, b, (((1,), (1,)), ((), ())), preferred_element_type=F32)


def _dot_tn(a, b):
    return lax.dot_general(a, b, (((0,), (0,)), ((), ())), preferred_element_type=F32)


_C0, _C1, _C2 = HEAD_DIM, HEAD_DIM + 3, HEAD_DIM + 6


def _split3(c):
    hi = c.astype(BF16).astype(F32)
    mid = (c - hi).astype(BF16).astype(F32)
    return hi, mid, c - hi - mid


def _put3(base, lane, start, pieces, sign=1.0):
    out = base
    for t, piece in enumerate(pieces):
        out = jnp.where(lane == start + t, sign * piece, out)
    return out


def _head_lanes(pair, odd):
    v = pair.astype(F32)
    return pltpu.roll(v, HEAD_DIM, axis=1) if odd else v


def _attn_prep(qkv, cum, dm):
    ts, A, H = dm.ts, dm.A, dm.H
    scale = HEAD_DIM ** -0.5

    def body(x_ref, c_ref, qa_ref, ka_ref, va_ref):
        lane = lax.broadcasted_iota(jnp.int32, (ts, LANES), 1)
        data = lane < HEAD_DIM
        for h in range(H):
            e, odd = h // 2, h % 2
            pieces = _split3(c_ref[:, h:h + 1])
            q = _head_lanes(x_ref[:, e * LANES:(e + 1) * LANES], odd) * scale
            k = _head_lanes(x_ref[:, A + e * LANES:A + (e + 1) * LANES], odd)
            v = _head_lanes(x_ref[:, 2 * A + e * LANES:2 * A + (e + 1) * LANES], odd)
            qa = jnp.where(data, q, jnp.where((lane >= _C1) & (lane < _C2), 1.0, 0.0))
            qa_ref[h] = _put3(qa, lane, _C0, pieces).astype(BF16)
            ka = jnp.where(data, k, jnp.where((lane < _C1) | ((lane >= _C2) & (lane < _C2 + 3)), 1.0, 0.0))
            ka_ref[h] = _put3(ka, lane, _C1, pieces, -1.0).astype(BF16)
            va_ref[h] = jnp.where(data, v, jnp.where(lane < _C1, 1.0, 0.0)).astype(BF16)

    spec = pl.BlockSpec((H, ts, LANES), lambda i: (0, i, 0))
    return pl.pallas_call(
        body, name="attn_prep", grid=(dm.S // ts,),
        in_specs=[_row_spec(ts, 3 * A), _row_spec(ts, LANES)], out_specs=[spec] * 3,
        out_shape=[jax.ShapeDtypeStruct((H, dm.S, LANES), BF16)] * 3, compiler_params=_cp(1),
    )(qkv, cum)


def _attn_fwd(qa, ka, va, dm):
    tq, A, H, S = dm.ts, dm.A, dm.H, dm.S
    nq = S // tq

    def body(qa_ref, ka_ref, va_ref, o_ref, lse_ref, top_scr, qb_scr, m_scr, acc_scr):
        i = pl.program_id(0)
        row = lax.broadcasted_iota(jnp.int32, (tq, tq), 0)
        col = lax.broadcasted_iota(jnp.int32, (tq, tq), 1)
        lane = lax.broadcasted_iota(jnp.int32, (tq, LANES), 1)

        def logits(q_ref, j, h, diagonal):
            s = _dot_nt(q_ref[h], ka_ref[h, pl.ds(pl.multiple_of(j * tq, tq), tq), :])
            return jnp.where(row >= col, s, -1e30) if diagonal else s

        def maxima(j, carry):
            for h in range(H):
                top_scr[h] = jnp.maximum(top_scr[h], logits(qa_ref, j, h, False))
            return carry

        for h in range(H):
            top_scr[h] = logits(qa_ref, i, h, True)
        lax.fori_loop(0, i, maxima, 0)
        for h in range(H):
            m = jnp.max(top_scr[h], axis=1, keepdims=True)
            m_scr[h] = m
            qb_scr[h] = _put3(qa_ref[h].astype(F32), lane, _C2, _split3(m), -1.0).astype(BF16)

        def weigh(j, carry):
            rows = pl.ds(pl.multiple_of(j * tq, tq), tq)
            for h in range(H):
                p = jnp.exp(logits(qb_scr, j, h, False)).astype(BF16)
                acc_scr[h] += jnp.dot(p, va_ref[h, rows, :], preferred_element_type=F32)
            return carry

        for h in range(H):
            p = jnp.exp(logits(qb_scr, i, h, True)).astype(BF16)
            acc_scr[h] = jnp.dot(p, va_ref[h, pl.ds(pl.multiple_of(i * tq, tq), tq), :], preferred_element_type=F32)
        lax.fori_loop(0, i, weigh, 0)
        lse_all = jnp.zeros((tq, LANES), F32)
        for h in range(H):
            acc = acc_scr[h]
            l = acc[:, _C0:_C0 + 1]
            o_ref[:, h * HEAD_DIM:(h + 1) * HEAD_DIM] = (acc[:, :HEAD_DIM] / l).astype(o_ref.dtype)
            lse_all = jnp.where(lane == h, m_scr[h] + jnp.log(l), lse_all)
        lse_ref[...] = lse_all

    full = pl.BlockSpec((H, S, LANES), lambda i: (0, 0, 0))
    return pl.pallas_call(
        body, name="attn_fwd", grid=(nq,),
        in_specs=[pl.BlockSpec((H, tq, LANES), lambda i: (0, i, 0)), full, full],
        out_specs=[pl.BlockSpec((tq, A), lambda i: (i, 0)), _row_spec(tq, LANES)],
        out_shape=[jax.ShapeDtypeStruct((S, A), BF16), jax.ShapeDtypeStruct((S, LANES), F32)],
        scratch_shapes=[pltpu.VMEM((H, tq, tq), F32), pltpu.VMEM((H, tq, LANES), BF16), pltpu.VMEM((H, tq, 1), F32),
                        pltpu.VMEM((H, tq, LANES), F32)],
        compiler_params=_cp(1),
    )(qa, ka, va)


def _attn_prep_bwd(qa, lse, dcat, o, dm):
    ts, A, H = dm.ts, dm.A, dm.H

    def body(qa_ref, lse_ref, do_ref, o_ref, qb_ref, doa_ref):
        lane = lax.broadcasted_iota(jnp.int32, (ts, LANES), 1)
        data = lane < HEAD_DIM
        for h in range(H):
            e, odd = h // 2, h % 2
            qb_ref[h] = _put3(qa_ref[h].astype(F32), lane, _C2, _split3(lse_ref[:, h:h + 1]), -1.0).astype(BF16)
            do_pair = do_ref[:, e * LANES:(e + 1) * LANES]
            prod = do_pair * o_ref[:, e * LANES:(e + 1) * LANES].astype(F32)
            mine = (lane >= HEAD_DIM) if odd else data
            delta = jnp.sum(jnp.where(mine, prod, 0.0), axis=1, keepdims=True)
            doa = jnp.where(data, _head_lanes(do_pair, odd), 0.0)
            doa_ref[h] = _put3(doa, lane, _C0, _split3(delta), -1.0).astype(BF16)

    spec = pl.BlockSpec((H, ts, LANES), lambda i: (0, i, 0))
    return pl.pallas_call(
        body, name="attn_prep_bwd", grid=(dm.S // ts,),
        in_specs=[spec, _row_spec(ts, LANES), _row_spec(ts, A, 0), _row_spec(ts, A)], out_specs=[spec] * 2,
        out_shape=[jax.ShapeDtypeStruct((H, dm.S, LANES), BF16)] * 2, compiler_params=_cp(1),
    )(qa, lse, dcat, o)


def _attn_bwd(qb, ka, va, doa, dm):
    tq, H, S = dm.ts, dm.H, dm.S
    nq = S // tq

    def body(ka_ref, va_ref, qb_ref, doa_ref, dq_ref, dk_ref, dv_ref):
        j = pl.program_id(0)

        @pl.when(j == 0)
        def _():
            dq_ref[...] = jnp.zeros(dq_ref.shape, F32)
        dk_ref[...] = jnp.zeros(dk_ref.shape, F32)
        dv_ref[...] = jnp.zeros(dv_ref.shape, F32)
        row = lax.broadcasted_iota(jnp.int32, (tq, tq), 0)
        col = lax.broadcasted_iota(jnp.int32, (tq, tq), 1)

        def block(i, masked):
            rows = pl.ds(pl.multiple_of(i * tq, tq), tq)
            for h in range(H):
                q, do_ = qb_ref[h, rows, :], doa_ref[h, rows, :]
                k, v = ka_ref[h], va_ref[h]
                s = _dot_nt(q, k)
                if masked:
                    s = jnp.where(row >= col, s, -1e30)
                p = jnp.exp(s)
                dsb = (p * _dot_nt(do_, v)).astype(BF16)
                dv_ref[h] += _dot_tn(p.astype(BF16), do_)
                dk_ref[h] += _dot_tn(dsb, q)
                dq_ref[h, rows, :] += jnp.dot(dsb, k, preferred_element_type=F32)

        block(j, True)

        def step(i, carry):
            block(i, False)
            return carry

        lax.fori_loop(j + 1, nq, step, 0)

    blk = pl.BlockSpec((H, tq, LANES), lambda j: (0, j, 0))
    full = pl.BlockSpec((H, S, LANES), lambda j: (0, 0, 0))
    return pl.pallas_call(
        body, name="attn_bwd", grid=(nq,),
        in_specs=[blk, blk, full, full], out_specs=[full, blk, blk],
        out_shape=[jax.ShapeDtypeStruct((H, S, LANES), F32)] * 3, compiler_params=_cp(1),
    )(ka, va, qb, doa)


def _attn_post(dqa, dka, dva, dm):
    ts, A, H = dm.ts, dm.A, dm.H
    scale = HEAD_DIM ** -0.5

    def body(dq_ref, dk_ref, dv_ref, o_ref, dc_ref):
        lane = lax.broadcasted_iota(jnp.int32, (ts, LANES), 1)
        data = lane < HEAD_DIM
        dc = jnp.zeros((ts, LANES), F32)
        for h in range(H):
            dc = jnp.where(lane == h, dq_ref[h][:, _C0:_C0 + 1] - dk_ref[h][:, _C1:_C1 + 1], dc)
        dc_ref[...] = dc
        for part, (ref, mul) in enumerate(((dq_ref, scale), (dk_ref, 1.0), (dv_ref, 1.0))):
            for e in range(H // 2):
                pair = jnp.where(data, ref[2 * e], pltpu.roll(ref[2 * e + 1], HEAD_DIM, axis=1))
                o_ref[:, part * A + e * LANES:part * A + (e + 1) * LANES] = (pair * mul).astype(o_ref.dtype)

    spec = pl.BlockSpec((H, ts, LANES), lambda i: (0, i, 0))
    return pl.pallas_call(
        body, name="attn_post", grid=(dm.S // ts,),
        in_specs=[spec] * 3, out_specs=[_row_spec(ts, 3 * A), _row_spec(ts, LANES)],
        out_shape=[jax.ShapeDtypeStruct((dm.S, 3 * A), BF16), jax.ShapeDtypeStruct((dm.S, LANES), F32)],
        compiler_params=_cp(1),
    )(dqa, dka, dva)


def _glu(cf_rows, c):
    return cf_rows[:, :c] * _sigmoid(cf_rows[:, c:2 * c])


def _conv_fwd(cf, cw, cb, lg, lb, dm):
    ts, C = dm.ts, dm.C
    per = ts // HALO

    def body(cf_ref, halo_ref, w_ref, cb_ref, lg_ref, lb_ref, u3_ref, u1_ref):
        i = pl.program_id(0)
        prev = jnp.where(i > 0, _glu(halo_ref[...], C), 0.0)
        win = jnp.concatenate([prev, _glu(cf_ref[...], C)], axis=0)
        u1 = jnp.zeros((ts, C), F32) + cb_ref[...]
        off = HALO - (CONV_K - 1)
        for k in range(CONV_K):
            u1 = u1 + w_ref[k:k + 1, :] * win[off + k:off + k + ts, :]
        u1_ref[...] = u1
        mu = jnp.mean(u1, axis=-1, keepdims=True)
        cen = u1 - mu
        rstd = lax.rsqrt(jnp.mean(cen * cen, axis=-1, keepdims=True) + EPS)
        u2 = cen * rstd * lg_ref[...] + lb_ref[...]
        u3_ref[...] = (u2 * _sigmoid(u2)).astype(u3_ref.dtype)

    return pl.pallas_call(
        body, name="conv_fwd", grid=(dm.S // ts,),
        in_specs=[pl.BlockSpec((ts, 2 * C), lambda i: (i, 0)),
                  pl.BlockSpec((HALO, 2 * C), lambda i: (jnp.maximum(i * per - 1, 0), 0)),
                  pl.BlockSpec((HALO, C), lambda i: (0, 0))] + [_vec_spec(C)] * 3,
        out_specs=[_row_spec(ts, C)] * 2,
        out_shape=[jax.ShapeDtypeStruct((dm.S, C), BF16), jax.ShapeDtypeStruct((dm.S, C), F32)],
        compiler_params=_cp(1),
    )(cf, cf, cw, cb, lg, lb)


def _conv_bwd(dcat, u1, cf, cw, lg, lb, dm):
    ts, C = dm.ts, dm.C
    per = ts // HALO
    nt = dm.S // ts
    last_halo = dm.S // HALO - 1

    def ln_bwd(du3, u1v, lg_v, lb_v):
        mu = jnp.mean(u1v, axis=-1, keepdims=True)
        cen = u1v - mu
        rstd = lax.rsqrt(jnp.mean(cen * cen, axis=-1, keepdims=True) + EPS)
        uhat = cen * rstd
        u2 = uhat * lg_v + lb_v
        sg = _sigmoid(u2)
        du2 = du3 * (sg * (1.0 + u2 * (1.0 - sg)))
        duh = du2 * lg_v
        du1 = rstd * (duh - jnp.mean(duh, axis=-1, keepdims=True) - uhat * jnp.mean(duh * uhat, axis=-1, keepdims=True))
        return du1, du2, uhat

    def body(d_ref, dn_ref, u1_ref, u1n_ref, cf_ref, halo_ref, w_ref, lg_ref, lb_ref,
             dcf_ref, dw_ref, dcb_ref, dlg_ref, dlb_ref):
        i = pl.program_id(0)

        @pl.when(i == 0)
        def _():
            dw_ref[...] = jnp.zeros_like(dw_ref)
            dcb_ref[...] = jnp.zeros_like(dcb_ref)
            dlg_ref[...] = jnp.zeros_like(dlg_ref)
            dlb_ref[...] = jnp.zeros_like(dlb_ref)
        lg_v, lb_v = lg_ref[...], lb_ref[...]
        du1, du2, uhat = ln_bwd(d_ref[...], u1_ref[...], lg_v, lb_v)
        du1n, _, _ = ln_bwd(dn_ref[...], u1n_ref[...], lg_v, lb_v)
        du1n = jnp.where(i < nt - 1, du1n, 0.0)
        dlg_ref[...] += jnp.sum(du2 * uhat, axis=0, keepdims=True)
        dlb_ref[...] += jnp.sum(du2, axis=0, keepdims=True)
        dcb_ref[...] += jnp.sum(du1, axis=0, keepdims=True)
        dwin = jnp.concatenate([du1, du1n], axis=0)
        cfv = cf_ref[...]
        cv, sg = cfv[:, :C], _sigmoid(cfv[:, C:2 * C])
        prev = jnp.where(i > 0, _glu(halo_ref[...], C), 0.0)
        uwin = jnp.concatenate([prev, cv * sg], axis=0)
        du0 = jnp.zeros((ts, C), F32)
        off = HALO - (CONV_K - 1)
        for k in range(CONV_K):
            back = CONV_K - 1 - k
            du0 = du0 + w_ref[k:k + 1, :] * dwin[back:back + ts, :]
            dw_ref[k:k + 1, :] += jnp.sum(du1 * uwin[off + k:off + k + ts, :], axis=0, keepdims=True)
        dcf_ref[:, :C] = (du0 * sg).astype(dcf_ref.dtype)
        dcf_ref[:, C:] = (du0 * cv * sg * (1.0 - sg)).astype(dcf_ref.dtype)

    ucol = dm.A // C
    return pl.pallas_call(
        body, name="conv_bwd", grid=(nt,),
        in_specs=[pl.BlockSpec((ts, C), lambda i: (i, ucol)),
                  pl.BlockSpec((HALO, C), lambda i: (jnp.minimum((i + 1) * per, last_halo), ucol)),
                  pl.BlockSpec((ts, C), lambda i: (i, 0)),
                  pl.BlockSpec((HALO, C), lambda i: (jnp.minimum((i + 1) * per, last_halo), 0)),
                  pl.BlockSpec((ts, 2 * C), lambda i: (i, 0)),
                  pl.BlockSpec((HALO, 2 * C), lambda i: (jnp.maximum(i * per - 1, 0), 0)),
                  pl.BlockSpec((HALO, C), lambda i: (0, 0)), _vec_spec(C), _vec_spec(C)],
        out_specs=[_row_spec(ts, 2 * C), pl.BlockSpec((HALO, C), lambda i: (0, 0))] + [_vec_spec(C)] * 3,
        out_shape=[jax.ShapeDtypeStruct((dm.S, 2 * C), BF16), jax.ShapeDtypeStruct((HALO, C), F32)]
        + [jax.ShapeDtypeStruct((1, C), F32)] * 3,
        compiler_params=_cp(1),
    )(dcat, dcat, u1, u1, cf, cf, cw, lg, lb)


def _ada_fwd(c16, ada_w, ada_b_cols, dm):
    L, D, n = ada_w.shape
    tn = _tile(n, 512)

    def body(c_ref, w_ref, b_ref, o_ref, a_ref):
        cv = c_ref[...]
        act = (cv * _sigmoid(cv)).astype(BF16)
        a_ref[...] = act
        o_ref[...] = jnp.dot(act, w_ref[...].astype(BF16), preferred_element_type=F32) + b_ref[...]

    return pl.pallas_call(
        body, name="ada_fwd", grid=(L, n // tn),
        in_specs=[pl.BlockSpec((16, D), lambda l, j: (0, 0)), pl.BlockSpec((None, D, tn), lambda l, j: (l, 0, j)),
                  pl.BlockSpec((None, 1, tn), lambda l, j: (l, 0, j))],
        out_specs=[pl.BlockSpec((None, 16, tn), lambda l, j: (l, 0, j)), pl.BlockSpec((16, D), lambda l, j: (0, 0))],
        out_shape=[jax.ShapeDtypeStruct((L, 16, n), F32), jax.ShapeDtypeStruct((16, D), BF16)],
        compiler_params=_cp(2),
    )(c16, ada_w, ada_b_cols)


def _ada_bwd(act16, dmod16):
    L, _, n = dmod16.shape
    D = act16.shape[1]
    tm = min(TN_TM, D)

    def body(a_ref, d_ref, o_ref):
        o_ref[...] = _dot_tn(a_ref[...], d_ref[...])

    return pl.pallas_call(
        body, name="ada_bwd", grid=(L, D // tm),
        in_specs=[pl.BlockSpec((16, tm), lambda l, i: (0, i)), pl.BlockSpec((None, 16, n), lambda l, i: (l, 0, 0))],
        out_specs=pl.BlockSpec((None, tm, n), lambda l, i: (l, i, 0)),
        out_shape=jax.ShapeDtypeStruct((L, D, n), F32), compiler_params=_cp(2),
    )(act16, dmod16)


def _sum_devices(g8, after=()):
    _, R, _ = g8.shape
    tr = _rows_tile(R)

    def body(g_ref, *rest):
        o_ref = rest[len(after)]
        acc = g_ref[0]
        for d in range(1, N_DEV):
            acc = acc + g_ref[d]
        o_ref[...] = acc

    return pl.pallas_call(
        body, name="sum_devices", grid=(R // tr,),
        in_specs=[pl.BlockSpec((N_DEV, tr, LANES), lambda i: (0, i, 0))] + [_ORDER] * len(after),
        out_specs=pl.BlockSpec((tr, LANES), lambda i: (i, 0)),
        out_shape=jax.ShapeDtypeStruct((R, LANES), F32), compiler_params=_cp(1),
    )(g8, *after)


def _rows_tile(r, cap=512):
    for t in (512, 256, 128, 64, 32, 16, 8):
        if t <= cap and r % t == 0:
            return t
    return r


def _adam_math(w, g, m, v):
    m = ADAM_B1 * m + (1.0 - ADAM_B1) * g
    v = ADAM_B2 * v + (1.0 - ADAM_B2) * (g * g)
    m_hat = m / (1.0 - ADAM_B1 ** ADAM_STEP)
    v_hat = v / (1.0 - ADAM_B2 ** ADAM_STEP)
    delta = -ADAM_LR * (m_hat / (jnp.sqrt(v_hat) + ADAM_EPS) + ADAM_WD * w)
    return delta, m, v


def _adamw_dense(w, m, v, g, name):
    R, Cc = w.shape
    tr = _rows_tile(R, 128)

    def body(w_ref, m_ref, v_ref, g_ref, d_ref, mo_ref, vo_ref):
        d, mn, vn = _adam_math(w_ref[...], g_ref[...], m_ref[...], v_ref[...])
        d_ref[...] = d
        mo_ref[...] = mn
        vo_ref[...] = vn

    spec = pl.BlockSpec((tr, Cc), lambda i: (i, 0))
    return pl.pallas_call(
        body, name=name, grid=(R // tr,), in_specs=[spec] * 4, out_specs=[spec] * 3,
        out_shape=[jax.ShapeDtypeStruct((R, Cc), F32)] * 3, compiler_params=_cp(1),
    )(w, m, v, g)


def _adamw_shard(w, m, v, near, far, layer, prev, name):
    L, r, cc = w.shape
    tr, tc = (_rows_tile(r, 128), cc) if r % 8 == 0 else (r, _tile(cc, 2 * LANES))

    def body(w_ref, m_ref, v_ref, n_ref, f_ref, *rest):
        g_ref, d_ref, mo_ref, vo_ref = rest[-4:]
        g = n_ref[0].astype(F32) + f_ref[0].astype(F32)
        for k in range(1, N_CHIPS):
            g = g + (n_ref[k].astype(F32) + f_ref[k].astype(F32))
        d, mn, vn = _adam_math(w_ref[...], g, m_ref[...], v_ref[...])
        g_ref[...] = g
        d_ref[...] = d
        mo_ref[...] = mn
        vo_ref[...] = vn

    wspec = pl.BlockSpec((None, tr, tc), lambda i, j: (layer, i, j))
    sspec = pl.BlockSpec((N_CHIPS, tr, tc), lambda i, j: (0, i, j))
    n_prev = 0 if prev is None else 4
    return pl.pallas_call(
        body, name=name, grid=(r // tr, cc // tc),
        in_specs=[wspec] * 3 + [sspec] * 2 + [pl.BlockSpec(memory_space=pl.ANY)] * n_prev,
        out_specs=[wspec] * 4,
        out_shape=[jax.ShapeDtypeStruct((L, r, cc), F32)] * 4,
        input_output_aliases={5 + t: t for t in range(n_prev)},
        compiler_params=_cp(2),
    )(w, m, v, near, far, *(prev or ()))


def _pack(vs):
    flat = jnp.concatenate([v.reshape(-1).astype(F32) for v in vs])
    pad = (-flat.shape[0]) % (64 * LANES)
    return jnp.pad(flat, (0, pad)).reshape(-1, LANES)


def _unpack(packed, shapes):
    flat = packed.reshape(-1)
    out, pos = [], 0
    for s in shapes:
        n = 1
        for d in s:
            n *= d
        out.append(flat[pos:pos + n].reshape(s))
        pos += n
    return out


def kernel(x, c, w_in, b_f, conv_w, conv_b, conv_ln_g, conv_ln_b, w_o, w_ffn_in, w_ffn_out, mix_pre_g, mix_post_g, ffn_pre_g, ffn_post_g, ada_w, ada_b, loss_target, m_w_in, m_b_f, m_conv_w, m_conv_b, m_conv_ln_g, m_conv_ln_b, m_w_o, m_w_ffn_in, m_w_ffn_out, m_mix_pre_g, m_mix_post_g, m_ffn_pre_g, m_ffn_post_g, m_ada_w, m_ada_b, v_w_in, v_b_f, v_conv_w, v_conv_b, v_conv_ln_g, v_conv_ln_b, v_w_o, v_w_ffn_in, v_w_ffn_out, v_mix_pre_g, v_mix_post_g, v_ffn_pre_g, v_ffn_post_g, v_ada_w, v_ada_b):
    params = dict(w_in=w_in, b_f=b_f, conv_w=conv_w, conv_b=conv_b, conv_ln_g=conv_ln_g, conv_ln_b=conv_ln_b, w_o=w_o,
                  w_ffn_in=w_ffn_in, w_ffn_out=w_ffn_out, mix_pre_g=mix_pre_g, mix_post_g=mix_post_g,
                  ffn_pre_g=ffn_pre_g, ffn_post_g=ffn_post_g, ada_w=ada_w, ada_b=ada_b)
    mom = dict(w_in=m_w_in, b_f=m_b_f, conv_w=m_conv_w, conv_b=m_conv_b, conv_ln_g=m_conv_ln_g, conv_ln_b=m_conv_ln_b,
               w_o=m_w_o, w_ffn_in=m_w_ffn_in, w_ffn_out=m_w_ffn_out, mix_pre_g=m_mix_pre_g, mix_post_g=m_mix_post_g,
               ffn_pre_g=m_ffn_pre_g, ffn_post_g=m_ffn_post_g, ada_w=m_ada_w, ada_b=m_ada_b)
    var = dict(w_in=v_w_in, b_f=v_b_f, conv_w=v_conv_w, conv_b=v_conv_b, conv_ln_g=v_conv_ln_g, conv_ln_b=v_conv_ln_b,
               w_o=v_w_o, w_ffn_in=v_w_ffn_in, w_ffn_out=v_w_ffn_out, mix_pre_g=v_mix_pre_g, mix_post_g=v_mix_post_g,
               ffn_pre_g=v_ffn_pre_g, ffn_post_g=v_ffn_post_g, ada_w=v_ada_w, ada_b=v_ada_b)

    S, D = x.shape[1], x.shape[2]
    L = w_in.shape[0]
    A = D // 2
    C = D - A
    H = A // HEAD_DIM
    F = w_ffn_out.shape[1] * N_CHIPS
    d_in = w_in.shape[2] * N_CHIPS
    NP = 3 * A + 2 * C + LANES
    dm = Dims(S=S, D=D, A=A, C=C, H=H, F=F, L=L, NP=NP, ts=min(ROW_TILE, S))
    assert H <= 8 and A == C and d_in == 3 * A + H + 2 * C

    ix, iy, ic = _place()
    chip = 2 * ix + iy
    dev = 4 * ix + 2 * iy + ic
    x2 = x.reshape(S, D)
    tgt = loss_target.reshape(S, D)

    swap = lambda t: jnp.transpose(t, (0, 2, 1))
    w_in_t = swap(w_in)
    big_state = dict(w_in=(w_in_t, swap(m_w_in), swap(v_w_in)), w_o=(w_o, m_w_o, v_w_o),
                     w_ffn_in=(w_ffn_in, m_w_ffn_in, v_w_ffn_in), w_ffn_out=(w_ffn_out, m_w_ffn_out, v_w_ffn_out))

    def gather_start(l, after):
        first = _exchange_start([w_in_t[l].astype(BF16)], "gather", after, f"gather_a_start_{l}")
        rest = _exchange_start([w_o[l].astype(BF16), w_ffn_in[l].astype(BF16), w_ffn_out[l].astype(BF16)], "gather",
                               [first["token"]], f"gather_b_start_{l}")
        return first, rest

    def gather_wait(st, after, name):
        return _exchange_wait(st, after, name)[1]

    c_all = _all_gather_devices(c.reshape(D // LANES, LANES), "gather_c").reshape(N_DEV, D)
    c16 = jnp.pad(c_all, ((0, 16 - N_DEV), (0, 0)))
    n_ada = ada_w.shape[2]
    ada_b_cols = lax.dynamic_slice_in_dim(ada_b, chip * n_ada, n_ada, axis=1).reshape(L, 1, n_ada)
    mod_cols, act16 = _ada_fwd(c16, ada_w, ada_b_cols, dm)
    conv_w_all, mod_all = _all_gather_chips([conv_w.reshape(L * CONV_K, -1), mod_cols.reshape(L * 16, n_ada)], "gather_mod")
    cwc = conv_w.shape[2]
    conv_w_full = conv_w_all.reshape(N_CHIPS, L, CONV_K, cwc).transpose(1, 2, 0, 3).reshape(L, CONV_K, C)
    conv_w_full = jnp.pad(conv_w_full, ((0, 0), (0, HALO - CONV_K), (0, 0)))
    mod_all = mod_all.reshape(N_CHIPS, L, 16, n_ada)
    mod_me = lax.dynamic_index_in_dim(mod_all, dev, axis=2, keepdims=False)
    mod_me = mod_me.transpose(1, 0, 2).reshape(L, N_MOD, 1, D)

    gather = [None] * L
    gather[0] = gather_start(0, [mod_all])

    def projection_of(g_in):
        w_nat = g_in.reshape(d_in, D)
        return jnp.concatenate([w_nat[:3 * A], w_nat[3 * A + H:], w_nat[3 * A:3 * A + H],
                                jnp.zeros((LANES - H, D), BF16)], axis=0)

    gathered = [None] * L
    vec = lambda p, l: p[l].reshape(1, -1)
    bf_pad = jnp.pad(b_f, ((0, 0), (0, LANES - H)))

    saved = []
    xin = x2
    h = _pre_norm(xin, vec(mix_pre_g, 0), mod_me[0, 1], mod_me[0, 0], dm)
    dx = loss_part = None
    for l in range(L):
        (g_in,) = gather_wait(gather[l][0], [h], f"gather_a_wait_{l}")
        w_p = projection_of(g_in)
        order = [gather[l][1]["token"]]
        if l + 1 < L:
            gather[l + 1] = gather_start(l + 1, [g_in, gather[l][1]["token"]])
            order.append(gather[l + 1][1]["token"])
        qkv = _mm_nt(h, w_p, BF16, "mm_qkv", 0, 3 * A, after=order)
        cf = _mm_nt(h, w_p, F32, "mm_cf", 3 * A, 2 * C + LANES)
        cum = _gates_fwd(cf, vec(bf_pad, l), dm)
        qa, ka, va = _attn_prep(qkv, cum, dm)
        o, lse = _attn_fwd(qa, ka, va, dm)
        u3, u1 = _conv_fwd(cf, conv_w_full[l], vec(conv_b, l), vec(conv_ln_g, l), vec(conv_ln_b, l), dm)
        cat = jnp.concatenate([o, u3], axis=1)
        g_o, wfi, g_fo = gather_wait(gather[l][1], [cat], f"gather_b_wait_{l}")
        wo, wfo = g_o.reshape(D, D), g_fo.reshape(F, D)
        gathered[l] = (w_p, wo, wfi, wfo)
        y = _mm_nn(cat, wo, F32, "mm_o")
        x1, h2 = _res_norm(xin, y, vec(mix_post_g, l), mod_me[l, 2], vec(ffn_pre_g, l), mod_me[l, 4], mod_me[l, 3], dm)
        gu, a = _ffn_in_swiglu(h2, wfi, "mm_ffn_in")
        y2 = _mm_nn(a, wfo, F32, "mm_ffn_out")
        saved.append(dict(xin=xin, h=h, qa=qa, ka=ka, va=va, cf=cf, o=o, lse=lse, u1=u1, cat=cat, y=y,
                          x1=x1, h2=h2, gu=gu, a=a, y2=y2))
        if l + 1 < L:
            xin, h = _res_norm(x1, y2, vec(ffn_post_g, l), mod_me[l, 5], vec(mix_pre_g, l + 1),
                               mod_me[l + 1, 1], mod_me[l + 1, 0], dm)
        else:
            dx, loss_part = _res_loss(x1, y2, vec(ffn_post_g, l), mod_me[l, 5], tgt, dm)
    loss = lax.psum(loss_part[0, 0], ("x", "y", "c"))

    small = [None] * L
    big = [None] * L
    order = []
    for l in reversed(range(L)):
        w_p, wo, wfi, wfo = gathered[l]
        sv = saved[l]
        dy2, d_gfpost, d_g2 = _post_bwd(dx, sv["y2"], vec(ffn_post_g, l), mod_me[l, 5], dm, after=order)
        dgu = _ffn_out_bwd_swiglu(dy2, wfo, sv["gu"], "mm_da")
        g_wfo = _mm_tn(sv["a"], dy2, BF16, "mm_dwfo")
        g_wfi = _ffn_in_bwd_w(sv["h2"], dgu, N_CHIPS, "mm_dwfi")
        dh2 = _ffn_in_bwd_x(dgu, wfi, "mm_dh2")
        scatter_ffn = _exchange_start([g_wfi, g_wfo.reshape(N_CHIPS, F // N_CHIPS, D)], "scatter", [], f"scatter_b_start_{l}")
        dx1, d_sh2, d_sc2, d_gfpre = _pre_bwd(dh2, dx, sv["x1"], vec(ffn_pre_g, l), mod_me[l, 4], dm,
                                              after=[scatter_ffn["token"]])
        dy, d_gpost, d_g1 = _post_bwd(dx1, sv["y"], vec(mix_post_g, l), mod_me[l, 2], dm)
        dcat = _mm_nt(dy, wo, F32, "mm_dcat")
        g_wo = _mm_tn(sv["cat"], dy, BF16, "mm_dwo")
        dcfc, d_cw, d_cb, d_lg, d_lb = _conv_bwd(dcat, sv["u1"], sv["cf"], conv_w_full[l], vec(conv_ln_g, l),
                                                 vec(conv_ln_b, l), dm)
        qb, doa = _attn_prep_bwd(sv["qa"], sv["lse"], dcat, sv["o"], dm)
        dqkv, dcum = _attn_post(*_attn_bwd(qb, sv["ka"], sv["va"], doa, dm), dm)
        dfl, d_bf = _gates_bwd(dcum, sv["cf"], vec(bf_pad, l), dm)
        dproj = jnp.concatenate([dqkv, dcfc, dfl], axis=1)
        dh = _mm_nn(dproj, w_p, F32, "mm_dh")
        g_wp = _mm_tn(dproj, sv["h"], BF16, "mm_dwp")
        dx, d_sh1, d_sc1, d_gpre = _pre_bwd(dh, dx1, sv["xin"], vec(mix_pre_g, l), mod_me[l, 1], dm)
        g_nat = jnp.concatenate([g_wp[:3 * A], g_wp[3 * A + 2 * C:3 * A + 2 * C + H], g_wp[3 * A:3 * A + 2 * C]], axis=0)
        g_win = g_nat.reshape(N_CHIPS, d_in // N_CHIPS, D)
        g_mix = [g_win, g_wo.reshape(N_CHIPS, D // N_CHIPS, D)]
        if l > 0:
            scatter_mix = _exchange_start(g_mix, "scatter", [], f"scatter_a_start_{l}")
            order = [scatter_mix["token"]]
            big[l] = (scatter_mix, scatter_ffn)
        small[l] = dict(b_f=d_bf[0, :H], conv_b=d_cb[0], conv_ln_g=d_lg[0], conv_ln_b=d_lb[0], mix_pre_g=d_gpre[0],
                        mix_post_g=d_gpost[0], ffn_pre_g=d_gfpre[0], ffn_post_g=d_gfpost[0],
                        dmod=jnp.concatenate([d_sh1, d_sc1, d_g1, d_sh2, d_sc2, d_g2], axis=1)[0],
                        conv_w=d_cw[:CONV_K])
    grad_x = dx.reshape(1, S, D)

    keys_small = ["b_f", "conv_b", "conv_ln_g", "conv_ln_b", "mix_pre_g", "mix_post_g", "ffn_pre_g", "ffn_post_g",
                  "dmod", "conv_w"]
    stacked = [jnp.stack([small[l][k] for l in range(L)]) for k in keys_small]
    shapes = [s.shape for s in stacked]
    pack = _pack(stacked)
    pack8 = _all_gather_devices(pack, "gather_small")
    big[0] = (_exchange_start(g_mix, "scatter", [pack8], "scatter_a_start_0"), scatter_ffn)
    summed = dict(zip(keys_small, _unpack(_sum_devices(pack8, after=[big[0][0]["token"]]), shapes)))
    dmod_all = jnp.stack([_unpack(pack8[d], shapes)[keys_small.index("dmod")] for d in range(N_DEV)])
    grads = {k: summed[k] for k in keys_small[:8]}
    grads["ada_b"] = summed["dmod"]
    grads["conv_w"] = lax.dynamic_slice_in_dim(summed["conv_w"], chip * cwc, cwc, axis=2)

    dmod_cols = lax.dynamic_slice_in_dim(dmod_all.reshape(N_DEV, L, N_CHIPS, n_ada), chip, 1, axis=2)
    dmod16 = jnp.pad(dmod_cols.reshape(N_DEV, L, n_ada).transpose(1, 0, 2), ((0, 0), (0, 16 - N_DEV), (0, 0))).astype(BF16)
    grads["ada_w"] = _ada_bwd(act16, dmod16)

    d_aw, m_aw, v_aw = _adamw_dense(ada_w.reshape(L * D, n_ada), m_ada_w.reshape(L * D, n_ada),
                                    v_ada_w.reshape(L * D, n_ada), grads["ada_w"].reshape(L * D, n_ada), "adamw_ada_w")
    names_small = ["b_f", "conv_w", "conv_b", "conv_ln_g", "conv_ln_b", "mix_pre_g", "mix_post_g", "ffn_pre_g",
                   "ffn_post_g", "ada_b"]
    shapes_small = [params[n].shape for n in names_small]
    d_s, m_s, v_s = _adamw_dense(_pack([params[n] for n in names_small]), _pack([mom[n] for n in names_small]),
                                 _pack([var[n] for n in names_small]), _pack([grads[n] for n in names_small]),
                                 "adamw_small")
    delta_w = dict(zip(names_small, _unpack(d_s, shapes_small)))
    new_m = dict(zip(names_small, _unpack(m_s, shapes_small)))
    new_v = dict(zip(names_small, _unpack(v_s, shapes_small)))
    delta_w["ada_w"], new_m["ada_w"], new_v["ada_w"] = (t.reshape(L, D, n_ada) for t in (d_aw, m_aw, v_aw))

    names_big = ["w_in", "w_o", "w_ffn_in", "w_ffn_out"]
    res_big = {n: None for n in names_big}
    forward = [None] * L

    def update(l, after):
        near, far = _exchange_wait(forward[l], after, f"forward_wait_{l}")
        for t, n in enumerate(names_big):
            res_big[n] = _adamw_shard(*big_state[n], near[t], far[t], l, res_big[n], f"adamw_{n}_{l}")
        return [res_big[names_big[-1]][1]]

    done = [dx]
    for l in reversed(range(L)):
        after = done + [d_s, d_aw] if l == 0 else done
        near = []
        for st, nm in zip(big[l], ("a", "b")):
            near += _exchange_wait(st, after, f"scatter_{nm}_wait_{l}")[1]
        forward[l] = _exchange_start(near, "sibling", [], f"forward_start_{l}")
        done = update(l + 1, [forward[l]["token"]]) if l + 1 < L else [forward[l]["token"]]
    update(0, [])
    res_big["w_in"] = [swap(t) for t in res_big["w_in"]]
    for n in names_big:
        grads[n], delta_w[n], new_m[n], new_v[n] = res_big[n]

    return (loss, grad_x, *[grads[n] for n in WEIGHTS], *[delta_w[n] for n in WEIGHTS],
            *[new_m[n] for n in WEIGHTS], *[new_v[n] for n in WEIGHTS])
```

```python
import collections
import functools

import jax
import jax.numpy as jnp
from jax import lax
from jax.experimental import pallas as pl
from jax.experimental.pallas import tpu as pltpu

F32 = jnp.float32
BF16 = jnp.bfloat16
MESH = pl.DeviceIdType.MESH

HEAD_DIM = 64
CONV_K = 31
N_MOD = 6
EPS = 1e-6
N_CHIPS = 4
N_DEV = 8
LANES = 128
HALO = 32
ROW_TILE = 256
NORM_TILE = 512
MM_TM = 512
MM_TM_PLAIN = 1024
MM_TN_MAX = 1408
ADAM_ROWS = 256
TN_TM = 256
TN_TM_MAX = 384
VMEM_LIMIT = 56 * 1024 * 1024

ADAM_LR = 0.001
ADAM_B1 = 0.9
ADAM_B2 = 0.999
ADAM_EPS = 1e-08
ADAM_WD = 0.01
ADAM_STEP = 10

WEIGHTS = ['w_in', 'b_f', 'conv_w', 'conv_b', 'conv_ln_g', 'conv_ln_b', 'w_o', 'w_ffn_in', 'w_ffn_out',
           'mix_pre_g', 'mix_post_g', 'ffn_pre_g', 'ffn_post_g', 'ada_w', 'ada_b']

Dims = collections.namedtuple("Dims", "S D A C H F L NP ts tr")


def _cp(n_grid=0):
    if n_grid:
        return pltpu.CompilerParams(dimension_semantics=("arbitrary",) * n_grid, vmem_limit_bytes=VMEM_LIMIT)
    return pltpu.CompilerParams(vmem_limit_bytes=VMEM_LIMIT)


def _tile(n, cap, also=None):
    best = None
    t = LANES
    while t <= min(n, cap):
        if n % t == 0 and (also is None or also % t == 0):
            best = t
        t += LANES
    assert best is not None, (n, cap, also)
    return best


def _bf(v):
    return v if v.dtype == BF16 else v.astype(BF16)


def _place():
    return lax.axis_index("x"), lax.axis_index("y"), lax.axis_index("c")


def _flip(v, d):
    return 1 - v if d else v


def _all_gather_devices(a, name, after=()):
    def body(a_ref, *rest):
        o_ref, send, recv, lsem = rest[len(after):]
        x, y, c = _place()
        me = 4 * x + 2 * y + c
        local = pltpu.make_async_copy(a_ref, o_ref.at[me], lsem)
        local.start()
        copies = []
        for k in range(1, N_DEV):
            peer = (_flip(x, (k >> 2) & 1), _flip(y, (k >> 1) & 1), _flip(c, k & 1))
            cp = pltpu.make_async_remote_copy(src_ref=a_ref, dst_ref=o_ref.at[me], send_sem=send.at[k - 1],
                                              recv_sem=recv.at[k - 1], device_id=peer, device_id_type=MESH)
            cp.start()
            copies.append(cp)
        for cp in copies:
            cp.wait()
        local.wait()

    return pl.pallas_call(
        body, name=name,
        out_shape=jax.ShapeDtypeStruct((N_DEV,) + a.shape, a.dtype),
        in_specs=[pl.BlockSpec(memory_space=pl.ANY)] * (1 + len(after)),
        out_specs=pl.BlockSpec(memory_space=pl.ANY),
        scratch_shapes=[pltpu.SemaphoreType.DMA((N_DEV - 1,)), pltpu.SemaphoreType.DMA((N_DEV - 1,)),
                        pltpu.SemaphoreType.DMA],
    )(a, *after)


def _all_gather_chips(arrays, name):
    n = len(arrays)

    def body(*refs):
        a_refs, o_refs = refs[:n], refs[n:2 * n]
        send, recv, lsem = refs[2 * n:]
        x, y, c = _place()
        me = 2 * x + y
        copies = []
        for i in range(n):
            local = pltpu.make_async_copy(a_refs[i], o_refs[i].at[me], lsem.at[i])
            local.start()
            copies.append(local)
            for k in range(1, N_CHIPS):
                peer = (_flip(x, (k >> 1) & 1), _flip(y, k & 1), c)
                cp = pltpu.make_async_remote_copy(src_ref=a_refs[i], dst_ref=o_refs[i].at[me],
                                                  send_sem=send.at[i, k - 1], recv_sem=recv.at[i, k - 1],
                                                  device_id=peer, device_id_type=MESH)
                cp.start()
                copies.append(cp)
        for cp in copies:
            cp.wait()

    return pl.pallas_call(
        body, name=name,
        out_shape=[jax.ShapeDtypeStruct((N_CHIPS,) + a.shape, a.dtype) for a in arrays],
        in_specs=[pl.BlockSpec(memory_space=pl.ANY)] * n,
        out_specs=[pl.BlockSpec(memory_space=pl.ANY)] * n,
        scratch_shapes=[pltpu.SemaphoreType.DMA((n, N_CHIPS - 1)), pltpu.SemaphoreType.DMA((n, N_CHIPS - 1)),
                        pltpu.SemaphoreType.DMA((n,))],
    )(*arrays)


_HBM = pl.BlockSpec(memory_space=pltpu.HBM)
_SEM = pl.BlockSpec(memory_space=pltpu.SEMAPHORE)
_EFFECT = pltpu.SideEffectType.DATAFLOW_SIDE_EFFECTING


def _n_copies(mode):
    return {"sibling": 1, "devices": N_DEV - 1}.get(mode, N_CHIPS - 1)


def _chip_copies(srcs, lands, send, recv, mode):
    x, y, c = _place()
    me = 2 * x + y
    copies = []
    for i in range(len(srcs)):
        if mode == "sibling":
            copies.append(pltpu.make_async_remote_copy(src_ref=srcs[i], dst_ref=lands[i], send_sem=send.at[i],
                                                       recv_sem=recv.at[i], device_id=(x, y, 1 - c), device_id_type=MESH))
            continue
        if mode == "devices":
            for k in range(1, N_DEV):
                peer = (_flip(x, (k >> 2) & 1), _flip(y, (k >> 1) & 1), _flip(c, k & 1))
                s = i * (N_DEV - 1) + k - 1
                copies.append(pltpu.make_async_remote_copy(src_ref=srcs[i], dst_ref=lands[i].at[2 * me + c],
                                                           send_sem=send.at[s], recv_sem=recv.at[s], device_id=peer,
                                                           device_id_type=MESH))
            continue
        for k in range(1, N_CHIPS):
            px, py = _flip(x, (k >> 1) & 1), _flip(y, k & 1)
            src = srcs[i].at[2 * px + py] if mode == "scatter" else srcs[i]
            s = i * (N_CHIPS - 1) + k - 1
            copies.append(pltpu.make_async_remote_copy(src_ref=src, dst_ref=lands[i].at[me], send_sem=send.at[s],
                                                       recv_sem=recv.at[s], device_id=(px, py, c), device_id_type=MESH))
    return copies


def _own_copies(srcs, lands, own, mode):
    x, y, c = _place()
    me = 2 * x + y
    slot = 2 * me + c if mode == "devices" else me
    return [pltpu.make_async_copy(srcs[i].at[me] if mode == "scatter" else srcs[i], lands[i].at[slot], own.at[i])
            for i in range(len(srcs))]


_ORDER = pl.BlockSpec(memory_space=pl.ANY)


def _exchange_start(arrays, mode, after, name):
    n = len(arrays)
    n_sems = 2 if mode == "sibling" else 3

    def body(*refs):
        srcs, lands = refs[:n], refs[n:2 * n]
        sems = refs[2 * n + len(after):2 * n + len(after) + n_sems]
        token = refs[-1]
        for cp in _chip_copies(srcs, lands, sems[0], sems[1], mode):
            cp.start()
        if mode != "sibling":
            for cp in _own_copies(srcs, lands, sems[2], mode):
                cp.start()
        token[...] = jnp.zeros_like(token)

    lead = {"gather": (N_CHIPS,), "devices": (N_DEV,)}.get(mode, ())
    land_shapes = [lead + a.shape for a in arrays]
    n_sem = n * _n_copies(mode)
    sem_shapes = [pltpu.SemaphoreType.DMA((n_sem,)), pltpu.SemaphoreType.DMA((n_sem,)), pltpu.SemaphoreType.DMA((n,))]
    outs = pl.pallas_call(
        body, name=name,
        out_shape=(*sem_shapes[:n_sems],
                   *[pltpu.HBM(a.shape, a.dtype) for a in arrays],
                   *[pltpu.HBM(s, a.dtype) for s, a in zip(land_shapes, arrays)],
                   jax.ShapeDtypeStruct((8, LANES), F32)),
        in_specs=[_HBM] * (2 * n) + [_ORDER] * len(after),
        out_specs=(*[_SEM] * n_sems, *[_HBM] * (2 * n), pl.BlockSpec(memory_space=pltpu.VMEM)),
        input_output_aliases={i: n_sems + i for i in range(2 * n)},
        compiler_params=pltpu.CompilerParams(has_side_effects=_EFFECT),
    )(*[pltpu.with_memory_space_constraint(a, pltpu.HBM) for a in arrays],
      *[pltpu.with_memory_space_constraint(lax.empty(s, a.dtype), pltpu.HBM) for s, a in zip(land_shapes, arrays)],
      *after)
    return dict(sems=outs[:n_sems], srcs=outs[n_sems:n_sems + n], lands=outs[n_sems + n:n_sems + 2 * n], token=outs[-1],
                mode=mode)


def _exchange_wait(st, after, name):
    n = len(st["srcs"])
    mode = st["mode"]
    n_sems = len(st["sems"])

    def body(*refs):
        srcs, lands = refs[:n], refs[n:2 * n]
        sems = refs[2 * n:2 * n + n_sems]
        for cp in _chip_copies(srcs, lands, sems[0], sems[1], mode):
            cp.wait_send()
            cp.wait_recv()
        if mode != "sibling":
            for cp in _own_copies(srcs, lands, sems[2], mode):
                cp.wait()

    outs = pl.pallas_call(
        body, name=name,
        out_shape=tuple(pltpu.HBM(a.shape, a.dtype) for a in (*st["srcs"], *st["lands"])),
        in_specs=[_HBM] * (2 * n) + [_SEM] * n_sems + [_ORDER] * len(after),
        out_specs=tuple([_HBM] * (2 * n)),
        input_output_aliases={i: i for i in range(2 * n)},
        compiler_params=pltpu.CompilerParams(has_side_effects=_EFFECT),
    )(*st["srcs"], *st["lands"], *st["sems"], *after)
    return outs[:n], outs[n:]


def _matmul(a, b, contract, grid, a_spec, b_spec, o_spec, out_shape, name, after=()):
    def body(a_ref, b_ref, *rest):
        o_ref = rest[len(after)]
        r = lax.dot_general(_bf(a_ref[...]), _bf(b_ref[...]), (contract, ((), ())), preferred_element_type=F32)
        o_ref[...] = r.astype(o_ref.dtype)

    return pl.pallas_call(
        body, name=name, grid=grid, in_specs=[a_spec, b_spec] + [_ORDER] * len(after), out_specs=o_spec,
        out_shape=out_shape, compiler_params=_cp(len(grid)),
    )(a, b, *after)


def _mm_nn(a, b, out_dtype, name, col0=0, n=None, after=()):
    m, k = a.shape
    n = b.shape[1] - col0 if n is None else n
    tm = min(MM_TM_PLAIN, m)
    tn = _tile(n, MM_TN_MAX, also=col0 if col0 else None)
    off = col0 // tn
    return _matmul(a, b, ((1,), (0,)), (n // tn, m // tm),
                   pl.BlockSpec((tm, k), lambda j, i: (i, 0)),
                   pl.BlockSpec((k, tn), lambda j, i: (0, j + off)),
                   pl.BlockSpec((tm, tn), lambda j, i: (i, j)),
                   jax.ShapeDtypeStruct((m, n), out_dtype), name, after=after)


def _mm_nt(a, b, out_dtype, name, row0=0, n=None, after=()):
    m, k = a.shape
    n = b.shape[0] - row0 if n is None else n
    tm = min(MM_TM_PLAIN, m)
    tn = _tile(n, MM_TN_MAX, also=row0 if row0 else None)
    off = row0 // tn
    return _matmul(a, b, ((1,), (1,)), (n // tn, m // tm),
                   pl.BlockSpec((tm, k), lambda j, i: (i, 0)),
                   pl.BlockSpec((tn, k), lambda j, i: (j + off, 0)),
                   pl.BlockSpec((tm, tn), lambda j, i: (i, j)),
                   jax.ShapeDtypeStruct((m, n), out_dtype), name, after=after)


def _mm_tn(a, b, out_dtype, name):
    k, m = a.shape
    n = b.shape[1]
    tm = _tile(m, TN_TM_MAX)
    tn = _tile(n, MM_TN_MAX)
    return _matmul(a, b, ((0,), (0,)), (n // tn, m // tm),
                   pl.BlockSpec((k, tm), lambda j, i: (0, i)),
                   pl.BlockSpec((k, tn), lambda j, i: (0, j)),
                   pl.BlockSpec((tm, tn), lambda j, i: (i, j)),
                   jax.ShapeDtypeStruct((m, n), out_dtype), name)


def _ffn_in_swiglu(x, w3, name):
    m, k = x.shape
    half, nb = w3.shape[0] // 2, w3.shape[2]
    tm = min(MM_TM, m)

    def body(x_ref, wg_ref, wu_ref, gu_ref, a_ref):
        xv = x_ref[...]
        g = jnp.dot(xv, wg_ref[...], preferred_element_type=F32)
        u = jnp.dot(xv, wu_ref[...], preferred_element_type=F32)
        gu_ref[0] = g.astype(gu_ref.dtype)
        gu_ref[1] = u.astype(gu_ref.dtype)
        a_ref[...] = (g * _sigmoid(g) * u).astype(a_ref.dtype)

    return pl.pallas_call(
        body, name=name, grid=(half, m // tm),
        in_specs=[pl.BlockSpec((tm, k), lambda j, i: (i, 0)), pl.BlockSpec((None, k, nb), lambda j, i: (j, 0, 0)),
                  pl.BlockSpec((None, k, nb), lambda j, i: (j + half, 0, 0))],
        out_specs=[pl.BlockSpec((2, tm, nb), lambda j, i: (0, i, j)), pl.BlockSpec((tm, nb), lambda j, i: (i, j))],
        out_shape=[jax.ShapeDtypeStruct((2, m, half * nb), BF16), jax.ShapeDtypeStruct((m, half * nb), BF16)],
        compiler_params=_cp(2),
    )(x, w3, w3)


def _ffn_out_bwd_swiglu(dy, w_out, gu, name):
    m, k = dy.shape
    f = w_out.shape[0]
    tm = min(MM_TM, m)
    tn = _tile(f, MM_TN_MAX)

    def body(dy_ref, w_ref, gu_ref, d_ref):
        da = _dot_nt(dy_ref[...], w_ref[...])
        g, u = gu_ref[0].astype(F32), gu_ref[1].astype(F32)
        sg = _sigmoid(g)
        d_ref[0] = (da * u * (sg * (1.0 + g * (1.0 - sg)))).astype(d_ref.dtype)
        d_ref[1] = (da * (g * sg)).astype(d_ref.dtype)

    return pl.pallas_call(
        body, name=name, grid=(f // tn, m // tm),
        in_specs=[pl.BlockSpec((tm, k), lambda j, i: (i, 0)), pl.BlockSpec((tn, k), lambda j, i: (j, 0)),
                  pl.BlockSpec((2, tm, tn), lambda j, i: (0, i, j))],
        out_specs=pl.BlockSpec((2, tm, tn), lambda j, i: (0, i, j)),
        out_shape=jax.ShapeDtypeStruct((2, m, f), BF16), compiler_params=_cp(2),
    )(dy, w_out, gu)


def _ffn_in_bwd_x(dgu, w3, name):
    _, m, f = dgu.shape
    nj, n, nb = w3.shape
    per = f // nb
    tm = min(MM_TM, m)
    tn = _tile(n, 512)

    def body(d_ref, w_ref, o_ref):
        acc = None
        for j in range(nj):
            part = _dot_nt(d_ref[j // per][:, (j % per) * nb:(j % per + 1) * nb], w_ref[j])
            acc = part if acc is None else acc + part
        o_ref[...] = acc.astype(o_ref.dtype)

    return pl.pallas_call(
        body, name=name, grid=(m // tm, n // tn),
        in_specs=[pl.BlockSpec((2, tm, f), lambda i, j: (0, i, 0)), pl.BlockSpec((nj, tn, nb), lambda i, j: (0, j, 0))],
        out_specs=pl.BlockSpec((tm, tn), lambda i, j: (i, j)),
        out_shape=jax.ShapeDtypeStruct((m, n), F32), compiler_params=_cp(2),
    )(dgu, w3)


def _ffn_in_bwd_w(x, dgu, nj, name):
    k, m = x.shape
    f = dgu.shape[2]
    per = nj // 2
    nb = f // per
    tm = min(TN_TM, m)
    return _matmul(x, dgu, ((0,), (0,)), (nj, m // tm),
                   pl.BlockSpec((k, tm), lambda j, i: (0, i)),
                   pl.BlockSpec((None, k, nb), lambda j, i: (j // per, 0, j % per)),
                   pl.BlockSpec((None, tm, nb), lambda j, i: (j, i, 0)),
                   jax.ShapeDtypeStruct((nj, m, nb), BF16), name)


def _vec_spec(d):
    return pl.BlockSpec((1, d), lambda i: (0, 0))


def _row_spec(ts, d, col=0):
    return pl.BlockSpec((ts, d), lambda i: (i, col))


def _rms(x):
    return lax.rsqrt(jnp.mean(x * x, axis=-1, keepdims=True) + EPS)


def _pre_norm(x, gain, scale, shift, dm, after=()):
    def body(x_ref, g_ref, sc_ref, sh_ref, *rest):
        h_ref = rest[len(after)]
        xv = x_ref[...]
        h_ref[...] = (((xv * _rms(xv)) * g_ref[...]) * (1.0 + sc_ref[...]) + sh_ref[...]).astype(h_ref.dtype)

    return pl.pallas_call(
        body, name="pre_norm", grid=(dm.S // dm.tr,),
        in_specs=[_row_spec(dm.tr, dm.D)] + [_vec_spec(dm.D)] * 3 + [_ORDER] * len(after),
        out_specs=_row_spec(dm.tr, dm.D),
        out_shape=jax.ShapeDtypeStruct((dm.S, dm.D), BF16), compiler_params=_cp(1),
    )(x, gain, scale, shift, *after)


def _res_norm(x, y, gpost, gate, gain, scale, shift, dm):
    def body(x_ref, y_ref, gp_ref, gt_ref, g_ref, sc_ref, sh_ref, xo_ref, h_ref):
        yv = y_ref[...]
        xn = x_ref[...] + gt_ref[...] * ((yv * _rms(yv)) * gp_ref[...])
        xo_ref[...] = xn
        h_ref[...] = (((xn * _rms(xn)) * g_ref[...]) * (1.0 + sc_ref[...]) + sh_ref[...]).astype(h_ref.dtype)

    return pl.pallas_call(
        body, name="res_norm", grid=(dm.S // dm.tr,),
        in_specs=[_row_spec(dm.tr, dm.D)] * 2 + [_vec_spec(dm.D)] * 5,
        out_specs=[_row_spec(dm.tr, dm.D)] * 2,
        out_shape=[jax.ShapeDtypeStruct((dm.S, dm.D), F32), jax.ShapeDtypeStruct((dm.S, dm.D), BF16)],
        compiler_params=_cp(1),
    )(x, y, gpost, gate, gain, scale, shift)


def _res_loss(x, y, gpost, gate, target, dm):
    def body(x_ref, y_ref, gp_ref, gt_ref, t_ref, dx_ref, loss_ref):
        i = pl.program_id(0)

        @pl.when(i == 0)
        def _():
            loss_ref[...] = jnp.zeros_like(loss_ref)
        yv = y_ref[...]
        err = x_ref[...] + gt_ref[...] * ((yv * _rms(yv)) * gp_ref[...]) - t_ref[...]
        dx_ref[...] = err * (1.0 / dm.D)
        per_row = jnp.mean(err * err, axis=-1, keepdims=True)
        loss_ref[...] += 0.5 * jnp.sum(per_row, axis=0, keepdims=True)

    return pl.pallas_call(
        body, name="res_loss", grid=(dm.S // dm.tr,),
        in_specs=[_row_spec(dm.tr, dm.D)] * 2 + [_vec_spec(dm.D)] * 2 + [_row_spec(dm.tr, dm.D)],
        out_specs=[_row_spec(dm.tr, dm.D), _vec_spec(LANES)],
        out_shape=[jax.ShapeDtypeStruct((dm.S, dm.D), F32), jax.ShapeDtypeStruct((1, LANES), F32)],
        compiler_params=_cp(1),
    )(x, y, gpost, gate, target)


def _post_bwd(dxo, y, gpost, gate, dm, after=()):
    def body(dx_ref, y_ref, gp_ref, gt_ref, *rest):
        dy_ref, dgp_ref, dgt_ref = rest[len(after):]
        i = pl.program_id(0)

        @pl.when(i == 0)
        def _():
            dgp_ref[...] = jnp.zeros_like(dgp_ref)
            dgt_ref[...] = jnp.zeros_like(dgt_ref)
        yv, dx = y_ref[...], dx_ref[...]
        r = _rms(yv)
        t = yv * r
        dgp_ref[...] += jnp.sum(dx * gt_ref[...] * t, axis=0, keepdims=True)
        dgt_ref[...] += jnp.sum(dx * (t * gp_ref[...]), axis=0, keepdims=True)
        dt = dx * (gt_ref[...] * gp_ref[...])
        dy_ref[...] = (r * (dt - t * jnp.mean(dt * t, axis=-1, keepdims=True))).astype(dy_ref.dtype)

    return pl.pallas_call(
        body, name="post_bwd", grid=(dm.S // dm.tr,),
        in_specs=[_row_spec(dm.tr, dm.D)] * 2 + [_vec_spec(dm.D)] * 2 + [_ORDER] * len(after),
        out_specs=[_row_spec(dm.tr, dm.D), _vec_spec(dm.D), _vec_spec(dm.D)],
        out_shape=[jax.ShapeDtypeStruct((dm.S, dm.D), BF16)] + [jax.ShapeDtypeStruct((1, dm.D), F32)] * 2,
        compiler_params=_cp(1),
    )(dxo, y, gpost, gate, *after)


def _pre_bwd(dh, dxo, x, gain, scale, dm, after=()):
    def body(dh_ref, dxo_ref, x_ref, g_ref, sc_ref, *rest):
        dx_ref, dsh_ref, dsc_ref, dg_ref = rest[len(after):]
        i = pl.program_id(0)

        @pl.when(i == 0)
        def _():
            dsh_ref[...] = jnp.zeros_like(dsh_ref)
            dsc_ref[...] = jnp.zeros_like(dsc_ref)
            dg_ref[...] = jnp.zeros_like(dg_ref)
        xv, dh_ = x_ref[...], dh_ref[...]
        r = _rms(xv)
        nrm = xv * r
        one_sc = 1.0 + sc_ref[...]
        dsh_ref[...] += jnp.sum(dh_, axis=0, keepdims=True)
        dsc_ref[...] += jnp.sum(dh_ * (nrm * g_ref[...]), axis=0, keepdims=True)
        dg_ref[...] += jnp.sum(dh_ * nrm * one_sc, axis=0, keepdims=True)
        dn = dh_ * (g_ref[...] * one_sc)
        dx_ref[...] = dxo_ref[...] + r * (dn - nrm * jnp.mean(dn * nrm, axis=-1, keepdims=True))

    return pl.pallas_call(
        body, name="pre_bwd", grid=(dm.S // dm.tr,),
        in_specs=[_row_spec(dm.tr, dm.D)] * 3 + [_vec_spec(dm.D)] * 2 + [_ORDER] * len(after),
        out_specs=[_row_spec(dm.tr, dm.D)] + [_vec_spec(dm.D)] * 3,
        out_shape=[jax.ShapeDtypeStruct((dm.S, dm.D), F32)] + [jax.ShapeDtypeStruct((1, dm.D), F32)] * 3,
        compiler_params=_cp(1),
    )(dh, dxo, x, gain, scale, *after)


def _sigmoid(z):
    return 1.0 / (1.0 + jnp.exp(-z))


def _tri(n, upper):
    r = lax.broadcasted_iota(jnp.int32, (n, n), 0)
    c = lax.broadcasted_iota(jnp.int32, (n, n), 1)
    return (c >= r if upper else r >= c).astype(F32)


def _gates_fwd(cf, bf, dm):
    ts = dm.ts
    fcol = 2 * dm.C // LANES

    def body(f_ref, b_ref, cum_ref, carry):
        i = pl.program_id(0)

        @pl.when(i == 0)
        def _():
            carry[...] = jnp.zeros_like(carry)
        z = f_ref[...] + b_ref[...]
        lf = jnp.minimum(z, 0.0) - jnp.log(1.0 + jnp.exp(-jnp.abs(z)))
        cs = jnp.dot(_tri(ts, False), lf, precision=lax.Precision.HIGHEST, preferred_element_type=F32) + carry[...]
        cum_ref[...] = cs
        carry[...] = cs[ts - 1:ts, :]

    return pl.pallas_call(
        body, name="gates_fwd", grid=(dm.S // ts,),
        in_specs=[pl.BlockSpec((ts, LANES), lambda i: (i, fcol)), _vec_spec(LANES)],
        out_specs=_row_spec(ts, LANES),
        out_shape=jax.ShapeDtypeStruct((dm.S, LANES), F32),
        scratch_shapes=[pltpu.VMEM((1, LANES), F32)], compiler_params=_cp(1),
    )(cf, bf)


def _gates_bwd(dc, cf, bf, dm):
    ts = dm.ts
    nb = dm.S // ts
    fcol = 2 * dm.C // LANES

    def body(dc_ref, f_ref, b_ref, df_ref, db_ref, carry):
        i = pl.program_id(0)

        @pl.when(i == 0)
        def _():
            carry[...] = jnp.zeros_like(carry)
            db_ref[...] = jnp.zeros_like(db_ref)
        dlf = jnp.dot(_tri(ts, True), dc_ref[...], precision=lax.Precision.HIGHEST, preferred_element_type=F32) + carry[...]
        carry[...] = dlf[0:1, :]
        dz = dlf * (1.0 - _sigmoid(f_ref[...] + b_ref[...]))
        df_ref[...] = dz.astype(df_ref.dtype)
        db_ref[...] += jnp.sum(dz, axis=0, keepdims=True)

    return pl.pallas_call(
        body, name="gates_bwd", grid=(nb,),
        in_specs=[pl.BlockSpec((ts, LANES), lambda i: (nb - 1 - i, 0)),
                  pl.BlockSpec((ts, LANES), lambda i: (nb - 1 - i, fcol)), _vec_spec(LANES)],
        out_specs=[pl.BlockSpec((ts, LANES), lambda i: (nb - 1 - i, 0)), _vec_spec(LANES)],
        out_shape=[jax.ShapeDtypeStruct((dm.S, LANES), BF16), jax.ShapeDtypeStruct((1, LANES), F32)],
        scratch_shapes=[pltpu.VMEM((1, LANES), F32)], compiler_params=_cp(1),
    )(dc, cf, bf)


def _dot_nt(a, b):
    return lax.dot_general(a, b, (((1,), (1,)), ((), ())), preferred_element_type=F32)


def _dot_tn(a, b):
    return lax.dot_general(a, b, (((0,), (0,)), ((), ())), preferred_element_type=F32)


_C0, _C1, _C2 = HEAD_DIM, HEAD_DIM + 3, HEAD_DIM + 6


def _split3(c):
    hi = c.astype(BF16).astype(F32)
    mid = (c - hi).astype(BF16).astype(F32)
    return hi, mid, c - hi - mid


def _put3(base, lane, start, pieces, sign=1.0):
    out = base
    for t, piece in enumerate(pieces):
        out = jnp.where(lane == start + t, sign * piece, out)
    return out


def _head_lanes(pair, odd):
    v = pair.astype(F32)
    return pltpu.roll(v, HEAD_DIM, axis=1) if odd else v


def _attn_prep(qkv, cum, dm):
    ts, A, H = dm.ts, dm.A, dm.H
    scale = HEAD_DIM ** -0.5

    def body(x_ref, c_ref, qa_ref, ka_ref, va_ref):
        lane = lax.broadcasted_iota(jnp.int32, (ts, LANES), 1)
        data = lane < HEAD_DIM
        for h in range(H):
            e, odd = h // 2, h % 2
            pieces = _split3(c_ref[:, h:h + 1])
            q = _head_lanes(x_ref[:, e * LANES:(e + 1) * LANES], odd) * scale
            k = _head_lanes(x_ref[:, A + e * LANES:A + (e + 1) * LANES], odd)
            v = _head_lanes(x_ref[:, 2 * A + e * LANES:2 * A + (e + 1) * LANES], odd)
            qa = jnp.where(data, q, jnp.where((lane >= _C1) & (lane < _C2), 1.0, 0.0))
            qa_ref[h] = _put3(qa, lane, _C0, pieces).astype(BF16)
            ka = jnp.where(data, k, jnp.where((lane < _C1) | ((lane >= _C2) & (lane < _C2 + 3)), 1.0, 0.0))
            ka_ref[h] = _put3(ka, lane, _C1, pieces, -1.0).astype(BF16)
            va_ref[h] = jnp.where(data, v, jnp.where(lane < _C1, 1.0, 0.0)).astype(BF16)

    spec = pl.BlockSpec((H, ts, LANES), lambda i: (0, i, 0))
    return pl.pallas_call(
        body, name="attn_prep", grid=(dm.S // ts,),
        in_specs=[_row_spec(ts, 3 * A), _row_spec(ts, LANES)], out_specs=[spec] * 3,
        out_shape=[jax.ShapeDtypeStruct((H, dm.S, LANES), BF16)] * 3, compiler_params=_cp(1),
    )(qkv, cum)


def _attn_fwd(qa, ka, va, dm):
    tq, A, H, S = dm.ts, dm.A, dm.H, dm.S
    nq = S // tq

    def body(qa_ref, ka_ref, va_ref, o_ref, lse_ref, top_scr, qb_scr, m_scr, acc_scr):
        i = pl.program_id(0)
        row = lax.broadcasted_iota(jnp.int32, (tq, tq), 0)
        col = lax.broadcasted_iota(jnp.int32, (tq, tq), 1)
        lane = lax.broadcasted_iota(jnp.int32, (tq, LANES), 1)

        def logits(q_ref, j, h, diagonal):
            s = _dot_nt(q_ref[h], ka_ref[h, pl.ds(pl.multiple_of(j * tq, tq), tq), :])
            return jnp.where(row >= col, s, -1e30) if diagonal else s

        def maxima(j, carry):
            for h in range(H):
                top_scr[h] = jnp.maximum(top_scr[h], logits(qa_ref, j, h, False))
            return carry

        for h in range(H):
            top_scr[h] = logits(qa_ref, i, h, True)
        lax.fori_loop(0, i, maxima, 0)
        for h in range(H):
            m = jnp.max(top_scr[h], axis=1, keepdims=True)
            m_scr[h] = m
            qb_scr[h] = _put3(qa_ref[h].astype(F32), lane, _C2, _split3(m), -1.0).astype(BF16)

        def weigh(j, carry):
            rows = pl.ds(pl.multiple_of(j * tq, tq), tq)
            for h in range(H):
                p = jnp.exp(logits(qb_scr, j, h, False)).astype(BF16)
                acc_scr[h] += jnp.dot(p, va_ref[h, rows, :], preferred_element_type=F32)
            return carry

        for h in range(H):
            p = jnp.exp(logits(qb_scr, i, h, True)).astype(BF16)
            acc_scr[h] = jnp.dot(p, va_ref[h, pl.ds(pl.multiple_of(i * tq, tq), tq), :], preferred_element_type=F32)
        lax.fori_loop(0, i, weigh, 0)
        lse_all = jnp.zeros((tq, LANES), F32)
        for h in range(H):
            acc = acc_scr[h]
            l = acc[:, _C0:_C0 + 1]
            o_ref[:, h * HEAD_DIM:(h + 1) * HEAD_DIM] = (acc[:, :HEAD_DIM] / l).astype(o_ref.dtype)
            lse_all = jnp.where(lane == h, m_scr[h] + jnp.log(l), lse_all)
        lse_ref[...] = lse_all

    full = pl.BlockSpec((H, S, LANES), lambda i: (0, 0, 0))
    return pl.pallas_call(
        body, name="attn_fwd", grid=(nq,),
        in_specs=[pl.BlockSpec((H, tq, LANES), lambda i: (0, i, 0)), full, full],
        out_specs=[pl.BlockSpec((tq, A), lambda i: (i, 0)), _row_spec(tq, LANES)],
        out_shape=[jax.ShapeDtypeStruct((S, A), BF16), jax.ShapeDtypeStruct((S, LANES), F32)],
        scratch_shapes=[pltpu.VMEM((H, tq, tq), F32), pltpu.VMEM((H, tq, LANES), BF16), pltpu.VMEM((H, tq, 1), F32),
                        pltpu.VMEM((H, tq, LANES), F32)],
        compiler_params=_cp(1),
    )(qa, ka, va)


def _attn_prep_bwd(qa, lse, dcat, o, dm):
    ts, A, H = dm.ts, dm.A, dm.H

    def body(qa_ref, lse_ref, do_ref, o_ref, qb_ref, doa_ref):
        lane = lax.broadcasted_iota(jnp.int32, (ts, LANES), 1)
        data = lane < HEAD_DIM
        for h in range(H):
            e, odd = h // 2, h % 2
            qb_ref[h] = _put3(qa_ref[h].astype(F32), lane, _C2, _split3(lse_ref[:, h:h + 1]), -1.0).astype(BF16)
            do_pair = do_ref[:, e * LANES:(e + 1) * LANES]
            prod = do_pair * o_ref[:, e * LANES:(e + 1) * LANES].astype(F32)
            mine = (lane >= HEAD_DIM) if odd else data
            delta = jnp.sum(jnp.where(mine, prod, 0.0), axis=1, keepdims=True)
            doa = jnp.where(data, _head_lanes(do_pair, odd), 0.0)
            doa_ref[h] = _put3(doa, lane, _C0, _split3(delta), -1.0).astype(BF16)

    spec = pl.BlockSpec((H, ts, LANES), lambda i: (0, i, 0))
    return pl.pallas_call(
        body, name="attn_prep_bwd", grid=(dm.S // ts,),
        in_specs=[spec, _row_spec(ts, LANES), _row_spec(ts, A, 0), _row_spec(ts, A)], out_specs=[spec] * 2,
        out_shape=[jax.ShapeDtypeStruct((H, dm.S, LANES), BF16)] * 2, compiler_params=_cp(1),
    )(qa, lse, dcat, o)


def _attn_bwd(qb, ka, va, doa, dm):
    tq, H, S = dm.ts, dm.H, dm.S
    nq = S // tq

    def body(ka_ref, va_ref, qb_ref, doa_ref, dq_ref, dk_ref, dv_ref):
        j = pl.program_id(0)

        @pl.when(j == 0)
        def _():
            dq_ref[...] = jnp.zeros(dq_ref.shape, F32)
        dk_ref[...] = jnp.zeros(dk_ref.shape, F32)
        dv_ref[...] = jnp.zeros(dv_ref.shape, F32)
        row = lax.broadcasted_iota(jnp.int32, (tq, tq), 0)
        col = lax.broadcasted_iota(jnp.int32, (tq, tq), 1)

        def block(i, masked):
            rows = pl.ds(pl.multiple_of(i * tq, tq), tq)
            for h in range(H):
                q, do_ = qb_ref[h, rows, :], doa_ref[h, rows, :]
                k, v = ka_ref[h], va_ref[h]
                s = _dot_nt(q, k)
                if masked:
                    s = jnp.where(row >= col, s, -1e30)
                p = jnp.exp(s)
                dsb = (p * _dot_nt(do_, v)).astype(BF16)
                dv_ref[h] += _dot_tn(p.astype(BF16), do_)
                dk_ref[h] += _dot_tn(dsb, q)
                dq_ref[h, rows, :] += jnp.dot(dsb, k, preferred_element_type=F32)

        block(j, True)

        def step(i, carry):
            block(i, False)
            return carry

        lax.fori_loop(j + 1, nq, step, 0)

    blk = pl.BlockSpec((H, tq, LANES), lambda j: (0, j, 0))
    full = pl.BlockSpec((H, S, LANES), lambda j: (0, 0, 0))
    return pl.pallas_call(
        body, name="attn_bwd", grid=(nq,),
        in_specs=[blk, blk, full, full], out_specs=[full, blk, blk],
        out_shape=[jax.ShapeDtypeStruct((H, S, LANES), F32)] * 3, compiler_params=_cp(1),
    )(ka, va, qb, doa)


def _attn_post(dqa, dka, dva, dm):
    ts, A, H = dm.ts, dm.A, dm.H
    scale = HEAD_DIM ** -0.5

    def body(dq_ref, dk_ref, dv_ref, o_ref, dc_ref):
        lane = lax.broadcasted_iota(jnp.int32, (ts, LANES), 1)
        data = lane < HEAD_DIM
        dc = jnp.zeros((ts, LANES), F32)
        for h in range(H):
            dc = jnp.where(lane == h, dq_ref[h][:, _C0:_C0 + 1] - dk_ref[h][:, _C1:_C1 + 1], dc)
        dc_ref[...] = dc
        for part, (ref, mul) in enumerate(((dq_ref, scale), (dk_ref, 1.0), (dv_ref, 1.0))):
            for e in range(H // 2):
                pair = jnp.where(data, ref[2 * e], pltpu.roll(ref[2 * e + 1], HEAD_DIM, axis=1))
                o_ref[:, part * A + e * LANES:part * A + (e + 1) * LANES] = (pair * mul).astype(o_ref.dtype)

    spec = pl.BlockSpec((H, ts, LANES), lambda i: (0, i, 0))
    return pl.pallas_call(
        body, name="attn_post", grid=(dm.S // ts,),
        in_specs=[spec] * 3, out_specs=[_row_spec(ts, 3 * A), _row_spec(ts, LANES)],
        out_shape=[jax.ShapeDtypeStruct((dm.S, 3 * A), BF16), jax.ShapeDtypeStruct((dm.S, LANES), F32)],
        compiler_params=_cp(1),
    )(dqa, dka, dva)


def _glu(cf_rows, c):
    return cf_rows[:, :c] * _sigmoid(cf_rows[:, c:2 * c])


def _conv_fwd(cf, cw, cb, lg, lb, dm):
    ts, C = dm.ts, dm.C
    per = ts // HALO

    def body(cf_ref, halo_ref, w_ref, cb_ref, lg_ref, lb_ref, u3_ref, u1_ref):
        i = pl.program_id(0)
        prev = jnp.where(i > 0, _glu(halo_ref[...], C), 0.0)
        win = jnp.concatenate([prev, _glu(cf_ref[...], C)], axis=0)
        u1 = jnp.zeros((ts, C), F32) + cb_ref[...]
        off = HALO - (CONV_K - 1)
        for k in range(CONV_K):
            u1 = u1 + w_ref[k:k + 1, :] * win[off + k:off + k + ts, :]
        u1_ref[...] = u1
        mu = jnp.mean(u1, axis=-1, keepdims=True)
        cen = u1 - mu
        rstd = lax.rsqrt(jnp.mean(cen * cen, axis=-1, keepdims=True) + EPS)
        u2 = cen * rstd * lg_ref[...] + lb_ref[...]
        u3_ref[...] = (u2 * _sigmoid(u2)).astype(u3_ref.dtype)

    return pl.pallas_call(
        body, name="conv_fwd", grid=(dm.S // ts,),
        in_specs=[pl.BlockSpec((ts, 2 * C), lambda i: (i, 0)),
                  pl.BlockSpec((HALO, 2 * C), lambda i: (jnp.maximum(i * per - 1, 0), 0)),
                  pl.BlockSpec((HALO, C), lambda i: (0, 0))] + [_vec_spec(C)] * 3,
        out_specs=[_row_spec(ts, C)] * 2,
        out_shape=[jax.ShapeDtypeStruct((dm.S, C), BF16), jax.ShapeDtypeStruct((dm.S, C), F32)],
        compiler_params=_cp(1),
    )(cf, cf, cw, cb, lg, lb)


def _conv_bwd(dcat, u1, cf, cw, lg, lb, dm):
    ts, C = dm.ts, dm.C
    per = ts // HALO
    nt = dm.S // ts
    last_halo = dm.S // HALO - 1

    def ln_bwd(du3, u1v, lg_v, lb_v):
        mu = jnp.mean(u1v, axis=-1, keepdims=True)
        cen = u1v - mu
        rstd = lax.rsqrt(jnp.mean(cen * cen, axis=-1, keepdims=True) + EPS)
        uhat = cen * rstd
        u2 = uhat * lg_v + lb_v
        sg = _sigmoid(u2)
        du2 = du3 * (sg * (1.0 + u2 * (1.0 - sg)))
        duh = du2 * lg_v
        du1 = rstd * (duh - jnp.mean(duh, axis=-1, keepdims=True) - uhat * jnp.mean(duh * uhat, axis=-1, keepdims=True))
        return du1, du2, uhat

    def body(d_ref, dn_ref, u1_ref, u1n_ref, cf_ref, halo_ref, w_ref, lg_ref, lb_ref,
             dcf_ref, dw_ref, dcb_ref, dlg_ref, dlb_ref):
        i = pl.program_id(0)

        @pl.when(i == 0)
        def _():
            dw_ref[...] = jnp.zeros_like(dw_ref)
            dcb_ref[...] = jnp.zeros_like(dcb_ref)
            dlg_ref[...] = jnp.zeros_like(dlg_ref)
            dlb_ref[...] = jnp.zeros_like(dlb_ref)
        lg_v, lb_v = lg_ref[...], lb_ref[...]
        du1, du2, uhat = ln_bwd(d_ref[...], u1_ref[...], lg_v, lb_v)
        du1n, _, _ = ln_bwd(dn_ref[...], u1n_ref[...], lg_v, lb_v)
        du1n = jnp.where(i < nt - 1, du1n, 0.0)
        dlg_ref[...] += jnp.sum(du2 * uhat, axis=0, keepdims=True)
        dlb_ref[...] += jnp.sum(du2, axis=0, keepdims=True)
        dcb_ref[...] += jnp.sum(du1, axis=0, keepdims=True)
        dwin = jnp.concatenate([du1, du1n], axis=0)
        cfv = cf_ref[...]
        cv, sg = cfv[:, :C], _sigmoid(cfv[:, C:2 * C])
        prev = jnp.where(i > 0, _glu(halo_ref[...], C), 0.0)
        uwin = jnp.concatenate([prev, cv * sg], axis=0)
        du0 = jnp.zeros((ts, C), F32)
        off = HALO - (CONV_K - 1)
        for k in range(CONV_K):
            back = CONV_K - 1 - k
            du0 = du0 + w_ref[k:k + 1, :] * dwin[back:back + ts, :]
            dw_ref[k:k + 1, :] += jnp.sum(du1 * uwin[off + k:off + k + ts, :], axis=0, keepdims=True)
        dcf_ref[:, :C] = (du0 * sg).astype(dcf_ref.dtype)
        dcf_ref[:, C:] = (du0 * cv * sg * (1.0 - sg)).astype(dcf_ref.dtype)

    ucol = dm.A // C
    return pl.pallas_call(
        body, name="conv_bwd", grid=(nt,),
        in_specs=[pl.BlockSpec((ts, C), lambda i: (i, ucol)),
                  pl.BlockSpec((HALO, C), lambda i: (jnp.minimum((i + 1) * per, last_halo), ucol)),
                  pl.BlockSpec((ts, C), lambda i: (i, 0)),
                  pl.BlockSpec((HALO, C), lambda i: (jnp.minimum((i + 1) * per, last_halo), 0)),
                  pl.BlockSpec((ts, 2 * C), lambda i: (i, 0)),
                  pl.BlockSpec((HALO, 2 * C), lambda i: (jnp.maximum(i * per - 1, 0), 0)),
                  pl.BlockSpec((HALO, C), lambda i: (0, 0)), _vec_spec(C), _vec_spec(C)],
        out_specs=[_row_spec(ts, 2 * C), pl.BlockSpec((HALO, C), lambda i: (0, 0))] + [_vec_spec(C)] * 3,
        out_shape=[jax.ShapeDtypeStruct((dm.S, 2 * C), BF16), jax.ShapeDtypeStruct((HALO, C), F32)]
        + [jax.ShapeDtypeStruct((1, C), F32)] * 3,
        compiler_params=_cp(1),
    )(dcat, dcat, u1, u1, cf, cf, cw, lg, lb)


def _ada_fwd(c16, ada_w, ada_b_cols, dm):
    L, D, n = ada_w.shape
    tn = _tile(n, 512)

    def body(c_ref, w_ref, b_ref, o_ref, a_ref):
        cv = c_ref[...]
        act = (cv * _sigmoid(cv)).astype(BF16)
        a_ref[...] = act
        o_ref[...] = jnp.dot(act, w_ref[...].astype(BF16), preferred_element_type=F32) + b_ref[...]

    return pl.pallas_call(
        body, name="ada_fwd", grid=(L, n // tn),
        in_specs=[pl.BlockSpec((16, D), lambda l, j: (0, 0)), pl.BlockSpec((None, D, tn), lambda l, j: (l, 0, j)),
                  pl.BlockSpec((None, 1, tn), lambda l, j: (l, 0, j))],
        out_specs=[pl.BlockSpec((None, 16, tn), lambda l, j: (l, 0, j)), pl.BlockSpec((16, D), lambda l, j: (0, 0))],
        out_shape=[jax.ShapeDtypeStruct((L, 16, n), F32), jax.ShapeDtypeStruct((16, D), BF16)],
        compiler_params=_cp(2),
    )(c16, ada_w, ada_b_cols)


def _ada_bwd(act16, dmod16):
    L, _, n = dmod16.shape
    D = act16.shape[1]
    tm = min(TN_TM, D)

    def body(a_ref, d_ref, o_ref):
        o_ref[...] = _dot_tn(a_ref[...], d_ref[...])

    return pl.pallas_call(
        body, name="ada_bwd", grid=(L, D // tm),
        in_specs=[pl.BlockSpec((16, tm), lambda l, i: (0, i)), pl.BlockSpec((None, 16, n), lambda l, i: (l, 0, 0))],
        out_specs=pl.BlockSpec((None, tm, n), lambda l, i: (l, i, 0)),
        out_shape=jax.ShapeDtypeStruct((L, D, n), F32), compiler_params=_cp(2),
    )(act16, dmod16)


def _sum_devices(g8, after=()):
    _, R, _ = g8.shape
    tr = _rows_tile(R)

    def body(g_ref, *rest):
        o_ref = rest[len(after)]
        acc = g_ref[0]
        for d in range(1, N_DEV):
            acc = acc + g_ref[d]
        o_ref[...] = acc

    return pl.pallas_call(
        body, name="sum_devices", grid=(R // tr,),
        in_specs=[pl.BlockSpec((N_DEV, tr, LANES), lambda i: (0, i, 0))] + [_ORDER] * len(after),
        out_specs=pl.BlockSpec((tr, LANES), lambda i: (i, 0)),
        out_shape=jax.ShapeDtypeStruct((R, LANES), F32), compiler_params=_cp(1),
    )(g8, *after)


def _rows_tile(r, cap=512):
    for t in (512, 256, 128, 64, 32, 16, 8):
        if t <= cap and r % t == 0:
            return t
    return r


def _adam_math(w, g, m, v):
    m = ADAM_B1 * m + (1.0 - ADAM_B1) * g
    v = ADAM_B2 * v + (1.0 - ADAM_B2) * (g * g)
    m_hat = m / (1.0 - ADAM_B1 ** ADAM_STEP)
    v_hat = v / (1.0 - ADAM_B2 ** ADAM_STEP)
    delta = -ADAM_LR * (m_hat / (jnp.sqrt(v_hat) + ADAM_EPS) + ADAM_WD * w)
    return delta, m, v


def _adamw_dense(w, m, v, g, name):
    R, Cc = w.shape
    tr = _rows_tile(R, ADAM_ROWS)

    def body(w_ref, m_ref, v_ref, g_ref, d_ref, mo_ref, vo_ref):
        d, mn, vn = _adam_math(w_ref[...], g_ref[...], m_ref[...], v_ref[...])
        d_ref[...] = d
        mo_ref[...] = mn
        vo_ref[...] = vn

    spec = pl.BlockSpec((tr, Cc), lambda i: (i, 0))
    return pl.pallas_call(
        body, name=name, grid=(R // tr,), in_specs=[spec] * 4, out_specs=[spec] * 3,
        out_shape=[jax.ShapeDtypeStruct((R, Cc), F32)] * 3, compiler_params=_cp(1),
    )(w, m, v, g)


def _adamw_shard(w, m, v, near, far, layer, prev, name):
    L, r, cc = w.shape
    tr, tc = (_rows_tile(r, ADAM_ROWS), cc) if r % 8 == 0 else (r, _tile(cc, 2 * LANES))

    def body(w_ref, m_ref, v_ref, n_ref, f_ref, *rest):
        g_ref, d_ref, mo_ref, vo_ref = rest[-4:]
        g = n_ref[0].astype(F32) + f_ref[0].astype(F32)
        for k in range(1, N_CHIPS):
            g = g + (n_ref[k].astype(F32) + f_ref[k].astype(F32))
        d, mn, vn = _adam_math(w_ref[...], g, m_ref[...], v_ref[...])
        g_ref[...] = g
        d_ref[...] = d
        mo_ref[...] = mn
        vo_ref[...] = vn

    wspec = pl.BlockSpec((None, tr, tc), lambda i, j: (layer, i, j))
    sspec = pl.BlockSpec((N_CHIPS, tr, tc), lambda i, j: (0, i, j))
    n_prev = 0 if prev is None else 4
    return pl.pallas_call(
        body, name=name, grid=(r // tr, cc // tc),
        in_specs=[wspec] * 3 + [sspec] * 2 + [pl.BlockSpec(memory_space=pl.ANY)] * n_prev,
        out_specs=[wspec] * 4,
        out_shape=[jax.ShapeDtypeStruct((L, r, cc), F32)] * 4,
        input_output_aliases={5 + t: t for t in range(n_prev)},
        compiler_params=_cp(2),
    )(w, m, v, near, far, *(prev or ()))


def _pack(vs):
    flat = jnp.concatenate([v.reshape(-1).astype(F32) for v in vs])
    pad = (-flat.shape[0]) % (64 * LANES)
    return jnp.pad(flat, (0, pad)).reshape(-1, LANES)


def _unpack(packed, shapes):
    flat = packed.reshape(-1)
    out, pos = [], 0
    for s in shapes:
        n = 1
        for d in s:
            n *= d
        out.append(flat[pos:pos + n].reshape(s))
        pos += n
    return out


def kernel(x, c, w_in, b_f, conv_w, conv_b, conv_ln_g, conv_ln_b, w_o, w_ffn_in, w_ffn_out, mix_pre_g, mix_post_g, ffn_pre_g, ffn_post_g, ada_w, ada_b, loss_target, m_w_in, m_b_f, m_conv_w, m_conv_b, m_conv_ln_g, m_conv_ln_b, m_w_o, m_w_ffn_in, m_w_ffn_out, m_mix_pre_g, m_mix_post_g, m_ffn_pre_g, m_ffn_post_g, m_ada_w, m_ada_b, v_w_in, v_b_f, v_conv_w, v_conv_b, v_conv_ln_g, v_conv_ln_b, v_w_o, v_w_ffn_in, v_w_ffn_out, v_mix_pre_g, v_mix_post_g, v_ffn_pre_g, v_ffn_post_g, v_ada_w, v_ada_b):
    params = dict(w_in=w_in, b_f=b_f, conv_w=conv_w, conv_b=conv_b, conv_ln_g=conv_ln_g, conv_ln_b=conv_ln_b, w_o=w_o,
                  w_ffn_in=w_ffn_in, w_ffn_out=w_ffn_out, mix_pre_g=mix_pre_g, mix_post_g=mix_post_g,
                  ffn_pre_g=ffn_pre_g, ffn_post_g=ffn_post_g, ada_w=ada_w, ada_b=ada_b)
    mom = dict(w_in=m_w_in, b_f=m_b_f, conv_w=m_conv_w, conv_b=m_conv_b, conv_ln_g=m_conv_ln_g, conv_ln_b=m_conv_ln_b,
               w_o=m_w_o, w_ffn_in=m_w_ffn_in, w_ffn_out=m_w_ffn_out, mix_pre_g=m_mix_pre_g, mix_post_g=m_mix_post_g,
               ffn_pre_g=m_ffn_pre_g, ffn_post_g=m_ffn_post_g, ada_w=m_ada_w, ada_b=m_ada_b)
    var = dict(w_in=v_w_in, b_f=v_b_f, conv_w=v_conv_w, conv_b=v_conv_b, conv_ln_g=v_conv_ln_g, conv_ln_b=v_conv_ln_b,
               w_o=v_w_o, w_ffn_in=v_w_ffn_in, w_ffn_out=v_w_ffn_out, mix_pre_g=v_mix_pre_g, mix_post_g=v_mix_post_g,
               ffn_pre_g=v_ffn_pre_g, ffn_post_g=v_ffn_post_g, ada_w=v_ada_w, ada_b=v_ada_b)

    S, D = x.shape[1], x.shape[2]
    L = w_in.shape[0]
    A = D // 2
    C = D - A
    H = A // HEAD_DIM
    F = w_ffn_out.shape[1] * N_CHIPS
    d_in = w_in.shape[2] * N_CHIPS
    NP = 3 * A + 2 * C + LANES
    dm = Dims(S=S, D=D, A=A, C=C, H=H, F=F, L=L, NP=NP, ts=min(ROW_TILE, S), tr=min(NORM_TILE, S))
    assert H <= 8 and A == C and d_in == 3 * A + H + 2 * C

    ix, iy, ic = _place()
    chip = 2 * ix + iy
    dev = 4 * ix + 2 * iy + ic
    x2 = x.reshape(S, D)
    tgt = loss_target.reshape(S, D)

    swap = lambda t: jnp.transpose(t, (0, 2, 1))
    w_in_t = swap(w_in)
    big_state = dict(w_in=(w_in_t, swap(m_w_in), swap(v_w_in)), w_o=(w_o, m_w_o, v_w_o),
                     w_ffn_in=(w_ffn_in, m_w_ffn_in, v_w_ffn_in), w_ffn_out=(w_ffn_out, m_w_ffn_out, v_w_ffn_out))

    def gather_start(l, after, after_rest=()):
        first = _exchange_start([w_in_t[l].astype(BF16)], "gather", after, f"gather_a_start_{l}")
        return first, gather_rest(l, [first["token"], *after_rest])

    def gather_rest(l, after):
        return _exchange_start([w_o[l].astype(BF16), w_ffn_in[l].astype(BF16), w_ffn_out[l].astype(BF16)], "gather",
                               after, f"gather_b_start_{l}")

    def gather_wait(st, after, name):
        return _exchange_wait(st, after, name)[1]

    first_0 = _exchange_start([w_in_t[0].astype(BF16)], "gather", [], "gather_a_start_0")

    c_all = _all_gather_devices(c.reshape(D // LANES, LANES), "gather_c", after=[first_0["token"]]).reshape(N_DEV, D)
    c16 = jnp.pad(c_all, ((0, 16 - N_DEV), (0, 0)))
    n_ada = ada_w.shape[2]
    ada_b_cols = lax.dynamic_slice_in_dim(ada_b, chip * n_ada, n_ada, axis=1).reshape(L, 1, n_ada)
    mod_cols, act16 = _ada_fwd(c16, ada_w, ada_b_cols, dm)
    conv_w_all, mod_all = _all_gather_chips([conv_w.reshape(L * CONV_K, -1), mod_cols.reshape(L * 16, n_ada)], "gather_mod")
    cwc = conv_w.shape[2]
    conv_w_full = conv_w_all.reshape(N_CHIPS, L, CONV_K, cwc).transpose(1, 2, 0, 3).reshape(L, CONV_K, C)
    conv_w_full = jnp.pad(conv_w_full, ((0, 0), (0, HALO - CONV_K), (0, 0)))
    mod_all = mod_all.reshape(N_CHIPS, L, 16, n_ada)
    mod_me = lax.dynamic_index_in_dim(mod_all, dev, axis=2, keepdims=False)
    mod_me = mod_me.transpose(1, 0, 2).reshape(L, N_MOD, 1, D)

    gather = [None] * L
    gather[0] = (first_0, gather_rest(0, [first_0["token"], mod_all]))

    def projection_of(g_in):
        w_nat = g_in.reshape(d_in, D)
        return jnp.concatenate([w_nat[:3 * A], w_nat[3 * A + H:], w_nat[3 * A:3 * A + H],
                                jnp.zeros((LANES - H, D), BF16)], axis=0)

    gathered = [None] * L
    vec = lambda p, l: p[l].reshape(1, -1)
    bf_pad = jnp.pad(b_f, ((0, 0), (0, LANES - H)))

    saved = []
    xin = x2
    h = _pre_norm(xin, vec(mix_pre_g, 0), mod_me[0, 1], mod_me[0, 0], dm, after=[gather[0][1]["token"]])
    dx = loss_part = None
    for l in range(L):
        (g_in,) = gather_wait(gather[l][0], [h], f"gather_a_wait_{l}")
        w_p = projection_of(g_in)
        order = [gather[l][1]["token"]]
        if l + 1 < L:
            gather[l + 1] = gather_start(l + 1, [g_in, gather[l][1]["token"]])
            order.append(gather[l + 1][1]["token"])
        qkv = _mm_nt(h, w_p, BF16, "mm_qkv", 0, 3 * A, after=order)
        cf = _mm_nt(h, w_p, F32, "mm_cf", 3 * A, 2 * C + LANES)
        cum = _gates_fwd(cf, vec(bf_pad, l), dm)
        qa, ka, va = _attn_prep(qkv, cum, dm)
        o, lse = _attn_fwd(qa, ka, va, dm)
        u3, u1 = _conv_fwd(cf, conv_w_full[l], vec(conv_b, l), vec(conv_ln_g, l), vec(conv_ln_b, l), dm)
        cat = jnp.concatenate([o, u3], axis=1)
        g_o, wfi, g_fo = gather_wait(gather[l][1], [cat], f"gather_b_wait_{l}")
        wo, wfo = g_o.reshape(D, D), g_fo.reshape(F, D)
        gathered[l] = (w_p, wo, wfi, wfo)
        y = _mm_nn(cat, wo, F32, "mm_o")
        x1, h2 = _res_norm(xin, y, vec(mix_post_g, l), mod_me[l, 2], vec(ffn_pre_g, l), mod_me[l, 4], mod_me[l, 3], dm)
        gu, a = _ffn_in_swiglu(h2, wfi, "mm_ffn_in")
        y2 = _mm_nn(a, wfo, F32, "mm_ffn_out")
        saved.append(dict(xin=xin, h=h, qa=qa, ka=ka, va=va, cf=cf, o=o, lse=lse, u1=u1, cat=cat, y=y,
                          x1=x1, h2=h2, gu=gu, a=a, y2=y2))
        if l + 1 < L:
            xin, h = _res_norm(x1, y2, vec(ffn_post_g, l), mod_me[l, 5], vec(mix_pre_g, l + 1),
                               mod_me[l + 1, 1], mod_me[l + 1, 0], dm)
        else:
            dx, loss_part = _res_loss(x1, y2, vec(ffn_post_g, l), mod_me[l, 5], tgt, dm)
    loss = lax.psum(loss_part[0, 0], ("x", "y", "c"))

    small = [None] * L
    big = [None] * L
    forward = [None] * L

    def forward_start(l, after):
        near = []
        for st, nm in zip(big[l], ("a", "b")):
            near += _exchange_wait(st, after, f"scatter_{nm}_wait_{l}")[1]
        return _exchange_start(near, "sibling", [], f"forward_start_{l}")

    order = []
    for l in reversed(range(L)):
        w_p, wo, wfi, wfo = gathered[l]
        sv = saved[l]
        dy2, d_gfpost, d_g2 = _post_bwd(dx, sv["y2"], vec(ffn_post_g, l), mod_me[l, 5], dm, after=order)
        dgu = _ffn_out_bwd_swiglu(dy2, wfo, sv["gu"], "mm_da")
        g_wfo = _mm_tn(sv["a"], dy2, BF16, "mm_dwfo")
        g_wfi = _ffn_in_bwd_w(sv["h2"], dgu, N_CHIPS, "mm_dwfi")
        dh2 = _ffn_in_bwd_x(dgu, wfi, "mm_dh2")
        scatter_ffn = _exchange_start([g_wfi, g_wfo.reshape(N_CHIPS, F // N_CHIPS, D)], "scatter", [], f"scatter_b_start_{l}")
        dx1, d_sh2, d_sc2, d_gfpre = _pre_bwd(dh2, dx, sv["x1"], vec(ffn_pre_g, l), mod_me[l, 4], dm,
                                              after=[scatter_ffn["token"]])
        dy, d_gpost, d_g1 = _post_bwd(dx1, sv["y"], vec(mix_post_g, l), mod_me[l, 2], dm)
        dcat = _mm_nt(dy, wo, F32, "mm_dcat")
        g_wo = _mm_tn(sv["cat"], dy, BF16, "mm_dwo")
        dcfc, d_cw, d_cb, d_lg, d_lb = _conv_bwd(dcat, sv["u1"], sv["cf"], conv_w_full[l], vec(conv_ln_g, l),
                                                 vec(conv_ln_b, l), dm)
        qb, doa = _attn_prep_bwd(sv["qa"], sv["lse"], dcat, sv["o"], dm)
        dqkv, dcum = _attn_post(*_attn_bwd(qb, sv["ka"], sv["va"], doa, dm), dm)
        dfl, d_bf = _gates_bwd(dcum, sv["cf"], vec(bf_pad, l), dm)
        dproj = jnp.concatenate([dqkv, dcfc, dfl], axis=1)
        dh = _mm_nn(dproj, w_p, F32, "mm_dh")
        g_wp = _mm_tn(dproj, sv["h"], BF16, "mm_dwp")
        dx, d_sh1, d_sc1, d_gpre = _pre_bwd(dh, dx1, sv["xin"], vec(mix_pre_g, l), mod_me[l, 1], dm)
        g_nat = jnp.concatenate([g_wp[:3 * A], g_wp[3 * A + 2 * C:3 * A + 2 * C + H], g_wp[3 * A:3 * A + 2 * C]], axis=0)
        g_win = g_nat.reshape(N_CHIPS, d_in // N_CHIPS, D)
        g_mix = [g_win, g_wo.reshape(N_CHIPS, D // N_CHIPS, D)]
        order = []
        if l > 0:
            scatter_mix = _exchange_start(g_mix, "scatter", [], f"scatter_a_start_{l}")
            order.append(scatter_mix["token"])
            big[l] = (scatter_mix, scatter_ffn)
        if l + 1 < L:
            forward[l + 1] = forward_start(l + 1, [dx])
            order.append(forward[l + 1]["token"])
        small[l] = dict(b_f=d_bf[0, :H], conv_b=d_cb[0], conv_ln_g=d_lg[0], conv_ln_b=d_lb[0], mix_pre_g=d_gpre[0],
                        mix_post_g=d_gpost[0], ffn_pre_g=d_gfpre[0], ffn_post_g=d_gfpost[0],
                        dmod=jnp.concatenate([d_sh1, d_sc1, d_g1, d_sh2, d_sc2, d_g2], axis=1)[0],
                        conv_w=d_cw[:CONV_K])
    grad_x = dx.reshape(1, S, D)

    keys_small = ["b_f", "conv_b", "conv_ln_g", "conv_ln_b", "mix_pre_g", "mix_post_g", "ffn_pre_g", "ffn_post_g",
                  "dmod", "conv_w"]
    stacked = [jnp.stack([small[l][k] for l in range(L)]) for k in keys_small]
    shapes = [s.shape for s in stacked]
    pack = _pack(stacked)
    small_exchange = _exchange_start([pack], "devices", order, "gather_small_start")
    big[0] = (_exchange_start(g_mix, "scatter", [small_exchange["token"]], "scatter_a_start_0"), scatter_ffn)

    names_big = ["w_in", "w_o", "w_ffn_in", "w_ffn_out"]
    res_big = {n: None for n in names_big}

    def update(l, after):
        near, far = _exchange_wait(forward[l], after, f"forward_wait_{l}")
        for t, n in enumerate(names_big):
            res_big[n] = _adamw_shard(*big_state[n], near[t], far[t], l, res_big[n], f"adamw_{n}_{l}")
        return [res_big[names_big[-1]][1]]

    done = [big[0][0]["token"]]
    for l in reversed(range(1, L)):
        done = update(l, done)
    forward[0] = forward_start(0, done)

    (pack8,) = _exchange_wait(small_exchange, [forward[0]["token"]], "gather_small_wait")[1]
    summed = dict(zip(keys_small, _unpack(_sum_devices(pack8), shapes)))
    dmod_all = jnp.stack([_unpack(pack8[d], shapes)[keys_small.index("dmod")] for d in range(N_DEV)])
    grads = {k: summed[k] for k in keys_small[:8]}
    grads["ada_b"] = summed["dmod"]
    grads["conv_w"] = lax.dynamic_slice_in_dim(summed["conv_w"], chip * cwc, cwc, axis=2)

    dmod_cols = lax.dynamic_slice_in_dim(dmod_all.reshape(N_DEV, L, N_CHIPS, n_ada), chip, 1, axis=2)
    dmod16 = jnp.pad(dmod_cols.reshape(N_DEV, L, n_ada).transpose(1, 0, 2), ((0, 0), (0, 16 - N_DEV), (0, 0))).astype(BF16)
    grads["ada_w"] = _ada_bwd(act16, dmod16)

    d_aw, m_aw, v_aw = _adamw_dense(ada_w.reshape(L * D, n_ada), m_ada_w.reshape(L * D, n_ada),
                                    v_ada_w.reshape(L * D, n_ada), grads["ada_w"].reshape(L * D, n_ada), "adamw_ada_w")
    names_small = ["b_f", "conv_w", "conv_b", "conv_ln_g", "conv_ln_b", "mix_pre_g", "mix_post_g", "ffn_pre_g",
                   "ffn_post_g", "ada_b"]
    shapes_small = [params[n].shape for n in names_small]
    d_s, m_s, v_s = _adamw_dense(_pack([params[n] for n in names_small]), _pack([mom[n] for n in names_small]),
                                 _pack([var[n] for n in names_small]), _pack([grads[n] for n in names_small]),
                                 "adamw_small")
    delta_w = dict(zip(names_small, _unpack(d_s, shapes_small)))
    new_m = dict(zip(names_small, _unpack(m_s, shapes_small)))
    new_v = dict(zip(names_small, _unpack(v_s, shapes_small)))
    delta_w["ada_w"], new_m["ada_w"], new_v["ada_w"] = (t.reshape(L, D, n_ada) for t in (d_aw, m_aw, v_aw))

    update(0, [d_s, d_aw])
    res_big["w_in"] = [swap(t) for t in res_big["w_in"]]
    for n in names_big:
        grads[n], delta_w[n], new_m[n], new_v[n] = res_big[n]

    return (loss, grad_x, *[grads[n] for n in WEIGHTS], *[delta_w[n] for n in WEIGHTS],
            *[new_m[n] for n in WEIGHTS], *[new_v[n] for n in WEIGHTS])
```

```python
import collections
import functools

import jax
import jax.numpy as jnp
from jax import lax
from jax.experimental import pallas as pl
from jax.experimental.pallas import tpu as pltpu

F32 = jnp.float32
BF16 = jnp.bfloat16
MESH = pl.DeviceIdType.MESH

HEAD_DIM = 64
CONV_K = 31
N_MOD = 6
EPS = 1e-6
N_CHIPS = 4
N_DEV = 8
LANES = 128
HALO = 32
ROW_TILE = 256
NORM_TILE = 512
MM_TM = 512
MM_TM_PLAIN = 1024
MM_TN_MAX = 1408
ADAM_ROWS = 256
TN_TM = 256
TN_TM_MAX = 384
VMEM_LIMIT = 56 * 1024 * 1024

ADAM_LR = 0.001
ADAM_B1 = 0.9
ADAM_B2 = 0.999
ADAM_EPS = 1e-08
ADAM_WD = 0.01
ADAM_STEP = 10

WEIGHTS = ['w_in', 'b_f', 'conv_w', 'conv_b', 'conv_ln_g', 'conv_ln_b', 'w_o', 'w_ffn_in', 'w_ffn_out',
           'mix_pre_g', 'mix_post_g', 'ffn_pre_g', 'ffn_post_g', 'ada_w', 'ada_b']

Dims = collections.namedtuple("Dims", "S D A C H F L NP ts tr")


def _cp(n_grid=0):
    if n_grid:
        return pltpu.CompilerParams(dimension_semantics=("arbitrary",) * n_grid, vmem_limit_bytes=VMEM_LIMIT)
    return pltpu.CompilerParams(vmem_limit_bytes=VMEM_LIMIT)


def _tile(n, cap, also=None):
    best = None
    t = LANES
    while t <= min(n, cap):
        if n % t == 0 and (also is None or also % t == 0):
            best = t
        t += LANES
    assert best is not None, (n, cap, also)
    return best


def _bf(v):
    return v if v.dtype == BF16 else v.astype(BF16)


def _place():
    return lax.axis_index("x"), lax.axis_index("y"), lax.axis_index("c")


def _flip(v, d):
    return 1 - v if d else v


def _all_gather_devices(a, name, after=()):
    def body(a_ref, *rest):
        o_ref, send, recv, lsem = rest[len(after):]
        x, y, c = _place()
        me = 4 * x + 2 * y + c
        local = pltpu.make_async_copy(a_ref, o_ref.at[me], lsem)
        local.start()
        copies = []
        for k in range(1, N_DEV):
            peer = (_flip(x, (k >> 2) & 1), _flip(y, (k >> 1) & 1), _flip(c, k & 1))
            cp = pltpu.make_async_remote_copy(src_ref=a_ref, dst_ref=o_ref.at[me], send_sem=send.at[k - 1],
                                              recv_sem=recv.at[k - 1], device_id=peer, device_id_type=MESH)
            cp.start()
            copies.append(cp)
        for cp in copies:
            cp.wait()
        local.wait()

    return pl.pallas_call(
        body, name=name,
        out_shape=jax.ShapeDtypeStruct((N_DEV,) + a.shape, a.dtype),
        in_specs=[pl.BlockSpec(memory_space=pl.ANY)] * (1 + len(after)),
        out_specs=pl.BlockSpec(memory_space=pl.ANY),
        scratch_shapes=[pltpu.SemaphoreType.DMA((N_DEV - 1,)), pltpu.SemaphoreType.DMA((N_DEV - 1,)),
                        pltpu.SemaphoreType.DMA],
    )(a, *after)


def _all_gather_chips(arrays, name):
    n = len(arrays)

    def body(*refs):
        a_refs, o_refs = refs[:n], refs[n:2 * n]
        send, recv, lsem = refs[2 * n:]
        x, y, c = _place()
        me = 2 * x + y
        copies = []
        for i in range(n):
            local = pltpu.make_async_copy(a_refs[i], o_refs[i].at[me], lsem.at[i])
            local.start()
            copies.append(local)
            for k in range(1, N_CHIPS):
                peer = (_flip(x, (k >> 1) & 1), _flip(y, k & 1), c)
                cp = pltpu.make_async_remote_copy(src_ref=a_refs[i], dst_ref=o_refs[i].at[me],
                                                  send_sem=send.at[i, k - 1], recv_sem=recv.at[i, k - 1],
                                                  device_id=peer, device_id_type=MESH)
                cp.start()
                copies.append(cp)
        for cp in copies:
            cp.wait()

    return pl.pallas_call(
        body, name=name,
        out_shape=[jax.ShapeDtypeStruct((N_CHIPS,) + a.shape, a.dtype) for a in arrays],
        in_specs=[pl.BlockSpec(memory_space=pl.ANY)] * n,
        out_specs=[pl.BlockSpec(memory_space=pl.ANY)] * n,
        scratch_shapes=[pltpu.SemaphoreType.DMA((n, N_CHIPS - 1)), pltpu.SemaphoreType.DMA((n, N_CHIPS - 1)),
                        pltpu.SemaphoreType.DMA((n,))],
    )(*arrays)


_HBM = pl.BlockSpec(memory_space=pltpu.HBM)
_SEM = pl.BlockSpec(memory_space=pltpu.SEMAPHORE)
_EFFECT = pltpu.SideEffectType.DATAFLOW_SIDE_EFFECTING


def _n_copies(mode):
    return {"sibling": 1, "devices": N_DEV - 1}.get(mode, N_CHIPS - 1)


def _chip_copies(srcs, lands, send, recv, mode):
    x, y, c = _place()
    me = 2 * x + y
    copies = []
    for i in range(len(srcs)):
        if mode == "sibling":
            copies.append(pltpu.make_async_remote_copy(src_ref=srcs[i], dst_ref=lands[i], send_sem=send.at[i],
                                                       recv_sem=recv.at[i], device_id=(x, y, 1 - c), device_id_type=MESH))
            continue
        if mode == "devices":
            for k in range(1, N_DEV):
                peer = (_flip(x, (k >> 2) & 1), _flip(y, (k >> 1) & 1), _flip(c, k & 1))
                s = i * (N_DEV - 1) + k - 1
                copies.append(pltpu.make_async_remote_copy(src_ref=srcs[i], dst_ref=lands[i].at[2 * me + c],
                                                           send_sem=send.at[s], recv_sem=recv.at[s], device_id=peer,
                                                           device_id_type=MESH))
            continue
        for k in range(1, N_CHIPS):
            px, py = _flip(x, (k >> 1) & 1), _flip(y, k & 1)
            src = srcs[i].at[2 * px + py] if mode == "scatter" else srcs[i]
            s = i * (N_CHIPS - 1) + k - 1
            copies.append(pltpu.make_async_remote_copy(src_ref=src, dst_ref=lands[i].at[me], send_sem=send.at[s],
                                                       recv_sem=recv.at[s], device_id=(px, py, c), device_id_type=MESH))
    return copies


def _own_copies(srcs, lands, own, mode):
    x, y, c = _place()
    me = 2 * x + y
    slot = 2 * me + c if mode == "devices" else me
    return [pltpu.make_async_copy(srcs[i].at[me] if mode == "scatter" else srcs[i], lands[i].at[slot], own.at[i])
            for i in range(len(srcs))]


_ORDER = pl.BlockSpec(memory_space=pl.ANY)


def _exchange_start(arrays, mode, after, name):
    n = len(arrays)
    n_sems = 2 if mode == "sibling" else 3

    def body(*refs):
        srcs, lands = refs[:n], refs[n:2 * n]
        sems = refs[2 * n + len(after):2 * n + len(after) + n_sems]
        token = refs[-1]
        for cp in _chip_copies(srcs, lands, sems[0], sems[1], mode):
            cp.start()
        if mode != "sibling":
            for cp in _own_copies(srcs, lands, sems[2], mode):
                cp.start()
        token[...] = jnp.zeros_like(token)

    lead = {"gather": (N_CHIPS,), "devices": (N_DEV,)}.get(mode, ())
    land_shapes = [lead + a.shape for a in arrays]
    n_sem = n * _n_copies(mode)
    sem_shapes = [pltpu.SemaphoreType.DMA((n_sem,)), pltpu.SemaphoreType.DMA((n_sem,)), pltpu.SemaphoreType.DMA((n,))]
    outs = pl.pallas_call(
        body, name=name,
        out_shape=(*sem_shapes[:n_sems],
                   *[pltpu.HBM(a.shape, a.dtype) for a in arrays],
                   *[pltpu.HBM(s, a.dtype) for s, a in zip(land_shapes, arrays)],
                   jax.ShapeDtypeStruct((8, LANES), F32)),
        in_specs=[_HBM] * (2 * n) + [_ORDER] * len(after),
        out_specs=(*[_SEM] * n_sems, *[_HBM] * (2 * n), pl.BlockSpec(memory_space=pltpu.VMEM)),
        input_output_aliases={i: n_sems + i for i in range(2 * n)},
        compiler_params=pltpu.CompilerParams(has_side_effects=_EFFECT),
    )(*[pltpu.with_memory_space_constraint(a, pltpu.HBM) for a in arrays],
      *[pltpu.with_memory_space_constraint(lax.empty(s, a.dtype), pltpu.HBM) for s, a in zip(land_shapes, arrays)],
      *after)
    return dict(sems=outs[:n_sems], srcs=outs[n_sems:n_sems + n], lands=outs[n_sems + n:n_sems + 2 * n], token=outs[-1],
                mode=mode)


def _exchange_wait(st, after, name):
    n = len(st["srcs"])
    mode = st["mode"]
    n_sems = len(st["sems"])

    def body(*refs):
        srcs, lands = refs[:n], refs[n:2 * n]
        sems = refs[2 * n:2 * n + n_sems]
        for cp in _chip_copies(srcs, lands, sems[0], sems[1], mode):
            cp.wait_send()
            cp.wait_recv()
        if mode != "sibling":
            for cp in _own_copies(srcs, lands, sems[2], mode):
                cp.wait()

    outs = pl.pallas_call(
        body, name=name,
        out_shape=tuple(pltpu.HBM(a.shape, a.dtype) for a in (*st["srcs"], *st["lands"])),
        in_specs=[_HBM] * (2 * n) + [_SEM] * n_sems + [_ORDER] * len(after),
        out_specs=tuple([_HBM] * (2 * n)),
        input_output_aliases={i: i for i in range(2 * n)},
        compiler_params=pltpu.CompilerParams(has_side_effects=_EFFECT),
    )(*st["srcs"], *st["lands"], *st["sems"], *after)
    return outs[:n], outs[n:]


def _matmul(a, b, contract, grid, a_spec, b_spec, o_spec, out_shape, name, after=()):
    def body(a_ref, b_ref, *rest):
        o_ref = rest[len(after)]
        r = lax.dot_general(_bf(a_ref[...]), _bf(b_ref[...]), (contract, ((), ())), preferred_element_type=F32)
        o_ref[...] = r.astype(o_ref.dtype)

    return pl.pallas_call(
        body, name=name, grid=grid, in_specs=[a_spec, b_spec] + [_ORDER] * len(after), out_specs=o_spec,
        out_shape=out_shape, compiler_params=_cp(len(grid)),
    )(a, b, *after)


def _mm_nn(a, b, out_dtype, name, col0=0, n=None, after=()):
    m, k = a.shape
    n = b.shape[1] - col0 if n is None else n
    tm = min(MM_TM_PLAIN, m)
    tn = _tile(n, MM_TN_MAX, also=col0 if col0 else None)
    off = col0 // tn
    return _matmul(a, b, ((1,), (0,)), (n // tn, m // tm),
                   pl.BlockSpec((tm, k), lambda j, i: (i, 0)),
                   pl.BlockSpec((k, tn), lambda j, i: (0, j + off)),
                   pl.BlockSpec((tm, tn), lambda j, i: (i, j)),
                   jax.ShapeDtypeStruct((m, n), out_dtype), name, after=after)


def _mm_nt(a, b, out_dtype, name, row0=0, n=None, after=()):
    m, k = a.shape
    n = b.shape[0] - row0 if n is None else n
    tm = min(MM_TM_PLAIN, m)
    tn = _tile(n, MM_TN_MAX, also=row0 if row0 else None)
    off = row0 // tn
    return _matmul(a, b, ((1,), (1,)), (n // tn, m // tm),
                   pl.BlockSpec((tm, k), lambda j, i: (i, 0)),
                   pl.BlockSpec((tn, k), lambda j, i: (j + off, 0)),
                   pl.BlockSpec((tm, tn), lambda j, i: (i, j)),
                   jax.ShapeDtypeStruct((m, n), out_dtype), name, after=after)


def _mm_tn(a, b, out_dtype, name):
    k, m = a.shape
    n = b.shape[1]
    tm = _tile(m, TN_TM_MAX)
    tn = _tile(n, MM_TN_MAX)
    return _matmul(a, b, ((0,), (0,)), (n // tn, m // tm),
                   pl.BlockSpec((k, tm), lambda j, i: (0, i)),
                   pl.BlockSpec((k, tn), lambda j, i: (0, j)),
                   pl.BlockSpec((tm, tn), lambda j, i: (i, j)),
                   jax.ShapeDtypeStruct((m, n), out_dtype), name)


def _ffn_in_swiglu(x, w3, name):
    m, k = x.shape
    half, nb = w3.shape[0] // 2, w3.shape[2]
    tm = min(MM_TM, m)

    def body(x_ref, wg_ref, wu_ref, gu_ref, a_ref):
        xv = x_ref[...]
        g = jnp.dot(xv, wg_ref[...], preferred_element_type=F32)
        u = jnp.dot(xv, wu_ref[...], preferred_element_type=F32)
        gu_ref[0] = g.astype(gu_ref.dtype)
        gu_ref[1] = u.astype(gu_ref.dtype)
        a_ref[...] = (g * _sigmoid(g) * u).astype(a_ref.dtype)

    return pl.pallas_call(
        body, name=name, grid=(half, m // tm),
        in_specs=[pl.BlockSpec((tm, k), lambda j, i: (i, 0)), pl.BlockSpec((None, k, nb), lambda j, i: (j, 0, 0)),
                  pl.BlockSpec((None, k, nb), lambda j, i: (j + half, 0, 0))],
        out_specs=[pl.BlockSpec((2, tm, nb), lambda j, i: (0, i, j)), pl.BlockSpec((tm, nb), lambda j, i: (i, j))],
        out_shape=[jax.ShapeDtypeStruct((2, m, half * nb), BF16), jax.ShapeDtypeStruct((m, half * nb), BF16)],
        compiler_params=_cp(2),
    )(x, w3, w3)


def _ffn_out_bwd_swiglu(dy, w_out, gu, name):
    m, k = dy.shape
    f = w_out.shape[0]
    tm = min(MM_TM, m)
    tn = _tile(f, MM_TN_MAX)

    def body(dy_ref, w_ref, gu_ref, d_ref):
        da = _dot_nt(dy_ref[...], w_ref[...])
        g, u = gu_ref[0].astype(F32), gu_ref[1].astype(F32)
        sg = _sigmoid(g)
        d_ref[0] = (da * u * (sg * (1.0 + g * (1.0 - sg)))).astype(d_ref.dtype)
        d_ref[1] = (da * (g * sg)).astype(d_ref.dtype)

    return pl.pallas_call(
        body, name=name, grid=(f // tn, m // tm),
        in_specs=[pl.BlockSpec((tm, k), lambda j, i: (i, 0)), pl.BlockSpec((tn, k), lambda j, i: (j, 0)),
                  pl.BlockSpec((2, tm, tn), lambda j, i: (0, i, j))],
        out_specs=pl.BlockSpec((2, tm, tn), lambda j, i: (0, i, j)),
        out_shape=jax.ShapeDtypeStruct((2, m, f), BF16), compiler_params=_cp(2),
    )(dy, w_out, gu)


def _ffn_in_bwd_x(dgu, w3, name):
    _, m, f = dgu.shape
    nj, n, nb = w3.shape
    per = f // nb
    tm = min(MM_TM, m)
    tn = _tile(n, 512)

    def body(d_ref, w_ref, o_ref):
        acc = None
        for j in range(nj):
            part = _dot_nt(d_ref[j // per][:, (j % per) * nb:(j % per + 1) * nb], w_ref[j])
            acc = part if acc is None else acc + part
        o_ref[...] = acc.astype(o_ref.dtype)

    return pl.pallas_call(
        body, name=name, grid=(m // tm, n // tn),
        in_specs=[pl.BlockSpec((2, tm, f), lambda i, j: (0, i, 0)), pl.BlockSpec((nj, tn, nb), lambda i, j: (0, j, 0))],
        out_specs=pl.BlockSpec((tm, tn), lambda i, j: (i, j)),
        out_shape=jax.ShapeDtypeStruct((m, n), F32), compiler_params=_cp(2),
    )(dgu, w3)


def _ffn_in_bwd_w(x, dgu, nj, name):
    k, m = x.shape
    f = dgu.shape[2]
    per = nj // 2
    nb = f // per
    tm = min(TN_TM, m)
    return _matmul(x, dgu, ((0,), (0,)), (nj, m // tm),
                   pl.BlockSpec((k, tm), lambda j, i: (0, i)),
                   pl.BlockSpec((None, k, nb), lambda j, i: (j // per, 0, j % per)),
                   pl.BlockSpec((None, tm, nb), lambda j, i: (j, i, 0)),
                   jax.ShapeDtypeStruct((nj, m, nb), BF16), name)


def _vec_spec(d):
    return pl.BlockSpec((1, d), lambda i: (0, 0))


def _row_spec(ts, d, col=0):
    return pl.BlockSpec((ts, d), lambda i: (i, col))


def _rms(x):
    return lax.rsqrt(jnp.mean(x * x, axis=-1, keepdims=True) + EPS)


def _pre_norm(x, gain, scale, shift, dm, after=()):
    def body(x_ref, g_ref, sc_ref, sh_ref, *rest):
        h_ref = rest[len(after)]
        xv = x_ref[...]
        h_ref[...] = (((xv * _rms(xv)) * g_ref[...]) * (1.0 + sc_ref[...]) + sh_ref[...]).astype(h_ref.dtype)

    return pl.pallas_call(
        body, name="pre_norm", grid=(dm.S // dm.tr,),
        in_specs=[_row_spec(dm.tr, dm.D)] + [_vec_spec(dm.D)] * 3 + [_ORDER] * len(after),
        out_specs=_row_spec(dm.tr, dm.D),
        out_shape=jax.ShapeDtypeStruct((dm.S, dm.D), BF16), compiler_params=_cp(1),
    )(x, gain, scale, shift, *after)


def _res_norm(x, y, gpost, gate, gain, scale, shift, dm):
    def body(x_ref, y_ref, gp_ref, gt_ref, g_ref, sc_ref, sh_ref, xo_ref, h_ref):
        yv = y_ref[...]
        xn = x_ref[...] + gt_ref[...] * ((yv * _rms(yv)) * gp_ref[...])
        xo_ref[...] = xn
        h_ref[...] = (((xn * _rms(xn)) * g_ref[...]) * (1.0 + sc_ref[...]) + sh_ref[...]).astype(h_ref.dtype)

    return pl.pallas_call(
        body, name="res_norm", grid=(dm.S // dm.tr,),
        in_specs=[_row_spec(dm.tr, dm.D)] * 2 + [_vec_spec(dm.D)] * 5,
        out_specs=[_row_spec(dm.tr, dm.D)] * 2,
        out_shape=[jax.ShapeDtypeStruct((dm.S, dm.D), F32), jax.ShapeDtypeStruct((dm.S, dm.D), BF16)],
        compiler_params=_cp(1),
    )(x, y, gpost, gate, gain, scale, shift)


def _res_loss(x, y, gpost, gate, target, dm):
    def body(x_ref, y_ref, gp_ref, gt_ref, t_ref, dx_ref, loss_ref):
        i = pl.program_id(0)

        @pl.when(i == 0)
        def _():
            loss_ref[...] = jnp.zeros_like(loss_ref)
        yv = y_ref[...]
        err = x_ref[...] + gt_ref[...] * ((yv * _rms(yv)) * gp_ref[...]) - t_ref[...]
        dx_ref[...] = err * (1.0 / dm.D)
        per_row = jnp.mean(err * err, axis=-1, keepdims=True)
        loss_ref[...] += 0.5 * jnp.sum(per_row, axis=0, keepdims=True)

    return pl.pallas_call(
        body, name="res_loss", grid=(dm.S // dm.tr,),
        in_specs=[_row_spec(dm.tr, dm.D)] * 2 + [_vec_spec(dm.D)] * 2 + [_row_spec(dm.tr, dm.D)],
        out_specs=[_row_spec(dm.tr, dm.D), _vec_spec(LANES)],
        out_shape=[jax.ShapeDtypeStruct((dm.S, dm.D), F32), jax.ShapeDtypeStruct((1, LANES), F32)],
        compiler_params=_cp(1),
    )(x, y, gpost, gate, target)


def _post_bwd(dxo, y, gpost, gate, dm, after=()):
    def body(dx_ref, y_ref, gp_ref, gt_ref, *rest):
        dy_ref, dgp_ref, dgt_ref = rest[len(after):]
        i = pl.program_id(0)

        @pl.when(i == 0)
        def _():
            dgp_ref[...] = jnp.zeros_like(dgp_ref)
            dgt_ref[...] = jnp.zeros_like(dgt_ref)
        yv, dx = y_ref[...], dx_ref[...]
        r = _rms(yv)
        t = yv * r
        dgp_ref[...] += jnp.sum(dx * gt_ref[...] * t, axis=0, keepdims=True)
        dgt_ref[...] += jnp.sum(dx * (t * gp_ref[...]), axis=0, keepdims=True)
        dt = dx * (gt_ref[...] * gp_ref[...])
        dy_ref[...] = (r * (dt - t * jnp.mean(dt * t, axis=-1, keepdims=True))).astype(dy_ref.dtype)

    return pl.pallas_call(
        body, name="post_bwd", grid=(dm.S // dm.tr,),
        in_specs=[_row_spec(dm.tr, dm.D)] * 2 + [_vec_spec(dm.D)] * 2 + [_ORDER] * len(after),
        out_specs=[_row_spec(dm.tr, dm.D), _vec_spec(dm.D), _vec_spec(dm.D)],
        out_shape=[jax.ShapeDtypeStruct((dm.S, dm.D), BF16)] + [jax.ShapeDtypeStruct((1, dm.D), F32)] * 2,
        compiler_params=_cp(1),
    )(dxo, y, gpost, gate, *after)


def _pre_bwd(dh, dxo, x, gain, scale, dm, after=()):
    def body(dh_ref, dxo_ref, x_ref, g_ref, sc_ref, *rest):
        dx_ref, dsh_ref, dsc_ref, dg_ref = rest[len(after):]
        i = pl.program_id(0)

        @pl.when(i == 0)
        def _():
            dsh_ref[...] = jnp.zeros_like(dsh_ref)
            dsc_ref[...] = jnp.zeros_like(dsc_ref)
            dg_ref[...] = jnp.zeros_like(dg_ref)
        xv, dh_ = x_ref[...], dh_ref[...]
        r = _rms(xv)
        nrm = xv * r
        one_sc = 1.0 + sc_ref[...]
        dsh_ref[...] += jnp.sum(dh_, axis=0, keepdims=True)
        dsc_ref[...] += jnp.sum(dh_ * (nrm * g_ref[...]), axis=0, keepdims=True)
        dg_ref[...] += jnp.sum(dh_ * nrm * one_sc, axis=0, keepdims=True)
        dn = dh_ * (g_ref[...] * one_sc)
        dx_ref[...] = dxo_ref[...] + r * (dn - nrm * jnp.mean(dn * nrm, axis=-1, keepdims=True))

    return pl.pallas_call(
        body, name="pre_bwd", grid=(dm.S // dm.tr,),
        in_specs=[_row_spec(dm.tr, dm.D)] * 3 + [_vec_spec(dm.D)] * 2 + [_ORDER] * len(after),
        out_specs=[_row_spec(dm.tr, dm.D)] + [_vec_spec(dm.D)] * 3,
        out_shape=[jax.ShapeDtypeStruct((dm.S, dm.D), F32)] + [jax.ShapeDtypeStruct((1, dm.D), F32)] * 3,
        compiler_params=_cp(1),
    )(dh, dxo, x, gain, scale, *after)


def _sigmoid(z):
    return 1.0 / (1.0 + jnp.exp(-z))


def _tri(n, upper):
    r = lax.broadcasted_iota(jnp.int32, (n, n), 0)
    c = lax.broadcasted_iota(jnp.int32, (n, n), 1)
    return (c >= r if upper else r >= c).astype(F32)


def _gates_fwd(cf, bf, dm):
    ts = dm.ts
    fcol = 2 * dm.C // LANES

    def body(f_ref, b_ref, cum_ref, carry):
        i = pl.program_id(0)

        @pl.when(i == 0)
        def _():
            carry[...] = jnp.zeros_like(carry)
        z = f_ref[...] + b_ref[...]
        lf = jnp.minimum(z, 0.0) - jnp.log(1.0 + jnp.exp(-jnp.abs(z)))
        cs = jnp.dot(_tri(ts, False), lf, precision=lax.Precision.HIGHEST, preferred_element_type=F32) + carry[...]
        cum_ref[...] = cs
        carry[...] = cs[ts - 1:ts, :]

    return pl.pallas_call(
        body, name="gates_fwd", grid=(dm.S // ts,),
        in_specs=[pl.BlockSpec((ts, LANES), lambda i: (i, fcol)), _vec_spec(LANES)],
        out_specs=_row_spec(ts, LANES),
        out_shape=jax.ShapeDtypeStruct((dm.S, LANES), F32),
        scratch_shapes=[pltpu.VMEM((1, LANES), F32)], compiler_params=_cp(1),
    )(cf, bf)


def _gates_bwd(dc, cf, bf, dm):
    ts = dm.ts
    nb = dm.S // ts
    fcol = 2 * dm.C // LANES

    def body(dc_ref, f_ref, b_ref, df_ref, db_ref, carry):
        i = pl.program_id(0)

        @pl.when(i == 0)
        def _():
            carry[...] = jnp.zeros_like(carry)
            db_ref[...] = jnp.zeros_like(db_ref)
        dlf = jnp.dot(_tri(ts, True), dc_ref[...], precision=lax.Precision.HIGHEST, preferred_element_type=F32) + carry[...]
        carry[...] = dlf[0:1, :]
        dz = dlf * (1.0 - _sigmoid(f_ref[...] + b_ref[...]))
        df_ref[...] = dz.astype(df_ref.dtype)
        db_ref[...] += jnp.sum(dz, axis=0, keepdims=True)

    return pl.pallas_call(
        body, name="gates_bwd", grid=(nb,),
        in_specs=[pl.BlockSpec((ts, LANES), lambda i: (nb - 1 - i, 0)),
                  pl.BlockSpec((ts, LANES), lambda i: (nb - 1 - i, fcol)), _vec_spec(LANES)],
        out_specs=[pl.BlockSpec((ts, LANES), lambda i: (nb - 1 - i, 0)), _vec_spec(LANES)],
        out_shape=[jax.ShapeDtypeStruct((dm.S, LANES), BF16), jax.ShapeDtypeStruct((1, LANES), F32)],
        scratch_shapes=[pltpu.VMEM((1, LANES), F32)], compiler_params=_cp(1),
    )(dc, cf, bf)


def _dot_nt(a, b):
    return lax.dot_general(a, b, (((1,), (1,)), ((), ())), preferred_element_type=F32)


def _dot_tn(a, b):
    return lax.dot_general(a, b, (((0,), (0,)), ((), ())), preferred_element_type=F32)


_C0, _C1, _C2 = HEAD_DIM, HEAD_DIM + 3, HEAD_DIM + 6


def _split3(c):
    hi = c.astype(BF16).astype(F32)
    mid = (c - hi).astype(BF16).astype(F32)
    return hi, mid, c - hi - mid


def _put3(base, lane, start, pieces, sign=1.0):
    out = base
    for t, piece in enumerate(pieces):
        out = jnp.where(lane == start + t, sign * piece, out)
    return out


def _head_lanes(pair, odd):
    v = pair.astype(F32)
    return pltpu.roll(v, HEAD_DIM, axis=1) if odd else v


def _attn_prep(qkv, cum, dm):
    ts, A, H = dm.ts, dm.A, dm.H
    scale = HEAD_DIM ** -0.5

    def body(x_ref, c_ref, qa_ref, ka_ref, va_ref):
        lane = lax.broadcasted_iota(jnp.int32, (ts, LANES), 1)
        data = lane < HEAD_DIM
        for h in range(H):
            e, odd = h // 2, h % 2
            pieces = _split3(c_ref[:, h:h + 1])
            q = _head_lanes(x_ref[:, e * LANES:(e + 1) * LANES], odd) * scale
            k = _head_lanes(x_ref[:, A + e * LANES:A + (e + 1) * LANES], odd)
            v = _head_lanes(x_ref[:, 2 * A + e * LANES:2 * A + (e + 1) * LANES], odd)
            qa = jnp.where(data, q, jnp.where((lane >= _C1) & (lane < _C2), 1.0, 0.0))
            qa_ref[h] = _put3(qa, lane, _C0, pieces).astype(BF16)
            ka = jnp.where(data, k, jnp.where((lane < _C1) | ((lane >= _C2) & (lane < _C2 + 3)), 1.0, 0.0))
            ka_ref[h] = _put3(ka, lane, _C1, pieces, -1.0).astype(BF16)
            va_ref[h] = jnp.where(data, v, jnp.where(lane < _C1, 1.0, 0.0)).astype(BF16)

    spec = pl.BlockSpec((H, ts, LANES), lambda i: (0, i, 0))
    return pl.pallas_call(
        body, name="attn_prep", grid=(dm.S // ts,),
        in_specs=[_row_spec(ts, 3 * A), _row_spec(ts, LANES)], out_specs=[spec] * 3,
        out_shape=[jax.ShapeDtypeStruct((H, dm.S, LANES), BF16)] * 3, compiler_params=_cp(1),
    )(qkv, cum)


def _attn_fwd(qa, ka, va, dm):
    tq, A, H, S = dm.ts, dm.A, dm.H, dm.S
    nq = S // tq

    def body(qa_ref, ka_ref, va_ref, o_ref, lse_ref, top_scr, qb_scr, m_scr, acc_scr):
        i = pl.program_id(0)
        row = lax.broadcasted_iota(jnp.int32, (tq, tq), 0)
        col = lax.broadcasted_iota(jnp.int32, (tq, tq), 1)
        lane = lax.broadcasted_iota(jnp.int32, (tq, LANES), 1)

        def logits(q_ref, j, h, diagonal):
            s = _dot_nt(q_ref[h], ka_ref[h, pl.ds(pl.multiple_of(j * tq, tq), tq), :])
            return jnp.where(row >= col, s, -1e30) if diagonal else s

        def maxima(j, carry):
            for h in range(H):
                top_scr[h] = jnp.maximum(top_scr[h], logits(qa_ref, j, h, False))
            return carry

        for h in range(H):
            top_scr[h] = logits(qa_ref, i, h, True)
        lax.fori_loop(0, i, maxima, 0)
        for h in range(H):
            m = jnp.max(top_scr[h], axis=1, keepdims=True)
            m_scr[h] = m
            qb_scr[h] = _put3(qa_ref[h].astype(F32), lane, _C2, _split3(m), -1.0).astype(BF16)

        def weigh(j, carry):
            rows = pl.ds(pl.multiple_of(j * tq, tq), tq)
            for h in range(H):
                p = jnp.exp(logits(qb_scr, j, h, False)).astype(BF16)
                acc_scr[h] += jnp.dot(p, va_ref[h, rows, :], preferred_element_type=F32)
            return carry

        for h in range(H):
            p = jnp.exp(logits(qb_scr, i, h, True)).astype(BF16)
            acc_scr[h] = jnp.dot(p, va_ref[h, pl.ds(pl.multiple_of(i * tq, tq), tq), :], preferred_element_type=F32)
        lax.fori_loop(0, i, weigh, 0)
        lse_all = jnp.zeros((tq, LANES), F32)
        for h in range(H):
            acc = acc_scr[h]
            l = acc[:, _C0:_C0 + 1]
            o_ref[:, h * HEAD_DIM:(h + 1) * HEAD_DIM] = (acc[:, :HEAD_DIM] / l).astype(o_ref.dtype)
            lse_all = jnp.where(lane == h, m_scr[h] + jnp.log(l), lse_all)
        lse_ref[...] = lse_all

    full = pl.BlockSpec((H, S, LANES), lambda i: (0, 0, 0))
    return pl.pallas_call(
        body, name="attn_fwd", grid=(nq,),
        in_specs=[pl.BlockSpec((H, tq, LANES), lambda i: (0, i, 0)), full, full],
        out_specs=[pl.BlockSpec((tq, A), lambda i: (i, 0)), _row_spec(tq, LANES)],
        out_shape=[jax.ShapeDtypeStruct((S, A), BF16), jax.ShapeDtypeStruct((S, LANES), F32)],
        scratch_shapes=[pltpu.VMEM((H, tq, tq), F32), pltpu.VMEM((H, tq, LANES), BF16), pltpu.VMEM((H, tq, 1), F32),
                        pltpu.VMEM((H, tq, LANES), F32)],
        compiler_params=_cp(1),
    )(qa, ka, va)


def _attn_prep_bwd(qa, lse, dcat, o, dm):
    ts, A, H = dm.ts, dm.A, dm.H

    def body(qa_ref, lse_ref, do_ref, o_ref, qb_ref, doa_ref):
        lane = lax.broadcasted_iota(jnp.int32, (ts, LANES), 1)
        data = lane < HEAD_DIM
        for h in range(H):
            e, odd = h // 2, h % 2
            qb_ref[h] = _put3(qa_ref[h].astype(F32), lane, _C2, _split3(lse_ref[:, h:h + 1]), -1.0).astype(BF16)
            do_pair = do_ref[:, e * LANES:(e + 1) * LANES]
            prod = do_pair * o_ref[:, e * LANES:(e + 1) * LANES].astype(F32)
            mine = (lane >= HEAD_DIM) if odd else data
            delta = jnp.sum(jnp.where(mine, prod, 0.0), axis=1, keepdims=True)
            doa = jnp.where(data, _head_lanes(do_pair, odd), 0.0)
            doa_ref[h] = _put3(doa, lane, _C0, _split3(delta), -1.0).astype(BF16)

    spec = pl.BlockSpec((H, ts, LANES), lambda i: (0, i, 0))
    return pl.pallas_call(
        body, name="attn_prep_bwd", grid=(dm.S // ts,),
        in_specs=[spec, _row_spec(ts, LANES), _row_spec(ts, A, 0), _row_spec(ts, A)], out_specs=[spec] * 2,
        out_shape=[jax.ShapeDtypeStruct((H, dm.S, LANES), BF16)] * 2, compiler_params=_cp(1),
    )(qa, lse, dcat, o)


def _attn_bwd(qb, ka, va, doa, dm):
    tq, H, S = dm.ts, dm.H, dm.S
    nq = S // tq

    def body(ka_ref, va_ref, qb_ref, doa_ref, dq_ref, dk_ref, dv_ref):
        j = pl.program_id(0)

        @pl.when(j == 0)
        def _():
            dq_ref[...] = jnp.zeros(dq_ref.shape, F32)
        dk_ref[...] = jnp.zeros(dk_ref.shape, F32)
        dv_ref[...] = jnp.zeros(dv_ref.shape, F32)
        row = lax.broadcasted_iota(jnp.int32, (tq, tq), 0)
        col = lax.broadcasted_iota(jnp.int32, (tq, tq), 1)

        def block(i, masked):
            rows = pl.ds(pl.multiple_of(i * tq, tq), tq)
            for h in range(H):
                q, do_ = qb_ref[h, rows, :], doa_ref[h, rows, :]
                k, v = ka_ref[h], va_ref[h]
                s = _dot_nt(q, k)
                if masked:
                    s = jnp.where(row >= col, s, -1e30)
                p = jnp.exp(s)
                dsb = (p * _dot_nt(do_, v)).astype(BF16)
                dv_ref[h] += _dot_tn(p.astype(BF16), do_)
                dk_ref[h] += _dot_tn(dsb, q)
                dq_ref[h, rows, :] += jnp.dot(dsb, k, preferred_element_type=F32)

        block(j, True)

        def step(i, carry):
            block(i, False)
            return carry

        lax.fori_loop(j + 1, nq, step, 0)

    blk = pl.BlockSpec((H, tq, LANES), lambda j: (0, j, 0))
    full = pl.BlockSpec((H, S, LANES), lambda j: (0, 0, 0))
    return pl.pallas_call(
        body, name="attn_bwd", grid=(nq,),
        in_specs=[blk, blk, full, full], out_specs=[full, blk, blk],
        out_shape=[jax.ShapeDtypeStruct((H, S, LANES), F32)] * 3, compiler_params=_cp(1),
    )(ka, va, qb, doa)


def _attn_post(dqa, dka, dva, dm):
    ts, A, H = dm.ts, dm.A, dm.H
    scale = HEAD_DIM ** -0.5

    def body(dq_ref, dk_ref, dv_ref, o_ref, dc_ref):
        lane = lax.broadcasted_iota(jnp.int32, (ts, LANES), 1)
        data = lane < HEAD_DIM
        dc = jnp.zeros((ts, LANES), F32)
        for h in range(H):
            dc = jnp.where(lane == h, dq_ref[h][:, _C0:_C0 + 1] - dk_ref[h][:, _C1:_C1 + 1], dc)
        dc_ref[...] = dc
        for part, (ref, mul) in enumerate(((dq_ref, scale), (dk_ref, 1.0), (dv_ref, 1.0))):
            for e in range(H // 2):
                pair = jnp.where(data, ref[2 * e], pltpu.roll(ref[2 * e + 1], HEAD_DIM, axis=1))
                o_ref[:, part * A + e * LANES:part * A + (e + 1) * LANES] = (pair * mul).astype(o_ref.dtype)

    spec = pl.BlockSpec((H, ts, LANES), lambda i: (0, i, 0))
    return pl.pallas_call(
        body, name="attn_post", grid=(dm.S // ts,),
        in_specs=[spec] * 3, out_specs=[_row_spec(ts, 3 * A), _row_spec(ts, LANES)],
        out_shape=[jax.ShapeDtypeStruct((dm.S, 3 * A), BF16), jax.ShapeDtypeStruct((dm.S, LANES), F32)],
        compiler_params=_cp(1),
    )(dqa, dka, dva)


def _glu(cf_rows, c):
    return cf_rows[:, :c] * _sigmoid(cf_rows[:, c:2 * c])


def _conv_fwd(cf, cw, cb, lg, lb, dm):
    ts, C = dm.ts, dm.C
    per = ts // HALO

    def body(cf_ref, halo_ref, w_ref, cb_ref, lg_ref, lb_ref, u3_ref, u1_ref):
        i = pl.program_id(0)
        prev = jnp.where(i > 0, _glu(halo_ref[...], C), 0.0)
        win = jnp.concatenate([prev, _glu(cf_ref[...], C)], axis=0)
        u1 = jnp.zeros((ts, C), F32) + cb_ref[...]
        off = HALO - (CONV_K - 1)
        for k in range(CONV_K):
            u1 = u1 + w_ref[k:k + 1, :] * win[off + k:off + k + ts, :]
        u1_ref[...] = u1
        mu = jnp.mean(u1, axis=-1, keepdims=True)
        cen = u1 - mu
        rstd = lax.rsqrt(jnp.mean(cen * cen, axis=-1, keepdims=True) + EPS)
        u2 = cen * rstd * lg_ref[...] + lb_ref[...]
        u3_ref[...] = (u2 * _sigmoid(u2)).astype(u3_ref.dtype)

    return pl.pallas_call(
        body, name="conv_fwd", grid=(dm.S // ts,),
        in_specs=[pl.BlockSpec((ts, 2 * C), lambda i: (i, 0)),
                  pl.BlockSpec((HALO, 2 * C), lambda i: (jnp.maximum(i * per - 1, 0), 0)),
                  pl.BlockSpec((HALO, C), lambda i: (0, 0))] + [_vec_spec(C)] * 3,
        out_specs=[_row_spec(ts, C)] * 2,
        out_shape=[jax.ShapeDtypeStruct((dm.S, C), BF16), jax.ShapeDtypeStruct((dm.S, C), F32)],
        compiler_params=_cp(1),
    )(cf, cf, cw, cb, lg, lb)


def _conv_bwd(dcat, u1, cf, cw, lg, lb, dm):
    ts, C = dm.ts, dm.C
    per = ts // HALO
    nt = dm.S // ts
    last_halo = dm.S // HALO - 1

    def ln_bwd(du3, u1v, lg_v, lb_v):
        mu = jnp.mean(u1v, axis=-1, keepdims=True)
        cen = u1v - mu
        rstd = lax.rsqrt(jnp.mean(cen * cen, axis=-1, keepdims=True) + EPS)
        uhat = cen * rstd
        u2 = uhat * lg_v + lb_v
        sg = _sigmoid(u2)
        du2 = du3 * (sg * (1.0 + u2 * (1.0 - sg)))
        duh = du2 * lg_v
        du1 = rstd * (duh - jnp.mean(duh, axis=-1, keepdims=True) - uhat * jnp.mean(duh * uhat, axis=-1, keepdims=True))
        return du1, du2, uhat

    def body(d_ref, dn_ref, u1_ref, u1n_ref, cf_ref, halo_ref, w_ref, lg_ref, lb_ref,
             dcf_ref, dw_ref, dcb_ref, dlg_ref, dlb_ref):
        i = pl.program_id(0)

        @pl.when(i == 0)
        def _():
            dw_ref[...] = jnp.zeros_like(dw_ref)
            dcb_ref[...] = jnp.zeros_like(dcb_ref)
            dlg_ref[...] = jnp.zeros_like(dlg_ref)
            dlb_ref[...] = jnp.zeros_like(dlb_ref)
        lg_v, lb_v = lg_ref[...], lb_ref[...]
        du1, du2, uhat = ln_bwd(d_ref[...], u1_ref[...], lg_v, lb_v)
        du1n, _, _ = ln_bwd(dn_ref[...], u1n_ref[...], lg_v, lb_v)
        du1n = jnp.where(i < nt - 1, du1n, 0.0)
        dlg_ref[...] += jnp.sum(du2 * uhat, axis=0, keepdims=True)
        dlb_ref[...] += jnp.sum(du2, axis=0, keepdims=True)
        dcb_ref[...] += jnp.sum(du1, axis=0, keepdims=True)
        dwin = jnp.concatenate([du1, du1n], axis=0)
        cfv = cf_ref[...]
        cv, sg = cfv[:, :C], _sigmoid(cfv[:, C:2 * C])
        prev = jnp.where(i > 0, _glu(halo_ref[...], C), 0.0)
        uwin = jnp.concatenate([prev, cv * sg], axis=0)
        du0 = jnp.zeros((ts, C), F32)
        off = HALO - (CONV_K - 1)
        for k in range(CONV_K):
            back = CONV_K - 1 - k
            du0 = du0 + w_ref[k:k + 1, :] * dwin[back:back + ts, :]
            dw_ref[k:k + 1, :] += jnp.sum(du1 * uwin[off + k:off + k + ts, :], axis=0, keepdims=True)
        dcf_ref[:, :C] = (du0 * sg).astype(dcf_ref.dtype)
        dcf_ref[:, C:] = (du0 * cv * sg * (1.0 - sg)).astype(dcf_ref.dtype)

    ucol = dm.A // C
    return pl.pallas_call(
        body, name="conv_bwd", grid=(nt,),
        in_specs=[pl.BlockSpec((ts, C), lambda i: (i, ucol)),
                  pl.BlockSpec((HALO, C), lambda i: (jnp.minimum((i + 1) * per, last_halo), ucol)),
                  pl.BlockSpec((ts, C), lambda i: (i, 0)),
                  pl.BlockSpec((HALO, C), lambda i: (jnp.minimum((i + 1) * per, last_halo), 0)),
                  pl.BlockSpec((ts, 2 * C), lambda i: (i, 0)),
                  pl.BlockSpec((HALO, 2 * C), lambda i: (jnp.maximum(i * per - 1, 0), 0)),
                  pl.BlockSpec((HALO, C), lambda i: (0, 0)), _vec_spec(C), _vec_spec(C)],
        out_specs=[_row_spec(ts, 2 * C), pl.BlockSpec((HALO, C), lambda i: (0, 0))] + [_vec_spec(C)] * 3,
        out_shape=[jax.ShapeDtypeStruct((dm.S, 2 * C), BF16), jax.ShapeDtypeStruct((HALO, C), F32)]
        + [jax.ShapeDtypeStruct((1, C), F32)] * 3,
        compiler_params=_cp(1),
    )(dcat, dcat, u1, u1, cf, cf, cw, lg, lb)


def _ada_fwd(c16, ada_w, ada_b_cols, dm):
    L, D, n = ada_w.shape
    tn = _tile(n, 512)

    def body(c_ref, w_ref, b_ref, o_ref, a_ref):
        cv = c_ref[...]
        act = (cv * _sigmoid(cv)).astype(BF16)
        a_ref[...] = act
        o_ref[...] = jnp.dot(act, w_ref[...].astype(BF16), preferred_element_type=F32) + b_ref[...]

    return pl.pallas_call(
        body, name="ada_fwd", grid=(L, n // tn),
        in_specs=[pl.BlockSpec((16, D), lambda l, j: (0, 0)), pl.BlockSpec((None, D, tn), lambda l, j: (l, 0, j)),
                  pl.BlockSpec((None, 1, tn), lambda l, j: (l, 0, j))],
        out_specs=[pl.BlockSpec((None, 16, tn), lambda l, j: (l, 0, j)), pl.BlockSpec((16, D), lambda l, j: (0, 0))],
        out_shape=[jax.ShapeDtypeStruct((L, 16, n), F32), jax.ShapeDtypeStruct((16, D), BF16)],
        compiler_params=_cp(2),
    )(c16, ada_w, ada_b_cols)


def _ada_bwd(act16, dmod16):
    L, _, n = dmod16.shape
    D = act16.shape[1]
    tm = min(TN_TM, D)

    def body(a_ref, d_ref, o_ref):
        o_ref[...] = _dot_tn(a_ref[...], d_ref[...])

    return pl.pallas_call(
        body, name="ada_bwd", grid=(L, D // tm),
        in_specs=[pl.BlockSpec((16, tm), lambda l, i: (0, i)), pl.BlockSpec((None, 16, n), lambda l, i: (l, 0, 0))],
        out_specs=pl.BlockSpec((None, tm, n), lambda l, i: (l, i, 0)),
        out_shape=jax.ShapeDtypeStruct((L, D, n), F32), compiler_params=_cp(2),
    )(act16, dmod16)


def _sum_devices(g8, after=()):
    _, R, _ = g8.shape
    tr = _rows_tile(R)

    def body(g_ref, *rest):
        o_ref = rest[len(after)]
        acc = g_ref[0]
        for d in range(1, N_DEV):
            acc = acc + g_ref[d]
        o_ref[...] = acc

    return pl.pallas_call(
        body, name="sum_devices", grid=(R // tr,),
        in_specs=[pl.BlockSpec((N_DEV, tr, LANES), lambda i: (0, i, 0))] + [_ORDER] * len(after),
        out_specs=pl.BlockSpec((tr, LANES), lambda i: (i, 0)),
        out_shape=jax.ShapeDtypeStruct((R, LANES), F32), compiler_params=_cp(1),
    )(g8, *after)


def _rows_tile(r, cap=512):
    for t in (512, 256, 128, 64, 32, 16, 8):
        if t <= cap and r % t == 0:
            return t
    return r


def _adam_math(w, g, m, v):
    m = ADAM_B1 * m + (1.0 - ADAM_B1) * g
    v = ADAM_B2 * v + (1.0 - ADAM_B2) * (g * g)
    m_hat = m / (1.0 - ADAM_B1 ** ADAM_STEP)
    v_hat = v / (1.0 - ADAM_B2 ** ADAM_STEP)
    delta = -ADAM_LR * (m_hat / (jnp.sqrt(v_hat) + ADAM_EPS) + ADAM_WD * w)
    return delta, m, v


def _adamw_dense(w, m, v, g, name):
    R, Cc = w.shape
    tr = _rows_tile(R, ADAM_ROWS)

    def body(w_ref, m_ref, v_ref, g_ref, d_ref, mo_ref, vo_ref):
        d, mn, vn = _adam_math(w_ref[...], g_ref[...], m_ref[...], v_ref[...])
        d_ref[...] = d
        mo_ref[...] = mn
        vo_ref[...] = vn

    spec = pl.BlockSpec((tr, Cc), lambda i: (i, 0))
    return pl.pallas_call(
        body, name=name, grid=(R // tr,), in_specs=[spec] * 4, out_specs=[spec] * 3,
        out_shape=[jax.ShapeDtypeStruct((R, Cc), F32)] * 3, compiler_params=_cp(1),
    )(w, m, v, g)


def _adamw_shard(w, m, v, near, far, layer, prev, name):
    L, r, cc = w.shape
    tr, tc = (_rows_tile(r, ADAM_ROWS), cc) if r % 8 == 0 else (r, _tile(cc, 2 * LANES))

    def body(w_ref, m_ref, v_ref, n_ref, f_ref, *rest):
        g_ref, d_ref, mo_ref, vo_ref = rest[-4:]
        g = n_ref[0].astype(F32) + f_ref[0].astype(F32)
        for k in range(1, N_CHIPS):
            g = g + (n_ref[k].astype(F32) + f_ref[k].astype(F32))
        d, mn, vn = _adam_math(w_ref[...], g, m_ref[...], v_ref[...])
        g_ref[...] = g
        d_ref[...] = d
        mo_ref[...] = mn
        vo_ref[...] = vn

    wspec = pl.BlockSpec((None, tr, tc), lambda i, j: (layer, i, j))
    sspec = pl.BlockSpec((N_CHIPS, tr, tc), lambda i, j: (0, i, j))
    n_prev = 0 if prev is None else 4
    return pl.pallas_call(
        body, name=name, grid=(r // tr, cc // tc),
        in_specs=[wspec] * 3 + [sspec] * 2 + [pl.BlockSpec(memory_space=pl.ANY)] * n_prev,
        out_specs=[wspec] * 4,
        out_shape=[jax.ShapeDtypeStruct((L, r, cc), F32)] * 4,
        input_output_aliases={5 + t: t for t in range(n_prev)},
        compiler_params=_cp(2),
    )(w, m, v, near, far, *(prev or ()))


def _pack(vs):
    flat = jnp.concatenate([v.reshape(-1).astype(F32) for v in vs])
    pad = (-flat.shape[0]) % (64 * LANES)
    return jnp.pad(flat, (0, pad)).reshape(-1, LANES)


def _unpack(packed, shapes):
    flat = packed.reshape(-1)
    out, pos = [], 0
    for s in shapes:
        n = 1
        for d in s:
            n *= d
        out.append(flat[pos:pos + n].reshape(s))
        pos += n
    return out


def kernel(x, c, w_in, b_f, conv_w, conv_b, conv_ln_g, conv_ln_b, w_o, w_ffn_in, w_ffn_out, mix_pre_g, mix_post_g, ffn_pre_g, ffn_post_g, ada_w, ada_b, loss_target, m_w_in, m_b_f, m_conv_w, m_conv_b, m_conv_ln_g, m_conv_ln_b, m_w_o, m_w_ffn_in, m_w_ffn_out, m_mix_pre_g, m_mix_post_g, m_ffn_pre_g, m_ffn_post_g, m_ada_w, m_ada_b, v_w_in, v_b_f, v_conv_w, v_conv_b, v_conv_ln_g, v_conv_ln_b, v_w_o, v_w_ffn_in, v_w_ffn_out, v_mix_pre_g, v_mix_post_g, v_ffn_pre_g, v_ffn_post_g, v_ada_w, v_ada_b):
    params = dict(w_in=w_in, b_f=b_f, conv_w=conv_w, conv_b=conv_b, conv_ln_g=conv_ln_g, conv_ln_b=conv_ln_b, w_o=w_o,
                  w_ffn_in=w_ffn_in, w_ffn_out=w_ffn_out, mix_pre_g=mix_pre_g, mix_post_g=mix_post_g,
                  ffn_pre_g=ffn_pre_g, ffn_post_g=ffn_post_g, ada_w=ada_w, ada_b=ada_b)
    mom = dict(w_in=m_w_in, b_f=m_b_f, conv_w=m_conv_w, conv_b=m_conv_b, conv_ln_g=m_conv_ln_g, conv_ln_b=m_conv_ln_b,
               w_o=m_w_o, w_ffn_in=m_w_ffn_in, w_ffn_out=m_w_ffn_out, mix_pre_g=m_mix_pre_g, mix_post_g=m_mix_post_g,
               ffn_pre_g=m_ffn_pre_g, ffn_post_g=m_ffn_post_g, ada_w=m_ada_w, ada_b=m_ada_b)
    var = dict(w_in=v_w_in, b_f=v_b_f, conv_w=v_conv_w, conv_b=v_conv_b, conv_ln_g=v_conv_ln_g, conv_ln_b=v_conv_ln_b,
               w_o=v_w_o, w_ffn_in=v_w_ffn_in, w_ffn_out=v_w_ffn_out, mix_pre_g=v_mix_pre_g, mix_post_g=v_mix_post_g,
               ffn_pre_g=v_ffn_pre_g, ffn_post_g=v_ffn_post_g, ada_w=v_ada_w, ada_b=v_ada_b)

    S, D = x.shape[1], x.shape[2]
    L = w_in.shape[0]
    A = D // 2
    C = D - A
    H = A // HEAD_DIM
    F = w_ffn_out.shape[1] * N_CHIPS
    d_in = w_in.shape[2] * N_CHIPS
    NP = 3 * A + 2 * C + LANES
    dm = Dims(S=S, D=D, A=A, C=C, H=H, F=F, L=L, NP=NP, ts=min(ROW_TILE, S), tr=min(NORM_TILE, S))
    assert H <= 8 and A == C and d_in == 3 * A + H + 2 * C

    ix, iy, ic = _place()
    chip = 2 * ix + iy
    dev = 4 * ix + 2 * iy + ic
    x2 = x.reshape(S, D)
    tgt = loss_target.reshape(S, D)

    swap = lambda t: jnp.transpose(t, (0, 2, 1))
    big_state = dict(w_in=(swap(w_in), swap(m_w_in), swap(v_w_in)), w_o=(w_o, m_w_o, v_w_o),
                     w_ffn_in=(w_ffn_in, m_w_ffn_in, v_w_ffn_in), w_ffn_out=(w_ffn_out, m_w_ffn_out, v_w_ffn_out))

    def gather_start(l, after, after_rest=()):
        first = _exchange_start([w_in[l].T.astype(BF16)], "gather", after, f"gather_a_start_{l}")
        return first, gather_rest(l, [first["token"], *after_rest])

    def gather_rest(l, after):
        return _exchange_start([w_o[l].astype(BF16), w_ffn_in[l].astype(BF16), w_ffn_out[l].astype(BF16)], "gather",
                               after, f"gather_b_start_{l}")

    def gather_wait(st, after, name):
        return _exchange_wait(st, after, name)[1]

    first_0 = _exchange_start([w_in[0].T.astype(BF16)], "gather", [], "gather_a_start_0")

    c_all = _all_gather_devices(c.reshape(D // LANES, LANES), "gather_c", after=[first_0["token"]]).reshape(N_DEV, D)
    c16 = jnp.pad(c_all, ((0, 16 - N_DEV), (0, 0)))
    n_ada = ada_w.shape[2]
    ada_b_cols = lax.dynamic_slice_in_dim(ada_b, chip * n_ada, n_ada, axis=1).reshape(L, 1, n_ada)
    mod_cols, act16 = _ada_fwd(c16, ada_w, ada_b_cols, dm)
    conv_w_all, mod_all = _all_gather_chips([conv_w.reshape(L * CONV_K, -1), mod_cols.reshape(L * 16, n_ada)], "gather_mod")
    cwc = conv_w.shape[2]
    conv_w_full = conv_w_all.reshape(N_CHIPS, L, CONV_K, cwc).transpose(1, 2, 0, 3).reshape(L, CONV_K, C)
    conv_w_full = jnp.pad(conv_w_full, ((0, 0), (0, HALO - CONV_K), (0, 0)))
    mod_all = mod_all.reshape(N_CHIPS, L, 16, n_ada)
    mod_me = lax.dynamic_index_in_dim(mod_all, dev, axis=2, keepdims=False)
    mod_me = mod_me.transpose(1, 0, 2).reshape(L, N_MOD, 1, D)

    gather = [None] * L
    gather[0] = (first_0, gather_rest(0, [first_0["token"], mod_all]))

    def projection_of(g_in):
        w_nat = g_in.reshape(d_in, D)
        return jnp.concatenate([w_nat[:3 * A], w_nat[3 * A + H:], w_nat[3 * A:3 * A + H],
                                jnp.zeros((LANES - H, D), BF16)], axis=0)

    gathered = [None] * L
    vec = lambda p, l: p[l].reshape(1, -1)
    bf_pad = jnp.pad(b_f, ((0, 0), (0, LANES - H)))

    saved = []
    xin = x2
    h = _pre_norm(xin, vec(mix_pre_g, 0), mod_me[0, 1], mod_me[0, 0], dm, after=[gather[0][1]["token"]])
    dx = loss_part = None
    for l in range(L):
        (g_in,) = gather_wait(gather[l][0], [h], f"gather_a_wait_{l}")
        w_p = projection_of(g_in)
        order = [gather[l][1]["token"]]
        if l + 1 < L:
            gather[l + 1] = gather_start(l + 1, [g_in, gather[l][1]["token"]])
            order.append(gather[l + 1][1]["token"])
        qkv = _mm_nt(h, w_p, BF16, "mm_qkv", 0, 3 * A, after=order)
        cf = _mm_nt(h, w_p, F32, "mm_cf", 3 * A, 2 * C + LANES)
        cum = _gates_fwd(cf, vec(bf_pad, l), dm)
        qa, ka, va = _attn_prep(qkv, cum, dm)
        o, lse = _attn_fwd(qa, ka, va, dm)
        u3, u1 = _conv_fwd(cf, conv_w_full[l], vec(conv_b, l), vec(conv_ln_g, l), vec(conv_ln_b, l), dm)
        cat = jnp.concatenate([o, u3], axis=1)
        g_o, wfi, g_fo = gather_wait(gather[l][1], [cat], f"gather_b_wait_{l}")
        wo, wfo = g_o.reshape(D, D), g_fo.reshape(F, D)
        gathered[l] = (w_p, wo, wfi, wfo)
        y = _mm_nn(cat, wo, F32, "mm_o")
        x1, h2 = _res_norm(xin, y, vec(mix_post_g, l), mod_me[l, 2], vec(ffn_pre_g, l), mod_me[l, 4], mod_me[l, 3], dm)
        gu, a = _ffn_in_swiglu(h2, wfi, "mm_ffn_in")
        y2 = _mm_nn(a, wfo, F32, "mm_ffn_out")
        saved.append(dict(xin=xin, h=h, qa=qa, ka=ka, va=va, cf=cf, o=o, lse=lse, u1=u1, cat=cat, y=y,
                          x1=x1, h2=h2, gu=gu, a=a, y2=y2))
        if l + 1 < L:
            xin, h = _res_norm(x1, y2, vec(ffn_post_g, l), mod_me[l, 5], vec(mix_pre_g, l + 1),
                               mod_me[l + 1, 1], mod_me[l + 1, 0], dm)
        else:
            dx, loss_part = _res_loss(x1, y2, vec(ffn_post_g, l), mod_me[l, 5], tgt, dm)
    loss = lax.psum(loss_part[0, 0], ("x", "y", "c"))

    small = [None] * L
    big = [None] * L
    forward = [None] * L

    def forward_start(l, after):
        near = []
        for st, nm in zip(big[l], ("a", "b")):
            near += _exchange_wait(st, after, f"scatter_{nm}_wait_{l}")[1]
        return _exchange_start(near, "sibling", [], f"forward_start_{l}")

    order = []
    for l in reversed(range(L)):
        w_p, wo, wfi, wfo = gathered[l]
        sv = saved[l]
        dy2, d_gfpost, d_g2 = _post_bwd(dx, sv["y2"], vec(ffn_post_g, l), mod_me[l, 5], dm, after=order)
        dgu = _ffn_out_bwd_swiglu(dy2, wfo, sv["gu"], "mm_da")
        g_wfo = _mm_tn(sv["a"], dy2, BF16, "mm_dwfo")
        g_wfi = _ffn_in_bwd_w(sv["h2"], dgu, N_CHIPS, "mm_dwfi")
        dh2 = _ffn_in_bwd_x(dgu, wfi, "mm_dh2")
        scatter_ffn = _exchange_start([g_wfi, g_wfo.reshape(N_CHIPS, F // N_CHIPS, D)], "scatter", [], f"scatter_b_start_{l}")
        dx1, d_sh2, d_sc2, d_gfpre = _pre_bwd(dh2, dx, sv["x1"], vec(ffn_pre_g, l), mod_me[l, 4], dm,
                                              after=[scatter_ffn["token"]])
        dy, d_gpost, d_g1 = _post_bwd(dx1, sv["y"], vec(mix_post_g, l), mod_me[l, 2], dm)
        dcat = _mm_nt(dy, wo, F32, "mm_dcat")
        g_wo = _mm_tn(sv["cat"], dy, BF16, "mm_dwo")
        dcfc, d_cw, d_cb, d_lg, d_lb = _conv_bwd(dcat, sv["u1"], sv["cf"], conv_w_full[l], vec(conv_ln_g, l),
                                                 vec(conv_ln_b, l), dm)
        qb, doa = _attn_prep_bwd(sv["qa"], sv["lse"], dcat, sv["o"], dm)
        dqkv, dcum = _attn_post(*_attn_bwd(qb, sv["ka"], sv["va"], doa, dm), dm)
        dfl, d_bf = _gates_bwd(dcum, sv["cf"], vec(bf_pad, l), dm)
        dproj = jnp.concatenate([dqkv, dcfc, dfl], axis=1)
        dh = _mm_nn(dproj, w_p, F32, "mm_dh")
        g_wp = _mm_tn(dproj, sv["h"], BF16, "mm_dwp")
        dx, d_sh1, d_sc1, d_gpre = _pre_bwd(dh, dx1, sv["xin"], vec(mix_pre_g, l), mod_me[l, 1], dm)
        g_nat = jnp.concatenate([g_wp[:3 * A], g_wp[3 * A + 2 * C:3 * A + 2 * C + H], g_wp[3 * A:3 * A + 2 * C]], axis=0)
        g_win = g_nat.reshape(N_CHIPS, d_in // N_CHIPS, D)
        g_mix = [g_win, g_wo.reshape(N_CHIPS, D // N_CHIPS, D)]
        order = []
        if l > 0:
            scatter_mix = _exchange_start(g_mix, "scatter", [], f"scatter_a_start_{l}")
            order.append(scatter_mix["token"])
            big[l] = (scatter_mix, scatter_ffn)
        if l + 1 < L:
            forward[l + 1] = forward_start(l + 1, [dx])
            order.append(forward[l + 1]["token"])
        small[l] = dict(b_f=d_bf[0, :H], conv_b=d_cb[0], conv_ln_g=d_lg[0], conv_ln_b=d_lb[0], mix_pre_g=d_gpre[0],
                        mix_post_g=d_gpost[0], ffn_pre_g=d_gfpre[0], ffn_post_g=d_gfpost[0],
                        dmod=jnp.concatenate([d_sh1, d_sc1, d_g1, d_sh2, d_sc2, d_g2], axis=1)[0],
                        conv_w=d_cw[:CONV_K])
    grad_x = dx.reshape(1, S, D)

    keys_small = ["b_f", "conv_b", "conv_ln_g", "conv_ln_b", "mix_pre_g", "mix_post_g", "ffn_pre_g", "ffn_post_g",
                  "dmod", "conv_w"]
    stacked = [jnp.stack([small[l][k] for l in range(L)]) for k in keys_small]
    shapes = [s.shape for s in stacked]
    pack = _pack(stacked)
    small_exchange = _exchange_start([pack], "devices", order, "gather_small_start")
    big[0] = (_exchange_start(g_mix, "scatter", [small_exchange["token"]], "scatter_a_start_0"), scatter_ffn)

    names_big = ["w_in", "w_o", "w_ffn_in", "w_ffn_out"]
    res_big = {n: None for n in names_big}

    def update(l, after):
        near, far = _exchange_wait(forward[l], after, f"forward_wait_{l}")
        for t, n in enumerate(names_big):
            res_big[n] = _adamw_shard(*big_state[n], near[t], far[t], l, res_big[n], f"adamw_{n}_{l}")
        return [res_big[n][1] for n in names_big]

    done = [big[0][0]["token"]]
    for l in reversed(range(1, L)):
        done = update(l, done)
    forward[0] = forward_start(0, done)

    (pack8,) = _exchange_wait(small_exchange, [forward[0]["token"]], "gather_small_wait")[1]
    summed = dict(zip(keys_small, _unpack(_sum_devices(pack8), shapes)))
    at_dmod = sum(s.size for s in stacked[:keys_small.index("dmod")])
    dmod_all = pack8.reshape(N_DEV, -1)[:, at_dmod:at_dmod + L * N_MOD * D].reshape(N_DEV, L, N_MOD * D)
    grads = {k: summed[k] for k in keys_small[:8]}
    grads["ada_b"] = summed["dmod"]
    grads["conv_w"] = lax.dynamic_slice_in_dim(summed["conv_w"], chip * cwc, cwc, axis=2)

    dmod_cols = lax.dynamic_slice_in_dim(dmod_all.reshape(N_DEV, L, N_CHIPS, n_ada), chip, 1, axis=2)
    dmod16 = jnp.pad(dmod_cols.reshape(N_DEV, L, n_ada).transpose(1, 0, 2), ((0, 0), (0, 16 - N_DEV), (0, 0))).astype(BF16)
    grads["ada_w"] = _ada_bwd(act16, dmod16)

    d_aw, m_aw, v_aw = _adamw_dense(ada_w.reshape(L * D, n_ada), m_ada_w.reshape(L * D, n_ada),
                                    v_ada_w.reshape(L * D, n_ada), grads["ada_w"].reshape(L * D, n_ada), "adamw_ada_w")
    names_small = ["b_f", "conv_w", "conv_b", "conv_ln_g", "conv_ln_b", "mix_pre_g", "mix_post_g", "ffn_pre_g",
                   "ffn_post_g", "ada_b"]
    shapes_small = [params[n].shape for n in names_small]
    d_s, m_s, v_s = _adamw_dense(_pack([params[n] for n in names_small]), _pack([mom[n] for n in names_small]),
                                 _pack([var[n] for n in names_small]), _pack([grads[n] for n in names_small]),
                                 "adamw_small")
    delta_w = dict(zip(names_small, _unpack(d_s, shapes_small)))
    new_m = dict(zip(names_small, _unpack(m_s, shapes_small)))
    new_v = dict(zip(names_small, _unpack(v_s, shapes_small)))
    delta_w["ada_w"], new_m["ada_w"], new_v["ada_w"] = (t.reshape(L, D, n_ada) for t in (d_aw, m_aw, v_aw))

    update(0, [d_s, d_aw])
    res_big["w_in"] = [swap(t) for t in res_big["w_in"]]
    for n in names_big:
        grads[n], delta_w[n], new_m[n], new_v[n] = res_big[n]

    return (loss, grad_x, *[grads[n] for n in WEIGHTS], *[delta_w[n] for n in WEIGHTS],
            *[new_m[n] for n in WEIGHTS], *[new_v[n] for n in WEIGHTS])
```

```python
import collections
import functools

import jax
import jax.numpy as jnp
from jax import lax
from jax.experimental import pallas as pl
from jax.experimental.pallas import tpu as pltpu

F32 = jnp.float32
BF16 = jnp.bfloat16
MESH = pl.DeviceIdType.MESH

HEAD_DIM = 64
CONV_K = 31
N_MOD = 6
EPS = 1e-6
N_CHIPS = 4
N_DEV = 8
LANES = 128
HALO = 32
ROW_TILE = 256
NORM_TILE = 512
MM_TM = 512
MM_TM_PLAIN = 1024
MM_TN_MAX = 1408
ADAM_ROWS = 256
TN_TM = 256
TN_TM_MAX = 384
VMEM_LIMIT = 56 * 1024 * 1024

ADAM_LR = 0.001
ADAM_B1 = 0.9
ADAM_B2 = 0.999
ADAM_EPS = 1e-08
ADAM_WD = 0.01
ADAM_STEP = 10

WEIGHTS = ['w_in', 'b_f', 'conv_w', 'conv_b', 'conv_ln_g', 'conv_ln_b', 'w_o', 'w_ffn_in', 'w_ffn_out',
           'mix_pre_g', 'mix_post_g', 'ffn_pre_g', 'ffn_post_g', 'ada_w', 'ada_b']

Dims = collections.namedtuple("Dims", "S D A C H F L NP ts tr")


def _cp(n_grid=0):
    if n_grid:
        return pltpu.CompilerParams(dimension_semantics=("arbitrary",) * n_grid, vmem_limit_bytes=VMEM_LIMIT)
    return pltpu.CompilerParams(vmem_limit_bytes=VMEM_LIMIT)


def _tile(n, cap, also=None):
    best = None
    t = LANES
    while t <= min(n, cap):
        if n % t == 0 and (also is None or also % t == 0):
            best = t
        t += LANES
    assert best is not None, (n, cap, also)
    return best


def _bf(v):
    return v if v.dtype == BF16 else v.astype(BF16)


def _place():
    return lax.axis_index("x"), lax.axis_index("y"), lax.axis_index("c")


def _flip(v, d):
    return 1 - v if d else v


def _all_gather_devices(a, name, after=()):
    def body(a_ref, *rest):
        o_ref, send, recv, lsem = rest[len(after):]
        x, y, c = _place()
        me = 4 * x + 2 * y + c
        local = pltpu.make_async_copy(a_ref, o_ref.at[me], lsem)
        local.start()
        copies = []
        for k in range(1, N_DEV):
            peer = (_flip(x, (k >> 2) & 1), _flip(y, (k >> 1) & 1), _flip(c, k & 1))
            cp = pltpu.make_async_remote_copy(src_ref=a_ref, dst_ref=o_ref.at[me], send_sem=send.at[k - 1],
                                              recv_sem=recv.at[k - 1], device_id=peer, device_id_type=MESH)
            cp.start()
            copies.append(cp)
        for cp in copies:
            cp.wait()
        local.wait()

    return pl.pallas_call(
        body, name=name,
        out_shape=jax.ShapeDtypeStruct((N_DEV,) + a.shape, a.dtype),
        in_specs=[pl.BlockSpec(memory_space=pl.ANY)] * (1 + len(after)),
        out_specs=pl.BlockSpec(memory_space=pl.ANY),
        scratch_shapes=[pltpu.SemaphoreType.DMA((N_DEV - 1,)), pltpu.SemaphoreType.DMA((N_DEV - 1,)),
                        pltpu.SemaphoreType.DMA],
    )(a, *after)


def _all_gather_chips(arrays, name):
    n = len(arrays)

    def body(*refs):
        a_refs, o_refs = refs[:n], refs[n:2 * n]
        send, recv, lsem = refs[2 * n:]
        x, y, c = _place()
        me = 2 * x + y
        copies = []
        for i in range(n):
            local = pltpu.make_async_copy(a_refs[i], o_refs[i].at[me], lsem.at[i])
            local.start()
            copies.append(local)
            for k in range(1, N_CHIPS):
                peer = (_flip(x, (k >> 1) & 1), _flip(y, k & 1), c)
                cp = pltpu.make_async_remote_copy(src_ref=a_refs[i], dst_ref=o_refs[i].at[me],
                                                  send_sem=send.at[i, k - 1], recv_sem=recv.at[i, k - 1],
                                                  device_id=peer, device_id_type=MESH)
                cp.start()
                copies.append(cp)
        for cp in copies:
            cp.wait()

    return pl.pallas_call(
        body, name=name,
        out_shape=[jax.ShapeDtypeStruct((N_CHIPS,) + a.shape, a.dtype) for a in arrays],
        in_specs=[pl.BlockSpec(memory_space=pl.ANY)] * n,
        out_specs=[pl.BlockSpec(memory_space=pl.ANY)] * n,
        scratch_shapes=[pltpu.SemaphoreType.DMA((n, N_CHIPS - 1)), pltpu.SemaphoreType.DMA((n, N_CHIPS - 1)),
                        pltpu.SemaphoreType.DMA((n,))],
    )(*arrays)


_HBM = pl.BlockSpec(memory_space=pltpu.HBM)
_SEM = pl.BlockSpec(memory_space=pltpu.SEMAPHORE)
_EFFECT = pltpu.SideEffectType.DATAFLOW_SIDE_EFFECTING


def _n_copies(mode):
    return {"sibling": 1, "devices": N_DEV - 1}.get(mode, N_CHIPS - 1)


def _chip_copies(srcs, lands, send, recv, mode):
    x, y, c = _place()
    me = 2 * x + y
    copies = []
    for i in range(len(srcs)):
        if mode == "sibling":
            copies.append(pltpu.make_async_remote_copy(src_ref=srcs[i], dst_ref=lands[i], send_sem=send.at[i],
                                                       recv_sem=recv.at[i], device_id=(x, y, 1 - c), device_id_type=MESH))
            continue
        if mode == "devices":
            for k in range(1, N_DEV):
                peer = (_flip(x, (k >> 2) & 1), _flip(y, (k >> 1) & 1), _flip(c, k & 1))
                s = i * (N_DEV - 1) + k - 1
                copies.append(pltpu.make_async_remote_copy(src_ref=srcs[i], dst_ref=lands[i].at[2 * me + c],
                                                           send_sem=send.at[s], recv_sem=recv.at[s], device_id=peer,
                                                           device_id_type=MESH))
            continue
        for k in range(1, N_CHIPS):
            px, py = _flip(x, (k >> 1) & 1), _flip(y, k & 1)
            src = srcs[i].at[2 * px + py] if mode == "scatter" else srcs[i]
            s = i * (N_CHIPS - 1) + k - 1
            copies.append(pltpu.make_async_remote_copy(src_ref=src, dst_ref=lands[i].at[me], send_sem=send.at[s],
                                                       recv_sem=recv.at[s], device_id=(px, py, c), device_id_type=MESH))
    return copies


def _own_copies(srcs, lands, own, mode):
    x, y, c = _place()
    me = 2 * x + y
    slot = 2 * me + c if mode == "devices" else me
    return [pltpu.make_async_copy(srcs[i].at[me] if mode == "scatter" else srcs[i], lands[i].at[slot], own.at[i])
            for i in range(len(srcs))]


_ORDER = pl.BlockSpec(memory_space=pl.ANY)


def _exchange_start(arrays, mode, after, name):
    n = len(arrays)
    n_sems = 2 if mode == "sibling" else 3

    def body(*refs):
        srcs, lands = refs[:n], refs[n:2 * n]
        sems = refs[2 * n + len(after):2 * n + len(after) + n_sems]
        token = refs[-1]
        for cp in _chip_copies(srcs, lands, sems[0], sems[1], mode):
            cp.start()
        if mode != "sibling":
            for cp in _own_copies(srcs, lands, sems[2], mode):
                cp.start()
        token[...] = jnp.zeros_like(token)

    lead = {"gather": (N_CHIPS,), "devices": (N_DEV,)}.get(mode, ())
    land_shapes = [lead + a.shape for a in arrays]
    n_sem = n * _n_copies(mode)
    sem_shapes = [pltpu.SemaphoreType.DMA((n_sem,)), pltpu.SemaphoreType.DMA((n_sem,)), pltpu.SemaphoreType.DMA((n,))]
    outs = pl.pallas_call(
        body, name=name,
        out_shape=(*sem_shapes[:n_sems],
                   *[pltpu.HBM(a.shape, a.dtype) for a in arrays],
                   *[pltpu.HBM(s, a.dtype) for s, a in zip(land_shapes, arrays)],
                   jax.ShapeDtypeStruct((8, LANES), F32)),
        in_specs=[_HBM] * (2 * n) + [_ORDER] * len(after),
        out_specs=(*[_SEM] * n_sems, *[_HBM] * (2 * n), pl.BlockSpec(memory_space=pltpu.VMEM)),
        input_output_aliases={i: n_sems + i for i in range(2 * n)},
        compiler_params=pltpu.CompilerParams(has_side_effects=_EFFECT),
    )(*[pltpu.with_memory_space_constraint(a, pltpu.HBM) for a in arrays],
      *[pltpu.with_memory_space_constraint(lax.empty(s, a.dtype), pltpu.HBM) for s, a in zip(land_shapes, arrays)],
      *after)
    return dict(sems=outs[:n_sems], srcs=outs[n_sems:n_sems + n], lands=outs[n_sems + n:n_sems + 2 * n], token=outs[-1],
                mode=mode)


def _exchange_wait(st, after, name):
    n = len(st["srcs"])
    mode = st["mode"]
    n_sems = len(st["sems"])

    def body(*refs):
        srcs, lands = refs[:n], refs[n:2 * n]
        sems = refs[2 * n:2 * n + n_sems]
        for cp in _chip_copies(srcs, lands, sems[0], sems[1], mode):
            cp.wait_send()
            cp.wait_recv()
        if mode != "sibling":
            for cp in _own_copies(srcs, lands, sems[2], mode):
                cp.wait()

    outs = pl.pallas_call(
        body, name=name,
        out_shape=tuple(pltpu.HBM(a.shape, a.dtype) for a in (*st["srcs"], *st["lands"])),
        in_specs=[_HBM] * (2 * n) + [_SEM] * n_sems + [_ORDER] * len(after),
        out_specs=tuple([_HBM] * (2 * n)),
        input_output_aliases={i: i for i in range(2 * n)},
        compiler_params=pltpu.CompilerParams(has_side_effects=_EFFECT),
    )(*st["srcs"], *st["lands"], *st["sems"], *after)
    return outs[:n], outs[n:]


def _matmul(a, b, contract, grid, a_spec, b_spec, o_spec, out_shape, name, after=()):
    def body(a_ref, b_ref, *rest):
        o_ref = rest[len(after)]
        r = lax.dot_general(_bf(a_ref[...]), _bf(b_ref[...]), (contract, ((), ())), preferred_element_type=F32)
        o_ref[...] = r.astype(o_ref.dtype)

    return pl.pallas_call(
        body, name=name, grid=grid, in_specs=[a_spec, b_spec] + [_ORDER] * len(after), out_specs=o_spec,
        out_shape=out_shape, compiler_params=_cp(len(grid)),
    )(a, b, *after)


def _mm_nn(a, b, out_dtype, name, col0=0, n=None, after=()):
    m, k = a.shape
    n = b.shape[1] - col0 if n is None else n
    tm = min(MM_TM_PLAIN, m)
    tn = _tile(n, MM_TN_MAX, also=col0 if col0 else None)
    off = col0 // tn
    return _matmul(a, b, ((1,), (0,)), (n // tn, m // tm),
                   pl.BlockSpec((tm, k), lambda j, i: (i, 0)),
                   pl.BlockSpec((k, tn), lambda j, i: (0, j + off)),
                   pl.BlockSpec((tm, tn), lambda j, i: (i, j)),
                   jax.ShapeDtypeStruct((m, n), out_dtype), name, after=after)


def _mm_nt(a, b, out_dtype, name, row0=0, n=None, after=()):
    m, k = a.shape
    n = b.shape[0] - row0 if n is None else n
    tm = min(MM_TM_PLAIN, m)
    tn = _tile(n, MM_TN_MAX, also=row0 if row0 else None)
    off = row0 // tn
    return _matmul(a, b, ((1,), (1,)), (n // tn, m // tm),
                   pl.BlockSpec((tm, k), lambda j, i: (i, 0)),
                   pl.BlockSpec((tn, k), lambda j, i: (j + off, 0)),
                   pl.BlockSpec((tm, tn), lambda j, i: (i, j)),
                   jax.ShapeDtypeStruct((m, n), out_dtype), name, after=after)


def _mm_tn(a, b, out_dtype, name):
    k, m = a.shape
    n = b.shape[1]
    tm = _tile(m, TN_TM_MAX)
    tn = _tile(n, MM_TN_MAX)
    return _matmul(a, b, ((0,), (0,)), (n // tn, m // tm),
                   pl.BlockSpec((k, tm), lambda j, i: (0, i)),
                   pl.BlockSpec((k, tn), lambda j, i: (0, j)),
                   pl.BlockSpec((tm, tn), lambda j, i: (i, j)),
                   jax.ShapeDtypeStruct((m, n), out_dtype), name)


def _ffn_in_swiglu(x, w3, name):
    m, k = x.shape
    half, nb = w3.shape[0] // 2, w3.shape[2]
    tm = min(MM_TM, m)

    def body(x_ref, wg_ref, wu_ref, gu_ref, a_ref):
        xv = x_ref[...]
        g = jnp.dot(xv, wg_ref[...], preferred_element_type=F32)
        u = jnp.dot(xv, wu_ref[...], preferred_element_type=F32)
        gu_ref[0] = g.astype(gu_ref.dtype)
        gu_ref[1] = u.astype(gu_ref.dtype)
        a_ref[...] = (g * _sigmoid(g) * u).astype(a_ref.dtype)

    return pl.pallas_call(
        body, name=name, grid=(half, m // tm),
        in_specs=[pl.BlockSpec((tm, k), lambda j, i: (i, 0)), pl.BlockSpec((None, k, nb), lambda j, i: (j, 0, 0)),
                  pl.BlockSpec((None, k, nb), lambda j, i: (j + half, 0, 0))],
        out_specs=[pl.BlockSpec((2, tm, nb), lambda j, i: (0, i, j)), pl.BlockSpec((tm, nb), lambda j, i: (i, j))],
        out_shape=[jax.ShapeDtypeStruct((2, m, half * nb), BF16), jax.ShapeDtypeStruct((m, half * nb), BF16)],
        compiler_params=_cp(2),
    )(x, w3, w3)


def _ffn_out_bwd_swiglu(dy, w_out, gu, name):
    m, k = dy.shape
    f = w_out.shape[0]
    tm = min(MM_TM, m)
    tn = _tile(f, MM_TN_MAX)

    def body(dy_ref, w_ref, gu_ref, d_ref):
        da = _dot_nt(dy_ref[...], w_ref[...])
        g, u = gu_ref[0].astype(F32), gu_ref[1].astype(F32)
        sg = _sigmoid(g)
        d_ref[0] = (da * u * (sg * (1.0 + g * (1.0 - sg)))).astype(d_ref.dtype)
        d_ref[1] = (da * (g * sg)).astype(d_ref.dtype)

    return pl.pallas_call(
        body, name=name, grid=(f // tn, m // tm),
        in_specs=[pl.BlockSpec((tm, k), lambda j, i: (i, 0)), pl.BlockSpec((tn, k), lambda j, i: (j, 0)),
                  pl.BlockSpec((2, tm, tn), lambda j, i: (0, i, j))],
        out_specs=pl.BlockSpec((2, tm, tn), lambda j, i: (0, i, j)),
        out_shape=jax.ShapeDtypeStruct((2, m, f), BF16), compiler_params=_cp(2),
    )(dy, w_out, gu)


def _ffn_in_bwd_x(dgu, w3, name):
    _, m, f = dgu.shape
    nj, n, nb = w3.shape
    per = f // nb
    tm = min(MM_TM, m)
    tn = _tile(n, 512)

    def body(d_ref, w_ref, o_ref):
        acc = None
        for j in range(nj):
            part = _dot_nt(d_ref[j // per][:, (j % per) * nb:(j % per + 1) * nb], w_ref[j])
            acc = part if acc is None else acc + part
        o_ref[...] = acc.astype(o_ref.dtype)

    return pl.pallas_call(
        body, name=name, grid=(m // tm, n // tn),
        in_specs=[pl.BlockSpec((2, tm, f), lambda i, j: (0, i, 0)), pl.BlockSpec((nj, tn, nb), lambda i, j: (0, j, 0))],
        out_specs=pl.BlockSpec((tm, tn), lambda i, j: (i, j)),
        out_shape=jax.ShapeDtypeStruct((m, n), BF16), compiler_params=_cp(2),
    )(dgu, w3)


def _ffn_in_bwd_w(x, dgu, nj, name):
    k, m = x.shape
    f = dgu.shape[2]
    per = nj // 2
    nb = f // per
    tm = min(TN_TM, m)
    return _matmul(x, dgu, ((0,), (0,)), (nj, m // tm),
                   pl.BlockSpec((k, tm), lambda j, i: (0, i)),
                   pl.BlockSpec((None, k, nb), lambda j, i: (j // per, 0, j % per)),
                   pl.BlockSpec((None, tm, nb), lambda j, i: (j, i, 0)),
                   jax.ShapeDtypeStruct((nj, m, nb), BF16), name)


def _vec_spec(d):
    return pl.BlockSpec((1, d), lambda i: (0, 0))


def _row_spec(ts, d, col=0):
    return pl.BlockSpec((ts, d), lambda i: (i, col))


def _rms(x):
    return lax.rsqrt(jnp.mean(x * x, axis=-1, keepdims=True) + EPS)


def _pre_norm(x, gain, scale, shift, dm, after=()):
    def body(x_ref, g_ref, sc_ref, sh_ref, *rest):
        h_ref = rest[len(after)]
        xv = x_ref[...]
        h_ref[...] = (((xv * _rms(xv)) * g_ref[...]) * (1.0 + sc_ref[...]) + sh_ref[...]).astype(h_ref.dtype)

    return pl.pallas_call(
        body, name="pre_norm", grid=(dm.S // dm.tr,),
        in_specs=[_row_spec(dm.tr, dm.D)] + [_vec_spec(dm.D)] * 3 + [_ORDER] * len(after),
        out_specs=_row_spec(dm.tr, dm.D),
        out_shape=jax.ShapeDtypeStruct((dm.S, dm.D), BF16), compiler_params=_cp(1),
    )(x, gain, scale, shift, *after)


def _res_norm(x, y, gpost, gate, gain, scale, shift, dm):
    def body(x_ref, y_ref, gp_ref, gt_ref, g_ref, sc_ref, sh_ref, xo_ref, h_ref):
        yv = y_ref[...].astype(F32)
        xn = x_ref[...] + gt_ref[...] * ((yv * _rms(yv)) * gp_ref[...])
        xo_ref[...] = xn
        h_ref[...] = (((xn * _rms(xn)) * g_ref[...]) * (1.0 + sc_ref[...]) + sh_ref[...]).astype(h_ref.dtype)

    return pl.pallas_call(
        body, name="res_norm", grid=(dm.S // dm.tr,),
        in_specs=[_row_spec(dm.tr, dm.D)] * 2 + [_vec_spec(dm.D)] * 5,
        out_specs=[_row_spec(dm.tr, dm.D)] * 2,
        out_shape=[jax.ShapeDtypeStruct((dm.S, dm.D), F32), jax.ShapeDtypeStruct((dm.S, dm.D), BF16)],
        compiler_params=_cp(1),
    )(x, y, gpost, gate, gain, scale, shift)


def _res_loss(x, y, gpost, gate, target, dm):
    def body(x_ref, y_ref, gp_ref, gt_ref, t_ref, dx_ref, loss_ref):
        i = pl.program_id(0)

        @pl.when(i == 0)
        def _():
            loss_ref[...] = jnp.zeros_like(loss_ref)
        yv = y_ref[...].astype(F32)
        err = x_ref[...] + gt_ref[...] * ((yv * _rms(yv)) * gp_ref[...]) - t_ref[...]
        dx_ref[...] = err * (1.0 / dm.D)
        per_row = jnp.mean(err * err, axis=-1, keepdims=True)
        loss_ref[...] += 0.5 * jnp.sum(per_row, axis=0, keepdims=True)

    return pl.pallas_call(
        body, name="res_loss", grid=(dm.S // dm.tr,),
        in_specs=[_row_spec(dm.tr, dm.D)] * 2 + [_vec_spec(dm.D)] * 2 + [_row_spec(dm.tr, dm.D)],
        out_specs=[_row_spec(dm.tr, dm.D), _vec_spec(LANES)],
        out_shape=[jax.ShapeDtypeStruct((dm.S, dm.D), F32), jax.ShapeDtypeStruct((1, LANES), F32)],
        compiler_params=_cp(1),
    )(x, y, gpost, gate, target)


def _post_bwd(dxo, y, gpost, gate, dm, after=()):
    def body(dx_ref, y_ref, gp_ref, gt_ref, *rest):
        dy_ref, dgp_ref, dgt_ref = rest[len(after):]
        i = pl.program_id(0)

        @pl.when(i == 0)
        def _():
            dgp_ref[...] = jnp.zeros_like(dgp_ref)
            dgt_ref[...] = jnp.zeros_like(dgt_ref)
        yv, dx = y_ref[...].astype(F32), dx_ref[...]
        r = _rms(yv)
        t = yv * r
        dgp_ref[...] += jnp.sum(dx * gt_ref[...] * t, axis=0, keepdims=True)
        dgt_ref[...] += jnp.sum(dx * (t * gp_ref[...]), axis=0, keepdims=True)
        dt = dx * (gt_ref[...] * gp_ref[...])
        dy_ref[...] = (r * (dt - t * jnp.mean(dt * t, axis=-1, keepdims=True))).astype(dy_ref.dtype)

    return pl.pallas_call(
        body, name="post_bwd", grid=(dm.S // dm.tr,),
        in_specs=[_row_spec(dm.tr, dm.D)] * 2 + [_vec_spec(dm.D)] * 2 + [_ORDER] * len(after),
        out_specs=[_row_spec(dm.tr, dm.D), _vec_spec(dm.D), _vec_spec(dm.D)],
        out_shape=[jax.ShapeDtypeStruct((dm.S, dm.D), BF16)] + [jax.ShapeDtypeStruct((1, dm.D), F32)] * 2,
        compiler_params=_cp(1),
    )(dxo, y, gpost, gate, *after)


def _pre_bwd(dh, dxo, x, gain, scale, dm, after=()):
    def body(dh_ref, dxo_ref, x_ref, g_ref, sc_ref, *rest):
        dx_ref, dsh_ref, dsc_ref, dg_ref = rest[len(after):]
        i = pl.program_id(0)

        @pl.when(i == 0)
        def _():
            dsh_ref[...] = jnp.zeros_like(dsh_ref)
            dsc_ref[...] = jnp.zeros_like(dsc_ref)
            dg_ref[...] = jnp.zeros_like(dg_ref)
        xv, dh_ = x_ref[...], dh_ref[...].astype(F32)
        r = _rms(xv)
        nrm = xv * r
        one_sc = 1.0 + sc_ref[...]
        dsh_ref[...] += jnp.sum(dh_, axis=0, keepdims=True)
        dsc_ref[...] += jnp.sum(dh_ * (nrm * g_ref[...]), axis=0, keepdims=True)
        dg_ref[...] += jnp.sum(dh_ * nrm * one_sc, axis=0, keepdims=True)
        dn = dh_ * (g_ref[...] * one_sc)
        dx_ref[...] = dxo_ref[...] + r * (dn - nrm * jnp.mean(dn * nrm, axis=-1, keepdims=True))

    return pl.pallas_call(
        body, name="pre_bwd", grid=(dm.S // dm.tr,),
        in_specs=[_row_spec(dm.tr, dm.D)] * 3 + [_vec_spec(dm.D)] * 2 + [_ORDER] * len(after),
        out_specs=[_row_spec(dm.tr, dm.D)] + [_vec_spec(dm.D)] * 3,
        out_shape=[jax.ShapeDtypeStruct((dm.S, dm.D), F32)] + [jax.ShapeDtypeStruct((1, dm.D), F32)] * 3,
        compiler_params=_cp(1),
    )(dh, dxo, x, gain, scale, *after)


def _sigmoid(z):
    return 1.0 / (1.0 + jnp.exp(-z))


def _tri(n, upper):
    r = lax.broadcasted_iota(jnp.int32, (n, n), 0)
    c = lax.broadcasted_iota(jnp.int32, (n, n), 1)
    return (c >= r if upper else r >= c).astype(F32)


def _gates_fwd(cf, bf, dm):
    ts = dm.ts
    fcol = 2 * dm.C // LANES

    def body(f_ref, b_ref, cum_ref, carry):
        i = pl.program_id(0)

        @pl.when(i == 0)
        def _():
            carry[...] = jnp.zeros_like(carry)
        z = f_ref[...] + b_ref[...]
        lf = jnp.minimum(z, 0.0) - jnp.log(1.0 + jnp.exp(-jnp.abs(z)))
        cs = jnp.dot(_tri(ts, False), lf, precision=lax.Precision.HIGHEST, preferred_element_type=F32) + carry[...]
        cum_ref[...] = cs
        carry[...] = cs[ts - 1:ts, :]

    return pl.pallas_call(
        body, name="gates_fwd", grid=(dm.S // ts,),
        in_specs=[pl.BlockSpec((ts, LANES), lambda i: (i, fcol)), _vec_spec(LANES)],
        out_specs=_row_spec(ts, LANES),
        out_shape=jax.ShapeDtypeStruct((dm.S, LANES), F32),
        scratch_shapes=[pltpu.VMEM((1, LANES), F32)], compiler_params=_cp(1),
    )(cf, bf)


def _gates_bwd(dc, cf, bf, dm):
    ts = dm.ts
    nb = dm.S // ts
    fcol = 2 * dm.C // LANES

    def body(dc_ref, f_ref, b_ref, df_ref, db_ref, carry):
        i = pl.program_id(0)

        @pl.when(i == 0)
        def _():
            carry[...] = jnp.zeros_like(carry)
            db_ref[...] = jnp.zeros_like(db_ref)
        dlf = jnp.dot(_tri(ts, True), dc_ref[...], precision=lax.Precision.HIGHEST, preferred_element_type=F32) + carry[...]
        carry[...] = dlf[0:1, :]
        dz = dlf * (1.0 - _sigmoid(f_ref[...] + b_ref[...]))
        df_ref[...] = dz.astype(df_ref.dtype)
        db_ref[...] += jnp.sum(dz, axis=0, keepdims=True)

    return pl.pallas_call(
        body, name="gates_bwd", grid=(nb,),
        in_specs=[pl.BlockSpec((ts, LANES), lambda i: (nb - 1 - i, 0)),
                  pl.BlockSpec((ts, LANES), lambda i: (nb - 1 - i, fcol)), _vec_spec(LANES)],
        out_specs=[pl.BlockSpec((ts, LANES), lambda i: (nb - 1 - i, 0)), _vec_spec(LANES)],
        out_shape=[jax.ShapeDtypeStruct((dm.S, LANES), BF16), jax.ShapeDtypeStruct((1, LANES), F32)],
        scratch_shapes=[pltpu.VMEM((1, LANES), F32)], compiler_params=_cp(1),
    )(dc, cf, bf)


def _dot_nt(a, b):
    return lax.dot_general(a, b, (((1,), (1,)), ((), ())), preferred_element_type=F32)


def _dot_tn(a, b):
    return lax.dot_general(a, b, (((0,), (0,)), ((), ())), preferred_element_type=F32)


_C0, _C1, _C2 = HEAD_DIM, HEAD_DIM + 3, HEAD_DIM + 6


def _split3(c):
    hi = c.astype(BF16).astype(F32)
    mid = (c - hi).astype(BF16).astype(F32)
    return hi, mid, c - hi - mid


def _put3(base, lane, start, pieces, sign=1.0):
    out = base
    for t, piece in enumerate(pieces):
        out = jnp.where(lane == start + t, sign * piece, out)
    return out


def _head_lanes(pair, odd):
    v = pair.astype(F32)
    return pltpu.roll(v, HEAD_DIM, axis=1) if odd else v


def _attn_prep(qkv, cum, dm):
    ts, A, H = dm.ts, dm.A, dm.H
    scale = HEAD_DIM ** -0.5

    def body(x_ref, c_ref, qa_ref, ka_ref, va_ref):
        lane = lax.broadcasted_iota(jnp.int32, (ts, LANES), 1)
        data = lane < HEAD_DIM
        for h in range(H):
            e, odd = h // 2, h % 2
            pieces = _split3(c_ref[:, h:h + 1])
            q = _head_lanes(x_ref[:, e * LANES:(e + 1) * LANES], odd) * scale
            k = _head_lanes(x_ref[:, A + e * LANES:A + (e + 1) * LANES], odd)
            v = _head_lanes(x_ref[:, 2 * A + e * LANES:2 * A + (e + 1) * LANES], odd)
            qa = jnp.where(data, q, jnp.where((lane >= _C1) & (lane < _C2), 1.0, 0.0))
            qa_ref[h] = _put3(qa, lane, _C0, pieces).astype(BF16)
            ka = jnp.where(data, k, jnp.where((lane < _C1) | ((lane >= _C2) & (lane < _C2 + 3)), 1.0, 0.0))
            ka_ref[h] = _put3(ka, lane, _C1, pieces, -1.0).astype(BF16)
            va_ref[h] = jnp.where(data, v, jnp.where(lane < _C1, 1.0, 0.0)).astype(BF16)

    spec = pl.BlockSpec((H, ts, LANES), lambda i: (0, i, 0))
    return pl.pallas_call(
        body, name="attn_prep", grid=(dm.S // ts,),
        in_specs=[_row_spec(ts, 3 * A), _row_spec(ts, LANES)], out_specs=[spec] * 3,
        out_shape=[jax.ShapeDtypeStruct((H, dm.S, LANES), BF16)] * 3, compiler_params=_cp(1),
    )(qkv, cum)


def _attn_fwd(qa, ka, va, dm):
    tq, A, H, S = dm.ts, dm.A, dm.H, dm.S
    nq = S // tq

    def body(qa_ref, ka_ref, va_ref, o_ref, lse_ref, top_scr, qb_scr, m_scr, acc_scr):
        i = pl.program_id(0)
        row = lax.broadcasted_iota(jnp.int32, (tq, tq), 0)
        col = lax.broadcasted_iota(jnp.int32, (tq, tq), 1)
        lane = lax.broadcasted_iota(jnp.int32, (tq, LANES), 1)

        def logits(q_ref, j, h, diagonal):
            s = _dot_nt(q_ref[h], ka_ref[h, pl.ds(pl.multiple_of(j * tq, tq), tq), :])
            return jnp.where(row >= col, s, -1e30) if diagonal else s

        def maxima(j, carry):
            for h in range(H):
                top_scr[h] = jnp.maximum(top_scr[h], logits(qa_ref, j, h, False))
            return carry

        for h in range(H):
            top_scr[h] = logits(qa_ref, i, h, True)
        lax.fori_loop(0, i, maxima, 0)
        for h in range(H):
            m = jnp.max(top_scr[h], axis=1, keepdims=True)
            m_scr[h] = m
            qb_scr[h] = _put3(qa_ref[h].astype(F32), lane, _C2, _split3(m), -1.0).astype(BF16)

        def weigh(j, carry):
            rows = pl.ds(pl.multiple_of(j * tq, tq), tq)
            for h in range(H):
                p = jnp.exp(logits(qb_scr, j, h, False)).astype(BF16)
                acc_scr[h] += jnp.dot(p, va_ref[h, rows, :], preferred_element_type=F32)
            return carry

        for h in range(H):
            p = jnp.exp(logits(qb_scr, i, h, True)).astype(BF16)
            acc_scr[h] = jnp.dot(p, va_ref[h, pl.ds(pl.multiple_of(i * tq, tq), tq), :], preferred_element_type=F32)
        lax.fori_loop(0, i, weigh, 0)
        lse_all = jnp.zeros((tq, LANES), F32)
        for h in range(H):
            acc = acc_scr[h]
            l = acc[:, _C0:_C0 + 1]
            o_ref[:, h * HEAD_DIM:(h + 1) * HEAD_DIM] = (acc[:, :HEAD_DIM] / l).astype(o_ref.dtype)
            lse_all = jnp.where(lane == h, m_scr[h] + jnp.log(l), lse_all)
        lse_ref[...] = lse_all

    full = pl.BlockSpec((H, S, LANES), lambda i: (0, 0, 0))
    return pl.pallas_call(
        body, name="attn_fwd", grid=(nq,),
        in_specs=[pl.BlockSpec((H, tq, LANES), lambda i: (0, i, 0)), full, full],
        out_specs=[pl.BlockSpec((tq, A), lambda i: (i, 0)), _row_spec(tq, LANES)],
        out_shape=[jax.ShapeDtypeStruct((S, A), BF16), jax.ShapeDtypeStruct((S, LANES), F32)],
        scratch_shapes=[pltpu.VMEM((H, tq, tq), F32), pltpu.VMEM((H, tq, LANES), BF16), pltpu.VMEM((H, tq, 1), F32),
                        pltpu.VMEM((H, tq, LANES), F32)],
        compiler_params=_cp(1),
    )(qa, ka, va)


def _attn_prep_bwd(qa, lse, dcat, o, dm):
    ts, A, H = dm.ts, dm.A, dm.H

    def body(qa_ref, lse_ref, do_ref, o_ref, qb_ref, doa_ref):
        lane = lax.broadcasted_iota(jnp.int32, (ts, LANES), 1)
        data = lane < HEAD_DIM
        for h in range(H):
            e, odd = h // 2, h % 2
            qb_ref[h] = _put3(qa_ref[h].astype(F32), lane, _C2, _split3(lse_ref[:, h:h + 1]), -1.0).astype(BF16)
            do_pair = do_ref[:, e * LANES:(e + 1) * LANES]
            prod = do_pair * o_ref[:, e * LANES:(e + 1) * LANES].astype(F32)
            mine = (lane >= HEAD_DIM) if odd else data
            delta = jnp.sum(jnp.where(mine, prod, 0.0), axis=1, keepdims=True)
            doa = jnp.where(data, _head_lanes(do_pair, odd), 0.0)
            doa_ref[h] = _put3(doa, lane, _C0, _split3(delta), -1.0).astype(BF16)

    spec = pl.BlockSpec((H, ts, LANES), lambda i: (0, i, 0))
    return pl.pallas_call(
        body, name="attn_prep_bwd", grid=(dm.S // ts,),
        in_specs=[spec, _row_spec(ts, LANES), _row_spec(ts, A, 0), _row_spec(ts, A)], out_specs=[spec] * 2,
        out_shape=[jax.ShapeDtypeStruct((H, dm.S, LANES), BF16)] * 2, compiler_params=_cp(1),
    )(qa, lse, dcat, o)


def _attn_bwd(qb, ka, va, doa, dm):
    tq, H, S = dm.ts, dm.H, dm.S
    nq = S // tq

    def body(ka_ref, va_ref, qb_ref, doa_ref, dq_ref, dk_ref, dv_ref):
        j = pl.program_id(0)

        @pl.when(j == 0)
        def _():
            dq_ref[...] = jnp.zeros(dq_ref.shape, F32)
        dk_ref[...] = jnp.zeros(dk_ref.shape, F32)
        dv_ref[...] = jnp.zeros(dv_ref.shape, F32)
        row = lax.broadcasted_iota(jnp.int32, (tq, tq), 0)
        col = lax.broadcasted_iota(jnp.int32, (tq, tq), 1)

        def block(i, masked):
            rows = pl.ds(pl.multiple_of(i * tq, tq), tq)
            for h in range(H):
                q, do_ = qb_ref[h, rows, :], doa_ref[h, rows, :]
                k, v = ka_ref[h], va_ref[h]
                s = _dot_nt(q, k)
                if masked:
                    s = jnp.where(row >= col, s, -1e30)
                p = jnp.exp(s)
                dsb = (p * _dot_nt(do_, v)).astype(BF16)
                dv_ref[h] += _dot_tn(p.astype(BF16), do_)
                dk_ref[h] += _dot_tn(dsb, q)
                dq_ref[h, rows, :] += jnp.dot(dsb, k, preferred_element_type=F32)

        block(j, True)

        def step(i, carry):
            block(i, False)
            return carry

        lax.fori_loop(j + 1, nq, step, 0)

    blk = pl.BlockSpec((H, tq, LANES), lambda j: (0, j, 0))
    full = pl.BlockSpec((H, S, LANES), lambda j: (0, 0, 0))
    return pl.pallas_call(
        body, name="attn_bwd", grid=(nq,),
        in_specs=[blk, blk, full, full], out_specs=[full, blk, blk],
        out_shape=[jax.ShapeDtypeStruct((H, S, LANES), F32)] * 3, compiler_params=_cp(1),
    )(ka, va, qb, doa)


def _attn_post(dqa, dka, dva, dm):
    ts, A, H = dm.ts, dm.A, dm.H
    scale = HEAD_DIM ** -0.5

    def body(dq_ref, dk_ref, dv_ref, o_ref, dc_ref):
        lane = lax.broadcasted_iota(jnp.int32, (ts, LANES), 1)
        data = lane < HEAD_DIM
        dc = jnp.zeros((ts, LANES), F32)
        for h in range(H):
            dc = jnp.where(lane == h, dq_ref[h][:, _C0:_C0 + 1] - dk_ref[h][:, _C1:_C1 + 1], dc)
        dc_ref[...] = dc
        for part, (ref, mul) in enumerate(((dq_ref, scale), (dk_ref, 1.0), (dv_ref, 1.0))):
            for e in range(H // 2):
                pair = jnp.where(data, ref[2 * e], pltpu.roll(ref[2 * e + 1], HEAD_DIM, axis=1))
                o_ref[:, part * A + e * LANES:part * A + (e + 1) * LANES] = (pair * mul).astype(o_ref.dtype)

    spec = pl.BlockSpec((H, ts, LANES), lambda i: (0, i, 0))
    return pl.pallas_call(
        body, name="attn_post", grid=(dm.S // ts,),
        in_specs=[spec] * 3, out_specs=[_row_spec(ts, 3 * A), _row_spec(ts, LANES)],
        out_shape=[jax.ShapeDtypeStruct((dm.S, 3 * A), BF16), jax.ShapeDtypeStruct((dm.S, LANES), F32)],
        compiler_params=_cp(1),
    )(dqa, dka, dva)


def _glu(cf_rows, c):
    return cf_rows[:, :c] * _sigmoid(cf_rows[:, c:2 * c])


def _conv_fwd(cf, cw, cb, lg, lb, dm):
    ts, C = dm.ts, dm.C
    per = ts // HALO

    def body(cf_ref, halo_ref, w_ref, cb_ref, lg_ref, lb_ref, u3_ref, u1_ref):
        i = pl.program_id(0)
        prev = jnp.where(i > 0, _glu(halo_ref[...], C), 0.0)
        win = jnp.concatenate([prev, _glu(cf_ref[...], C)], axis=0)
        u1 = jnp.zeros((ts, C), F32) + cb_ref[...]
        off = HALO - (CONV_K - 1)
        for k in range(CONV_K):
            u1 = u1 + w_ref[k:k + 1, :] * win[off + k:off + k + ts, :]
        u1_ref[...] = u1
        mu = jnp.mean(u1, axis=-1, keepdims=True)
        cen = u1 - mu
        rstd = lax.rsqrt(jnp.mean(cen * cen, axis=-1, keepdims=True) + EPS)
        u2 = cen * rstd * lg_ref[...] + lb_ref[...]
        u3_ref[...] = (u2 * _sigmoid(u2)).astype(u3_ref.dtype)

    return pl.pallas_call(
        body, name="conv_fwd", grid=(dm.S // ts,),
        in_specs=[pl.BlockSpec((ts, 2 * C), lambda i: (i, 0)),
                  pl.BlockSpec((HALO, 2 * C), lambda i: (jnp.maximum(i * per - 1, 0), 0)),
                  pl.BlockSpec((HALO, C), lambda i: (0, 0))] + [_vec_spec(C)] * 3,
        out_specs=[_row_spec(ts, C)] * 2,
        out_shape=[jax.ShapeDtypeStruct((dm.S, C), BF16), jax.ShapeDtypeStruct((dm.S, C), F32)],
        compiler_params=_cp(1),
    )(cf, cf, cw, cb, lg, lb)


def _conv_bwd(dcat, u1, cf, cw, lg, lb, dm):
    ts, C = dm.ts, dm.C
    per = ts // HALO
    nt = dm.S // ts
    last_halo = dm.S // HALO - 1

    def ln_bwd(du3, u1v, lg_v, lb_v):
        mu = jnp.mean(u1v, axis=-1, keepdims=True)
        cen = u1v - mu
        rstd = lax.rsqrt(jnp.mean(cen * cen, axis=-1, keepdims=True) + EPS)
        uhat = cen * rstd
        u2 = uhat * lg_v + lb_v
        sg = _sigmoid(u2)
        du2 = du3 * (sg * (1.0 + u2 * (1.0 - sg)))
        duh = du2 * lg_v
        du1 = rstd * (duh - jnp.mean(duh, axis=-1, keepdims=True) - uhat * jnp.mean(duh * uhat, axis=-1, keepdims=True))
        return du1, du2, uhat

    def body(d_ref, dn_ref, u1_ref, u1n_ref, cf_ref, halo_ref, w_ref, lg_ref, lb_ref,
             dcf_ref, dw_ref, dcb_ref, dlg_ref, dlb_ref):
        i = pl.program_id(0)

        @pl.when(i == 0)
        def _():
            dw_ref[...] = jnp.zeros_like(dw_ref)
            dcb_ref[...] = jnp.zeros_like(dcb_ref)
            dlg_ref[...] = jnp.zeros_like(dlg_ref)
            dlb_ref[...] = jnp.zeros_like(dlb_ref)
        lg_v, lb_v = lg_ref[...], lb_ref[...]
        du1, du2, uhat = ln_bwd(d_ref[...], u1_ref[...], lg_v, lb_v)
        du1n, _, _ = ln_bwd(dn_ref[...], u1n_ref[...], lg_v, lb_v)
        du1n = jnp.where(i < nt - 1, du1n, 0.0)
        dlg_ref[...] += jnp.sum(du2 * uhat, axis=0, keepdims=True)
        dlb_ref[...] += jnp.sum(du2, axis=0, keepdims=True)
        dcb_ref[...] += jnp.sum(du1, axis=0, keepdims=True)
        dwin = jnp.concatenate([du1, du1n], axis=0)
        cfv = cf_ref[...]
        cv, sg = cfv[:, :C], _sigmoid(cfv[:, C:2 * C])
        prev = jnp.where(i > 0, _glu(halo_ref[...], C), 0.0)
        uwin = jnp.concatenate([prev, cv * sg], axis=0)
        du0 = jnp.zeros((ts, C), F32)
        off = HALO - (CONV_K - 1)
        for k in range(CONV_K):
            back = CONV_K - 1 - k
            du0 = du0 + w_ref[k:k + 1, :] * dwin[back:back + ts, :]
            dw_ref[k:k + 1, :] += jnp.sum(du1 * uwin[off + k:off + k + ts, :], axis=0, keepdims=True)
        dcf_ref[:, :C] = (du0 * sg).astype(dcf_ref.dtype)
        dcf_ref[:, C:] = (du0 * cv * sg * (1.0 - sg)).astype(dcf_ref.dtype)

    ucol = dm.A // C
    return pl.pallas_call(
        body, name="conv_bwd", grid=(nt,),
        in_specs=[pl.BlockSpec((ts, C), lambda i: (i, ucol)),
                  pl.BlockSpec((HALO, C), lambda i: (jnp.minimum((i + 1) * per, last_halo), ucol)),
                  pl.BlockSpec((ts, C), lambda i: (i, 0)),
                  pl.BlockSpec((HALO, C), lambda i: (jnp.minimum((i + 1) * per, last_halo), 0)),
                  pl.BlockSpec((ts, 2 * C), lambda i: (i, 0)),
                  pl.BlockSpec((HALO, 2 * C), lambda i: (jnp.maximum(i * per - 1, 0), 0)),
                  pl.BlockSpec((HALO, C), lambda i: (0, 0)), _vec_spec(C), _vec_spec(C)],
        out_specs=[_row_spec(ts, 2 * C), pl.BlockSpec((HALO, C), lambda i: (0, 0))] + [_vec_spec(C)] * 3,
        out_shape=[jax.ShapeDtypeStruct((dm.S, 2 * C), BF16), jax.ShapeDtypeStruct((HALO, C), F32)]
        + [jax.ShapeDtypeStruct((1, C), F32)] * 3,
        compiler_params=_cp(1),
    )(dcat, dcat, u1, u1, cf, cf, cw, lg, lb)


def _ada_fwd(c16, ada_w, ada_b_cols, dm):
    L, D, n = ada_w.shape
    tn = _tile(n, 512)

    def body(c_ref, w_ref, b_ref, o_ref, a_ref):
        cv = c_ref[...]
        act = (cv * _sigmoid(cv)).astype(BF16)
        a_ref[...] = act
        o_ref[...] = jnp.dot(act, w_ref[...].astype(BF16), preferred_element_type=F32) + b_ref[...]

    return pl.pallas_call(
        body, name="ada_fwd", grid=(L, n // tn),
        in_specs=[pl.BlockSpec((16, D), lambda l, j: (0, 0)), pl.BlockSpec((None, D, tn), lambda l, j: (l, 0, j)),
                  pl.BlockSpec((None, 1, tn), lambda l, j: (l, 0, j))],
        out_specs=[pl.BlockSpec((None, 16, tn), lambda l, j: (l, 0, j)), pl.BlockSpec((16, D), lambda l, j: (0, 0))],
        out_shape=[jax.ShapeDtypeStruct((L, 16, n), F32), jax.ShapeDtypeStruct((16, D), BF16)],
        compiler_params=_cp(2),
    )(c16, ada_w, ada_b_cols)


def _ada_bwd(act16, dmod16):
    L, _, n = dmod16.shape
    D = act16.shape[1]
    tm = min(TN_TM, D)

    def body(a_ref, d_ref, o_ref):
        o_ref[...] = _dot_tn(a_ref[...], d_ref[...])

    return pl.pallas_call(
        body, name="ada_bwd", grid=(L, D // tm),
        in_specs=[pl.BlockSpec((16, tm), lambda l, i: (0, i)), pl.BlockSpec((None, 16, n), lambda l, i: (l, 0, 0))],
        out_specs=pl.BlockSpec((None, tm, n), lambda l, i: (l, i, 0)),
        out_shape=jax.ShapeDtypeStruct((L, D, n), F32), compiler_params=_cp(2),
    )(act16, dmod16)


def _sum_devices(g8):
    _, R, _ = g8.shape
    tr = _rows_tile(R)

    def body(g_ref, o_ref):
        acc = g_ref[0]
        for d in range(1, N_DEV):
            acc = acc + g_ref[d]
        o_ref[...] = acc

    return pl.pallas_call(
        body, name="sum_devices", grid=(R // tr,),
        in_specs=[pl.BlockSpec((N_DEV, tr, LANES), lambda i: (0, i, 0))],
        out_specs=pl.BlockSpec((tr, LANES), lambda i: (i, 0)),
        out_shape=jax.ShapeDtypeStruct((R, LANES), F32), compiler_params=_cp(1),
    )(g8)


def _rows_tile(r, cap=512):
    for t in (512, 256, 128, 64, 32, 16, 8):
        if t <= cap and r % t == 0:
            return t
    return r


def _adam_math(w, g, m, v):
    m = ADAM_B1 * m + (1.0 - ADAM_B1) * g
    v = ADAM_B2 * v + (1.0 - ADAM_B2) * (g * g)
    m_hat = m / (1.0 - ADAM_B1 ** ADAM_STEP)
    v_hat = v / (1.0 - ADAM_B2 ** ADAM_STEP)
    delta = -ADAM_LR * (m_hat / (jnp.sqrt(v_hat) + ADAM_EPS) + ADAM_WD * w)
    return delta, m, v


def _adamw_dense(w, m, v, g, name):
    R, Cc = w.shape
    tr = _rows_tile(R, ADAM_ROWS)

    def body(w_ref, m_ref, v_ref, g_ref, d_ref, mo_ref, vo_ref):
        d, mn, vn = _adam_math(w_ref[...], g_ref[...], m_ref[...], v_ref[...])
        d_ref[...] = d
        mo_ref[...] = mn
        vo_ref[...] = vn

    spec = pl.BlockSpec((tr, Cc), lambda i: (i, 0))
    return pl.pallas_call(
        body, name=name, grid=(R // tr,), in_specs=[spec] * 4, out_specs=[spec] * 3,
        out_shape=[jax.ShapeDtypeStruct((R, Cc), F32)] * 3, compiler_params=_cp(1),
    )(w, m, v, g)


def _adamw_shard(w, m, v, near, far, layer, prev, name):
    L, r, cc = w.shape
    tr, tc = (_rows_tile(r, ADAM_ROWS), cc) if r % 8 == 0 else (r, _tile(cc, 2 * LANES))

    def body(w_ref, m_ref, v_ref, n_ref, f_ref, *rest):
        g_ref, d_ref, mo_ref, vo_ref = rest[-4:]
        g = n_ref[0].astype(F32) + f_ref[0].astype(F32)
        for k in range(1, N_CHIPS):
            g = g + (n_ref[k].astype(F32) + f_ref[k].astype(F32))
        d, mn, vn = _adam_math(w_ref[...], g, m_ref[...], v_ref[...])
        g_ref[...] = g
        d_ref[...] = d
        mo_ref[...] = mn
        vo_ref[...] = vn

    wspec = pl.BlockSpec((None, tr, tc), lambda i, j: (layer, i, j))
    sspec = pl.BlockSpec((N_CHIPS, tr, tc), lambda i, j: (0, i, j))
    n_prev = 0 if prev is None else 4
    return pl.pallas_call(
        body, name=name, grid=(r // tr, cc // tc),
        in_specs=[wspec] * 3 + [sspec] * 2 + [pl.BlockSpec(memory_space=pl.ANY)] * n_prev,
        out_specs=[wspec] * 4,
        out_shape=[jax.ShapeDtypeStruct((L, r, cc), F32)] * 4,
        input_output_aliases={5 + t: t for t in range(n_prev)},
        compiler_params=_cp(2),
    )(w, m, v, near, far, *(prev or ()))


def _pack(vs):
    flat = jnp.concatenate([v.reshape(-1).astype(F32) for v in vs])
    pad = (-flat.shape[0]) % (64 * LANES)
    return jnp.pad(flat, (0, pad)).reshape(-1, LANES)


def _unpack(packed, shapes):
    flat = packed.reshape(-1)
    out, pos = [], 0
    for s in shapes:
        n = 1
        for d in s:
            n *= d
        out.append(flat[pos:pos + n].reshape(s))
        pos += n
    return out


def kernel(x, c, w_in, b_f, conv_w, conv_b, conv_ln_g, conv_ln_b, w_o, w_ffn_in, w_ffn_out, mix_pre_g, mix_post_g, ffn_pre_g, ffn_post_g, ada_w, ada_b, loss_target, m_w_in, m_b_f, m_conv_w, m_conv_b, m_conv_ln_g, m_conv_ln_b, m_w_o, m_w_ffn_in, m_w_ffn_out, m_mix_pre_g, m_mix_post_g, m_ffn_pre_g, m_ffn_post_g, m_ada_w, m_ada_b, v_w_in, v_b_f, v_conv_w, v_conv_b, v_conv_ln_g, v_conv_ln_b, v_w_o, v_w_ffn_in, v_w_ffn_out, v_mix_pre_g, v_mix_post_g, v_ffn_pre_g, v_ffn_post_g, v_ada_w, v_ada_b):
    params = dict(w_in=w_in, b_f=b_f, conv_w=conv_w, conv_b=conv_b, conv_ln_g=conv_ln_g, conv_ln_b=conv_ln_b, w_o=w_o,
                  w_ffn_in=w_ffn_in, w_ffn_out=w_ffn_out, mix_pre_g=mix_pre_g, mix_post_g=mix_post_g,
                  ffn_pre_g=ffn_pre_g, ffn_post_g=ffn_post_g, ada_w=ada_w, ada_b=ada_b)
    mom = dict(w_in=m_w_in, b_f=m_b_f, conv_w=m_conv_w, conv_b=m_conv_b, conv_ln_g=m_conv_ln_g, conv_ln_b=m_conv_ln_b,
               w_o=m_w_o, w_ffn_in=m_w_ffn_in, w_ffn_out=m_w_ffn_out, mix_pre_g=m_mix_pre_g, mix_post_g=m_mix_post_g,
               ffn_pre_g=m_ffn_pre_g, ffn_post_g=m_ffn_post_g, ada_w=m_ada_w, ada_b=m_ada_b)
    var = dict(w_in=v_w_in, b_f=v_b_f, conv_w=v_conv_w, conv_b=v_conv_b, conv_ln_g=v_conv_ln_g, conv_ln_b=v_conv_ln_b,
               w_o=v_w_o, w_ffn_in=v_w_ffn_in, w_ffn_out=v_w_ffn_out, mix_pre_g=v_mix_pre_g, mix_post_g=v_mix_post_g,
               ffn_pre_g=v_ffn_pre_g, ffn_post_g=v_ffn_post_g, ada_w=v_ada_w, ada_b=v_ada_b)

    S, D = x.shape[1], x.shape[2]
    L = w_in.shape[0]
    A = D // 2
    C = D - A
    H = A // HEAD_DIM
    F = w_ffn_out.shape[1] * N_CHIPS
    d_in = w_in.shape[2] * N_CHIPS
    NP = 3 * A + 2 * C + LANES
    dm = Dims(S=S, D=D, A=A, C=C, H=H, F=F, L=L, NP=NP, ts=min(ROW_TILE, S), tr=min(NORM_TILE, S))
    assert H <= 8 and A == C and d_in == 3 * A + H + 2 * C

    ix, iy, ic = _place()
    chip = 2 * ix + iy
    dev = 4 * ix + 2 * iy + ic
    x2 = x.reshape(S, D)
    tgt = loss_target.reshape(S, D)

    swap = lambda t: jnp.transpose(t, (0, 2, 1))
    big_state = dict(w_in=(swap(w_in), swap(m_w_in), swap(v_w_in)), w_o=(w_o, m_w_o, v_w_o),
                     w_ffn_in=(w_ffn_in, m_w_ffn_in, v_w_ffn_in), w_ffn_out=(w_ffn_out, m_w_ffn_out, v_w_ffn_out))

    def gather_start(l, after, after_rest=()):
        first = _exchange_start([w_in[l].T.astype(BF16)], "gather", after, f"gather_a_start_{l}")
        return first, gather_rest(l, [first["token"], *after_rest])

    def gather_rest(l, after):
        return _exchange_start([w_o[l].astype(BF16), w_ffn_in[l].astype(BF16), w_ffn_out[l].astype(BF16)], "gather",
                               after, f"gather_b_start_{l}")

    def gather_wait(st, after, name):
        return _exchange_wait(st, after, name)[1]

    first_0 = _exchange_start([w_in[0].T.astype(BF16)], "gather", [], "gather_a_start_0")

    c_all = _all_gather_devices(c.reshape(D // LANES, LANES), "gather_c", after=[first_0["token"]]).reshape(N_DEV, D)
    c16 = jnp.pad(c_all, ((0, 16 - N_DEV), (0, 0)))
    n_ada = ada_w.shape[2]
    ada_b_cols = lax.dynamic_slice_in_dim(ada_b, chip * n_ada, n_ada, axis=1).reshape(L, 1, n_ada)
    mod_cols, act16 = _ada_fwd(c16, ada_w, ada_b_cols, dm)
    conv_w_all, mod_all = _all_gather_chips([conv_w.reshape(L * CONV_K, -1), mod_cols.reshape(L * 16, n_ada)], "gather_mod")
    cwc = conv_w.shape[2]
    conv_w_full = conv_w_all.reshape(N_CHIPS, L, CONV_K, cwc).transpose(1, 2, 0, 3).reshape(L, CONV_K, C)
    conv_w_full = jnp.pad(conv_w_full, ((0, 0), (0, HALO - CONV_K), (0, 0)))
    mod_all = mod_all.reshape(N_CHIPS, L, 16, n_ada)
    mod_me = lax.dynamic_index_in_dim(mod_all, dev, axis=2, keepdims=False)
    mod_me = mod_me.transpose(1, 0, 2).reshape(L, N_MOD, 1, D)

    gather = [None] * L
    gather[0] = (first_0, gather_rest(0, [first_0["token"], mod_all]))

    def projection_of(g_in):
        w_nat = g_in.reshape(d_in, D)
        return jnp.concatenate([w_nat[:3 * A], w_nat[3 * A + H:], w_nat[3 * A:3 * A + H],
                                jnp.zeros((LANES - H, D), BF16)], axis=0)

    gathered = [None] * L
    vec = lambda p, l: p[l].reshape(1, -1)
    bf_pad = jnp.pad(b_f, ((0, 0), (0, LANES - H)))

    saved = []
    xin = x2
    h = _pre_norm(xin, vec(mix_pre_g, 0), mod_me[0, 1], mod_me[0, 0], dm, after=[gather[0][1]["token"]])
    dx = loss_part = None
    for l in range(L):
        (g_in,) = gather_wait(gather[l][0], [h], f"gather_a_wait_{l}")
        w_p = projection_of(g_in)
        order = [gather[l][1]["token"]]
        if l + 1 < L:
            gather[l + 1] = gather_start(l + 1, [g_in, gather[l][1]["token"]])
            order.append(gather[l + 1][1]["token"])
        qkv = _mm_nt(h, w_p, BF16, "mm_qkv", 0, 3 * A, after=order)
        cf = _mm_nt(h, w_p, F32, "mm_cf", 3 * A, 2 * C + LANES)
        cum = _gates_fwd(cf, vec(bf_pad, l), dm)
        qa, ka, va = _attn_prep(qkv, cum, dm)
        o, lse = _attn_fwd(qa, ka, va, dm)
        u3, u1 = _conv_fwd(cf, conv_w_full[l], vec(conv_b, l), vec(conv_ln_g, l), vec(conv_ln_b, l), dm)
        cat = jnp.concatenate([o, u3], axis=1)
        g_o, wfi, g_fo = gather_wait(gather[l][1], [cat], f"gather_b_wait_{l}")
        wo, wfo = g_o.reshape(D, D), g_fo.reshape(F, D)
        gathered[l] = (w_p, wo, wfi, wfo)
        y = _mm_nn(cat, wo, BF16, "mm_o")
        x1, h2 = _res_norm(xin, y, vec(mix_post_g, l), mod_me[l, 2], vec(ffn_pre_g, l), mod_me[l, 4], mod_me[l, 3], dm)
        gu, a = _ffn_in_swiglu(h2, wfi, "mm_ffn_in")
        y2 = _mm_nn(a, wfo, BF16, "mm_ffn_out")
        saved.append(dict(xin=xin, h=h, qa=qa, ka=ka, va=va, cf=cf, o=o, lse=lse, u1=u1, cat=cat, y=y,
                          x1=x1, h2=h2, gu=gu, a=a, y2=y2))
        if l + 1 < L:
            xin, h = _res_norm(x1, y2, vec(ffn_post_g, l), mod_me[l, 5], vec(mix_pre_g, l + 1),
                               mod_me[l + 1, 1], mod_me[l + 1, 0], dm)
        else:
            dx, loss_part = _res_loss(x1, y2, vec(ffn_post_g, l), mod_me[l, 5], tgt, dm)
    loss = lax.psum(loss_part[0, 0], ("x", "y", "c"))

    small = [None] * L
    big = [None] * L
    forward = [None] * L

    def forward_start(l, after):
        near = []
        for st, nm in zip(big[l], ("a", "b")):
            near += _exchange_wait(st, after, f"scatter_{nm}_wait_{l}")[1]
        return _exchange_start(near, "sibling", [], f"forward_start_{l}")

    order = []
    for l in reversed(range(L)):
        w_p, wo, wfi, wfo = gathered[l]
        sv = saved[l]
        dy2, d_gfpost, d_g2 = _post_bwd(dx, sv["y2"], vec(ffn_post_g, l), mod_me[l, 5], dm, after=order)
        dgu = _ffn_out_bwd_swiglu(dy2, wfo, sv["gu"], "mm_da")
        g_wfo = _mm_tn(sv["a"], dy2, BF16, "mm_dwfo")
        g_wfi = _ffn_in_bwd_w(sv["h2"], dgu, N_CHIPS, "mm_dwfi")
        dh2 = _ffn_in_bwd_x(dgu, wfi, "mm_dh2")
        scatter_ffn = _exchange_start([g_wfi, g_wfo.reshape(N_CHIPS, F // N_CHIPS, D)], "scatter", [], f"scatter_b_start_{l}")
        dx1, d_sh2, d_sc2, d_gfpre = _pre_bwd(dh2, dx, sv["x1"], vec(ffn_pre_g, l), mod_me[l, 4], dm,
                                              after=[scatter_ffn["token"]])
        dy, d_gpost, d_g1 = _post_bwd(dx1, sv["y"], vec(mix_post_g, l), mod_me[l, 2], dm)
        dcat = _mm_nt(dy, wo, F32, "mm_dcat")
        g_wo = _mm_tn(sv["cat"], dy, BF16, "mm_dwo")
        dcfc, d_cw, d_cb, d_lg, d_lb = _conv_bwd(dcat, sv["u1"], sv["cf"], conv_w_full[l], vec(conv_ln_g, l),
                                                 vec(conv_ln_b, l), dm)
        qb, doa = _attn_prep_bwd(sv["qa"], sv["lse"], dcat, sv["o"], dm)
        dqkv, dcum = _attn_post(*_attn_bwd(qb, sv["ka"], sv["va"], doa, dm), dm)
        dfl, d_bf = _gates_bwd(dcum, sv["cf"], vec(bf_pad, l), dm)
        dproj = jnp.concatenate([dqkv, dcfc, dfl], axis=1)
        dh = _mm_nn(dproj, w_p, BF16, "mm_dh")
        g_wp = _mm_tn(dproj, sv["h"], BF16, "mm_dwp")
        dx, d_sh1, d_sc1, d_gpre = _pre_bwd(dh, dx1, sv["xin"], vec(mix_pre_g, l), mod_me[l, 1], dm)
        g_nat = jnp.concatenate([g_wp[:3 * A], g_wp[3 * A + 2 * C:3 * A + 2 * C + H], g_wp[3 * A:3 * A + 2 * C]], axis=0)
        g_win = g_nat.reshape(N_CHIPS, d_in // N_CHIPS, D)
        g_mix = [g_win, g_wo.reshape(N_CHIPS, D // N_CHIPS, D)]
        order = []
        if l > 0:
            scatter_mix = _exchange_start(g_mix, "scatter", [], f"scatter_a_start_{l}")
            order.append(scatter_mix["token"])
            big[l] = (scatter_mix, scatter_ffn)
        if l + 1 < L:
            forward[l + 1] = forward_start(l + 1, [dx])
            order.append(forward[l + 1]["token"])
        small[l] = dict(b_f=d_bf[0, :H], conv_b=d_cb[0], conv_ln_g=d_lg[0], conv_ln_b=d_lb[0], mix_pre_g=d_gpre[0],
                        mix_post_g=d_gpost[0], ffn_pre_g=d_gfpre[0], ffn_post_g=d_gfpost[0],
                        dmod=jnp.concatenate([d_sh1, d_sc1, d_g1, d_sh2, d_sc2, d_g2], axis=1)[0],
                        conv_w=d_cw[:CONV_K])
    grad_x = dx.reshape(1, S, D)

    keys_small = ["b_f", "conv_b", "conv_ln_g", "conv_ln_b", "mix_pre_g", "mix_post_g", "ffn_pre_g", "ffn_post_g",
                  "dmod", "conv_w"]
    stacked = [jnp.stack([small[l][k] for l in range(L)]) for k in keys_small]
    shapes = [s.shape for s in stacked]
    pack = _pack(stacked)
    small_exchange = _exchange_start([pack], "devices", order, "gather_small_start")
    big[0] = (_exchange_start(g_mix, "scatter", [small_exchange["token"]], "scatter_a_start_0"), scatter_ffn)

    names_big = ["w_in", "w_o", "w_ffn_in", "w_ffn_out"]
    res_big = {n: None for n in names_big}

    def update(l, after):
        near, far = _exchange_wait(forward[l], after, f"forward_wait_{l}")
        for t, n in enumerate(names_big):
            res_big[n] = _adamw_shard(*big_state[n], near[t], far[t], l, res_big[n], f"adamw_{n}_{l}")
        return [res_big[n][1] for n in names_big]

    done = [big[0][0]["token"]]
    for l in reversed(range(1, L)):
        done = update(l, done)
    forward[0] = forward_start(0, done)

    (pack8,) = _exchange_wait(small_exchange, [forward[0]["token"]], "gather_small_wait")[1]
    summed = dict(zip(keys_small, _unpack(_sum_devices(pack8), shapes)))
    at_dmod = sum(s.size for s in stacked[:keys_small.index("dmod")])
    dmod_all = pack8.reshape(N_DEV, -1)[:, at_dmod:at_dmod + L * N_MOD * D].reshape(N_DEV, L, N_MOD * D)
    grads = {k: summed[k] for k in keys_small[:8]}
    grads["ada_b"] = summed["dmod"]
    grads["conv_w"] = lax.dynamic_slice_in_dim(summed["conv_w"], chip * cwc, cwc, axis=2)

    dmod_cols = lax.dynamic_slice_in_dim(dmod_all.reshape(N_DEV, L, N_CHIPS, n_ada), chip, 1, axis=2)
    dmod16 = jnp.pad(dmod_cols.reshape(N_DEV, L, n_ada).transpose(1, 0, 2), ((0, 0), (0, 16 - N_DEV), (0, 0))).astype(BF16)
    grads["ada_w"] = _ada_bwd(act16, dmod16)

    d_aw, m_aw, v_aw = _adamw_dense(ada_w.reshape(L * D, n_ada), m_ada_w.reshape(L * D, n_ada),
                                    v_ada_w.reshape(L * D, n_ada), grads["ada_w"].reshape(L * D, n_ada), "adamw_ada_w")
    names_small = ["b_f", "conv_w", "conv_b", "conv_ln_g", "conv_ln_b", "mix_pre_g", "mix_post_g", "ffn_pre_g",
                   "ffn_post_g", "ada_b"]
    shapes_small = [params[n].shape for n in names_small]
    d_s, m_s, v_s = _adamw_dense(_pack([params[n] for n in names_small]), _pack([mom[n] for n in names_small]),
                                 _pack([var[n] for n in names_small]), _pack([grads[n] for n in names_small]),
                                 "adamw_small")
    delta_w = dict(zip(names_small, _unpack(d_s, shapes_small)))
    new_m = dict(zip(names_small, _unpack(m_s, shapes_small)))
    new_v = dict(zip(names_small, _unpack(v_s, shapes_small)))
    delta_w["ada_w"], new_m["ada_w"], new_v["ada_w"] = (t.reshape(L, D, n_ada) for t in (d_aw, m_aw, v_aw))

    update(0, [d_s, d_aw])
    res_big["w_in"] = [swap(t) for t in res_big["w_in"]]
    for n in names_big:
        grads[n], delta_w[n], new_m[n], new_v[n] = res_big[n]

    return (loss, grad_x, *[grads[n] for n in WEIGHTS], *[delta_w[n] for n in WEIGHTS],
            *[new_m[n] for n in WEIGHTS], *[new_v[n] for n in WEIGHTS])
```

```python
import collections
import functools

import jax
import jax.numpy as jnp
from jax import lax
from jax.experimental import pallas as pl
from jax.experimental.pallas import tpu as pltpu

F32 = jnp.float32
BF16 = jnp.bfloat16
MESH = pl.DeviceIdType.MESH

HEAD_DIM = 64
CONV_K = 31
N_MOD = 6
EPS = 1e-6
N_CHIPS = 4
N_DEV = 8
LANES = 128
HALO = 32
ROW_TILE = 256
NORM_TILE = 512
MM_TM = 512
MM_TM_PLAIN = 1024
MM_TN_MAX = 1408
ADAM_ROWS = 256
TN_TM = 512
TN_TM_MAX = 512
VMEM_LIMIT = 56 * 1024 * 1024

ADAM_LR = 0.001
ADAM_B1 = 0.9
ADAM_B2 = 0.999
ADAM_EPS = 1e-08
ADAM_WD = 0.01
ADAM_STEP = 10

WEIGHTS = ['w_in', 'b_f', 'conv_w', 'conv_b', 'conv_ln_g', 'conv_ln_b', 'w_o', 'w_ffn_in', 'w_ffn_out',
           'mix_pre_g', 'mix_post_g', 'ffn_pre_g', 'ffn_post_g', 'ada_w', 'ada_b']

Dims = collections.namedtuple("Dims", "S D A C H F L NP ts tr")


def _cp(n_grid=0):
    if n_grid:
        return pltpu.CompilerParams(dimension_semantics=("arbitrary",) * n_grid, vmem_limit_bytes=VMEM_LIMIT)
    return pltpu.CompilerParams(vmem_limit_bytes=VMEM_LIMIT)


def _tile(n, cap, also=None):
    best = None
    t = LANES
    while t <= min(n, cap):
        if n % t == 0 and (also is None or also % t == 0):
            best = t
        t += LANES
    assert best is not None, (n, cap, also)
    return best


def _bf(v):
    return v if v.dtype == BF16 else v.astype(BF16)


def _place():
    return lax.axis_index("x"), lax.axis_index("y"), lax.axis_index("c")


def _flip(v, d):
    return 1 - v if d else v


def _all_gather_devices(a, name, after=()):
    def body(a_ref, *rest):
        o_ref, send, recv, lsem = rest[len(after):]
        x, y, c = _place()
        me = 4 * x + 2 * y + c
        local = pltpu.make_async_copy(a_ref, o_ref.at[me], lsem)
        local.start()
        copies = []
        for k in range(1, N_DEV):
            peer = (_flip(x, (k >> 2) & 1), _flip(y, (k >> 1) & 1), _flip(c, k & 1))
            cp = pltpu.make_async_remote_copy(src_ref=a_ref, dst_ref=o_ref.at[me], send_sem=send.at[k - 1],
                                              recv_sem=recv.at[k - 1], device_id=peer, device_id_type=MESH)
            cp.start()
            copies.append(cp)
        for cp in copies:
            cp.wait()
        local.wait()

    return pl.pallas_call(
        body, name=name,
        out_shape=jax.ShapeDtypeStruct((N_DEV,) + a.shape, a.dtype),
        in_specs=[pl.BlockSpec(memory_space=pl.ANY)] * (1 + len(after)),
        out_specs=pl.BlockSpec(memory_space=pl.ANY),
        scratch_shapes=[pltpu.SemaphoreType.DMA((N_DEV - 1,)), pltpu.SemaphoreType.DMA((N_DEV - 1,)),
                        pltpu.SemaphoreType.DMA],
    )(a, *after)


def _all_gather_chips(arrays, name):
    n = len(arrays)

    def body(*refs):
        a_refs, o_refs = refs[:n], refs[n:2 * n]
        send, recv, lsem = refs[2 * n:]
        x, y, c = _place()
        me = 2 * x + y
        copies = []
        for i in range(n):
            local = pltpu.make_async_copy(a_refs[i], o_refs[i].at[me], lsem.at[i])
            local.start()
            copies.append(local)
            for k in range(1, N_CHIPS):
                peer = (_flip(x, (k >> 1) & 1), _flip(y, k & 1), c)
                cp = pltpu.make_async_remote_copy(src_ref=a_refs[i], dst_ref=o_refs[i].at[me],
                                                  send_sem=send.at[i, k - 1], recv_sem=recv.at[i, k - 1],
                                                  device_id=peer, device_id_type=MESH)
                cp.start()
                copies.append(cp)
        for cp in copies:
            cp.wait()

    return pl.pallas_call(
        body, name=name,
        out_shape=[jax.ShapeDtypeStruct((N_CHIPS,) + a.shape, a.dtype) for a in arrays],
        in_specs=[pl.BlockSpec(memory_space=pl.ANY)] * n,
        out_specs=[pl.BlockSpec(memory_space=pl.ANY)] * n,
        scratch_shapes=[pltpu.SemaphoreType.DMA((n, N_CHIPS - 1)), pltpu.SemaphoreType.DMA((n, N_CHIPS - 1)),
                        pltpu.SemaphoreType.DMA((n,))],
    )(*arrays)


_HBM = pl.BlockSpec(memory_space=pltpu.HBM)
_SEM = pl.BlockSpec(memory_space=pltpu.SEMAPHORE)
_EFFECT = pltpu.SideEffectType.DATAFLOW_SIDE_EFFECTING


def _n_copies(mode):
    return {"sibling": 1, "devices": N_DEV - 1}.get(mode, N_CHIPS - 1)


def _chip_copies(srcs, lands, send, recv, mode):
    x, y, c = _place()
    me = 2 * x + y
    copies = []
    for i in range(len(srcs)):
        if mode == "sibling":
            copies.append(pltpu.make_async_remote_copy(src_ref=srcs[i], dst_ref=lands[i], send_sem=send.at[i],
                                                       recv_sem=recv.at[i], device_id=(x, y, 1 - c), device_id_type=MESH))
            continue
        if mode == "devices":
            for k in range(1, N_DEV):
                peer = (_flip(x, (k >> 2) & 1), _flip(y, (k >> 1) & 1), _flip(c, k & 1))
                s = i * (N_DEV - 1) + k - 1
                copies.append(pltpu.make_async_remote_copy(src_ref=srcs[i], dst_ref=lands[i].at[2 * me + c],
                                                           send_sem=send.at[s], recv_sem=recv.at[s], device_id=peer,
                                                           device_id_type=MESH))
            continue
        for k in range(1, N_CHIPS):
            px, py = _flip(x, (k >> 1) & 1), _flip(y, k & 1)
            src = srcs[i].at[2 * px + py] if mode == "scatter" else srcs[i]
            s = i * (N_CHIPS - 1) + k - 1
            copies.append(pltpu.make_async_remote_copy(src_ref=src, dst_ref=lands[i].at[me], send_sem=send.at[s],
                                                       recv_sem=recv.at[s], device_id=(px, py, c), device_id_type=MESH))
    return copies


def _own_copies(srcs, lands, own, mode):
    x, y, c = _place()
    me = 2 * x + y
    slot = 2 * me + c if mode == "devices" else me
    return [pltpu.make_async_copy(srcs[i].at[me] if mode == "scatter" else srcs[i], lands[i].at[slot], own.at[i])
            for i in range(len(srcs))]


_ORDER = pl.BlockSpec(memory_space=pl.ANY)


def _exchange_start(arrays, mode, after, name):
    n = len(arrays)
    n_sems = 2 if mode == "sibling" else 3

    def body(*refs):
        srcs, lands = refs[:n], refs[n:2 * n]
        sems = refs[2 * n + len(after):2 * n + len(after) + n_sems]
        token = refs[-1]
        for cp in _chip_copies(srcs, lands, sems[0], sems[1], mode):
            cp.start()
        if mode != "sibling":
            for cp in _own_copies(srcs, lands, sems[2], mode):
                cp.start()
        token[...] = jnp.zeros_like(token)

    lead = {"gather": (N_CHIPS,), "devices": (N_DEV,)}.get(mode, ())
    land_shapes = [lead + a.shape for a in arrays]
    n_sem = n * _n_copies(mode)
    sem_shapes = [pltpu.SemaphoreType.DMA((n_sem,)), pltpu.SemaphoreType.DMA((n_sem,)), pltpu.SemaphoreType.DMA((n,))]
    outs = pl.pallas_call(
        body, name=name,
        out_shape=(*sem_shapes[:n_sems],
                   *[pltpu.HBM(a.shape, a.dtype) for a in arrays],
                   *[pltpu.HBM(s, a.dtype) for s, a in zip(land_shapes, arrays)],
                   jax.ShapeDtypeStruct((8, LANES), F32)),
        in_specs=[_HBM] * (2 * n) + [_ORDER] * len(after),
        out_specs=(*[_SEM] * n_sems, *[_HBM] * (2 * n), pl.BlockSpec(memory_space=pltpu.VMEM)),
        input_output_aliases={i: n_sems + i for i in range(2 * n)},
        compiler_params=pltpu.CompilerParams(has_side_effects=_EFFECT),
    )(*[pltpu.with_memory_space_constraint(a, pltpu.HBM) for a in arrays],
      *[pltpu.with_memory_space_constraint(lax.empty(s, a.dtype), pltpu.HBM) for s, a in zip(land_shapes, arrays)],
      *after)
    return dict(sems=outs[:n_sems], srcs=outs[n_sems:n_sems + n], lands=outs[n_sems + n:n_sems + 2 * n], token=outs[-1],
                mode=mode)


def _exchange_wait(st, after, name):
    n = len(st["srcs"])
    mode = st["mode"]
    n_sems = len(st["sems"])

    def body(*refs):
        srcs, lands = refs[:n], refs[n:2 * n]
        sems = refs[2 * n:2 * n + n_sems]
        for cp in _chip_copies(srcs, lands, sems[0], sems[1], mode):
            cp.wait_send()
            cp.wait_recv()
        if mode != "sibling":
            for cp in _own_copies(srcs, lands, sems[2], mode):
                cp.wait()

    outs = pl.pallas_call(
        body, name=name,
        out_shape=tuple(pltpu.HBM(a.shape, a.dtype) for a in (*st["srcs"], *st["lands"])),
        in_specs=[_HBM] * (2 * n) + [_SEM] * n_sems + [_ORDER] * len(after),
        out_specs=tuple([_HBM] * (2 * n)),
        input_output_aliases={i: i for i in range(2 * n)},
        compiler_params=pltpu.CompilerParams(has_side_effects=_EFFECT),
    )(*st["srcs"], *st["lands"], *st["sems"], *after)
    return outs[:n], outs[n:]


def _matmul(a, b, contract, grid, a_spec, b_spec, o_spec, out_shape, name, after=()):
    def body(a_ref, b_ref, *rest):
        o_ref = rest[len(after)]
        r = lax.dot_general(_bf(a_ref[...]), _bf(b_ref[...]), (contract, ((), ())), preferred_element_type=F32)
        o_ref[...] = r.astype(o_ref.dtype)

    return pl.pallas_call(
        body, name=name, grid=grid, in_specs=[a_spec, b_spec] + [_ORDER] * len(after), out_specs=o_spec,
        out_shape=out_shape, compiler_params=_cp(len(grid)),
    )(a, b, *after)


def _mm_nn(a, b, out_dtype, name, col0=0, n=None, after=()):
    m, k = a.shape
    n = b.shape[1] - col0 if n is None else n
    tm = min(MM_TM_PLAIN, m)
    tn = _tile(n, MM_TN_MAX, also=col0 if col0 else None)
    off = col0 // tn
    return _matmul(a, b, ((1,), (0,)), (n // tn, m // tm),
                   pl.BlockSpec((tm, k), lambda j, i: (i, 0)),
                   pl.BlockSpec((k, tn), lambda j, i: (0, j + off)),
                   pl.BlockSpec((tm, tn), lambda j, i: (i, j)),
                   jax.ShapeDtypeStruct((m, n), out_dtype), name, after=after)


def _mm_nt(a, b, out_dtype, name, row0=0, n=None, after=()):
    m, k = a.shape
    n = b.shape[0] - row0 if n is None else n
    tm = min(MM_TM_PLAIN, m)
    tn = _tile(n, MM_TN_MAX, also=row0 if row0 else None)
    off = row0 // tn
    return _matmul(a, b, ((1,), (1,)), (n // tn, m // tm),
                   pl.BlockSpec((tm, k), lambda j, i: (i, 0)),
                   pl.BlockSpec((tn, k), lambda j, i: (j + off, 0)),
                   pl.BlockSpec((tm, tn), lambda j, i: (i, j)),
                   jax.ShapeDtypeStruct((m, n), out_dtype), name, after=after)


def _mm_tn(a, b, out_dtype, name):
    k, m = a.shape
    n = b.shape[1]
    tm = _tile(m, TN_TM_MAX)
    tn = _tile(n, MM_TN_MAX)
    return _matmul(a, b, ((0,), (0,)), (n // tn, m // tm),
                   pl.BlockSpec((k, tm), lambda j, i: (0, i)),
                   pl.BlockSpec((k, tn), lambda j, i: (0, j)),
                   pl.BlockSpec((tm, tn), lambda j, i: (i, j)),
                   jax.ShapeDtypeStruct((m, n), out_dtype), name)


def _ffn_in_swiglu(x, w3, name):
    m, k = x.shape
    half, nb = w3.shape[0] // 2, w3.shape[2]
    tm = min(MM_TM, m)

    def body(x_ref, wg_ref, wu_ref, gu_ref, a_ref):
        xv = x_ref[...]
        g = jnp.dot(xv, wg_ref[...], preferred_element_type=F32)
        u = jnp.dot(xv, wu_ref[...], preferred_element_type=F32)
        gu_ref[0] = g.astype(gu_ref.dtype)
        gu_ref[1] = u.astype(gu_ref.dtype)
        a_ref[...] = (g * _sigmoid(g) * u).astype(a_ref.dtype)

    return pl.pallas_call(
        body, name=name, grid=(half, m // tm),
        in_specs=[pl.BlockSpec((tm, k), lambda j, i: (i, 0)), pl.BlockSpec((None, k, nb), lambda j, i: (j, 0, 0)),
                  pl.BlockSpec((None, k, nb), lambda j, i: (j + half, 0, 0))],
        out_specs=[pl.BlockSpec((2, tm, nb), lambda j, i: (0, i, j)), pl.BlockSpec((tm, nb), lambda j, i: (i, j))],
        out_shape=[jax.ShapeDtypeStruct((2, m, half * nb), BF16), jax.ShapeDtypeStruct((m, half * nb), BF16)],
        compiler_params=_cp(2),
    )(x, w3, w3)


def _ffn_out_bwd_swiglu(dy, w_out, gu, name):
    m, k = dy.shape
    f = w_out.shape[0]
    tm = min(MM_TM, m)
    tn = _tile(f, MM_TN_MAX)

    def body(dy_ref, w_ref, gu_ref, d_ref):
        da = _dot_nt(dy_ref[...], w_ref[...])
        g, u = gu_ref[0].astype(F32), gu_ref[1].astype(F32)
        sg = _sigmoid(g)
        d_ref[0] = (da * u * (sg * (1.0 + g * (1.0 - sg)))).astype(d_ref.dtype)
        d_ref[1] = (da * (g * sg)).astype(d_ref.dtype)

    return pl.pallas_call(
        body, name=name, grid=(f // tn, m // tm),
        in_specs=[pl.BlockSpec((tm, k), lambda j, i: (i, 0)), pl.BlockSpec((tn, k), lambda j, i: (j, 0)),
                  pl.BlockSpec((2, tm, tn), lambda j, i: (0, i, j))],
        out_specs=pl.BlockSpec((2, tm, tn), lambda j, i: (0, i, j)),
        out_shape=jax.ShapeDtypeStruct((2, m, f), BF16), compiler_params=_cp(2),
    )(dy, w_out, gu)


def _ffn_in_bwd_x(dgu, w3, name):
    _, m, f = dgu.shape
    nj, n, nb = w3.shape
    per = f // nb
    tm = min(MM_TM, m)
    tn = _tile(n, 512)

    def body(d_ref, w_ref, o_ref):
        acc = None
        for j in range(nj):
            part = _dot_nt(d_ref[j // per][:, (j % per) * nb:(j % per + 1) * nb], w_ref[j])
            acc = part if acc is None else acc + part
        o_ref[...] = acc.astype(o_ref.dtype)

    return pl.pallas_call(
        body, name=name, grid=(m // tm, n // tn),
        in_specs=[pl.BlockSpec((2, tm, f), lambda i, j: (0, i, 0)), pl.BlockSpec((nj, tn, nb), lambda i, j: (0, j, 0))],
        out_specs=pl.BlockSpec((tm, tn), lambda i, j: (i, j)),
        out_shape=jax.ShapeDtypeStruct((m, n), BF16), compiler_params=_cp(2),
    )(dgu, w3)


def _ffn_in_bwd_w(x, dgu, nj, name):
    k, m = x.shape
    f = dgu.shape[2]
    per = nj // 2
    nb = f // per
    tm = min(TN_TM, m)
    return _matmul(x, dgu, ((0,), (0,)), (nj, m // tm),
                   pl.BlockSpec((k, tm), lambda j, i: (0, i)),
                   pl.BlockSpec((None, k, nb), lambda j, i: (j // per, 0, j % per)),
                   pl.BlockSpec((None, tm, nb), lambda j, i: (j, i, 0)),
                   jax.ShapeDtypeStruct((nj, m, nb), BF16), name)


def _vec_spec(d):
    return pl.BlockSpec((1, d), lambda i: (0, 0))


def _row_spec(ts, d, col=0):
    return pl.BlockSpec((ts, d), lambda i: (i, col))


def _rms(x):
    return lax.rsqrt(jnp.mean(x * x, axis=-1, keepdims=True) + EPS)


def _pre_norm(x, gain, scale, shift, dm, after=()):
    def body(x_ref, g_ref, sc_ref, sh_ref, *rest):
        h_ref = rest[len(after)]
        xv = x_ref[...]
        h_ref[...] = (((xv * _rms(xv)) * g_ref[...]) * (1.0 + sc_ref[...]) + sh_ref[...]).astype(h_ref.dtype)

    return pl.pallas_call(
        body, name="pre_norm", grid=(dm.S // dm.tr,),
        in_specs=[_row_spec(dm.tr, dm.D)] + [_vec_spec(dm.D)] * 3 + [_ORDER] * len(after),
        out_specs=_row_spec(dm.tr, dm.D),
        out_shape=jax.ShapeDtypeStruct((dm.S, dm.D), BF16), compiler_params=_cp(1),
    )(x, gain, scale, shift, *after)


def _res_norm(x, y, gpost, gate, gain, scale, shift, dm):
    def body(x_ref, y_ref, gp_ref, gt_ref, g_ref, sc_ref, sh_ref, xo_ref, h_ref):
        yv = y_ref[...].astype(F32)
        xn = x_ref[...] + gt_ref[...] * ((yv * _rms(yv)) * gp_ref[...])
        xo_ref[...] = xn
        h_ref[...] = (((xn * _rms(xn)) * g_ref[...]) * (1.0 + sc_ref[...]) + sh_ref[...]).astype(h_ref.dtype)

    return pl.pallas_call(
        body, name="res_norm", grid=(dm.S // dm.tr,),
        in_specs=[_row_spec(dm.tr, dm.D)] * 2 + [_vec_spec(dm.D)] * 5,
        out_specs=[_row_spec(dm.tr, dm.D)] * 2,
        out_shape=[jax.ShapeDtypeStruct((dm.S, dm.D), F32), jax.ShapeDtypeStruct((dm.S, dm.D), BF16)],
        compiler_params=_cp(1),
    )(x, y, gpost, gate, gain, scale, shift)


def _res_loss(x, y, gpost, gate, target, dm):
    def body(x_ref, y_ref, gp_ref, gt_ref, t_ref, dx_ref, loss_ref):
        i = pl.program_id(0)

        @pl.when(i == 0)
        def _():
            loss_ref[...] = jnp.zeros_like(loss_ref)
        yv = y_ref[...].astype(F32)
        err = x_ref[...] + gt_ref[...] * ((yv * _rms(yv)) * gp_ref[...]) - t_ref[...]
        dx_ref[...] = err * (1.0 / dm.D)
        per_row = jnp.mean(err * err, axis=-1, keepdims=True)
        loss_ref[...] += 0.5 * jnp.sum(per_row, axis=0, keepdims=True)

    return pl.pallas_call(
        body, name="res_loss", grid=(dm.S // dm.tr,),
        in_specs=[_row_spec(dm.tr, dm.D)] * 2 + [_vec_spec(dm.D)] * 2 + [_row_spec(dm.tr, dm.D)],
        out_specs=[_row_spec(dm.tr, dm.D), _vec_spec(LANES)],
        out_shape=[jax.ShapeDtypeStruct((dm.S, dm.D), F32), jax.ShapeDtypeStruct((1, LANES), F32)],
        compiler_params=_cp(1),
    )(x, y, gpost, gate, target)


def _post_bwd(dxo, y, gpost, gate, dm, after=()):
    def body(dx_ref, y_ref, gp_ref, gt_ref, *rest):
        dy_ref, dgp_ref, dgt_ref = rest[len(after):]
        i = pl.program_id(0)

        @pl.when(i == 0)
        def _():
            dgp_ref[...] = jnp.zeros_like(dgp_ref)
            dgt_ref[...] = jnp.zeros_like(dgt_ref)
        yv, dx = y_ref[...].astype(F32), dx_ref[...]
        r = _rms(yv)
        t = yv * r
        dgp_ref[...] += jnp.sum(dx * gt_ref[...] * t, axis=0, keepdims=True)
        dgt_ref[...] += jnp.sum(dx * (t * gp_ref[...]), axis=0, keepdims=True)
        dt = dx * (gt_ref[...] * gp_ref[...])
        dy_ref[...] = (r * (dt - t * jnp.mean(dt * t, axis=-1, keepdims=True))).astype(dy_ref.dtype)

    return pl.pallas_call(
        body, name="post_bwd", grid=(dm.S // dm.tr,),
        in_specs=[_row_spec(dm.tr, dm.D)] * 2 + [_vec_spec(dm.D)] * 2 + [_ORDER] * len(after),
        out_specs=[_row_spec(dm.tr, dm.D), _vec_spec(dm.D), _vec_spec(dm.D)],
        out_shape=[jax.ShapeDtypeStruct((dm.S, dm.D), BF16)] + [jax.ShapeDtypeStruct((1, dm.D), F32)] * 2,
        compiler_params=_cp(1),
    )(dxo, y, gpost, gate, *after)


def _pre_bwd(dh, dxo, x, gain, scale, dm, after=()):
    def body(dh_ref, dxo_ref, x_ref, g_ref, sc_ref, *rest):
        dx_ref, dsh_ref, dsc_ref, dg_ref = rest[len(after):]
        i = pl.program_id(0)

        @pl.when(i == 0)
        def _():
            dsh_ref[...] = jnp.zeros_like(dsh_ref)
            dsc_ref[...] = jnp.zeros_like(dsc_ref)
            dg_ref[...] = jnp.zeros_like(dg_ref)
        xv, dh_ = x_ref[...], dh_ref[...].astype(F32)
        r = _rms(xv)
        nrm = xv * r
        one_sc = 1.0 + sc_ref[...]
        dsh_ref[...] += jnp.sum(dh_, axis=0, keepdims=True)
        dsc_ref[...] += jnp.sum(dh_ * (nrm * g_ref[...]), axis=0, keepdims=True)
        dg_ref[...] += jnp.sum(dh_ * nrm * one_sc, axis=0, keepdims=True)
        dn = dh_ * (g_ref[...] * one_sc)
        dx_ref[...] = dxo_ref[...] + r * (dn - nrm * jnp.mean(dn * nrm, axis=-1, keepdims=True))

    return pl.pallas_call(
        body, name="pre_bwd", grid=(dm.S // dm.tr,),
        in_specs=[_row_spec(dm.tr, dm.D)] * 3 + [_vec_spec(dm.D)] * 2 + [_ORDER] * len(after),
        out_specs=[_row_spec(dm.tr, dm.D)] + [_vec_spec(dm.D)] * 3,
        out_shape=[jax.ShapeDtypeStruct((dm.S, dm.D), F32)] + [jax.ShapeDtypeStruct((1, dm.D), F32)] * 3,
        compiler_params=_cp(1),
    )(dh, dxo, x, gain, scale, *after)


def _sigmoid(z):
    return 1.0 / (1.0 + jnp.exp(-z))


def _tri(n, upper):
    r = lax.broadcasted_iota(jnp.int32, (n, n), 0)
    c = lax.broadcasted_iota(jnp.int32, (n, n), 1)
    return (c >= r if upper else r >= c).astype(F32)


def _gates_fwd(cf, bf, dm):
    ts = dm.ts
    fcol = 2 * dm.C // LANES

    def body(f_ref, b_ref, cum_ref, carry):
        i = pl.program_id(0)

        @pl.when(i == 0)
        def _():
            carry[...] = jnp.zeros_like(carry)
        z = f_ref[...] + b_ref[...]
        lf = jnp.minimum(z, 0.0) - jnp.log(1.0 + jnp.exp(-jnp.abs(z)))
        cs = jnp.dot(_tri(ts, False), lf, precision=lax.Precision.HIGHEST, preferred_element_type=F32) + carry[...]
        cum_ref[...] = cs
        carry[...] = cs[ts - 1:ts, :]

    return pl.pallas_call(
        body, name="gates_fwd", grid=(dm.S // ts,),
        in_specs=[pl.BlockSpec((ts, LANES), lambda i: (i, fcol)), _vec_spec(LANES)],
        out_specs=_row_spec(ts, LANES),
        out_shape=jax.ShapeDtypeStruct((dm.S, LANES), F32),
        scratch_shapes=[pltpu.VMEM((1, LANES), F32)], compiler_params=_cp(1),
    )(cf, bf)


def _gates_bwd(dc, cf, bf, dm):
    ts = dm.ts
    nb = dm.S // ts
    fcol = 2 * dm.C // LANES

    def body(dc_ref, f_ref, b_ref, df_ref, db_ref, carry):
        i = pl.program_id(0)

        @pl.when(i == 0)
        def _():
            carry[...] = jnp.zeros_like(carry)
            db_ref[...] = jnp.zeros_like(db_ref)
        dlf = jnp.dot(_tri(ts, True), dc_ref[...], precision=lax.Precision.HIGHEST, preferred_element_type=F32) + carry[...]
        carry[...] = dlf[0:1, :]
        dz = dlf * (1.0 - _sigmoid(f_ref[...] + b_ref[...]))
        df_ref[...] = dz.astype(df_ref.dtype)
        db_ref[...] += jnp.sum(dz, axis=0, keepdims=True)

    return pl.pallas_call(
        body, name="gates_bwd", grid=(nb,),
        in_specs=[pl.BlockSpec((ts, LANES), lambda i: (nb - 1 - i, 0)),
                  pl.BlockSpec((ts, LANES), lambda i: (nb - 1 - i, fcol)), _vec_spec(LANES)],
        out_specs=[pl.BlockSpec((ts, LANES), lambda i: (nb - 1 - i, 0)), _vec_spec(LANES)],
        out_shape=[jax.ShapeDtypeStruct((dm.S, LANES), BF16), jax.ShapeDtypeStruct((1, LANES), F32)],
        scratch_shapes=[pltpu.VMEM((1, LANES), F32)], compiler_params=_cp(1),
    )(dc, cf, bf)


def _dot_nt(a, b):
    return lax.dot_general(a, b, (((1,), (1,)), ((), ())), preferred_element_type=F32)


def _dot_tn(a, b):
    return lax.dot_general(a, b, (((0,), (0,)), ((), ())), preferred_element_type=F32)


_C0, _C1, _C2 = HEAD_DIM, HEAD_DIM + 3, HEAD_DIM + 6


def _split3(c):
    hi = c.astype(BF16).astype(F32)
    mid = (c - hi).astype(BF16).astype(F32)
    return hi, mid, c - hi - mid


def _put3(base, lane, start, pieces, sign=1.0):
    out = base
    for t, piece in enumerate(pieces):
        out = jnp.where(lane == start + t, sign * piece, out)
    return out


def _head_lanes(pair, odd):
    v = pair.astype(F32)
    return pltpu.roll(v, HEAD_DIM, axis=1) if odd else v


def _attn_prep(qkv, cum, dm):
    ts, A, H = dm.ts, dm.A, dm.H
    scale = HEAD_DIM ** -0.5

    def body(x_ref, c_ref, qa_ref, ka_ref, va_ref):
        lane = lax.broadcasted_iota(jnp.int32, (ts, LANES), 1)
        data = lane < HEAD_DIM
        for h in range(H):
            e, odd = h // 2, h % 2
            pieces = _split3(c_ref[:, h:h + 1])
            q = _head_lanes(x_ref[:, e * LANES:(e + 1) * LANES], odd) * scale
            k = _head_lanes(x_ref[:, A + e * LANES:A + (e + 1) * LANES], odd)
            v = _head_lanes(x_ref[:, 2 * A + e * LANES:2 * A + (e + 1) * LANES], odd)
            qa = jnp.where(data, q, jnp.where((lane >= _C1) & (lane < _C2), 1.0, 0.0))
            qa_ref[h] = _put3(qa, lane, _C0, pieces).astype(BF16)
            ka = jnp.where(data, k, jnp.where((lane < _C1) | ((lane >= _C2) & (lane < _C2 + 3)), 1.0, 0.0))
            ka_ref[h] = _put3(ka, lane, _C1, pieces, -1.0).astype(BF16)
            va_ref[h] = jnp.where(data, v, jnp.where(lane < _C1, 1.0, 0.0)).astype(BF16)

    spec = pl.BlockSpec((H, ts, LANES), lambda i: (0, i, 0))
    return pl.pallas_call(
        body, name="attn_prep", grid=(dm.S // ts,),
        in_specs=[_row_spec(ts, 3 * A), _row_spec(ts, LANES)], out_specs=[spec] * 3,
        out_shape=[jax.ShapeDtypeStruct((H, dm.S, LANES), BF16)] * 3, compiler_params=_cp(1),
    )(qkv, cum)


def _attn_fwd(qa, ka, va, dm):
    tq, A, H, S = dm.ts, dm.A, dm.H, dm.S
    nq = S // tq

    def body(qa_ref, ka_ref, va_ref, o_ref, lse_ref, top_scr, qb_scr, m_scr, acc_scr):
        i = pl.program_id(0)
        row = lax.broadcasted_iota(jnp.int32, (tq, tq), 0)
        col = lax.broadcasted_iota(jnp.int32, (tq, tq), 1)
        lane = lax.broadcasted_iota(jnp.int32, (tq, LANES), 1)

        def logits(q_ref, j, h, diagonal):
            s = _dot_nt(q_ref[h], ka_ref[h, pl.ds(pl.multiple_of(j * tq, tq), tq), :])
            return jnp.where(row >= col, s, -1e30) if diagonal else s

        def maxima(j, carry):
            for h in range(H):
                top_scr[h] = jnp.maximum(top_scr[h], logits(qa_ref, j, h, False))
            return carry

        for h in range(H):
            top_scr[h] = logits(qa_ref, i, h, True)
        lax.fori_loop(0, i, maxima, 0)
        for h in range(H):
            m = jnp.max(top_scr[h], axis=1, keepdims=True)
            m_scr[h] = m
            qb_scr[h] = _put3(qa_ref[h].astype(F32), lane, _C2, _split3(m), -1.0).astype(BF16)

        def weigh(j, carry):
            rows = pl.ds(pl.multiple_of(j * tq, tq), tq)
            for h in range(H):
                p = jnp.exp(logits(qb_scr, j, h, False)).astype(BF16)
                acc_scr[h] += jnp.dot(p, va_ref[h, rows, :], preferred_element_type=F32)
            return carry

        for h in range(H):
            p = jnp.exp(logits(qb_scr, i, h, True)).astype(BF16)
            acc_scr[h] = jnp.dot(p, va_ref[h, pl.ds(pl.multiple_of(i * tq, tq), tq), :], preferred_element_type=F32)
        lax.fori_loop(0, i, weigh, 0)
        lse_all = jnp.zeros((tq, LANES), F32)
        for h in range(H):
            acc = acc_scr[h]
            l = acc[:, _C0:_C0 + 1]
            o_ref[:, h * HEAD_DIM:(h + 1) * HEAD_DIM] = (acc[:, :HEAD_DIM] / l).astype(o_ref.dtype)
            lse_all = jnp.where(lane == h, m_scr[h] + jnp.log(l), lse_all)
        lse_ref[...] = lse_all

    full = pl.BlockSpec((H, S, LANES), lambda i: (0, 0, 0))
    return pl.pallas_call(
        body, name="attn_fwd", grid=(nq,),
        in_specs=[pl.BlockSpec((H, tq, LANES), lambda i: (0, i, 0)), full, full],
        out_specs=[pl.BlockSpec((tq, A), lambda i: (i, 0)), _row_spec(tq, LANES)],
        out_shape=[jax.ShapeDtypeStruct((S, A), BF16), jax.ShapeDtypeStruct((S, LANES), F32)],
        scratch_shapes=[pltpu.VMEM((H, tq, tq), F32), pltpu.VMEM((H, tq, LANES), BF16), pltpu.VMEM((H, tq, 1), F32),
                        pltpu.VMEM((H, tq, LANES), F32)],
        compiler_params=_cp(1),
    )(qa, ka, va)


def _attn_prep_bwd(qa, lse, dcat, o, dm):
    ts, A, H = dm.ts, dm.A, dm.H

    def body(qa_ref, lse_ref, do_ref, o_ref, qb_ref, doa_ref):
        lane = lax.broadcasted_iota(jnp.int32, (ts, LANES), 1)
        data = lane < HEAD_DIM
        for h in range(H):
            e, odd = h // 2, h % 2
            qb_ref[h] = _put3(qa_ref[h].astype(F32), lane, _C2, _split3(lse_ref[:, h:h + 1]), -1.0).astype(BF16)
            do_pair = do_ref[:, e * LANES:(e + 1) * LANES]
            prod = do_pair * o_ref[:, e * LANES:(e + 1) * LANES].astype(F32)
            mine = (lane >= HEAD_DIM) if odd else data
            delta = jnp.sum(jnp.where(mine, prod, 0.0), axis=1, keepdims=True)
            doa = jnp.where(data, _head_lanes(do_pair, odd), 0.0)
            doa_ref[h] = _put3(doa, lane, _C0, _split3(delta), -1.0).astype(BF16)

    spec = pl.BlockSpec((H, ts, LANES), lambda i: (0, i, 0))
    return pl.pallas_call(
        body, name="attn_prep_bwd", grid=(dm.S // ts,),
        in_specs=[spec, _row_spec(ts, LANES), _row_spec(ts, A, 0), _row_spec(ts, A)], out_specs=[spec] * 2,
        out_shape=[jax.ShapeDtypeStruct((H, dm.S, LANES), BF16)] * 2, compiler_params=_cp(1),
    )(qa, lse, dcat, o)


def _attn_bwd(qb, ka, va, doa, dm):
    tq, H, S = dm.ts, dm.H, dm.S
    nq = S // tq

    def body(ka_ref, va_ref, qb_ref, doa_ref, dq_ref, dk_ref, dv_ref):
        j = pl.program_id(0)

        @pl.when(j == 0)
        def _():
            dq_ref[...] = jnp.zeros(dq_ref.shape, F32)
        dk_ref[...] = jnp.zeros(dk_ref.shape, F32)
        dv_ref[...] = jnp.zeros(dv_ref.shape, F32)
        row = lax.broadcasted_iota(jnp.int32, (tq, tq), 0)
        col = lax.broadcasted_iota(jnp.int32, (tq, tq), 1)

        def block(i, masked):
            rows = pl.ds(pl.multiple_of(i * tq, tq), tq)
            for h in range(H):
                q, do_ = qb_ref[h, rows, :], doa_ref[h, rows, :]
                k, v = ka_ref[h], va_ref[h]
                s = _dot_nt(q, k)
                if masked:
                    s = jnp.where(row >= col, s, -1e30)
                p = jnp.exp(s)
                dsb = (p * _dot_nt(do_, v)).astype(BF16)
                dv_ref[h] += _dot_tn(p.astype(BF16), do_)
                dk_ref[h] += _dot_tn(dsb, q)
                dq_ref[h, rows, :] += jnp.dot(dsb, k, preferred_element_type=F32)

        block(j, True)

        def step(i, carry):
            block(i, False)
            return carry

        lax.fori_loop(j + 1, nq, step, 0)

    blk = pl.BlockSpec((H, tq, LANES), lambda j: (0, j, 0))
    full = pl.BlockSpec((H, S, LANES), lambda j: (0, 0, 0))
    return pl.pallas_call(
        body, name="attn_bwd", grid=(nq,),
        in_specs=[blk, blk, full, full], out_specs=[full, blk, blk],
        out_shape=[jax.ShapeDtypeStruct((H, S, LANES), F32)] * 3, compiler_params=_cp(1),
    )(ka, va, qb, doa)


def _attn_post(dqa, dka, dva, dm):
    ts, A, H = dm.ts, dm.A, dm.H
    scale = HEAD_DIM ** -0.5

    def body(dq_ref, dk_ref, dv_ref, o_ref, dc_ref):
        lane = lax.broadcasted_iota(jnp.int32, (ts, LANES), 1)
        data = lane < HEAD_DIM
        dc = jnp.zeros((ts, LANES), F32)
        for h in range(H):
            dc = jnp.where(lane == h, dq_ref[h][:, _C0:_C0 + 1] - dk_ref[h][:, _C1:_C1 + 1], dc)
        dc_ref[...] = dc
        for part, (ref, mul) in enumerate(((dq_ref, scale), (dk_ref, 1.0), (dv_ref, 1.0))):
            for e in range(H // 2):
                pair = jnp.where(data, ref[2 * e], pltpu.roll(ref[2 * e + 1], HEAD_DIM, axis=1))
                o_ref[:, part * A + e * LANES:part * A + (e + 1) * LANES] = (pair * mul).astype(o_ref.dtype)

    spec = pl.BlockSpec((H, ts, LANES), lambda i: (0, i, 0))
    return pl.pallas_call(
        body, name="attn_post", grid=(dm.S // ts,),
        in_specs=[spec] * 3, out_specs=[_row_spec(ts, 3 * A), _row_spec(ts, LANES)],
        out_shape=[jax.ShapeDtypeStruct((dm.S, 3 * A), BF16), jax.ShapeDtypeStruct((dm.S, LANES), F32)],
        compiler_params=_cp(1),
    )(dqa, dka, dva)


def _glu(cf_rows, c):
    return cf_rows[:, :c] * _sigmoid(cf_rows[:, c:2 * c])


def _conv_fwd(cf, cw, cb, lg, lb, dm):
    ts, C = dm.ts, dm.C
    per = ts // HALO

    def body(cf_ref, halo_ref, w_ref, cb_ref, lg_ref, lb_ref, u3_ref, u1_ref):
        i = pl.program_id(0)
        prev = jnp.where(i > 0, _glu(halo_ref[...], C), 0.0)
        win = jnp.concatenate([prev, _glu(cf_ref[...], C)], axis=0)
        u1 = jnp.zeros((ts, C), F32) + cb_ref[...]
        off = HALO - (CONV_K - 1)
        for k in range(CONV_K):
            u1 = u1 + w_ref[k:k + 1, :] * win[off + k:off + k + ts, :]
        u1_ref[...] = u1
        mu = jnp.mean(u1, axis=-1, keepdims=True)
        cen = u1 - mu
        rstd = lax.rsqrt(jnp.mean(cen * cen, axis=-1, keepdims=True) + EPS)
        u2 = cen * rstd * lg_ref[...] + lb_ref[...]
        u3_ref[...] = (u2 * _sigmoid(u2)).astype(u3_ref.dtype)

    return pl.pallas_call(
        body, name="conv_fwd", grid=(dm.S // ts,),
        in_specs=[pl.BlockSpec((ts, 2 * C), lambda i: (i, 0)),
                  pl.BlockSpec((HALO, 2 * C), lambda i: (jnp.maximum(i * per - 1, 0), 0)),
                  pl.BlockSpec((HALO, C), lambda i: (0, 0))] + [_vec_spec(C)] * 3,
        out_specs=[_row_spec(ts, C)] * 2,
        out_shape=[jax.ShapeDtypeStruct((dm.S, C), BF16), jax.ShapeDtypeStruct((dm.S, C), F32)],
        compiler_params=_cp(1),
    )(cf, cf, cw, cb, lg, lb)


def _conv_bwd(dcat, u1, cf, cw, lg, lb, dm):
    ts, C = dm.ts, dm.C
    per = ts // HALO
    nt = dm.S // ts
    last_halo = dm.S // HALO - 1

    def ln_bwd(du3, u1v, lg_v, lb_v):
        mu = jnp.mean(u1v, axis=-1, keepdims=True)
        cen = u1v - mu
        rstd = lax.rsqrt(jnp.mean(cen * cen, axis=-1, keepdims=True) + EPS)
        uhat = cen * rstd
        u2 = uhat * lg_v + lb_v
        sg = _sigmoid(u2)
        du2 = du3 * (sg * (1.0 + u2 * (1.0 - sg)))
        duh = du2 * lg_v
        du1 = rstd * (duh - jnp.mean(duh, axis=-1, keepdims=True) - uhat * jnp.mean(duh * uhat, axis=-1, keepdims=True))
        return du1, du2, uhat

    def body(d_ref, dn_ref, u1_ref, u1n_ref, cf_ref, halo_ref, w_ref, lg_ref, lb_ref,
             dcf_ref, dw_ref, dcb_ref, dlg_ref, dlb_ref):
        i = pl.program_id(0)

        @pl.when(i == 0)
        def _():
            dw_ref[...] = jnp.zeros_like(dw_ref)
            dcb_ref[...] = jnp.zeros_like(dcb_ref)
            dlg_ref[...] = jnp.zeros_like(dlg_ref)
            dlb_ref[...] = jnp.zeros_like(dlb_ref)
        lg_v, lb_v = lg_ref[...], lb_ref[...]
        du1, du2, uhat = ln_bwd(d_ref[...], u1_ref[...], lg_v, lb_v)
        du1n, _, _ = ln_bwd(dn_ref[...], u1n_ref[...], lg_v, lb_v)
        du1n = jnp.where(i < nt - 1, du1n, 0.0)
        dlg_ref[...] += jnp.sum(du2 * uhat, axis=0, keepdims=True)
        dlb_ref[...] += jnp.sum(du2, axis=0, keepdims=True)
        dcb_ref[...] += jnp.sum(du1, axis=0, keepdims=True)
        dwin = jnp.concatenate([du1, du1n], axis=0)
        cfv = cf_ref[...]
        cv, sg = cfv[:, :C], _sigmoid(cfv[:, C:2 * C])
        prev = jnp.where(i > 0, _glu(halo_ref[...], C), 0.0)
        uwin = jnp.concatenate([prev, cv * sg], axis=0)
        du0 = jnp.zeros((ts, C), F32)
        off = HALO - (CONV_K - 1)
        for k in range(CONV_K):
            back = CONV_K - 1 - k
            du0 = du0 + w_ref[k:k + 1, :] * dwin[back:back + ts, :]
            dw_ref[k:k + 1, :] += jnp.sum(du1 * uwin[off + k:off + k + ts, :], axis=0, keepdims=True)
        dcf_ref[:, :C] = (du0 * sg).astype(dcf_ref.dtype)
        dcf_ref[:, C:] = (du0 * cv * sg * (1.0 - sg)).astype(dcf_ref.dtype)

    ucol = dm.A // C
    return pl.pallas_call(
        body, name="conv_bwd", grid=(nt,),
        in_specs=[pl.BlockSpec((ts, C), lambda i: (i, ucol)),
                  pl.BlockSpec((HALO, C), lambda i: (jnp.minimum((i + 1) * per, last_halo), ucol)),
                  pl.BlockSpec((ts, C), lambda i: (i, 0)),
                  pl.BlockSpec((HALO, C), lambda i: (jnp.minimum((i + 1) * per, last_halo), 0)),
                  pl.BlockSpec((ts, 2 * C), lambda i: (i, 0)),
                  pl.BlockSpec((HALO, 2 * C), lambda i: (jnp.maximum(i * per - 1, 0), 0)),
                  pl.BlockSpec((HALO, C), lambda i: (0, 0)), _vec_spec(C), _vec_spec(C)],
        out_specs=[_row_spec(ts, 2 * C), pl.BlockSpec((HALO, C), lambda i: (0, 0))] + [_vec_spec(C)] * 3,
        out_shape=[jax.ShapeDtypeStruct((dm.S, 2 * C), BF16), jax.ShapeDtypeStruct((HALO, C), F32)]
        + [jax.ShapeDtypeStruct((1, C), F32)] * 3,
        compiler_params=_cp(1),
    )(dcat, dcat, u1, u1, cf, cf, cw, lg, lb)


def _ada_fwd(c16, ada_w, ada_b_cols, dm):
    L, D, n = ada_w.shape
    tn = _tile(n, 512)

    def body(c_ref, w_ref, b_ref, o_ref, a_ref):
        cv = c_ref[...]
        act = (cv * _sigmoid(cv)).astype(BF16)
        a_ref[...] = act
        o_ref[...] = jnp.dot(act, w_ref[...].astype(BF16), preferred_element_type=F32) + b_ref[...]

    return pl.pallas_call(
        body, name="ada_fwd", grid=(L, n // tn),
        in_specs=[pl.BlockSpec((16, D), lambda l, j: (0, 0)), pl.BlockSpec((None, D, tn), lambda l, j: (l, 0, j)),
                  pl.BlockSpec((None, 1, tn), lambda l, j: (l, 0, j))],
        out_specs=[pl.BlockSpec((None, 16, tn), lambda l, j: (l, 0, j)), pl.BlockSpec((16, D), lambda l, j: (0, 0))],
        out_shape=[jax.ShapeDtypeStruct((L, 16, n), F32), jax.ShapeDtypeStruct((16, D), BF16)],
        compiler_params=_cp(2),
    )(c16, ada_w, ada_b_cols)


def _ada_bwd(act16, dmod16):
    L, _, n = dmod16.shape
    D = act16.shape[1]
    tm = min(TN_TM, D)

    def body(a_ref, d_ref, o_ref):
        o_ref[...] = _dot_tn(a_ref[...], d_ref[...])

    return pl.pallas_call(
        body, name="ada_bwd", grid=(L, D // tm),
        in_specs=[pl.BlockSpec((16, tm), lambda l, i: (0, i)), pl.BlockSpec((None, 16, n), lambda l, i: (l, 0, 0))],
        out_specs=pl.BlockSpec((None, tm, n), lambda l, i: (l, i, 0)),
        out_shape=jax.ShapeDtypeStruct((L, D, n), F32), compiler_params=_cp(2),
    )(act16, dmod16)


def _sum_devices(g8):
    _, R, _ = g8.shape
    tr = _rows_tile(R)

    def body(g_ref, o_ref):
        acc = g_ref[0]
        for d in range(1, N_DEV):
            acc = acc + g_ref[d]
        o_ref[...] = acc

    return pl.pallas_call(
        body, name="sum_devices", grid=(R // tr,),
        in_specs=[pl.BlockSpec((N_DEV, tr, LANES), lambda i: (0, i, 0))],
        out_specs=pl.BlockSpec((tr, LANES), lambda i: (i, 0)),
        out_shape=jax.ShapeDtypeStruct((R, LANES), F32), compiler_params=_cp(1),
    )(g8)


def _rows_tile(r, cap=512):
    for t in (512, 256, 128, 64, 32, 16, 8):
        if t <= cap and r % t == 0:
            return t
    return r


def _adam_math(w, g, m, v):
    m = ADAM_B1 * m + (1.0 - ADAM_B1) * g
    v = ADAM_B2 * v + (1.0 - ADAM_B2) * (g * g)
    m_hat = m / (1.0 - ADAM_B1 ** ADAM_STEP)
    v_hat = v / (1.0 - ADAM_B2 ** ADAM_STEP)
    delta = -ADAM_LR * (m_hat / (jnp.sqrt(v_hat) + ADAM_EPS) + ADAM_WD * w)
    return delta, m, v


def _adamw_dense(w, m, v, g, name):
    R, Cc = w.shape
    tr = _rows_tile(R, ADAM_ROWS)

    def body(w_ref, m_ref, v_ref, g_ref, d_ref, mo_ref, vo_ref):
        d, mn, vn = _adam_math(w_ref[...], g_ref[...], m_ref[...], v_ref[...])
        d_ref[...] = d
        mo_ref[...] = mn
        vo_ref[...] = vn

    spec = pl.BlockSpec((tr, Cc), lambda i: (i, 0))
    return pl.pallas_call(
        body, name=name, grid=(R // tr,), in_specs=[spec] * 4, out_specs=[spec] * 3,
        out_shape=[jax.ShapeDtypeStruct((R, Cc), F32)] * 3, compiler_params=_cp(1),
    )(w, m, v, g)


def _adamw_shard(w, m, v, near, far, layer, prev, name):
    L, r, cc = w.shape
    tr, tc = (_rows_tile(r, ADAM_ROWS), cc) if r % 8 == 0 else (r, _tile(cc, 2 * LANES))

    def body(w_ref, m_ref, v_ref, n_ref, f_ref, *rest):
        g_ref, d_ref, mo_ref, vo_ref = rest[-4:]
        g = n_ref[0].astype(F32) + f_ref[0].astype(F32)
        for k in range(1, N_CHIPS):
            g = g + (n_ref[k].astype(F32) + f_ref[k].astype(F32))
        d, mn, vn = _adam_math(w_ref[...], g, m_ref[...], v_ref[...])
        g_ref[...] = g
        d_ref[...] = d
        mo_ref[...] = mn
        vo_ref[...] = vn

    wspec = pl.BlockSpec((None, tr, tc), lambda i, j: (layer, i, j))
    sspec = pl.BlockSpec((N_CHIPS, tr, tc), lambda i, j: (0, i, j))
    n_prev = 0 if prev is None else 4
    return pl.pallas_call(
        body, name=name, grid=(r // tr, cc // tc),
        in_specs=[wspec] * 3 + [sspec] * 2 + [pl.BlockSpec(memory_space=pl.ANY)] * n_prev,
        out_specs=[wspec] * 4,
        out_shape=[jax.ShapeDtypeStruct((L, r, cc), F32)] * 4,
        input_output_aliases={5 + t: t for t in range(n_prev)},
        compiler_params=_cp(2),
    )(w, m, v, near, far, *(prev or ()))


def _pack(vs):
    flat = jnp.concatenate([v.reshape(-1).astype(F32) for v in vs])
    pad = (-flat.shape[0]) % (64 * LANES)
    return jnp.pad(flat, (0, pad)).reshape(-1, LANES)


def _unpack(packed, shapes):
    flat = packed.reshape(-1)
    out, pos = [], 0
    for s in shapes:
        n = 1
        for d in s:
            n *= d
        out.append(flat[pos:pos + n].reshape(s))
        pos += n
    return out


def kernel(x, c, w_in, b_f, conv_w, conv_b, conv_ln_g, conv_ln_b, w_o, w_ffn_in, w_ffn_out, mix_pre_g, mix_post_g, ffn_pre_g, ffn_post_g, ada_w, ada_b, loss_target, m_w_in, m_b_f, m_conv_w, m_conv_b, m_conv_ln_g, m_conv_ln_b, m_w_o, m_w_ffn_in, m_w_ffn_out, m_mix_pre_g, m_mix_post_g, m_ffn_pre_g, m_ffn_post_g, m_ada_w, m_ada_b, v_w_in, v_b_f, v_conv_w, v_conv_b, v_conv_ln_g, v_conv_ln_b, v_w_o, v_w_ffn_in, v_w_ffn_out, v_mix_pre_g, v_mix_post_g, v_ffn_pre_g, v_ffn_post_g, v_ada_w, v_ada_b):
    params = dict(w_in=w_in, b_f=b_f, conv_w=conv_w, conv_b=conv_b, conv_ln_g=conv_ln_g, conv_ln_b=conv_ln_b, w_o=w_o,
                  w_ffn_in=w_ffn_in, w_ffn_out=w_ffn_out, mix_pre_g=mix_pre_g, mix_post_g=mix_post_g,
                  ffn_pre_g=ffn_pre_g, ffn_post_g=ffn_post_g, ada_w=ada_w, ada_b=ada_b)
    mom = dict(w_in=m_w_in, b_f=m_b_f, conv_w=m_conv_w, conv_b=m_conv_b, conv_ln_g=m_conv_ln_g, conv_ln_b=m_conv_ln_b,
               w_o=m_w_o, w_ffn_in=m_w_ffn_in, w_ffn_out=m_w_ffn_out, mix_pre_g=m_mix_pre_g, mix_post_g=m_mix_post_g,
               ffn_pre_g=m_ffn_pre_g, ffn_post_g=m_ffn_post_g, ada_w=m_ada_w, ada_b=m_ada_b)
    var = dict(w_in=v_w_in, b_f=v_b_f, conv_w=v_conv_w, conv_b=v_conv_b, conv_ln_g=v_conv_ln_g, conv_ln_b=v_conv_ln_b,
               w_o=v_w_o, w_ffn_in=v_w_ffn_in, w_ffn_out=v_w_ffn_out, mix_pre_g=v_mix_pre_g, mix_post_g=v_mix_post_g,
               ffn_pre_g=v_ffn_pre_g, ffn_post_g=v_ffn_post_g, ada_w=v_ada_w, ada_b=v_ada_b)

    S, D = x.shape[1], x.shape[2]
    L = w_in.shape[0]
    A = D // 2
    C = D - A
    H = A // HEAD_DIM
    F = w_ffn_out.shape[1] * N_CHIPS
    d_in = w_in.shape[2] * N_CHIPS
    NP = 3 * A + 2 * C + LANES
    dm = Dims(S=S, D=D, A=A, C=C, H=H, F=F, L=L, NP=NP, ts=min(ROW_TILE, S), tr=min(NORM_TILE, S))
    assert H <= 8 and A == C and d_in == 3 * A + H + 2 * C

    ix, iy, ic = _place()
    chip = 2 * ix + iy
    dev = 4 * ix + 2 * iy + ic
    x2 = x.reshape(S, D)
    tgt = loss_target.reshape(S, D)

    swap = lambda t: jnp.transpose(t, (0, 2, 1))
    big_state = dict(w_in=(swap(w_in), swap(m_w_in), swap(v_w_in)), w_o=(w_o, m_w_o, v_w_o),
                     w_ffn_in=(w_ffn_in, m_w_ffn_in, v_w_ffn_in), w_ffn_out=(w_ffn_out, m_w_ffn_out, v_w_ffn_out))

    def gather_start(l, after, after_rest=()):
        first = _exchange_start([w_in[l].T.astype(BF16)], "gather", after, f"gather_a_start_{l}")
        return first, gather_rest(l, [first["token"], *after_rest])

    def gather_rest(l, after):
        return _exchange_start([w_o[l].astype(BF16), w_ffn_in[l].astype(BF16), w_ffn_out[l].astype(BF16)], "gather",
                               after, f"gather_b_start_{l}")

    def gather_wait(st, after, name):
        return _exchange_wait(st, after, name)[1]

    first_0 = _exchange_start([w_in[0].T.astype(BF16)], "gather", [], "gather_a_start_0")

    c_all = _all_gather_devices(c.reshape(D // LANES, LANES), "gather_c", after=[first_0["token"]]).reshape(N_DEV, D)
    c16 = jnp.pad(c_all, ((0, 16 - N_DEV), (0, 0)))
    n_ada = ada_w.shape[2]
    ada_b_cols = lax.dynamic_slice_in_dim(ada_b, chip * n_ada, n_ada, axis=1).reshape(L, 1, n_ada)
    mod_cols, act16 = _ada_fwd(c16, ada_w, ada_b_cols, dm)
    conv_w_all, mod_all = _all_gather_chips([conv_w.reshape(L * CONV_K, -1), mod_cols.reshape(L * 16, n_ada)], "gather_mod")
    cwc = conv_w.shape[2]
    conv_w_full = conv_w_all.reshape(N_CHIPS, L, CONV_K, cwc).transpose(1, 2, 0, 3).reshape(L, CONV_K, C)
    conv_w_full = jnp.pad(conv_w_full, ((0, 0), (0, HALO - CONV_K), (0, 0)))
    mod_all = mod_all.reshape(N_CHIPS, L, 16, n_ada)
    mod_me = lax.dynamic_index_in_dim(mod_all, dev, axis=2, keepdims=False)
    mod_me = mod_me.transpose(1, 0, 2).reshape(L, N_MOD, 1, D)

    gather = [None] * L
    gather[0] = (first_0, gather_rest(0, [first_0["token"], mod_all]))

    def projection_of(g_in):
        w_nat = g_in.reshape(d_in, D)
        return jnp.concatenate([w_nat[:3 * A], w_nat[3 * A + H:], w_nat[3 * A:3 * A + H],
                                jnp.zeros((LANES - H, D), BF16)], axis=0)

    gathered = [None] * L
    vec = lambda p, l: p[l].reshape(1, -1)
    bf_pad = jnp.pad(b_f, ((0, 0), (0, LANES - H)))

    saved = []
    xin = x2
    h = _pre_norm(xin, vec(mix_pre_g, 0), mod_me[0, 1], mod_me[0, 0], dm, after=[gather[0][1]["token"]])
    dx = loss_part = None
    for l in range(L):
        (g_in,) = gather_wait(gather[l][0], [h], f"gather_a_wait_{l}")
        w_p = projection_of(g_in)
        order = [gather[l][1]["token"]]
        if l + 1 < L:
            gather[l + 1] = gather_start(l + 1, [g_in, gather[l][1]["token"]])
            order.append(gather[l + 1][1]["token"])
        qkv = _mm_nt(h, w_p, BF16, "mm_qkv", 0, 3 * A, after=order)
        cf = _mm_nt(h, w_p, F32, "mm_cf", 3 * A, 2 * C + LANES)
        cum = _gates_fwd(cf, vec(bf_pad, l), dm)
        qa, ka, va = _attn_prep(qkv, cum, dm)
        o, lse = _attn_fwd(qa, ka, va, dm)
        u3, u1 = _conv_fwd(cf, conv_w_full[l], vec(conv_b, l), vec(conv_ln_g, l), vec(conv_ln_b, l), dm)
        cat = jnp.concatenate([o, u3], axis=1)
        g_o, wfi, g_fo = gather_wait(gather[l][1], [cat], f"gather_b_wait_{l}")
        wo, wfo = g_o.reshape(D, D), g_fo.reshape(F, D)
        gathered[l] = (w_p, wo, wfi, wfo)
        y = _mm_nn(cat, wo, BF16, "mm_o")
        x1, h2 = _res_norm(xin, y, vec(mix_post_g, l), mod_me[l, 2], vec(ffn_pre_g, l), mod_me[l, 4], mod_me[l, 3], dm)
        gu, a = _ffn_in_swiglu(h2, wfi, "mm_ffn_in")
        y2 = _mm_nn(a, wfo, BF16, "mm_ffn_out")
        saved.append(dict(xin=xin, h=h, qa=qa, ka=ka, va=va, cf=cf, o=o, lse=lse, u1=u1, cat=cat, y=y,
                          x1=x1, h2=h2, gu=gu, a=a, y2=y2))
        if l + 1 < L:
            xin, h = _res_norm(x1, y2, vec(ffn_post_g, l), mod_me[l, 5], vec(mix_pre_g, l + 1),
                               mod_me[l + 1, 1], mod_me[l + 1, 0], dm)
        else:
            dx, loss_part = _res_loss(x1, y2, vec(ffn_post_g, l), mod_me[l, 5], tgt, dm)
    loss = lax.psum(loss_part[0, 0], ("x", "y", "c"))

    small = [None] * L
    big = [None] * L
    forward = [None] * L

    def forward_start(l, after):
        near = []
        for st, nm in zip(big[l], ("a", "b")):
            near += _exchange_wait(st, after, f"scatter_{nm}_wait_{l}")[1]
        return _exchange_start(near, "sibling", [], f"forward_start_{l}")

    order = []
    for l in reversed(range(L)):
        w_p, wo, wfi, wfo = gathered[l]
        sv = saved[l]
        dy2, d_gfpost, d_g2 = _post_bwd(dx, sv["y2"], vec(ffn_post_g, l), mod_me[l, 5], dm, after=order)
        dgu = _ffn_out_bwd_swiglu(dy2, wfo, sv["gu"], "mm_da")
        g_wfo = _mm_tn(sv["a"], dy2, BF16, "mm_dwfo")
        g_wfi = _ffn_in_bwd_w(sv["h2"], dgu, N_CHIPS, "mm_dwfi")
        dh2 = _ffn_in_bwd_x(dgu, wfi, "mm_dh2")
        scatter_ffn = _exchange_start([g_wfi, g_wfo.reshape(N_CHIPS, F // N_CHIPS, D)], "scatter", [], f"scatter_b_start_{l}")
        dx1, d_sh2, d_sc2, d_gfpre = _pre_bwd(dh2, dx, sv["x1"], vec(ffn_pre_g, l), mod_me[l, 4], dm,
                                              after=[scatter_ffn["token"]])
        dy, d_gpost, d_g1 = _post_bwd(dx1, sv["y"], vec(mix_post_g, l), mod_me[l, 2], dm)
        dcat = _mm_nt(dy, wo, F32, "mm_dcat")
        g_wo = _mm_tn(sv["cat"], dy, BF16, "mm_dwo")
        dcfc, d_cw, d_cb, d_lg, d_lb = _conv_bwd(dcat, sv["u1"], sv["cf"], conv_w_full[l], vec(conv_ln_g, l),
                                                 vec(conv_ln_b, l), dm)
        qb, doa = _attn_prep_bwd(sv["qa"], sv["lse"], dcat, sv["o"], dm)
        dqkv, dcum = _attn_post(*_attn_bwd(qb, sv["ka"], sv["va"], doa, dm), dm)
        dfl, d_bf = _gates_bwd(dcum, sv["cf"], vec(bf_pad, l), dm)
        dproj = jnp.concatenate([dqkv, dcfc, dfl], axis=1)
        dh = _mm_nn(dproj, w_p, BF16, "mm_dh")
        g_wp = _mm_tn(dproj, sv["h"], BF16, "mm_dwp")
        dx, d_sh1, d_sc1, d_gpre = _pre_bwd(dh, dx1, sv["xin"], vec(mix_pre_g, l), mod_me[l, 1], dm)
        g_nat = jnp.concatenate([g_wp[:3 * A], g_wp[3 * A + 2 * C:3 * A + 2 * C + H], g_wp[3 * A:3 * A + 2 * C]], axis=0)
        g_win = g_nat.reshape(N_CHIPS, d_in // N_CHIPS, D)
        g_mix = [g_win, g_wo.reshape(N_CHIPS, D // N_CHIPS, D)]
        order = []
        if l > 0:
            scatter_mix = _exchange_start(g_mix, "scatter", [], f"scatter_a_start_{l}")
            order.append(scatter_mix["token"])
            big[l] = (scatter_mix, scatter_ffn)
        if l + 1 < L:
            forward[l + 1] = forward_start(l + 1, [dx])
            order.append(forward[l + 1]["token"])
        small[l] = dict(b_f=d_bf[0, :H], conv_b=d_cb[0], conv_ln_g=d_lg[0], conv_ln_b=d_lb[0], mix_pre_g=d_gpre[0],
                        mix_post_g=d_gpost[0], ffn_pre_g=d_gfpre[0], ffn_post_g=d_gfpost[0],
                        dmod=jnp.concatenate([d_sh1, d_sc1, d_g1, d_sh2, d_sc2, d_g2], axis=1)[0],
                        conv_w=d_cw[:CONV_K])
    grad_x = dx.reshape(1, S, D)

    keys_small = ["b_f", "conv_b", "conv_ln_g", "conv_ln_b", "mix_pre_g", "mix_post_g", "ffn_pre_g", "ffn_post_g",
                  "dmod", "conv_w"]
    stacked = [jnp.stack([small[l][k] for l in range(L)]) for k in keys_small]
    shapes = [s.shape for s in stacked]
    pack = _pack(stacked)
    small_exchange = _exchange_start([pack], "devices", order, "gather_small_start")
    big[0] = (_exchange_start(g_mix, "scatter", [small_exchange["token"]], "scatter_a_start_0"), scatter_ffn)

    names_big = ["w_in", "w_o", "w_ffn_in", "w_ffn_out"]
    res_big = {n: None for n in names_big}

    def update(l, after):
        near, far = _exchange_wait(forward[l], after, f"forward_wait_{l}")
        for t, n in enumerate(names_big):
            res_big[n] = _adamw_shard(*big_state[n], near[t], far[t], l, res_big[n], f"adamw_{n}_{l}")
        return [res_big[n][1] for n in names_big]

    done = [big[0][0]["token"]]
    for l in reversed(range(1, L)):
        done = update(l, done)
    forward[0] = forward_start(0, done)

    (pack8,) = _exchange_wait(small_exchange, [forward[0]["token"]], "gather_small_wait")[1]
    summed = dict(zip(keys_small, _unpack(_sum_devices(pack8), shapes)))
    at_dmod = sum(s.size for s in stacked[:keys_small.index("dmod")])
    dmod_all = pack8.reshape(N_DEV, -1)[:, at_dmod:at_dmod + L * N_MOD * D].reshape(N_DEV, L, N_MOD * D)
    grads = {k: summed[k] for k in keys_small[:8]}
    grads["ada_b"] = summed["dmod"]
    grads["conv_w"] = lax.dynamic_slice_in_dim(summed["conv_w"], chip * cwc, cwc, axis=2)

    dmod_cols = lax.dynamic_slice_in_dim(dmod_all.reshape(N_DEV, L, N_CHIPS, n_ada), chip, 1, axis=2)
    dmod16 = jnp.pad(dmod_cols.reshape(N_DEV, L, n_ada).transpose(1, 0, 2), ((0, 0), (0, 16 - N_DEV), (0, 0))).astype(BF16)
    grads["ada_w"] = _ada_bwd(act16, dmod16)

    d_aw, m_aw, v_aw = _adamw_dense(ada_w.reshape(L * D, n_ada), m_ada_w.reshape(L * D, n_ada),
                                    v_ada_w.reshape(L * D, n_ada), grads["ada_w"].reshape(L * D, n_ada), "adamw_ada_w")
    names_small = ["b_f", "conv_w", "conv_b", "conv_ln_g", "conv_ln_b", "mix_pre_g", "mix_post_g", "ffn_pre_g",
                   "ffn_post_g", "ada_b"]
    shapes_small = [params[n].shape for n in names_small]
    d_s, m_s, v_s = _adamw_dense(_pack([params[n] for n in names_small]), _pack([mom[n] for n in names_small]),
                                 _pack([var[n] for n in names_small]), _pack([grads[n] for n in names_small]),
                                 "adamw_small")
    delta_w = dict(zip(names_small, _unpack(d_s, shapes_small)))
    new_m = dict(zip(names_small, _unpack(m_s, shapes_small)))
    new_v = dict(zip(names_small, _unpack(v_s, shapes_small)))
    delta_w["ada_w"], new_m["ada_w"], new_v["ada_w"] = (t.reshape(L, D, n_ada) for t in (d_aw, m_aw, v_aw))

    update(0, [d_s, d_aw])
    res_big["w_in"] = [swap(t) for t in res_big["w_in"]]
    for n in names_big:
        grads[n], delta_w[n], new_m[n], new_v[n] = res_big[n]

    return (loss, grad_x, *[grads[n] for n in WEIGHTS], *[delta_w[n] for n in WEIGHTS],
            *[new_m[n] for n in WEIGHTS], *[new_v[n] for n in WEIGHTS])
```
